```python
import math
import jax, jax.numpy as jnp
from jax import lax
import numpy as np

D_MODEL = 1024
BATCH = 32
SEQ = 256
DEPTH = 2
DEC_BATCH = 2
DEC_SEQ = 1024
PAST_LEN = 256

GRID_W = 64
N_DIR = 2
CHUNK = 128
H_A = 4
DH_A = 128
W_A = H_A * DH_A
H_B = 8
P_B = 64
W_B = H_B * P_B
G_B = 2
N_B = 128
W_XBC = W_B + 2 * G_B * N_B
SSM_CONV = 3
W_C = 512
CONV_W = 31
D_MIX = W_A + W_B + W_C
D_A_IN = 4 * W_A + N_DIR * 2 * H_A
D_B_IN = W_B + W_XBC + N_DIR * H_B
D_C_IN = 2 * W_C
D_IN = D_A_IN + D_B_IN + D_C_IN
N_EXPERTS = 16
N_EGROUPS = 4
EXPERTS_PER_GROUP = N_EXPERTS // N_EGROUPS
TOP_K = 2
D_FF_EXPERT = 512
ALPHA = (2 * DEPTH) ** 0.25
BETA = (8 * DEPTH) ** -0.25
EPS = 1e-5
F32 = jnp.float32

kernel_name = 'hybrid_mlstm_ssd_conformer_moe_diffusion_step'


def _norm(x):
    x32 = x.astype(F32)
    mu = jnp.mean(x32, axis=-1, keepdims=True)
    xc = x32 - mu
    var = jnp.mean(xc * xc, axis=-1, keepdims=True)
    return (xc * lax.rsqrt(var + EPS)).astype(x.dtype)


def _rms(x, g):
    x32 = x.astype(F32)
    return x32 * lax.rsqrt(jnp.mean(x32 * x32, axis=-1, keepdims=True) + EPS) * g.astype(F32)


def _dwconv(x, w, b):
    k, ch = w.shape
    y = lax.conv_general_dilated(x, w.astype(x.dtype)[:, None, :], window_strides=(1,),
                                 padding=[(k // 2, k // 2)],
                                 dimension_numbers=('NWC', 'WIO', 'NWC'),
                                 feature_group_count=ch)
    return y + b.astype(x.dtype)


def _flip(t):
    return jnp.flip(t, axis=1)


def _to_chunks(t):
    b, s, h = t.shape[:3]
    t = t.reshape((b, s // CHUNK, CHUNK, h) + t.shape[3:])
    return jnp.moveaxis(t, (1, 3), (0, 2))


def _from_chunks(t):
    t = jnp.moveaxis(t, (0, 2), (1, 3))
    return t.reshape((t.shape[0], t.shape[1] * t.shape[2]) + t.shape[3:])


def _grid_pos(n_tok, dtype):
    rows = n_tok // GRID_W
    r, col = jnp.meshgrid(jnp.arange(rows, dtype=F32), jnp.arange(GRID_W, dtype=F32), indexing='ij')
    quarter = D_MODEL // 4
    omega = 1.0 / (10000.0 ** (jnp.arange(quarter, dtype=F32) / quarter))

    def emb(p):
        ang = p.reshape(-1)[:, None] * omega[None, :]
        return jnp.concatenate([jnp.sin(ang), jnp.cos(ang)], axis=-1)

    return jnp.concatenate([emb(r), emb(col)], axis=-1).astype(dtype)


def _mlstm_scan(q, k, v, log_i, log_f, C0, n0, m0):
    causal = jnp.tril(jnp.ones((CHUNK, CHUNK), dtype=bool))
    xs = (_to_chunks(q), _to_chunks(k), _to_chunks(v), _to_chunks(log_i), _to_chunks(log_f))

    def step(carry, inp):
        C, n, m = carry
        qc, kc, vc, li, lf = inp
        b = jnp.cumsum(lf, axis=-1)
        g = b[..., -1]
        d = jnp.where(causal, b[..., :, None] - b[..., None, :] + li[..., None, :], -jnp.inf)
        inter = b + m[..., None]
        m_t = jnp.maximum(inter, jnp.max(d, axis=-1))
        s = jnp.einsum('bhtd,bhjd->bhtj', qc, kc) * jnp.exp(d - m_t[..., None])
        w_int = jnp.exp(inter - m_t)
        num = w_int[..., None] * jnp.einsum('bhtd,bhde->bhte', qc, C) + jnp.einsum('bhtj,bhje->bhte', s, vc)
        den = w_int * jnp.einsum('bhtd,bhd->bht', qc, n) + jnp.sum(s, axis=-1)
        h = num / jnp.maximum(jnp.abs(den), jnp.exp(-m_t))[..., None]
        a = g[..., None] - b + li
        m_new = jnp.maximum(g + m, jnp.max(a, axis=-1))
        w_prev = jnp.exp(g + m - m_new)
        w_tok = jnp.exp(a - m_new[..., None])
        C_new = w_prev[..., None, None] * C + jnp.einsum('bhj,bhjd,bhje->bhde', w_tok, kc, vc)
        n_new = w_prev[..., None] * n + jnp.einsum('bhj,bhjd->bhd', w_tok, kc)
        return (C_new, n_new, m_new), h

    (C, n, m), hs = lax.scan(step, (C0.astype(F32), n0.astype(F32), m0.astype(F32)), xs)
    return _from_chunks(hs), (C, n, m)


def _mlstm_mixer(pa, b_i, b_f, norm_g, C0, n0, m0):
    bsz, s, _ = pa.shape
    q, k, v, o, gates = jnp.split(pa, [W_A, 2 * W_A, 3 * W_A, 4 * W_A], axis=-1)

    def heads(t):
        return t.reshape(bsz, s, H_A, DH_A).astype(F32)

    q = heads(q) * (DH_A ** -0.5)
    k = heads(k)
    v = heads(v)
    gates = gates.reshape(bsz, s, N_DIR, 2, H_A).astype(F32)
    log_i = gates[:, :, :, 0] + b_i.astype(F32)
    log_f = jax.nn.log_sigmoid(gates[:, :, :, 1] + b_f.astype(F32))
    h_f, st_f = _mlstm_scan(q, k, v, log_i[:, :, 0], log_f[:, :, 0], C0[:, 0], n0[:, 0], m0[:, 0])
    h_b, st_b = _mlstm_scan(_flip(q), _flip(k), _flip(v), _flip(log_i[:, :, 1]), _flip(log_f[:, :, 1]),
                            C0[:, 1], n0[:, 1], m0[:, 1])
    h = h_f + _flip(h_b)
    h = _norm(h).reshape(bsz, s, W_A) * norm_g.astype(F32)
    h = h * jax.nn.sigmoid(o.astype(F32))
    states = (jnp.stack([st_f[0], st_b[0]], axis=1),
              jnp.stack([st_f[1], st_b[1]], axis=1),
              jnp.stack([st_f[2], st_b[2]], axis=1))
    return h, states


def _ssd_scan(x, dt, A, Bm, Cm, h0):
    causal = jnp.tril(jnp.ones((CHUNK, CHUNK), dtype=bool))
    la = dt * A
    xs = (_to_chunks(x), _to_chunks(dt), _to_chunks(la), _to_chunks(Bm), _to_chunks(Cm))

    def step(h, inp):
        xc, dtc, lac, bc, cc = inp
        b = jnp.cumsum(lac, axis=-1)
        g = b[..., -1]
        seg = jnp.where(causal, b[..., :, None] - b[..., None, :], -jnp.inf)
        s = jnp.einsum('bhtn,bhjn->bhtj', cc, bc) * jnp.exp(seg) * dtc[..., None, :]
        y = jnp.einsum('bhtj,bhjp->bhtp', s, xc) + jnp.exp(b)[..., None] * jnp.einsum('bhtn,bhpn->bhtp', cc, h)
        w_tok = jnp.exp(g[..., None] - b) * dtc
        h_new = jnp.exp(g)[..., None, None] * h + jnp.einsum('bhj,bhjp,bhjn->bhpn', w_tok, xc, bc)
        return h_new, y

    h_fin, ys = lax.scan(step, h0.astype(F32), xs)
    return _from_chunks(ys), h_fin


def _ssd_mixer(pb, conv_w, conv_b, dt_bias, A_log, D_skip, norm_g, h0):
    bsz, s, _ = pb.shape
    z, xbc, dt_pre = jnp.split(pb, [W_B, W_B + W_XBC], axis=-1)
    xbc = jax.nn.silu(_dwconv(xbc, conv_w, conv_b))
    x, Bm, Cm = jnp.split(xbc, [W_B, W_B + G_B * N_B], axis=-1)
    x = x.reshape(bsz, s, H_B, P_B).astype(F32)
    Bm = jnp.repeat(Bm.reshape(bsz, s, G_B, N_B), H_B // G_B, axis=2).astype(F32)
    Cm = jnp.repeat(Cm.reshape(bsz, s, G_B, N_B), H_B // G_B, axis=2).astype(F32)
    dt = jax.nn.softplus(dt_pre.reshape(bsz, s, N_DIR, H_B).astype(F32) + dt_bias.astype(F32))
    A = -jnp.exp(A_log.astype(F32))
    y_f, h_f = _ssd_scan(x, dt[:, :, 0], A[0], Bm, Cm, h0[:, 0])
    y_b, h_b = _ssd_scan(_flip(x), _flip(dt[:, :, 1]), A[1], _flip(Bm), _flip(Cm), h0[:, 1])
    y = y_f + _flip(y_b) + D_skip.astype(F32)[:, None] * x
    y = y.reshape(bsz, s, W_B) * jax.nn.silu(z.astype(F32))
    return _rms(y, norm_g), jnp.stack([h_f, h_b], axis=1)


def _conv_mixer(pc, dw_w, dw_b, ln_g, ln_b):
    a, g = jnp.split(pc, 2, axis=-1)
    u = a * jax.nn.sigmoid(g)
    u = _dwconv(u, dw_w, dw_b)
    u = _norm(u) * ln_g + ln_b
    return jax.nn.silu(u)


def _moe(u, w_router, b_router, w_gate, w_up, w_down):
    bsz, s, d = u.shape
    t = u.reshape(bsz * s, d)
    logits = t.astype(F32) @ w_router.astype(F32) + b_router.astype(F32)
    probs = jax.nn.softmax(logits, axis=-1)
    grp_score = jnp.sum(lax.top_k(probs.reshape(-1, N_EGROUPS, EXPERTS_PER_GROUP), TOP_K)[0], axis=-1)
    sel = jnp.argmax(grp_score, axis=-1)
    expert_group = jnp.arange(N_EXPERTS) // EXPERTS_PER_GROUP
    in_grp = expert_group[None, :] == sel[:, None]
    top_p, top_i = lax.top_k(jnp.where(in_grp, probs, -jnp.inf), TOP_K)
    top_w = top_p / jnp.sum(top_p, axis=-1, keepdims=True)
    gate = jnp.sum(jax.nn.one_hot(top_i, N_EXPERTS, dtype=F32) * top_w[..., None], axis=1)
    h = jnp.einsum('td,edf->tef', t, w_gate)
    up = jnp.einsum('td,edf->tef', t, w_up)
    act = jax.nn.silu(h) * up * gate[:, :, None].astype(t.dtype)
    y = jnp.einsum('tef,efd->td', act, w_down)
    return y.reshape(bsz, s, d)


def _layer(x, mod, C0, n0, m0, h0, lp, w_router, b_router):
    (w_in, w_o, b_i, b_f, m_norm_g, s_conv_w, s_conv_b, dt_bias, A_log, D_skip, s_norm_g,
     dw_w, dw_b, cln_g, cln_b, p1g, p1b, p2g, p2b, weg, weu, wed) = lp
    shift1, scale1, gate1, shift2, scale2, gate2 = jnp.split(mod, 6, axis=-1)
    u = _norm(x) * (1 + scale1) + shift1
    proj = u @ w_in
    pa, pb, pc = jnp.split(proj, [D_A_IN, D_A_IN + D_B_IN], axis=-1)
    ha, st_a = _mlstm_mixer(pa, b_i, b_f, m_norm_g, C0, n0, m0)
    hb, st_b = _ssd_mixer(pb, s_conv_w, s_conv_b, dt_bias, A_log, D_skip, s_norm_g, h0)
    hc = _conv_mixer(pc, dw_w, dw_b, cln_g, cln_b)
    mix = jnp.concatenate([ha.astype(x.dtype), hb.astype(x.dtype), hc.astype(x.dtype)], axis=-1) @ w_o
    x = _norm(ALPHA * x + gate1 * mix) * p1g + p1b
    u = _norm(x) * (1 + scale2) + shift2
    x = _norm(ALPHA * x + gate2 * _moe(u, w_router, b_router, weg, weu, wed)) * p2g + p2b
    return x, (st_a[0], st_a[1], st_a[2], st_b)


def setup_inputs(seed: int = 0) -> dict:
    key = jax.random.key(seed)
    k = jax.random.split(key, 34)

    def nrm(i, shape, s):
        return s * jax.random.normal(k[i], shape, F32)

    def uni(i, shape, lo, hi):
        return jax.random.uniform(k[i], shape, F32, lo, hi)

    dt0 = jnp.exp(uni(15, (DEPTH, N_DIR, H_B), math.log(1e-3), math.log(1e-1)))
    return {
        'x_prompt': nrm(0, (BATCH, SEQ, D_MODEL), 1.0),
        'x_sample': nrm(1, (DEC_BATCH, DEC_SEQ, D_MODEL), 1.0),
        'state_mlstm_C': nrm(2, (DEC_BATCH, DEPTH, N_DIR, H_A, DH_A, DH_A), 0.1),
        'state_mlstm_n': nrm(3, (DEC_BATCH, DEPTH, N_DIR, H_A, DH_A), 0.1),
        'state_mlstm_m': nrm(4, (DEC_BATCH, DEPTH, N_DIR, H_A), 0.5),
        'state_ssd': nrm(5, (DEC_BATCH, DEPTH, N_DIR, H_B, P_B, N_B), 0.1),
        'c': nrm(6, (DEC_BATCH, D_MODEL), 1.0),
        'c_ctx': nrm(7, (D_MODEL,), 1.0),
        'w_in': nrm(8, (DEPTH, D_MODEL, D_IN), D_MODEL ** -0.5),
        'w_o': nrm(9, (DEPTH, D_MIX, D_MODEL), BETA * D_MIX ** -0.5),
        'mlstm_b_i': nrm(10, (DEPTH, N_DIR, H_A), 0.1),
        'mlstm_b_f': uni(11, (DEPTH, N_DIR, H_A), 3.0, 6.0),
        'mlstm_norm_g': 1.0 + nrm(12, (DEPTH, W_A), 0.02),
        'ssd_conv_w': nrm(13, (DEPTH, SSM_CONV, W_XBC), SSM_CONV ** -0.5),
        'ssd_conv_b': nrm(14, (DEPTH, W_XBC), 0.02),
        'ssd_dt_bias': dt0 + jnp.log(-jnp.expm1(-dt0)),
        'ssd_A_log': jnp.log(uni(16, (DEPTH, N_DIR, H_B), 1.0, 16.0)),
        'ssd_D': 1.0 + nrm(17, (DEPTH, H_B), 0.1),
        'ssd_norm_g': 1.0 + nrm(18, (DEPTH, W_B), 0.02),
        'conv_dw_w': nrm(19, (DEPTH, CONV_W, W_C), CONV_W ** -0.5),
        'conv_dw_b': nrm(20, (DEPTH, W_C), 0.02),
        'conv_ln_g': 1.0 + nrm(21, (DEPTH, W_C), 0.02),
        'conv_ln_b': nrm(22, (DEPTH, W_C), 0.02),
        'w_ada': nrm(23, (DEPTH, D_MODEL, 6 * D_MODEL), 0.5 * D_MODEL ** -0.5),
        'b_ada': nrm(24, (DEPTH, 6 * D_MODEL), 0.02),
        'post1_g': 1.0 + nrm(25, (DEPTH, D_MODEL), 0.02),
        'post1_b': nrm(26, (DEPTH, D_MODEL), 0.02),
        'post2_g': 1.0 + nrm(27, (DEPTH, D_MODEL), 0.02),
        'post2_b': nrm(28, (DEPTH, D_MODEL), 0.02),
        'w_router': nrm(29, (D_MODEL, N_EXPERTS), D_MODEL ** -0.5),
        'b_router': nrm(30, (N_EXPERTS,), 0.01),
        'w_e_gate': nrm(31, (DEPTH, N_EXPERTS, D_MODEL, D_FF_EXPERT), D_MODEL ** -0.5),
        'w_e_up': nrm(32, (DEPTH, N_EXPERTS, D_MODEL, D_FF_EXPERT), D_MODEL ** -0.5),
        'w_e_down': nrm(33, (DEPTH, N_EXPERTS, D_FF_EXPERT, D_MODEL), BETA * D_FF_EXPERT ** -0.5),
    }


def reference(x_prompt, x_sample, state_mlstm_C, state_mlstm_n, state_mlstm_m, state_ssd, c, c_ctx,
              w_in, w_o, mlstm_b_i, mlstm_b_f, mlstm_norm_g, ssd_conv_w, ssd_conv_b, ssd_dt_bias,
              ssd_A_log, ssd_D, ssd_norm_g, conv_dw_w, conv_dw_b, conv_ln_g, conv_ln_b, w_ada, b_ada,
              post1_g, post1_b, post2_g, post2_b, w_router, b_router, w_e_gate, w_e_up, w_e_down):
    n_ctx = x_prompt.shape[0]
    zC = jnp.zeros((n_ctx, N_DIR, H_A, DH_A, DH_A), F32)
    zn = jnp.zeros((n_ctx, N_DIR, H_A, DH_A), F32)
    zm = jnp.zeros((n_ctx, N_DIR, H_A), F32)
    zh = jnp.zeros((n_ctx, N_DIR, H_B, P_B, N_B), F32)
    xc = x_prompt
    xl = x_sample + _grid_pos(x_sample.shape[1], x_sample.dtype)[None]
    Cs, ns, ms, hs = [], [], [], []
    for l in range(DEPTH):
        lp = (w_in[l], w_o[l], mlstm_b_i[l], mlstm_b_f[l], mlstm_norm_g[l], ssd_conv_w[l], ssd_conv_b[l],
              ssd_dt_bias[l], ssd_A_log[l], ssd_D[l], ssd_norm_g[l], conv_dw_w[l], conv_dw_b[l],
              conv_ln_g[l], conv_ln_b[l], post1_g[l], post1_b[l], post2_g[l], post2_b[l],
              w_e_gate[l], w_e_up[l], w_e_down[l])
        mod_ctx = (jax.nn.silu(c_ctx) @ w_ada[l] + b_ada[l])[None, None, :]
        mod_lat = (jax.nn.silu(c) @ w_ada[l] + b_ada[l])[:, None, :]
        xc, st = _layer(xc, mod_ctx, zC, zn, zm, zh, lp, w_router, b_router)
        Cs.append(st[0])
        ns.append(st[1])
        ms.append(st[2])
        hs.append(st[3])
        xl, _ = _layer(xl, mod_lat, state_mlstm_C[:, l], state_mlstm_n[:, l], state_mlstm_m[:, l],
                       state_ssd[:, l], lp, w_router, b_router)
    new_C = jnp.stack(Cs, axis=1).astype(x_prompt.dtype)
    new_n = jnp.stack(ns, axis=1).astype(x_prompt.dtype)
    new_m = jnp.stack(ms, axis=1).astype(x_prompt.dtype)
    new_h = jnp.stack(hs, axis=1).astype(x_prompt.dtype)
    return (xc, xl, new_C, new_n, new_m, new_h)
```

```python
import functools

import jax
import jax.numpy as jnp
from jax import lax
from jax.experimental import pallas as pl
from jax.experimental.pallas import tpu as pltpu

D_MODEL = 1024
BATCH = 32
SEQ = 256
DEPTH = 2
DEC_BATCH = 2
DEC_SEQ = 1024
GRID_W = 64
N_DIR = 2
CHUNK = 128
H_A = 4
DH_A = 128
W_A = H_A * DH_A
H_B = 8
P_B = 64
W_B = H_B * P_B
G_B = 2
N_B = 128
W_XBC = W_B + 2 * G_B * N_B
SSM_CONV = 3
W_C = 512
CONV_W = 31
D_MIX = W_A + W_B + W_C
N_EXPERTS = 16
N_EGROUPS = 4
EXPERTS_PER_GROUP = N_EXPERTS // N_EGROUPS
TOP_K = 2
D_FF_EXPERT = 512
ALPHA = (2 * DEPTH) ** 0.25
EPS = 1e-5
F32 = jnp.float32
BF16 = jnp.bfloat16

N_CTX_TOK = BATCH * SEQ
N_LAT_TOK = DEC_BATCH * DEC_SEQ
T_ALL = N_CTX_TOK + N_LAT_TOK
N_MOD_ROWS = 8
D_MAIN = 4 * W_A + W_B + W_XBC + 2 * W_C
LANES = 128
HALO = 16
CONV_ROWS = 256
TM_PROJ = 512
TM_MOE = 256
N_ASSIGN = TOP_K * T_ALL
N_MOE_ROWS = N_ASSIGN + N_EXPERTS * TM_MOE
N_MOE_TILES = N_MOE_ROWS // TM_MOE
VMEM_LIMIT = 48 * 1024 * 1024

_NT = (((1,), (1,)), ((), ()))
_TN = (((0,), (0,)), ((), ()))


def _ln(x):
    mu = jnp.mean(x, axis=-1, keepdims=True)
    xc = x - mu
    var = jnp.mean(xc * xc, axis=-1, keepdims=True)
    return xc * lax.rsqrt(var + EPS)


def _sigmoid(x):
    return 1.0 / (1.0 + jnp.exp(-x))


def _silu(x):
    return x * _sigmoid(x)


def _softplus(x):
    return jnp.maximum(x, 0.0) + jnp.log1p(jnp.exp(-jnp.abs(x)))


def _dot(a, b):
    return jnp.dot(a, b, preferred_element_type=F32)


def _dotg(a, b, dims):
    return lax.dot_general(a, b, dims, preferred_element_type=F32)


def _tri_cumsum(tri, x):
    hi = x.astype(BF16)
    r1 = x - hi.astype(F32)
    mid = r1.astype(BF16)
    lo = (r1 - mid.astype(F32)).astype(BF16)
    return _dot(tri, hi) + _dot(tri, mid) + _dot(tri, lo)


def _tri_mask(d):
    row = lax.broadcasted_iota(jnp.int32, (CHUNK, CHUNK), 0)
    col = lax.broadcasted_iota(jnp.int32, (CHUNK, CHUNK), 1)
    return (row >= col) if d == 0 else (row <= col)


def _mod_row(row_start):
    return jnp.where(row_start < N_CTX_TOK, 0, 1 + (row_start - N_CTX_TOK) // DEC_SEQ)


def _cparams(n_axes):
    return pltpu.CompilerParams(dimension_semantics=("arbitrary",) * n_axes, vmem_limit_bytes=VMEM_LIMIT)


def _ada_kernel(c_ref, w_ref, b_ref, o_ref):
    o_ref[0] = _dot(_silu(c_ref[...]), w_ref[0]) + b_ref[0]


def _ada(cvec, w_ada, b_ada):
    tn = 1536
    return pl.pallas_call(
        _ada_kernel,
        grid=(DEPTH, 6 * D_MODEL // tn),
        in_specs=[
            pl.BlockSpec((N_MOD_ROWS, D_MODEL), lambda l, j: (0, 0)),
            pl.BlockSpec((1, D_MODEL, tn), lambda l, j: (l, 0, j)),
            pl.BlockSpec((1, 1, tn), lambda l, j: (l, 0, j)),
        ],
        out_specs=pl.BlockSpec((1, N_MOD_ROWS, tn), lambda l, j: (l, 0, j)),
        out_shape=jax.ShapeDtypeStruct((DEPTH, N_MOD_ROWS, 6 * D_MODEL), F32),
        compiler_params=_cparams(2),
    )(cvec, w_ada, b_ada.reshape(DEPTH, 1, 6 * D_MODEL))


def _embed_kernel(xp_ref, xs_ref, pos_ref, o_ref):
    i = pl.program_id(0)

    @pl.when(i < N_CTX_TOK // DEC_SEQ)
    def _():
        o_ref[...] = xp_ref[...]

    @pl.when(i >= N_CTX_TOK // DEC_SEQ)
    def _():
        o_ref[...] = xs_ref[...] + pos_ref[...]


def _embed(xp, xs, pos):
    n_ctx_blk = N_CTX_TOK // DEC_SEQ
    return pl.pallas_call(
        _embed_kernel,
        grid=(T_ALL // DEC_SEQ,),
        in_specs=[
            pl.BlockSpec((DEC_SEQ, D_MODEL), lambda i: (jnp.minimum(i, n_ctx_blk - 1), 0)),
            pl.BlockSpec((DEC_SEQ, D_MODEL), lambda i: (jnp.maximum(i - n_ctx_blk, 0), 0)),
            pl.BlockSpec((DEC_SEQ, D_MODEL), lambda i: (0, 0)),
        ],
        out_specs=pl.BlockSpec((DEC_SEQ, D_MODEL), lambda i: (i, 0)),
        out_shape=jax.ShapeDtypeStruct((T_ALL, D_MODEL), F32),
        compiler_params=_cparams(1),
    )(xp, xs, pos)


def _inproj_kernel(x_ref, mod_ref, wm_ref, wg_ref, om_ref, og_ref, u_scr, *, tm):
    i = pl.program_id(0)
    j = pl.program_id(1)

    @pl.when(j == 0)
    def _():
        r = _mod_row(i * tm)
        shift = mod_ref[pl.ds(r, 1), 0:D_MODEL]
        scale = mod_ref[pl.ds(r, 1), D_MODEL:2 * D_MODEL]
        u = (_ln(x_ref[...]) * (1.0 + scale) + shift).astype(BF16)
        u_scr[...] = u
        og_ref[...] = _dot(u, wg_ref[...])

    om_ref[...] = _dot(u_scr[...], wm_ref[...])


def _inproj(x, mod, w_main, w_small):
    tm, tn = TM_PROJ, D_MAIN // 2
    return pl.pallas_call(
        functools.partial(_inproj_kernel, tm=tm),
        grid=(T_ALL // tm, D_MAIN // tn),
        in_specs=[
            pl.BlockSpec((tm, D_MODEL), lambda i, j: (i, 0)),
            pl.BlockSpec((N_MOD_ROWS, 6 * D_MODEL), lambda i, j: (0, 0)),
            pl.BlockSpec((D_MODEL, tn), lambda i, j: (0, j)),
            pl.BlockSpec((D_MODEL, LANES), lambda i, j: (0, 0)),
        ],
        out_specs=[
            pl.BlockSpec((tm, tn), lambda i, j: (i, j)),
            pl.BlockSpec((tm, LANES), lambda i, j: (i, 0)),
        ],
        out_shape=[
            jax.ShapeDtypeStruct((T_ALL, D_MAIN), F32),
            jax.ShapeDtypeStruct((T_ALL, LANES), F32),
        ],
        scratch_shapes=[pltpu.VMEM((tm, D_MODEL), BF16)],
        compiler_params=_cparams(2),
    )(x, mod, w_main, w_small)


def _mlstm_kernel(*refs, seq, has_init, emit_state):
    it = iter(refs)
    q_ref, k_ref, v_ref, o_ref, g_ref, gb_ref, ng_ref = (next(it) for _ in range(7))
    if has_init:
        c0_ref, n0_ref, m0_ref = (next(it) for _ in range(3))
    out_ref = next(it)
    if emit_state:
        co_ref, no_ref, mo_ref = (next(it) for _ in range(3))
    tg_scr, hf_scr, c_scr, n_scr, m_scr = (next(it) for _ in range(5))
    n_chunks = seq // CHUNK

    y = g_ref[...] + gb_ref[...]
    lane = lax.broadcasted_iota(jnp.int32, y.shape, 1)
    is_forget = (lane < N_DIR * 2 * H_A) & ((lane & H_A) != 0)
    tg_scr[...] = jnp.where(is_forget, -_softplus(-y), y)

    def run_direction(d):
        for h in range(H_A):
            row = d * H_A + h
            if has_init:
                c_scr[h] = c0_ref[0, d, h]
                n_scr[h:h + 1, :] = n0_ref[0, row:row + 1, :]
                m_scr[h:h + 1, :] = m0_ref[0, row:row + 1, :]
            else:
                c_scr[h] = jnp.zeros((DH_A, DH_A), F32)
                n_scr[h:h + 1, :] = jnp.zeros((1, DH_A), F32)
                m_scr[h:h + 1, :] = jnp.zeros((1, LANES), F32)

        mask = _tri_mask(d)
        tri = mask.astype(BF16)

        def body(ci, carry):
            c = ci if d == 0 else n_chunks - 1 - ci
            r0 = pl.multiple_of(c * CHUNK, CHUNK)
            rows = pl.ds(r0, CHUNK)
            tc = tg_scr[rows, :]
            cum = _tri_cumsum(tri, tc)
            cum_t = cum.T
            tc_t = tc.T
            for h in range(H_A):
                ci_col = d * 2 * H_A + h
                cf_col = ci_col + H_A
                cols = slice(h * DH_A, (h + 1) * DH_A)
                li_c = tc[:, ci_col:ci_col + 1]
                li_r = tc_t[ci_col:ci_col + 1, :]
                b_c = cum[:, cf_col:cf_col + 1]
                b_r = cum_t[cf_col:cf_col + 1, :]
                g = b_r[:, CHUNK - 1:CHUNK] if d == 0 else b_r[:, 0:1]
                qs = q_ref[rows, cols] * (DH_A ** -0.5)
                kf = k_ref[rows, cols]
                qb = qs.astype(BF16)
                kb = kf.astype(BF16)
                vb = v_ref[rows, cols].astype(BF16)
                c_st = c_scr[h]
                n_st = n_scr[h:h + 1, :]
                m_st = m_scr[h:h + 1, 0:1]

                dm = jnp.where(mask, b_c - b_r + li_r, -jnp.inf)
                inter = b_c + m_st
                m_t = jnp.maximum(inter, jnp.max(dm, axis=-1, keepdims=True))
                s = _dotg(qb, kb, _NT) * jnp.exp(dm - m_t)
                w_int = jnp.exp(inter - m_t)
                num = w_int * _dot(qb, c_st.astype(BF16)) + _dot(s.astype(BF16), vb)
                den = w_int * jnp.sum(qs * n_st, axis=-1, keepdims=True) + jnp.sum(s, axis=-1, keepdims=True)
                hh = num / jnp.maximum(jnp.abs(den), jnp.exp(-m_t))

                a_c = g - b_c + li_c
                a_r = g - b_r + li_r
                m_new = jnp.maximum(g + m_st, jnp.max(a_r, axis=-1, keepdims=True))
                w_prev = jnp.exp(g + m_st - m_new)
                kw = jnp.exp(a_c - m_new) * kf
                c_scr[h] = w_prev * c_st + _dotg(kw.astype(BF16), vb, _TN)
                n_scr[h:h + 1, :] = w_prev * n_st + jnp.sum(kw, axis=0, keepdims=True)
                m_scr[h:h + 1, :] = jnp.broadcast_to(m_new, (1, LANES))

                if d == 0:
                    hf_scr[rows, cols] = hh
                else:
                    hn = _ln(hf_scr[rows, cols] + hh) * ng_ref[:, cols]
                    out_ref[rows, cols] = (hn * _sigmoid(o_ref[rows, cols])).astype(out_ref.dtype)
            return carry

        lax.fori_loop(0, n_chunks, body, 0)

        if emit_state:
            for h in range(H_A):
                row = d * H_A + h
                co_ref[0, d, h] = c_scr[h]
                no_ref[0, row:row + 1, :] = n_scr[h:h + 1, :]
                mo_ref[0, row:row + 1, :] = m_scr[h:h + 1, :]

    run_direction(0)
    run_direction(1)


def _mlstm(proj, gates, gate_bias, norm_g, seq, n_seq, row_off, init=None):
    blk0 = row_off // seq
    has_init = init is not None
    emit_state = not has_init
    col = lambda c: pl.BlockSpec((seq, W_A), lambda s: (s + blk0, c))
    in_specs = [col(0), col(1), col(2), col(3),
                pl.BlockSpec((seq, LANES), lambda s: (s + blk0, 0)),
                pl.BlockSpec((1, LANES), lambda s: (0, 0)),
                pl.BlockSpec((1, W_A), lambda s: (0, 0))]
    args = [proj, proj, proj, proj, gates, gate_bias, norm_g]
    if has_init:
        in_specs += [pl.BlockSpec((1, N_DIR, H_A, DH_A, DH_A), lambda s: (s, 0, 0, 0, 0)),
                     pl.BlockSpec((1, N_DIR * H_A, DH_A), lambda s: (s, 0, 0)),
                     pl.BlockSpec((1, N_DIR * H_A, LANES), lambda s: (s, 0, 0))]
        args += list(init)
    out_specs = [pl.BlockSpec((seq, W_A), lambda s: (s, 0))]
    out_shape = [jax.ShapeDtypeStruct((n_seq * seq, W_A), BF16)]
    if emit_state:
        out_specs += [pl.BlockSpec((1, N_DIR, H_A, DH_A, DH_A), lambda s: (s, 0, 0, 0, 0)),
                      pl.BlockSpec((1, N_DIR * H_A, DH_A), lambda s: (s, 0, 0)),
                      pl.BlockSpec((1, N_DIR * H_A, LANES), lambda s: (s, 0, 0))]
        out_shape += [jax.ShapeDtypeStruct((n_seq, N_DIR, H_A, DH_A, DH_A), F32),
                      jax.ShapeDtypeStruct((n_seq, N_DIR * H_A, DH_A), F32),
                      jax.ShapeDtypeStruct((n_seq, N_DIR * H_A, LANES), F32)]
    return pl.pallas_call(
        functools.partial(_mlstm_kernel, seq=seq, has_init=has_init, emit_state=emit_state),
        grid=(n_seq,),
        in_specs=in_specs,
        out_specs=out_specs,
        out_shape=out_shape,
        scratch_shapes=[pltpu.VMEM((seq, LANES), F32),
                        pltpu.VMEM((seq, W_A), F32),
                        pltpu.VMEM((H_A, DH_A, DH_A), F32),
                        pltpu.VMEM((8, DH_A), F32),
                        pltpu.VMEM((8, LANES), F32)],
        compiler_params=_cparams(1),
    )(*args)


_DT_COL0 = N_DIR * 2 * H_A
_HG = H_B // G_B
_WG = _HG * P_B


def _ssd_kernel(*refs, seq, has_init, emit_state):
    it = iter(refs)
    (z_ref, x_ref, b_ref, c_ref, g_ref, gb_ref, alog_ref, cw_ref, cb_ref, dsk_ref, ng_ref) = (
        next(it) for _ in range(11))
    if has_init:
        h0_ref = next(it)
    out_ref = next(it)
    if emit_state:
        ho_ref = next(it)
    pad_scr, xbc_scr, dt_scr, yf_scr, h_scr = (next(it) for _ in range(5))
    n_chunks = seq // CHUNK
    pad = 8

    pad_scr[0:pad, :] = jnp.zeros((pad, W_XBC), F32)
    pad_scr[pad + seq:2 * pad + seq, :] = jnp.zeros((pad, W_XBC), F32)
    pad_scr[pad:pad + seq, 0:W_B] = x_ref[...]
    pad_scr[pad:pad + seq, W_B:W_B + G_B * N_B] = b_ref[...]
    pad_scr[pad:pad + seq, W_B + G_B * N_B:W_XBC] = c_ref[...]
    for c in range(n_chunks):
        for lb in range(W_XBC // 256):
            cols = slice(lb * 256, (lb + 1) * 256)
            acc = cb_ref[:, cols]
            for k in range(SSM_CONV):
                r = c * CHUNK + pad - SSM_CONV // 2 + k
                acc = acc + cw_ref[k:k + 1, cols] * pad_scr[r:r + CHUNK, cols]
            xbc_scr[c * CHUNK:(c + 1) * CHUNK, cols] = _silu(acc)

    dt_scr[...] = _softplus(g_ref[...] + gb_ref[...])
    a_row = -jnp.exp(alog_ref[...])

    lane_blk = lax.broadcasted_iota(jnp.int32, (CHUNK, _WG), 1) // P_B
    row_blk = lax.broadcasted_iota(jnp.int32, (_WG, 1), 0) // P_B

    def run_direction(d):
        for g in range(G_B):
            for hh in range(_HG):
                head = g * _HG + hh
                if has_init:
                    h_scr[g, hh * P_B:(hh + 1) * P_B, :] = h0_ref[0, d, head]
                else:
                    h_scr[g, hh * P_B:(hh + 1) * P_B, :] = jnp.zeros((P_B, N_B), F32)

        mask = _tri_mask(d)
        tri = mask.astype(BF16)

        def body(ci, carry):
            c = ci if d == 0 else n_chunks - 1 - ci
            r0 = pl.multiple_of(c * CHUNK, CHUNK)
            rows = pl.ds(r0, CHUNK)
            dtc = dt_scr[rows, :]
            cum = _tri_cumsum(tri, dtc * a_row)
            cum_t = cum.T
            dt_t = dtc.T
            for g in range(G_B):
                xg = xbc_scr[rows, g * _WG:(g + 1) * _WG]
                xgb = xg.astype(BF16)
                bg = xbc_scr[rows, W_B + g * N_B:W_B + (g + 1) * N_B].astype(BF16)
                cg = xbc_scr[rows, W_B + G_B * N_B + g * N_B:W_B + G_B * N_B + (g + 1) * N_B].astype(BF16)
                h_st = h_scr[g]
                cb = _dotg(cg, bg, _NT)
                ch = _dotg(cg, h_st.astype(BF16), _NT)
                y = jnp.zeros((CHUNK, _WG), F32)
                e_full = jnp.zeros((CHUNK, _WG), F32)
                w_full = jnp.zeros((CHUNK, _WG), F32)
                decay = jnp.zeros((_WG, 1), F32)
                for hh in range(_HG):
                    colh = _DT_COL0 + d * H_B + g * _HG + hh
                    b_c = cum[:, colh:colh + 1]
                    b_r = cum_t[colh:colh + 1, :]
                    dt_c = dtc[:, colh:colh + 1]
                    dt_r = dt_t[colh:colh + 1, :]
                    gt = b_r[:, CHUNK - 1:CHUNK] if d == 0 else b_r[:, 0:1]
                    seg = jnp.where(mask, b_c - b_r, -jnp.inf)
                    sm = cb * jnp.exp(seg) * dt_r
                    sel = lane_blk == hh
                    y = jnp.where(sel, _dot(sm.astype(BF16), xgb), y)
                    e_full = jnp.where(sel, jnp.exp(b_c), e_full)
                    w_full = jnp.where(sel, jnp.exp(gt - b_c) * dt_c, w_full)
                    decay = jnp.where(row_blk == hh, jnp.exp(gt), decay)
                y = y + e_full * ch
                h_scr[g] = decay * h_st + _dotg((w_full * xg).astype(BF16), bg, _TN)
                gcols = slice(g * _WG, (g + 1) * _WG)
                if d == 0:
                    yf_scr[rows, gcols] = y
                else:
                    yf_scr[rows, gcols] = yf_scr[rows, gcols] + y + dsk_ref[:, gcols] * xg
            if d == 1:
                yz = yf_scr[rows, :] * _silu(z_ref[rows, :])
                rms = lax.rsqrt(jnp.mean(yz * yz, axis=-1, keepdims=True) + EPS)
                out_ref[rows, :] = (yz * rms * ng_ref[...]).astype(out_ref.dtype)
            return carry

        lax.fori_loop(0, n_chunks, body, 0)

        if emit_state:
            for g in range(G_B):
                for hh in range(_HG):
                    ho_ref[0, d, g * _HG + hh] = h_scr[g, hh * P_B:(hh + 1) * P_B, :]

    run_direction(0)
    run_direction(1)


def _ssd(proj, gates, gate_bias, alog_row, conv_w, conv_b, d_row, norm_g, seq, n_seq, row_off, init=None):
    blk0 = row_off // seq
    has_init = init is not None
    emit_state = not has_init
    zc = W_A * 4 // W_B
    bc = (4 * W_A + 2 * W_B) // (G_B * N_B)
    const = lambda shape: pl.BlockSpec(shape, lambda s: (0,) * len(shape))
    in_specs = [pl.BlockSpec((seq, W_B), lambda s: (s + blk0, zc)),
                pl.BlockSpec((seq, W_B), lambda s: (s + blk0, zc + 1)),
                pl.BlockSpec((seq, G_B * N_B), lambda s: (s + blk0, bc)),
                pl.BlockSpec((seq, G_B * N_B), lambda s: (s + blk0, bc + 1)),
                pl.BlockSpec((seq, LANES), lambda s: (s + blk0, 0)),
                const((1, LANES)), const((1, LANES)), const((8, W_XBC)), const((1, W_XBC)),
                const((1, W_B)), const((1, W_B))]
    args = [proj, proj, proj, proj, gates, gate_bias, alog_row, conv_w, conv_b, d_row, norm_g]
    st_spec = pl.BlockSpec((1, N_DIR, H_B, P_B, N_B), lambda s: (s, 0, 0, 0, 0))
    if has_init:
        in_specs.append(st_spec)
        args.append(init)
    out_specs = [pl.BlockSpec((seq, W_B), lambda s: (s, 0))]
    out_shape = [jax.ShapeDtypeStruct((n_seq * seq, W_B), BF16)]
    if emit_state:
        out_specs.append(st_spec)
        out_shape.append(jax.ShapeDtypeStruct((n_seq, N_DIR, H_B, P_B, N_B), F32))
    return pl.pallas_call(
        functools.partial(_ssd_kernel, seq=seq, has_init=has_init, emit_state=emit_state),
        grid=(n_seq,),
        in_specs=in_specs,
        out_specs=out_specs,
        out_shape=out_shape,
        scratch_shapes=[pltpu.VMEM((seq + 16, W_XBC), F32),
                        pltpu.VMEM((seq, W_XBC), F32),
                        pltpu.VMEM((seq, LANES), F32),
                        pltpu.VMEM((seq, W_B), F32),
                        pltpu.VMEM((G_B, _WG, N_B), F32)],
        compiler_params=_cparams(1),
    )(*args)


def _conf_kernel(a_ref, ap_ref, an_ref, g_ref, gp_ref, gn_ref, w_ref, b_ref, lg_ref, lb_ref, out_ref, pad_scr):
    i = pl.program_id(0)
    n_ctx_blk = N_CTX_TOK // CONV_ROWS
    blk_per_lat = DEC_SEQ // CONV_ROWS
    j = (i - n_ctx_blk) % blk_per_lat
    is_ctx = i < n_ctx_blk * (SEQ // CONV_ROWS)
    keep_prev = jnp.where(is_ctx | (j == 0), 0.0, 1.0)
    keep_next = jnp.where(is_ctx | (j == blk_per_lat - 1), 0.0, 1.0)
    pad_scr[0:HALO, :] = ap_ref[...] * _sigmoid(gp_ref[...]) * keep_prev
    pad_scr[HALO:HALO + CONV_ROWS, :] = a_ref[...] * _sigmoid(g_ref[...])
    pad_scr[HALO + CONV_ROWS:2 * HALO + CONV_ROWS, :] = an_ref[...] * _sigmoid(gn_ref[...]) * keep_next
    rc = 64
    for c in range(CONV_ROWS // rc):
        acc = jnp.broadcast_to(b_ref[...], (rc, W_C))
        for k in range(CONV_W):
            r = c * rc + HALO - CONV_W // 2 + k
            acc = acc + w_ref[k:k + 1, :] * pad_scr[r:r + rc, :]
        u = _ln(acc) * lg_ref[...] + lb_ref[...]
        out_ref[c * rc:(c + 1) * rc, :] = _silu(u).astype(out_ref.dtype)


def _conf(proj, dw_w, dw_b, ln_g, ln_b):
    assert SEQ == CONV_ROWS and DEC_SEQ % CONV_ROWS == 0
    ac = (4 * W_A + W_B + W_XBC) // W_C
    hb = CONV_ROWS // HALO
    n_halo = T_ALL // HALO
    const = lambda shape: pl.BlockSpec(shape, lambda i: (0,) * len(shape))

    def specs(c):
        return [pl.BlockSpec((CONV_ROWS, W_C), lambda i: (i, c)),
                pl.BlockSpec((HALO, W_C), lambda i: (jnp.maximum(i * hb - 1, 0), c)),
                pl.BlockSpec((HALO, W_C), lambda i: (jnp.minimum((i + 1) * hb, n_halo - 1), c))]

    return pl.pallas_call(
        _conf_kernel,
        grid=(T_ALL // CONV_ROWS,),
        in_specs=specs(ac) + specs(ac + 1) + [const((32, W_C)), const((1, W_C)), const((1, W_C)), const((1, W_C))],
        out_specs=pl.BlockSpec((CONV_ROWS, W_C), lambda i: (i, 0)),
        out_shape=jax.ShapeDtypeStruct((T_ALL, W_C), BF16),
        scratch_shapes=[pltpu.VMEM((CONV_ROWS + 2 * HALO, W_C), F32)],
        compiler_params=_cparams(1),
    )(proj, proj, proj, proj, proj, proj, dw_w, dw_b, ln_g, ln_b)


def _outproj_kernel(ha_ref, hb_ref, hc_ref, x_ref, mod_ref, wo_ref, pg_ref, pb_ref, wr_ref, br_ref,
                    x1_ref, u2_ref, rt_ref, *, tm):
    i = pl.program_id(0)
    r = _mod_row(i * tm)
    gate1 = mod_ref[pl.ds(r, 1), 2 * D_MODEL:3 * D_MODEL]
    shift2 = mod_ref[pl.ds(r, 1), 3 * D_MODEL:4 * D_MODEL]
    scale2 = mod_ref[pl.ds(r, 1), 4 * D_MODEL:5 * D_MODEL]
    mix = (_dot(ha_ref[...], wo_ref[0:W_A, :]) + _dot(hb_ref[...], wo_ref[W_A:W_A + W_B, :])
           + _dot(hc_ref[...], wo_ref[W_A + W_B:D_MIX, :]))
    x1 = _ln(ALPHA * x_ref[...] + gate1 * mix) * pg_ref[...] + pb_ref[...]
    x1_ref[...] = x1
    u2 = _ln(x1) * (1.0 + scale2) + shift2
    u2_ref[...] = u2

    logits = lax.dot_general(wr_ref[...], u2, _NT, precision=lax.Precision.HIGHEST,
                             preferred_element_type=F32) + br_ref[...]
    ex = jnp.exp(logits - jnp.max(logits, axis=0, keepdims=True))
    probs = ex / jnp.sum(ex, axis=0, keepdims=True)
    scores = []
    for g in range(N_EGROUPS):
        p = [probs[g * EXPERTS_PER_GROUP + e:g * EXPERTS_PER_GROUP + e + 1, :] for e in range(EXPERTS_PER_GROUP)]
        best = p[0] + p[1]
        for a in range(EXPERTS_PER_GROUP):
            for b in range(a + 1, EXPERTS_PER_GROUP):
                if (a, b) != (0, 1):
                    best = jnp.maximum(best, p[a] + p[b])
        scores.append(best)
    gmax = functools.reduce(jnp.maximum, scores)
    sel = jnp.full(gmax.shape, N_EGROUPS - 1, jnp.int32)
    for g in range(N_EGROUPS - 2, -1, -1):
        sel = jnp.where(scores[g] == gmax, g, sel)
    eidx = lax.broadcasted_iota(jnp.int32, probs.shape, 0)
    pm = jnp.where((eidx // EXPERTS_PER_GROUP) == sel, probs, -jnp.inf)
    p1 = jnp.max(pm, axis=0, keepdims=True)
    i1 = jnp.min(jnp.where(pm == p1, eidx, N_EXPERTS), axis=0, keepdims=True)
    pm2 = jnp.where(eidx == i1, -jnp.inf, pm)
    p2 = jnp.max(pm2, axis=0, keepdims=True)
    i2 = jnp.min(jnp.where(pm2 == p2, eidx, N_EXPERTS), axis=0, keepdims=True)
    den = p1 + p2
    row = lax.broadcasted_iota(jnp.int32, rt_ref.shape, 0)
    rt = jnp.where(row == 0, i1.astype(F32), 0.0)
    rt = jnp.where(row == 1, i2.astype(F32), rt)
    rt = jnp.where(row == 2, p1 / den, rt)
    rt = jnp.where(row == 3, p2 / den, rt)
    rt_ref[...] = rt


def _outproj(ha, hb, hc, x, mod, w_o, pg, pb, w_rt, b_r):
    tm = TM_PROJ
    const = lambda shape: pl.BlockSpec(shape, lambda i: (0,) * len(shape))
    rows = lambda w: pl.BlockSpec((tm, w), lambda i: (i, 0))
    return pl.pallas_call(
        functools.partial(_outproj_kernel, tm=tm),
        grid=(T_ALL // tm,),
        in_specs=[rows(W_A), rows(W_B), rows(W_C), rows(D_MODEL), const((N_MOD_ROWS, 6 * D_MODEL)),
                  const((D_MIX, D_MODEL)), const((1, D_MODEL)), const((1, D_MODEL)),
                  const((N_EXPERTS, D_MODEL)), const((N_EXPERTS, 1))],
        out_specs=[rows(D_MODEL), rows(D_MODEL), pl.BlockSpec((8, tm), lambda i: (0, i))],
        out_shape=[jax.ShapeDtypeStruct((T_ALL, D_MODEL), F32),
                   jax.ShapeDtypeStruct((T_ALL, D_MODEL), F32),
                   jax.ShapeDtypeStruct((8, T_ALL), F32)],
        compiler_params=_cparams(1),
    )(ha, hb, hc, x, mod, w_o, pg, pb, w_rt, b_r)


def _moe_kernel(tok_ref, dst_ref, texp_ref, nval_ref, nused_ref,
                u_hbm, wrow_ref, wg_ref, wu_ref, wd_ref, y_hbm, xbuf, ybuf, gsem, ssem):
    del texp_ref
    i = pl.program_id(0)

    def gather_copy(tok, r):
        return pltpu.make_async_copy(u_hbm.at[pl.ds(tok, 1), :], xbuf.at[pl.ds(r, 1), :], gsem)

    def scatter_copy(r, dst):
        return pltpu.make_async_copy(ybuf.at[pl.ds(r, 1), :], y_hbm.at[pl.ds(dst, 1), :], ssem)

    @pl.when(i < nused_ref[0])
    def _():
        base = i * TM_MOE

        def g_start(r, c):
            gather_copy(tok_ref[base + r], r).start()
            return c

        def g_wait(r, c):
            gather_copy(0, r).wait()
            return c

        lax.fori_loop(0, TM_MOE, g_start, 0)
        lax.fori_loop(0, TM_MOE, g_wait, 0)

        x = xbuf[...].astype(BF16)
        act = _silu(_dot(x, wg_ref[0])) * _dot(x, wu_ref[0]) * wrow_ref[...]
        ybuf[...] = _dot(act.astype(BF16), wd_ref[0])

        n_valid = nval_ref[i]

        def s_start(r, c):
            scatter_copy(r, dst_ref[base + r]).start()
            return c

        def s_wait(r, c):
            scatter_copy(r, 0).wait()
            return c

        lax.fori_loop(0, n_valid, s_start, 0)
        lax.fori_loop(0, n_valid, s_wait, 0)


def _moe(u2, row_tok, row_dst, tile_exp, n_valid, n_used, row_w, w_gate, w_up, w_down):
    wspec = lambda shape: pl.BlockSpec(shape, lambda i, tok, dst, texp, nval, nused: (texp[i], 0, 0))
    grid_spec = pltpu.PrefetchScalarGridSpec(
        num_scalar_prefetch=5,
        grid=(N_MOE_TILES,),
        in_specs=[pl.BlockSpec(memory_space=pl.ANY),
                  pl.BlockSpec((TM_MOE, 1), lambda i, *_: (i, 0)),
                  wspec((1, D_MODEL, D_FF_EXPERT)), wspec((1, D_MODEL, D_FF_EXPERT)),
                  wspec((1, D_FF_EXPERT, D_MODEL))],
        out_specs=pl.BlockSpec(memory_space=pl.ANY),
        scratch_shapes=[pltpu.VMEM((TM_MOE, D_MODEL), F32),
                        pltpu.VMEM((TM_MOE, D_MODEL), F32),
                        pltpu.SemaphoreType.DMA(()),
                        pltpu.SemaphoreType.DMA(())],
    )
    return pl.pallas_call(
        _moe_kernel,
        grid_spec=grid_spec,
        out_shape=jax.ShapeDtypeStruct((N_ASSIGN, D_MODEL), F32),
        compiler_params=pltpu.CompilerParams(dimension_semantics=("arbitrary",), vmem_limit_bytes=VMEM_LIMIT,
                                             has_side_effects=True),
    )(row_tok, row_dst, tile_exp, n_valid, n_used, u2, row_w, w_gate, w_up, w_down)


def _route_tables(route):
    e = route[0:TOP_K].astype(jnp.int32).reshape(N_ASSIGN)
    w = route[TOP_K:2 * TOP_K].reshape(N_ASSIGN)
    onehot = (e[:, None] == jnp.arange(N_EXPERTS, dtype=jnp.int32)[None, :]).astype(jnp.int32)
    rank = jnp.sum((jnp.cumsum(onehot, axis=0) - onehot) * onehot, axis=1)
    counts = jnp.sum(onehot, axis=0)
    padded = (counts + TM_MOE - 1) // TM_MOE * TM_MOE
    ends = jnp.cumsum(padded)
    offs = ends - padded
    pos = offs[e] + rank
    a = jnp.arange(N_ASSIGN, dtype=jnp.int32)
    row_tok = jnp.zeros((N_MOE_ROWS,), jnp.int32).at[pos].set(a % T_ALL)
    row_dst = jnp.zeros((N_MOE_ROWS,), jnp.int32).at[pos].set(a)
    row_w = jnp.zeros((N_MOE_ROWS,), F32).at[pos].set(w).reshape(N_MOE_ROWS, 1)
    tile_start = jnp.arange(N_MOE_TILES, dtype=jnp.int32) * TM_MOE
    n_used = (ends[-1] // TM_MOE).astype(jnp.int32)
    tile_exp = jnp.sum((ends[None, :] <= tile_start[:, None]).astype(jnp.int32), axis=1)
    last_exp = jnp.sum((ends <= (n_used - 1) * TM_MOE).astype(jnp.int32))
    tile_exp = jnp.where(tile_start < ends[-1], tile_exp, last_exp)
    tile_exp = jnp.minimum(tile_exp, N_EXPERTS - 1).astype(jnp.int32)
    n_valid = jnp.clip(counts[tile_exp] - (tile_start - offs[tile_exp]), 0, TM_MOE)
    n_valid = jnp.where(tile_start < ends[-1], n_valid, 0).astype(jnp.int32)
    return row_tok, row_dst, tile_exp, n_valid, n_used.reshape(1), row_w


def _final_kernel(x1_ref, y0_ref, y1_ref, mod_ref, pg_ref, pb_ref, o_ref, *, tm):
    r = _mod_row(pl.program_id(0) * tm)
    gate2 = mod_ref[pl.ds(r, 1), 5 * D_MODEL:6 * D_MODEL]
    y = y0_ref[...] + y1_ref[...]
    o_ref[...] = _ln(ALPHA * x1_ref[...] + gate2 * y) * pg_ref[...] + pb_ref[...]


def _final(x1, y, mod, pg, pb):
    tm = TM_PROJ
    nblk = T_ALL // tm
    const = lambda shape: pl.BlockSpec(shape, lambda i: (0,) * len(shape))
    return pl.pallas_call(
        functools.partial(_final_kernel, tm=tm),
        grid=(nblk,),
        in_specs=[pl.BlockSpec((tm, D_MODEL), lambda i: (i, 0)),
                  pl.BlockSpec((tm, D_MODEL), lambda i: (i, 0)),
                  pl.BlockSpec((tm, D_MODEL), lambda i: (i + nblk, 0)),
                  const((N_MOD_ROWS, 6 * D_MODEL)), const((1, D_MODEL)), const((1, D_MODEL))],
        out_specs=pl.BlockSpec((tm, D_MODEL), lambda i: (i, 0)),
        out_shape=jax.ShapeDtypeStruct((T_ALL, D_MODEL), F32),
        compiler_params=_cparams(1),
    )(x1, y, y, mod, pg, pb)


def _grid_pos(n_tok):
    rows = n_tok // GRID_W
    r, col = jnp.meshgrid(jnp.arange(rows, dtype=F32), jnp.arange(GRID_W, dtype=F32), indexing='ij')
    quarter = D_MODEL // 4
    omega = 1.0 / (10000.0 ** (jnp.arange(quarter, dtype=F32) / quarter))

    def emb(p):
        ang = p.reshape(-1)[:, None] * omega[None, :]
        return jnp.concatenate([jnp.sin(ang), jnp.cos(ang)], axis=-1)

    return jnp.concatenate([emb(r), emb(col)], axis=-1)


def _pad_lanes(v, start):
    v = v.reshape(1, -1).astype(F32)
    return jnp.pad(v, ((0, 0), (start, LANES - start - v.shape[1])))


def kernel(x_prompt, x_sample, state_mlstm_C, state_mlstm_n, state_mlstm_m, state_ssd, c, c_ctx, w_in, w_o, mlstm_b_i, mlstm_b_f, mlstm_norm_g, ssd_conv_w, ssd_conv_b, ssd_dt_bias, ssd_A_log, ssd_D, ssd_norm_g, conv_dw_w, conv_dw_b, conv_ln_g, conv_ln_b, w_ada, b_ada, post1_g, post1_b, post2_g, post2_b, w_router, b_router, w_e_gate, w_e_up, w_e_down):
    cvec = jnp.concatenate([c_ctx[None, :], c, jnp.zeros((N_MOD_ROWS - 1 - DEC_BATCH, D_MODEL), F32)], axis=0)
    mod_all = _ada(cvec, w_ada, b_ada)
    x = _embed(x_prompt.reshape(N_CTX_TOK, D_MODEL), x_sample.reshape(N_LAT_TOK, D_MODEL), _grid_pos(DEC_SEQ))
    w_rt = w_router.T
    b_r = b_router.reshape(N_EXPERTS, 1)

    a_end = 4 * W_A + N_DIR * 2 * H_A
    b_end = a_end + W_B + W_XBC + N_DIR * H_B
    new_c, new_n, new_m, new_h = [], [], [], []
    for l in range(DEPTH):
        w = w_in[l]
        w_main = jnp.concatenate([w[:, 0:4 * W_A], w[:, a_end:a_end + W_B + W_XBC], w[:, b_end:]], axis=1).astype(BF16)
        w_small = jnp.concatenate([w[:, 4 * W_A:a_end], w[:, b_end - N_DIR * H_B:b_end],
                                   jnp.zeros((D_MODEL, LANES - _DT_COL0 - N_DIR * H_B), F32)], axis=1).astype(BF16)
        gate_bias = (_pad_lanes(jnp.stack([mlstm_b_i[l], mlstm_b_f[l]], axis=1), 0)
                     + _pad_lanes(ssd_dt_bias[l], _DT_COL0))
        alog_row = _pad_lanes(ssd_A_log[l], _DT_COL0)
        mod = mod_all[l]

        proj, gates = _inproj(x, mod, w_main, w_small)

        m_norm = mlstm_norm_g[l].reshape(1, W_A)
        ha_c, c_new, n_new, m_new = _mlstm(proj, gates, gate_bias, m_norm, SEQ, BATCH, 0)
        init = (state_mlstm_C[:, l],
                state_mlstm_n[:, l].reshape(DEC_BATCH, N_DIR * H_A, DH_A),
                jnp.broadcast_to(state_mlstm_m[:, l].reshape(DEC_BATCH, N_DIR * H_A, 1),
                                 (DEC_BATCH, N_DIR * H_A, LANES)))
        (ha_l,) = _mlstm(proj, gates, gate_bias, m_norm, DEC_SEQ, DEC_BATCH, N_CTX_TOK, init)

        cw = jnp.pad(ssd_conv_w[l], ((0, 8 - SSM_CONV), (0, 0)))
        cb = ssd_conv_b[l].reshape(1, W_XBC)
        d_row = jnp.repeat(ssd_D[l], P_B).reshape(1, W_B)
        s_norm = ssd_norm_g[l].reshape(1, W_B)
        hb_c, h_new = _ssd(proj, gates, gate_bias, alog_row, cw, cb, d_row, s_norm, SEQ, BATCH, 0)
        (hb_l,) = _ssd(proj, gates, gate_bias, alog_row, cw, cb, d_row, s_norm, DEC_SEQ, DEC_BATCH, N_CTX_TOK,
                       state_ssd[:, l])

        hc = _conf(proj, jnp.pad(conv_dw_w[l], ((0, 32 - CONV_W), (0, 0))), conv_dw_b[l].reshape(1, W_C),
                   conv_ln_g[l].reshape(1, W_C), conv_ln_b[l].reshape(1, W_C))

        ha = jnp.concatenate([ha_c, ha_l], axis=0)
        hb = jnp.concatenate([hb_c, hb_l], axis=0)
        x1, u2, route = _outproj(ha, hb, hc, x, mod, w_o[l].astype(BF16), post1_g[l].reshape(1, D_MODEL),
                                 post1_b[l].reshape(1, D_MODEL), w_rt, b_r)

        row_tok, row_dst, tile_exp, n_valid, n_used, row_w = _route_tables(route)
        y = _moe(u2, row_tok, row_dst, tile_exp, n_valid, n_used, row_w,
                 w_e_gate[l].astype(BF16), w_e_up[l].astype(BF16), w_e_down[l].astype(BF16))
        x = _final(x1, y, mod, post2_g[l].reshape(1, D_MODEL), post2_b[l].reshape(1, D_MODEL))

        new_c.append(c_new)
        new_n.append(n_new.reshape(BATCH, N_DIR, H_A, DH_A))
        new_m.append(m_new[:, :, 0].reshape(BATCH, N_DIR, H_A))
        new_h.append(h_new)

    y_prompt = x[:N_CTX_TOK].reshape(BATCH, SEQ, D_MODEL)
    y_sample = x[N_CTX_TOK:].reshape(DEC_BATCH, DEC_SEQ, D_MODEL)
    return (y_prompt, y_sample, jnp.stack(new_c, axis=1), jnp.stack(new_n, axis=1),
            jnp.stack(new_m, axis=1), jnp.stack(new_h, axis=1))
```

```python
import functools

import jax
import jax.numpy as jnp
from jax import lax
from jax.experimental import pallas as pl
from jax.experimental.pallas import tpu as pltpu

D_MODEL = 1024
BATCH = 32
SEQ = 256
DEPTH = 2
DEC_BATCH = 2
DEC_SEQ = 1024
GRID_W = 64
N_DIR = 2
CHUNK = 128
H_A = 4
DH_A = 128
W_A = H_A * DH_A
H_B = 8
P_B = 64
W_B = H_B * P_B
G_B = 2
N_B = 128
W_XBC = W_B + 2 * G_B * N_B
SSM_CONV = 3
W_C = 512
CONV_W = 31
D_MIX = W_A + W_B + W_C
N_EXPERTS = 16
N_EGROUPS = 4
EXPERTS_PER_GROUP = N_EXPERTS // N_EGROUPS
D_FF_EXPERT = 512
ALPHA = (2 * DEPTH) ** 0.25
EPS = 1e-5
F32 = jnp.float32
BF16 = jnp.bfloat16

N_CTX_TOK = BATCH * SEQ
N_LAT_TOK = DEC_BATCH * DEC_SEQ
T_ALL = N_CTX_TOK + N_LAT_TOK
N_MOD_ROWS = 8
D_MAIN = 4 * W_A + W_B + W_XBC + 2 * W_C
LANES = 128
HALO = 16
CONV_ROWS = 256
TM_PROJ = 512
TM_MOE = 256
VMEM_LIMIT = 48 * 1024 * 1024
MOE_VMEM_LIMIT = 56 * 1024 * 1024

_NT = (((1,), (1,)), ((), ()))
_TN = (((0,), (0,)), ((), ()))


def _ln(x):
    mu = jnp.mean(x, axis=-1, keepdims=True)
    xc = x - mu
    var = jnp.mean(xc * xc, axis=-1, keepdims=True)
    return xc * lax.rsqrt(var + EPS)


def _sigmoid(x):
    return 1.0 / (1.0 + jnp.exp(-x))


def _silu(x):
    return x * _sigmoid(x)


def _softplus(x):
    return jnp.maximum(x, 0.0) + jnp.log1p(jnp.exp(-jnp.abs(x)))


def _dot(a, b):
    return jnp.dot(a, b, preferred_element_type=F32)


def _dotg(a, b, dims):
    return lax.dot_general(a, b, dims, preferred_element_type=F32)


def _tri_cumsum(tri, x):
    hi = x.astype(BF16)
    r1 = x - hi.astype(F32)
    mid = r1.astype(BF16)
    lo = (r1 - mid.astype(F32)).astype(BF16)
    return _dot(tri, hi) + _dot(tri, mid) + _dot(tri, lo)


def _tri_mask(d):
    row = lax.broadcasted_iota(jnp.int32, (CHUNK, CHUNK), 0)
    col = lax.broadcasted_iota(jnp.int32, (CHUNK, CHUNK), 1)
    return (row >= col) if d == 0 else (row <= col)


def _mod_row(row_start):
    return jnp.where(row_start < N_CTX_TOK, 0, 1 + (row_start - N_CTX_TOK) // DEC_SEQ)


def _cparams(n_axes):
    return pltpu.CompilerParams(dimension_semantics=("arbitrary",) * n_axes, vmem_limit_bytes=VMEM_LIMIT)


def _ada_kernel(c_ref, w_ref, b_ref, o_ref):
    o_ref[0] = _dot(_silu(c_ref[...]), w_ref[0]) + b_ref[0]


def _ada(cvec, w_ada, b_ada):
    tn = 1536
    return pl.pallas_call(
        _ada_kernel,
        grid=(DEPTH, 6 * D_MODEL // tn),
        in_specs=[
            pl.BlockSpec((N_MOD_ROWS, D_MODEL), lambda l, j: (0, 0)),
            pl.BlockSpec((1, D_MODEL, tn), lambda l, j: (l, 0, j)),
            pl.BlockSpec((1, 1, tn), lambda l, j: (l, 0, j)),
        ],
        out_specs=pl.BlockSpec((1, N_MOD_ROWS, tn), lambda l, j: (l, 0, j)),
        out_shape=jax.ShapeDtypeStruct((DEPTH, N_MOD_ROWS, 6 * D_MODEL), F32),
        compiler_params=_cparams(2),
        name="ada",
    )(cvec, w_ada, b_ada.reshape(DEPTH, 1, 6 * D_MODEL))


def _embed_kernel(xp_ref, xs_ref, pos_ref, o_ref):
    i = pl.program_id(0)

    @pl.when(i < N_CTX_TOK // DEC_SEQ)
    def _():
        o_ref[...] = xp_ref[...]

    @pl.when(i >= N_CTX_TOK // DEC_SEQ)
    def _():
        o_ref[...] = xs_ref[...] + pos_ref[...]


def _embed(xp, xs, pos):
    n_ctx_blk = N_CTX_TOK // DEC_SEQ
    return pl.pallas_call(
        _embed_kernel,
        grid=(T_ALL // DEC_SEQ,),
        in_specs=[
            pl.BlockSpec((DEC_SEQ, D_MODEL), lambda i: (jnp.minimum(i, n_ctx_blk - 1), 0)),
            pl.BlockSpec((DEC_SEQ, D_MODEL), lambda i: (jnp.maximum(i - n_ctx_blk, 0), 0)),
            pl.BlockSpec((DEC_SEQ, D_MODEL), lambda i: (0, 0)),
        ],
        out_specs=pl.BlockSpec((DEC_SEQ, D_MODEL), lambda i: (i, 0)),
        out_shape=jax.ShapeDtypeStruct((T_ALL, D_MODEL), F32),
        compiler_params=_cparams(1),
        name="embed",
    )(xp, xs, pos)


def _inproj_kernel(x_ref, mod_ref, wm_ref, wg_ref, om_ref, og_ref, u_scr, *, tm):
    i = pl.program_id(0)
    j = pl.program_id(1)

    @pl.when(j == 0)
    def _():
        r = _mod_row(i * tm)
        shift = mod_ref[pl.ds(r, 1), 0:D_MODEL]
        scale = mod_ref[pl.ds(r, 1), D_MODEL:2 * D_MODEL]
        u = (_ln(x_ref[...]) * (1.0 + scale) + shift).astype(BF16)
        u_scr[...] = u
        og_ref[...] = _dot(u, wg_ref[...])

    om_ref[...] = _dot(u_scr[...], wm_ref[...])


def _inproj(x, mod, w_main, w_small):
    tm, tn = TM_PROJ, D_MAIN // 2
    return pl.pallas_call(
        functools.partial(_inproj_kernel, tm=tm),
        grid=(T_ALL // tm, D_MAIN // tn),
        in_specs=[
            pl.BlockSpec((tm, D_MODEL), lambda i, j: (i, 0)),
            pl.BlockSpec((N_MOD_ROWS, 6 * D_MODEL), lambda i, j: (0, 0)),
            pl.BlockSpec((D_MODEL, tn), lambda i, j: (0, j)),
            pl.BlockSpec((D_MODEL, LANES), lambda i, j: (0, 0)),
        ],
        out_specs=[
            pl.BlockSpec((tm, tn), lambda i, j: (i, j)),
            pl.BlockSpec((tm, LANES), lambda i, j: (i, 0)),
        ],
        out_shape=[
            jax.ShapeDtypeStruct((T_ALL, D_MAIN), F32),
            jax.ShapeDtypeStruct((T_ALL, LANES), F32),
        ],
        scratch_shapes=[pltpu.VMEM((tm, D_MODEL), BF16)],
        compiler_params=_cparams(2),
        name="inproj",
    )(x, mod, w_main, w_small)


def _mlstm_kernel(*refs, seq, has_init, emit_state, layer):
    it = iter(refs)
    q_ref, k_ref, v_ref, o_ref, g_ref, gb_ref, ng_ref = (next(it) for _ in range(7))
    if has_init:
        c0_ref, n0_ref, m0_ref = (next(it) for _ in range(3))
    if emit_state and layer > 0:
        prev_refs = [next(it) for _ in range(3)]
    out_ref = next(it)
    if emit_state:
        co_ref, no_ref, mo_ref = (next(it) for _ in range(3))
    tg_scr, hf_scr, c_scr, n_scr, m_scr = (next(it) for _ in range(5))
    n_chunks = seq // CHUNK
    if emit_state and layer > 0:
        for prev_ref, st_ref in zip(prev_refs, (co_ref, no_ref, mo_ref)):
            st_ref[0, 0:layer] = prev_ref[0]

    y = g_ref[...] + gb_ref[...]
    lane = lax.broadcasted_iota(jnp.int32, y.shape, 1)
    is_forget = (lane < N_DIR * 2 * H_A) & ((lane & H_A) != 0)
    tg_scr[...] = jnp.where(is_forget, -_softplus(-y), y)

    def run_direction(d):
        for h in range(H_A):
            row = d * H_A + h
            if has_init:
                c_scr[h] = c0_ref[0, 0, d, h]
                n_scr[h:h + 1, :] = n0_ref[0, 0, row:row + 1, :]
                m_scr[h:h + 1, :] = m0_ref[0, 0, row:row + 1, :]
            else:
                c_scr[h] = jnp.zeros((DH_A, DH_A), F32)
                n_scr[h:h + 1, :] = jnp.zeros((1, DH_A), F32)
                m_scr[h:h + 1, :] = jnp.zeros((1, LANES), F32)

        mask = _tri_mask(d)
        tri = mask.astype(BF16)

        def body(ci, carry):
            c = ci if d == 0 else n_chunks - 1 - ci
            r0 = pl.multiple_of(c * CHUNK, CHUNK)
            rows = pl.ds(r0, CHUNK)
            tc = tg_scr[rows, :]
            cum = _tri_cumsum(tri, tc)
            cum_t = cum.T
            tc_t = tc.T
            for h in range(H_A):
                ci_col = d * 2 * H_A + h
                cf_col = ci_col + H_A
                cols = slice(h * DH_A, (h + 1) * DH_A)
                li_c = tc[:, ci_col:ci_col + 1]
                li_r = tc_t[ci_col:ci_col + 1, :]
                b_c = cum[:, cf_col:cf_col + 1]
                b_r = cum_t[cf_col:cf_col + 1, :]
                g = b_r[:, CHUNK - 1:CHUNK] if d == 0 else b_r[:, 0:1]
                qs = q_ref[rows, cols] * (DH_A ** -0.5)
                kf = k_ref[rows, cols]
                qb = qs.astype(BF16)
                kb = kf.astype(BF16)
                vb = v_ref[rows, cols].astype(BF16)
                c_st = c_scr[h]
                n_st = n_scr[h:h + 1, :]
                m_st = m_scr[h:h + 1, 0:1]

                dm = jnp.where(mask, b_c - b_r + li_r, -jnp.inf)
                inter = b_c + m_st
                m_t = jnp.maximum(inter, jnp.max(dm, axis=-1, keepdims=True))
                s = _dotg(qb, kb, _NT) * jnp.exp(dm - m_t)
                w_int = jnp.exp(inter - m_t)
                num = w_int * _dot(qb, c_st.astype(BF16)) + _dot(s.astype(BF16), vb)
                den = w_int * jnp.sum(qs * n_st, axis=-1, keepdims=True) + jnp.sum(s, axis=-1, keepdims=True)
                hh = num / jnp.maximum(jnp.abs(den), jnp.exp(-m_t))

                a_c = g - b_c + li_c
                a_r = g - b_r + li_r
                m_new = jnp.maximum(g + m_st, jnp.max(a_r, axis=-1, keepdims=True))
                w_prev = jnp.exp(g + m_st - m_new)
                kw = jnp.exp(a_c - m_new) * kf
                c_scr[h] = w_prev * c_st + _dotg(kw.astype(BF16), vb, _TN)
                n_scr[h:h + 1, :] = w_prev * n_st + jnp.sum(kw, axis=0, keepdims=True)
                m_scr[h:h + 1, :] = jnp.broadcast_to(m_new, (1, LANES))

                if d == 0:
                    hf_scr[rows, cols] = hh
                else:
                    hn = _ln(hf_scr[rows, cols] + hh) * ng_ref[:, cols]
                    out_ref[rows, cols] = (hn * _sigmoid(o_ref[rows, cols])).astype(out_ref.dtype)
            return carry

        lax.fori_loop(0, n_chunks, body, 0)

        if emit_state:
            for h in range(H_A):
                row = d * H_A + h
                co_ref[0, layer, d, h] = c_scr[h]
                no_ref[0, layer, row:row + 1, :] = n_scr[h:h + 1, :]
                mo_ref[0, layer, row:row + 1, :] = m_scr[h:h + 1, :]

    run_direction(0)
    run_direction(1)


def _mlstm(proj, gates, gate_bias, norm_g, seq, n_seq, row_off, layer, init=None, prev=None):
    blk0 = row_off // seq
    has_init = init is not None
    emit_state = not has_init
    col = lambda c: pl.BlockSpec((seq, W_A), lambda s: (s + blk0, c))
    st_shapes = [(N_DIR, H_A, DH_A, DH_A), (N_DIR * H_A, DH_A), (N_DIR * H_A, LANES)]

    def st_specs(n_layers, first):
        return [pl.BlockSpec((1, n_layers) + shp, lambda s, nd=len(shp): (s, first) + (0,) * nd) for shp in st_shapes]

    in_specs = [col(0), col(1), col(2), col(3),
                pl.BlockSpec((seq, LANES), lambda s: (s + blk0, 0)),
                pl.BlockSpec((1, LANES), lambda s: (0, 0)),
                pl.BlockSpec((1, W_A), lambda s: (0, 0))]
    args = [proj, proj, proj, proj, gates, gate_bias, norm_g]
    if has_init:
        in_specs += st_specs(1, layer)
        args += list(init)
    if emit_state and layer > 0:
        in_specs += st_specs(layer, 0)
        args += list(prev)
    out_specs = [pl.BlockSpec((seq, W_A), lambda s: (s, 0))]
    out_shape = [jax.ShapeDtypeStruct((n_seq * seq, W_A), BF16)]
    if emit_state:
        out_specs += st_specs(layer + 1, 0)
        out_shape += [jax.ShapeDtypeStruct((n_seq, layer + 1) + shp, F32) for shp in st_shapes]
    return pl.pallas_call(
        functools.partial(_mlstm_kernel, seq=seq, has_init=has_init, emit_state=emit_state, layer=layer),
        grid=(n_seq,),
        in_specs=in_specs,
        out_specs=out_specs,
        out_shape=out_shape,
        scratch_shapes=[pltpu.VMEM((seq, LANES), F32),
                        pltpu.VMEM((seq, W_A), F32),
                        pltpu.VMEM((H_A, DH_A, DH_A), F32),
                        pltpu.VMEM((8, DH_A), F32),
                        pltpu.VMEM((8, LANES), F32)],
        compiler_params=_cparams(1),
        name="mlstm_lat" if has_init else "mlstm_ctx",
    )(*args)


_DT_COL0 = N_DIR * 2 * H_A
_HG = H_B // G_B
_WG = _HG * P_B


def _ssd_kernel(*refs, seq, has_init, emit_state, layer):
    it = iter(refs)
    (z_ref, x_ref, b_ref, c_ref, g_ref, gb_ref, alog_ref, cw_ref, cb_ref, dsk_ref, ng_ref) = (
        next(it) for _ in range(11))
    if has_init:
        h0_ref = next(it)
    if emit_state and layer > 0:
        prev_ref = next(it)
    out_ref = next(it)
    if emit_state:
        ho_ref = next(it)
    pad_scr, xbc_scr, dt_scr, yf_scr, h_scr = (next(it) for _ in range(5))
    n_chunks = seq // CHUNK
    pad = 8
    if emit_state and layer > 0:
        ho_ref[0, 0:layer] = prev_ref[0]

    pad_scr[0:pad, :] = jnp.zeros((pad, W_XBC), F32)
    pad_scr[pad + seq:2 * pad + seq, :] = jnp.zeros((pad, W_XBC), F32)
    pad_scr[pad:pad + seq, 0:W_B] = x_ref[...]
    pad_scr[pad:pad + seq, W_B:W_B + G_B * N_B] = b_ref[...]
    pad_scr[pad:pad + seq, W_B + G_B * N_B:W_XBC] = c_ref[...]
    for c in range(n_chunks):
        for lb in range(W_XBC // 256):
            cols = slice(lb * 256, (lb + 1) * 256)
            acc = cb_ref[:, cols]
            for k in range(SSM_CONV):
                r = c * CHUNK + pad - SSM_CONV // 2 + k
                acc = acc + cw_ref[k:k + 1, cols] * pad_scr[r:r + CHUNK, cols]
            xbc_scr[c * CHUNK:(c + 1) * CHUNK, cols] = _silu(acc)

    dt_scr[...] = _softplus(g_ref[...] + gb_ref[...])
    a_row = -jnp.exp(alog_ref[...])

    lane_blk = lax.broadcasted_iota(jnp.int32, (CHUNK, _WG), 1) // P_B
    row_blk = lax.broadcasted_iota(jnp.int32, (_WG, 1), 0) // P_B

    def run_direction(d):
        for g in range(G_B):
            for hh in range(_HG):
                head = g * _HG + hh
                if has_init:
                    h_scr[g, hh * P_B:(hh + 1) * P_B, :] = h0_ref[0, 0, d, head]
                else:
                    h_scr[g, hh * P_B:(hh + 1) * P_B, :] = jnp.zeros((P_B, N_B), F32)

        mask = _tri_mask(d)
        tri = mask.astype(BF16)

        def body(ci, carry):
            c = ci if d == 0 else n_chunks - 1 - ci
            r0 = pl.multiple_of(c * CHUNK, CHUNK)
            rows = pl.ds(r0, CHUNK)
            dtc = dt_scr[rows, :]
            cum = _tri_cumsum(tri, dtc * a_row)
            cum_t = cum.T
            dt_t = dtc.T
            for g in range(G_B):
                xg = xbc_scr[rows, g * _WG:(g + 1) * _WG]
                xgb = xg.astype(BF16)
                bg = xbc_scr[rows, W_B + g * N_B:W_B + (g + 1) * N_B].astype(BF16)
                cg = xbc_scr[rows, W_B + G_B * N_B + g * N_B:W_B + G_B * N_B + (g + 1) * N_B].astype(BF16)
                h_st = h_scr[g]
                cb = _dotg(cg, bg, _NT)
                ch = _dotg(cg, h_st.astype(BF16), _NT)
                y = jnp.zeros((CHUNK, _WG), F32)
                e_full = jnp.zeros((CHUNK, _WG), F32)
                w_full = jnp.zeros((CHUNK, _WG), F32)
                decay = jnp.zeros((_WG, 1), F32)
                for hh in range(_HG):
                    colh = _DT_COL0 + d * H_B + g * _HG + hh
                    b_c = cum[:, colh:colh + 1]
                    b_r = cum_t[colh:colh + 1, :]
                    dt_c = dtc[:, colh:colh + 1]
                    dt_r = dt_t[colh:colh + 1, :]
                    gt = b_r[:, CHUNK - 1:CHUNK] if d == 0 else b_r[:, 0:1]
                    seg = jnp.where(mask, b_c - b_r, -jnp.inf)
                    sm = cb * jnp.exp(seg) * dt_r
                    sel = lane_blk == hh
                    y = jnp.where(sel, _dot(sm.astype(BF16), xgb), y)
                    e_full = jnp.where(sel, jnp.exp(b_c), e_full)
                    w_full = jnp.where(sel, jnp.exp(gt - b_c) * dt_c, w_full)
                    decay = jnp.where(row_blk == hh, jnp.exp(gt), decay)
                y = y + e_full * ch
                h_scr[g] = decay * h_st + _dotg((w_full * xg).astype(BF16), bg, _TN)
                gcols = slice(g * _WG, (g + 1) * _WG)
                if d == 0:
                    yf_scr[rows, gcols] = y
                else:
                    yf_scr[rows, gcols] = yf_scr[rows, gcols] + y + dsk_ref[:, gcols] * xg
            if d == 1:
                yz = yf_scr[rows, :] * _silu(z_ref[rows, :])
                rms = lax.rsqrt(jnp.mean(yz * yz, axis=-1, keepdims=True) + EPS)
                out_ref[rows, :] = (yz * rms * ng_ref[...]).astype(out_ref.dtype)
            return carry

        lax.fori_loop(0, n_chunks, body, 0)

        if emit_state:
            for g in range(G_B):
                for hh in range(_HG):
                    ho_ref[0, layer, d, g * _HG + hh] = h_scr[g, hh * P_B:(hh + 1) * P_B, :]

    run_direction(0)
    run_direction(1)


def _ssd(proj, gates, gate_bias, alog_row, conv_w, conv_b, d_row, norm_g, seq, n_seq, row_off, layer,
         init=None, prev=None):
    blk0 = row_off // seq
    has_init = init is not None
    emit_state = not has_init
    zc = W_A * 4 // W_B
    bc = (4 * W_A + 2 * W_B) // (G_B * N_B)
    const = lambda shape: pl.BlockSpec(shape, lambda s: (0,) * len(shape))
    st_spec = lambda n_layers, first: pl.BlockSpec((1, n_layers, N_DIR, H_B, P_B, N_B),
                                                   lambda s: (s, first, 0, 0, 0, 0))
    in_specs = [pl.BlockSpec((seq, W_B), lambda s: (s + blk0, zc)),
                pl.BlockSpec((seq, W_B), lambda s: (s + blk0, zc + 1)),
                pl.BlockSpec((seq, G_B * N_B), lambda s: (s + blk0, bc)),
                pl.BlockSpec((seq, G_B * N_B), lambda s: (s + blk0, bc + 1)),
                pl.BlockSpec((seq, LANES), lambda s: (s + blk0, 0)),
                const((1, LANES)), const((1, LANES)), const((8, W_XBC)), const((1, W_XBC)),
                const((1, W_B)), const((1, W_B))]
    args = [proj, proj, proj, proj, gates, gate_bias, alog_row, conv_w, conv_b, d_row, norm_g]
    if has_init:
        in_specs.append(st_spec(1, layer))
        args.append(init)
    if emit_state and layer > 0:
        in_specs.append(st_spec(layer, 0))
        args.append(prev)
    out_specs = [pl.BlockSpec((seq, W_B), lambda s: (s, 0))]
    out_shape = [jax.ShapeDtypeStruct((n_seq * seq, W_B), BF16)]
    if emit_state:
        out_specs.append(st_spec(layer + 1, 0))
        out_shape.append(jax.ShapeDtypeStruct((n_seq, layer + 1, N_DIR, H_B, P_B, N_B), F32))
    return pl.pallas_call(
        functools.partial(_ssd_kernel, seq=seq, has_init=has_init, emit_state=emit_state, layer=layer),
        grid=(n_seq,),
        in_specs=in_specs,
        out_specs=out_specs,
        out_shape=out_shape,
        scratch_shapes=[pltpu.VMEM((seq + 16, W_XBC), F32),
                        pltpu.VMEM((seq, W_XBC), F32),
                        pltpu.VMEM((seq, LANES), F32),
                        pltpu.VMEM((seq, W_B), F32),
                        pltpu.VMEM((G_B, _WG, N_B), F32)],
        compiler_params=_cparams(1),
        name="ssd_lat" if has_init else "ssd_ctx",
    )(*args)


def _conf_kernel(a_ref, ap_ref, an_ref, g_ref, gp_ref, gn_ref, w_ref, b_ref, lg_ref, lb_ref, out_ref, pad_scr):
    i = pl.program_id(0)
    n_ctx_blk = N_CTX_TOK // CONV_ROWS
    blk_per_lat = DEC_SEQ // CONV_ROWS
    j = (i - n_ctx_blk) % blk_per_lat
    is_ctx = i < n_ctx_blk
    keep_prev = jnp.where(is_ctx | (j == 0), 0.0, 1.0)
    keep_next = jnp.where(is_ctx | (j == blk_per_lat - 1), 0.0, 1.0)
    pad_scr[0:HALO, :] = ap_ref[...] * _sigmoid(gp_ref[...]) * keep_prev
    pad_scr[HALO:HALO + CONV_ROWS, :] = a_ref[...] * _sigmoid(g_ref[...])
    pad_scr[HALO + CONV_ROWS:2 * HALO + CONV_ROWS, :] = an_ref[...] * _sigmoid(gn_ref[...]) * keep_next
    rc = 64
    for c in range(CONV_ROWS // rc):
        acc = jnp.broadcast_to(b_ref[...], (rc, W_C))
        for k in range(CONV_W):
            r = c * rc + HALO - CONV_W // 2 + k
            acc = acc + w_ref[k:k + 1, :] * pad_scr[r:r + rc, :]
        u = _ln(acc) * lg_ref[...] + lb_ref[...]
        out_ref[c * rc:(c + 1) * rc, :] = _silu(u).astype(out_ref.dtype)


def _conf(proj, dw_w, dw_b, ln_g, ln_b):
    assert SEQ == CONV_ROWS and DEC_SEQ % CONV_ROWS == 0
    ac = (4 * W_A + W_B + W_XBC) // W_C
    hb = CONV_ROWS // HALO
    n_halo = T_ALL // HALO
    const = lambda shape: pl.BlockSpec(shape, lambda i: (0,) * len(shape))

    def specs(c):
        return [pl.BlockSpec((CONV_ROWS, W_C), lambda i: (i, c)),
                pl.BlockSpec((HALO, W_C), lambda i: (jnp.maximum(i * hb - 1, 0), c)),
                pl.BlockSpec((HALO, W_C), lambda i: (jnp.minimum((i + 1) * hb, n_halo - 1), c))]

    return pl.pallas_call(
        _conf_kernel,
        grid=(T_ALL // CONV_ROWS,),
        in_specs=specs(ac) + specs(ac + 1) + [const((32, W_C)), const((1, W_C)), const((1, W_C)), const((1, W_C))],
        out_specs=pl.BlockSpec((CONV_ROWS, W_C), lambda i: (i, 0)),
        out_shape=jax.ShapeDtypeStruct((T_ALL, W_C), BF16),
        scratch_shapes=[pltpu.VMEM((CONV_ROWS + 2 * HALO, W_C), F32)],
        compiler_params=_cparams(1),
        name="conf",
    )(proj, proj, proj, proj, proj, proj, dw_w, dw_b, ln_g, ln_b)


_PAIRS = [(a, b) for a in range(EXPERTS_PER_GROUP) for b in range(a + 1, EXPERTS_PER_GROUP)]
N_CLASSES = N_EGROUPS * len(_PAIRS)
D_U2X = D_MODEL + LANES
N_MOE_ROWS = T_ALL + N_CLASSES * TM_MOE
N_MOE_TILES = N_MOE_ROWS // TM_MOE


def _outproj_kernel(hac_ref, hal_ref, hbc_ref, hbl_ref, hc_ref, x_ref, mod_ref, wo_ref, pg_ref, pb_ref,
                    wr_ref, br_ref, x1_ref, u2_ref, rt_ref, *, tm):
    i = pl.program_id(0)
    r = _mod_row(i * tm)
    gate1 = mod_ref[pl.ds(r, 1), 2 * D_MODEL:3 * D_MODEL]
    shift2 = mod_ref[pl.ds(r, 1), 3 * D_MODEL:4 * D_MODEL]
    scale2 = mod_ref[pl.ds(r, 1), 4 * D_MODEL:5 * D_MODEL]
    is_ctx = i < N_CTX_TOK // tm
    ha = jnp.where(is_ctx, hac_ref[...], hal_ref[...])
    hb = jnp.where(is_ctx, hbc_ref[...], hbl_ref[...])
    mix = (_dot(ha, wo_ref[0:W_A, :]) + _dot(hb, wo_ref[W_A:W_A + W_B, :])
           + _dot(hc_ref[...], wo_ref[W_A + W_B:D_MIX, :]))
    x1 = _ln(ALPHA * x_ref[...] + gate1 * mix) * pg_ref[...] + pb_ref[...]
    x1_ref[...] = x1
    u2 = _ln(x1) * (1.0 + scale2) + shift2
    u2_ref[:, 0:D_MODEL] = u2

    logits = lax.dot_general(wr_ref[...], u2, _NT, precision=lax.Precision.HIGHEST,
                             preferred_element_type=F32) + br_ref[...]
    ex = jnp.exp(logits - jnp.max(logits, axis=0, keepdims=True))
    probs = ex / jnp.sum(ex, axis=0, keepdims=True)
    scores = []
    for g in range(N_EGROUPS):
        p = [probs[g * EXPERTS_PER_GROUP + e:g * EXPERTS_PER_GROUP + e + 1, :] for e in range(EXPERTS_PER_GROUP)]
        best = p[0] + p[1]
        for a in range(EXPERTS_PER_GROUP):
            for b in range(a + 1, EXPERTS_PER_GROUP):
                if (a, b) != (0, 1):
                    best = jnp.maximum(best, p[a] + p[b])
        scores.append(best)
    gmax = functools.reduce(jnp.maximum, scores)
    sel = jnp.full(gmax.shape, N_EGROUPS - 1, jnp.int32)
    for g in range(N_EGROUPS - 2, -1, -1):
        sel = jnp.where(scores[g] == gmax, g, sel)
    eidx = lax.broadcasted_iota(jnp.int32, probs.shape, 0)
    pm = jnp.where((eidx // EXPERTS_PER_GROUP) == sel, probs, -jnp.inf)
    p1 = jnp.max(pm, axis=0, keepdims=True)
    i1 = jnp.min(jnp.where(pm == p1, eidx, N_EXPERTS), axis=0, keepdims=True)
    pm2 = jnp.where(eidx == i1, -jnp.inf, pm)
    p2 = jnp.max(pm2, axis=0, keepdims=True)
    i2 = jnp.min(jnp.where(pm2 == p2, eidx, N_EXPERTS), axis=0, keepdims=True)
    den = p1 + p2
    first_lo = i1 < i2
    w_lo = jnp.where(first_lo, p1, p2) / den
    w_hi = jnp.where(first_lo, p2, p1) / den
    a = jnp.minimum(i1, i2) - sel * EXPERTS_PER_GROUP
    b = jnp.maximum(i1, i2) - sel * EXPERTS_PER_GROUP
    pair = jnp.zeros_like(a)
    for k, (pa, pb_) in enumerate(_PAIRS):
        pair = jnp.where((a == pa) & (b == pb_), k, pair)
    cls = sel * len(_PAIRS) + pair
    rt_ref[...] = jnp.broadcast_to(cls, rt_ref.shape)
    wrow = lax.broadcasted_iota(jnp.int32, (LANES, tm), 0)
    wt = jnp.where(wrow == 0, w_lo, jnp.where(wrow == 1, w_hi, 0.0))
    u2_ref[:, D_MODEL:D_U2X] = wt.T


def _outproj(ha_c, ha_l, hb_c, hb_l, hc, x, mod, w_o, pg, pb, w_rt, b_r):
    tm = TM_PROJ
    n_ctx_blk = N_CTX_TOK // tm
    const = lambda shape: pl.BlockSpec(shape, lambda i: (0,) * len(shape))
    rows = lambda w: pl.BlockSpec((tm, w), lambda i: (i, 0))
    ctx_rows = lambda w: pl.BlockSpec((tm, w), lambda i: (jnp.minimum(i, n_ctx_blk - 1), 0))
    lat_rows = lambda w: pl.BlockSpec((tm, w), lambda i: (jnp.maximum(i - n_ctx_blk, 0), 0))
    return pl.pallas_call(
        functools.partial(_outproj_kernel, tm=tm),
        grid=(T_ALL // tm,),
        in_specs=[ctx_rows(W_A), lat_rows(W_A), ctx_rows(W_B), lat_rows(W_B),
                  rows(W_C), rows(D_MODEL), const((N_MOD_ROWS, 6 * D_MODEL)),
                  const((D_MIX, D_MODEL)), const((1, D_MODEL)), const((1, D_MODEL)),
                  const((N_EXPERTS, D_MODEL)), const((N_EXPERTS, 1))],
        out_specs=[rows(D_MODEL), rows(D_U2X), pl.BlockSpec((8, tm), lambda i: (0, i))],
        out_shape=[jax.ShapeDtypeStruct((T_ALL, D_MODEL), F32),
                   jax.ShapeDtypeStruct((T_ALL, D_U2X), F32),
                   jax.ShapeDtypeStruct((8, T_ALL), jnp.int32)],
        compiler_params=_cparams(1),
        name="outproj",
    )(ha_c, ha_l, hb_c, hb_l, hc, x, mod, w_o, pg, pb, w_rt, b_r)


def _moe_kernel(tok_ref, tlo_ref, thi_ref, nval_ref, nused_ref,
                u_hbm, wgl_ref, wul_ref, wdl_ref, wgh_ref, wuh_ref, wdh_ref, y_hbm,
                xbuf, ybuf, gsem, ssem):
    del tlo_ref, thi_ref
    i = pl.program_id(0)
    n_used = nused_ref[0]
    slot = i % 2

    def gather_row(tok, s, r):
        return pltpu.make_async_copy(u_hbm.at[pl.ds(tok, 1), :], xbuf.at[s, pl.ds(r, 1), :], gsem.at[s])

    def scatter_row(s, r, tok):
        return pltpu.make_async_copy(ybuf.at[s, pl.ds(r, 1), :], y_hbm.at[pl.ds(tok, 1), :], ssem.at[s])

    def start_gather(t, s):
        def body(r, c):
            gather_row(tok_ref[t * TM_MOE + r], s, r).start()
            return c
        lax.fori_loop(0, TM_MOE, body, 0, unroll=8)

    def wait_gather(s):
        def body(r, c):
            gather_row(0, s, r).wait()
            return c
        lax.fori_loop(0, TM_MOE, body, 0, unroll=8)

    def start_scatter(t, s):
        def body(r, c):
            scatter_row(s, r, tok_ref[t * TM_MOE + r]).start()
            return c
        lax.fori_loop(0, nval_ref[t], body, 0)

    def wait_scatter(t, s):
        def body(r, c):
            scatter_row(s, r, 0).wait()
            return c
        lax.fori_loop(0, nval_ref[t], body, 0)

    @pl.when(i == 0)
    def _():
        start_gather(0, 0)

    @pl.when(i + 1 < n_used)
    def _():
        start_gather(i + 1, 1 - slot)

    @pl.when(i < n_used)
    def _():
        wait_gather(slot)

        @pl.when(i >= 2)
        def _():
            wait_scatter(i - 2, slot)

        xx = xbuf[slot]
        x = xx[:, 0:D_MODEL].astype(BF16)

        def ffn(wg_ref, wu_ref, wd_ref, w):
            act = _silu(_dot(x, wg_ref[0, 0].astype(BF16))) * _dot(x, wu_ref[0, 0].astype(BF16)) * w
            return _dot(act.astype(BF16), wd_ref[0, 0].astype(BF16))

        ybuf[slot] = (ffn(wgl_ref, wul_ref, wdl_ref, xx[:, D_MODEL:D_MODEL + 1])
                      + ffn(wgh_ref, wuh_ref, wdh_ref, xx[:, D_MODEL + 1:D_MODEL + 2]))
        start_scatter(i, slot)

    @pl.when(i == n_used - 1)
    def _():
        wait_scatter(i, slot)

        @pl.when(i >= 1)
        def _():
            wait_scatter(i - 1, 1 - slot)


def _moe(u2x, row_tok, tile_lo, tile_hi, n_valid, n_used, w_gate, w_up, w_down, layer):
    lo = lambda shape: pl.BlockSpec(shape, lambda i, tok, tlo, thi, nval, nused: (layer, tlo[i], 0, 0))
    hi = lambda shape: pl.BlockSpec(shape, lambda i, tok, tlo, thi, nval, nused: (layer, thi[i], 0, 0))
    up_shape = (1, 1, D_MODEL, D_FF_EXPERT)
    down_shape = (1, 1, D_FF_EXPERT, D_MODEL)
    grid_spec = pltpu.PrefetchScalarGridSpec(
        num_scalar_prefetch=5,
        grid=(N_MOE_TILES,),
        in_specs=[pl.BlockSpec(memory_space=pl.ANY),
                  lo(up_shape), lo(up_shape), lo(down_shape),
                  hi(up_shape), hi(up_shape), hi(down_shape)],
        out_specs=pl.BlockSpec(memory_space=pl.ANY),
        scratch_shapes=[pltpu.VMEM((2, TM_MOE, D_U2X), F32),
                        pltpu.VMEM((2, TM_MOE, D_MODEL), F32),
                        pltpu.SemaphoreType.DMA((2,)),
                        pltpu.SemaphoreType.DMA((2,))],
    )
    return pl.pallas_call(
        _moe_kernel,
        grid_spec=grid_spec,
        out_shape=jax.ShapeDtypeStruct((T_ALL, D_MODEL), F32),
        compiler_params=pltpu.CompilerParams(dimension_semantics=("arbitrary",), vmem_limit_bytes=MOE_VMEM_LIMIT,
                                             has_side_effects=True),
        name="moe",
    )(row_tok, tile_lo, tile_hi, n_valid, n_used, u2x, w_gate, w_up, w_down, w_gate, w_up, w_down)


def _route_tables(cls):
    order = jnp.argsort(cls, stable=True).astype(jnp.int32)
    cids = jnp.arange(N_CLASSES, dtype=jnp.int32)
    counts = jnp.sum((cls[:, None] == cids[None, :]).astype(jnp.int32), axis=0)
    offs = jnp.cumsum(counts) - counts
    padded = (counts + TM_MOE - 1) // TM_MOE * TM_MOE
    ends = jnp.cumsum(padded)
    offs_p = ends - padded
    n_used = ends[-1] // TM_MOE
    tile_start = jnp.arange(N_MOE_TILES, dtype=jnp.int32) * TM_MOE
    tile_cls = jnp.sum((ends[None, :] <= jnp.minimum(tile_start, ends[-1] - TM_MOE)[:, None]).astype(jnp.int32),
                       axis=1)
    n_valid = jnp.where(tile_start < ends[-1],
                        jnp.clip(counts[tile_cls] - (tile_start - offs_p[tile_cls]), 0, TM_MOE), 0)
    row_cls = jnp.repeat(tile_cls, TM_MOE)
    k = jnp.arange(N_MOE_ROWS, dtype=jnp.int32) - offs_p[row_cls]
    src = offs[row_cls] + jnp.clip(k, 0, jnp.maximum(counts[row_cls] - 1, 0))
    row_tok = order[jnp.clip(src, 0, T_ALL - 1)]
    grp = tile_cls // len(_PAIRS)
    pair = tile_cls % len(_PAIRS)
    tile_lo = grp * EXPERTS_PER_GROUP + jnp.array([p[0] for p in _PAIRS], jnp.int32)[pair]
    tile_hi = grp * EXPERTS_PER_GROUP + jnp.array([p[1] for p in _PAIRS], jnp.int32)[pair]
    i32 = lambda v: v.astype(jnp.int32)
    return i32(row_tok), i32(tile_lo), i32(tile_hi), i32(n_valid), i32(n_used).reshape(1)


def _final_kernel(x1_ref, y_ref, mod_ref, pg_ref, pb_ref, *o_refs, tm):
    i = pl.program_id(0)
    r = _mod_row(i * tm)
    gate2 = mod_ref[pl.ds(r, 1), 5 * D_MODEL:6 * D_MODEL]
    out = _ln(ALPHA * x1_ref[...] + gate2 * y_ref[...]) * pg_ref[...] + pb_ref[...]
    if len(o_refs) == 1:
        o_refs[0][...] = out
    else:
        @pl.when(i < N_CTX_TOK // tm)
        def _():
            o_refs[0][...] = out

        @pl.when(i >= N_CTX_TOK // tm)
        def _():
            o_refs[1][...] = out


def _final(x1, y, mod, pg, pb, split):
    tm = TM_PROJ
    n_ctx_blk = N_CTX_TOK // tm
    const = lambda shape: pl.BlockSpec(shape, lambda i: (0,) * len(shape))
    rows = pl.BlockSpec((tm, D_MODEL), lambda i: (i, 0))
    if split:
        out_specs = [pl.BlockSpec((tm, D_MODEL), lambda i: (jnp.minimum(i, n_ctx_blk - 1), 0)),
                     pl.BlockSpec((tm, D_MODEL), lambda i: (jnp.maximum(i - n_ctx_blk, 0), 0))]
        out_shape = [jax.ShapeDtypeStruct((N_CTX_TOK, D_MODEL), F32), jax.ShapeDtypeStruct((N_LAT_TOK, D_MODEL), F32)]
    else:
        out_specs = [rows]
        out_shape = [jax.ShapeDtypeStruct((T_ALL, D_MODEL), F32)]
    return pl.pallas_call(
        functools.partial(_final_kernel, tm=tm),
        grid=(T_ALL // tm,),
        in_specs=[rows, rows, const((N_MOD_ROWS, 6 * D_MODEL)), const((1, D_MODEL)), const((1, D_MODEL))],
        out_specs=out_specs,
        out_shape=out_shape,
        compiler_params=_cparams(1),
        name="final",
    )(x1, y, mod, pg, pb)


def _grid_pos(n_tok):
    rows = n_tok // GRID_W
    r, col = jnp.meshgrid(jnp.arange(rows, dtype=F32), jnp.arange(GRID_W, dtype=F32), indexing='ij')
    quarter = D_MODEL // 4
    omega = 1.0 / (10000.0 ** (jnp.arange(quarter, dtype=F32) / quarter))

    def emb(p):
        ang = p.reshape(-1)[:, None] * omega[None, :]
        return jnp.concatenate([jnp.sin(ang), jnp.cos(ang)], axis=-1)

    return jnp.concatenate([emb(r), emb(col)], axis=-1)


def _pad_lanes(v, start):
    v = v.reshape(1, -1).astype(F32)
    return jnp.pad(v, ((0, 0), (start, LANES - start - v.shape[1])))


def kernel(x_prompt, x_sample, state_mlstm_C, state_mlstm_n, state_mlstm_m, state_ssd, c, c_ctx, w_in, w_o, mlstm_b_i, mlstm_b_f, mlstm_norm_g, ssd_conv_w, ssd_conv_b, ssd_dt_bias, ssd_A_log, ssd_D, ssd_norm_g, conv_dw_w, conv_dw_b, conv_ln_g, conv_ln_b, w_ada, b_ada, post1_g, post1_b, post2_g, post2_b, w_router, b_router, w_e_gate, w_e_up, w_e_down):
    cvec = jnp.concatenate([c_ctx[None, :], c, jnp.zeros((N_MOD_ROWS - 1 - DEC_BATCH, D_MODEL), F32)], axis=0)
    mod_all = _ada(cvec, w_ada, b_ada)
    x = _embed(x_prompt.reshape(N_CTX_TOK, D_MODEL), x_sample.reshape(N_LAT_TOK, D_MODEL), _grid_pos(DEC_SEQ))
    w_rt = w_router.T
    b_r = b_router.reshape(N_EXPERTS, 1)

    a_end = 4 * W_A + N_DIR * 2 * H_A
    b_end = a_end + W_B + W_XBC + N_DIR * H_B
    init = (state_mlstm_C,
            state_mlstm_n.reshape(DEC_BATCH, DEPTH, N_DIR * H_A, DH_A),
            jnp.broadcast_to(state_mlstm_m.reshape(DEC_BATCH, DEPTH, N_DIR * H_A, 1),
                             (DEC_BATCH, DEPTH, N_DIR * H_A, LANES)))
    st_c = st_n = st_m = st_h = None
    for l in range(DEPTH):
        w = w_in[l]
        w_main = jnp.concatenate([w[:, 0:4 * W_A], w[:, a_end:a_end + W_B + W_XBC], w[:, b_end:]], axis=1).astype(BF16)
        w_small = jnp.concatenate([w[:, 4 * W_A:a_end], w[:, b_end - N_DIR * H_B:b_end],
                                   jnp.zeros((D_MODEL, LANES - _DT_COL0 - N_DIR * H_B), F32)], axis=1).astype(BF16)
        gate_bias = (_pad_lanes(jnp.stack([mlstm_b_i[l], mlstm_b_f[l]], axis=1), 0)
                     + _pad_lanes(ssd_dt_bias[l], _DT_COL0))
        alog_row = _pad_lanes(ssd_A_log[l], _DT_COL0)
        mod = mod_all[l]

        proj, gates = _inproj(x, mod, w_main, w_small)

        m_norm = mlstm_norm_g[l].reshape(1, W_A)
        ha_c, st_c, st_n, st_m = _mlstm(proj, gates, gate_bias, m_norm, SEQ, BATCH, 0, l, prev=(st_c, st_n, st_m))
        (ha_l,) = _mlstm(proj, gates, gate_bias, m_norm, DEC_SEQ, DEC_BATCH, N_CTX_TOK, l, init=init)

        cw = jnp.pad(ssd_conv_w[l], ((0, 8 - SSM_CONV), (0, 0)))
        cb = ssd_conv_b[l].reshape(1, W_XBC)
        d_row = jnp.repeat(ssd_D[l], P_B).reshape(1, W_B)
        s_norm = ssd_norm_g[l].reshape(1, W_B)
        hb_c, st_h = _ssd(proj, gates, gate_bias, alog_row, cw, cb, d_row, s_norm, SEQ, BATCH, 0, l, prev=st_h)
        (hb_l,) = _ssd(proj, gates, gate_bias, alog_row, cw, cb, d_row, s_norm, DEC_SEQ, DEC_BATCH, N_CTX_TOK, l,
                       init=state_ssd)

        hc = _conf(proj, jnp.pad(conv_dw_w[l], ((0, 32 - CONV_W), (0, 0))), conv_dw_b[l].reshape(1, W_C),
                   conv_ln_g[l].reshape(1, W_C), conv_ln_b[l].reshape(1, W_C))

        x1, u2x, route = _outproj(ha_c, ha_l, hb_c, hb_l, hc, x, mod, w_o[l].astype(BF16), post1_g[l].reshape(1, D_MODEL),
                                  post1_b[l].reshape(1, D_MODEL), w_rt, b_r)
        row_tok, tile_lo, tile_hi, n_valid, n_used = _route_tables(route[0])
        y = _moe(u2x, row_tok, tile_lo, tile_hi, n_valid, n_used, w_e_gate, w_e_up, w_e_down, l)
        outs = _final(x1, y, mod, post2_g[l].reshape(1, D_MODEL), post2_b[l].reshape(1, D_MODEL), l == DEPTH - 1)
        x = outs[0]

    y_prompt = outs[0].reshape(BATCH, SEQ, D_MODEL)
    y_sample = outs[1].reshape(DEC_BATCH, DEC_SEQ, D_MODEL)
    return (y_prompt, y_sample, st_c, st_n.reshape(BATCH, DEPTH, N_DIR, H_A, DH_A),
            st_m[:, :, :, 0].reshape(BATCH, DEPTH, N_DIR, H_A), st_h)
```

```python
import functools

import jax
import jax.numpy as jnp
from jax import lax
from jax.experimental import pallas as pl
from jax.experimental.pallas import tpu as pltpu

D_MODEL = 1024
BATCH = 32
SEQ = 256
DEPTH = 2
DEC_BATCH = 2
DEC_SEQ = 1024
GRID_W = 64
N_DIR = 2
CHUNK = 128
H_A = 4
DH_A = 128
W_A = H_A * DH_A
H_B = 8
P_B = 64
W_B = H_B * P_B
G_B = 2
N_B = 128
W_XBC = W_B + 2 * G_B * N_B
SSM_CONV = 3
W_C = 512
CONV_W = 31
D_MIX = W_A + W_B + W_C
N_EXPERTS = 16
N_EGROUPS = 4
EXPERTS_PER_GROUP = N_EXPERTS // N_EGROUPS
D_FF_EXPERT = 512
ALPHA = (2 * DEPTH) ** 0.25
EPS = 1e-5
F32 = jnp.float32
BF16 = jnp.bfloat16

N_CTX_TOK = BATCH * SEQ
N_LAT_TOK = DEC_BATCH * DEC_SEQ
T_ALL = N_CTX_TOK + N_LAT_TOK
N_MOD_ROWS = 8
D_MAIN = 4 * W_A + W_B + W_XBC + 2 * W_C
LANES = 128
HALO = 16
CONV_ROWS = 256
TM_PROJ = 512
TM_MOE = 256
VMEM_LIMIT = 48 * 1024 * 1024
MOE_VMEM_LIMIT = 56 * 1024 * 1024

_NT = (((1,), (1,)), ((), ()))
_TN = (((0,), (0,)), ((), ()))


def _ln(x):
    mu = jnp.mean(x, axis=-1, keepdims=True)
    xc = x - mu
    var = jnp.mean(xc * xc, axis=-1, keepdims=True)
    return xc * lax.rsqrt(var + EPS)


def _sigmoid(x):
    return 1.0 / (1.0 + jnp.exp(-x))


def _silu(x):
    return x * _sigmoid(x)


def _softplus(x):
    return jnp.maximum(x, 0.0) + jnp.log1p(jnp.exp(-jnp.abs(x)))


def _dot(a, b):
    return jnp.dot(a, b, preferred_element_type=F32)


def _dotg(a, b, dims):
    return lax.dot_general(a, b, dims, preferred_element_type=F32)


def _tri_cumsum(tri, x):
    hi = x.astype(BF16)
    r1 = x - hi.astype(F32)
    mid = r1.astype(BF16)
    lo = (r1 - mid.astype(F32)).astype(BF16)
    return _dot(tri, hi) + _dot(tri, mid) + _dot(tri, lo)


def _tri_mask(d):
    row = lax.broadcasted_iota(jnp.int32, (CHUNK, CHUNK), 0)
    col = lax.broadcasted_iota(jnp.int32, (CHUNK, CHUNK), 1)
    return (row >= col) if d == 0 else (row <= col)


def _mod_row(row_start):
    return jnp.where(row_start < N_CTX_TOK, 0, 1 + (row_start - N_CTX_TOK) // DEC_SEQ)


def _cparams(n_axes):
    return pltpu.CompilerParams(dimension_semantics=("arbitrary",) * n_axes, vmem_limit_bytes=VMEM_LIMIT)


def _ada_kernel(c_ref, w_ref, b_ref, o_ref):
    o_ref[0] = _dot(_silu(c_ref[...]), w_ref[0]) + b_ref[0]


def _ada(cvec, w_ada, b_ada):
    tn = 1536
    return pl.pallas_call(
        _ada_kernel,
        grid=(DEPTH, 6 * D_MODEL // tn),
        in_specs=[
            pl.BlockSpec((N_MOD_ROWS, D_MODEL), lambda l, j: (0, 0)),
            pl.BlockSpec((1, D_MODEL, tn), lambda l, j: (l, 0, j)),
            pl.BlockSpec((1, 1, tn), lambda l, j: (l, 0, j)),
        ],
        out_specs=pl.BlockSpec((1, N_MOD_ROWS, tn), lambda l, j: (l, 0, j)),
        out_shape=jax.ShapeDtypeStruct((DEPTH, N_MOD_ROWS, 6 * D_MODEL), F32),
        compiler_params=_cparams(2),
        name="ada",
    )(cvec, w_ada, b_ada.reshape(DEPTH, 1, 6 * D_MODEL))


def _embed_kernel(xp_ref, xs_ref, pos_ref, o_ref):
    i = pl.program_id(0)

    @pl.when(i < N_CTX_TOK // DEC_SEQ)
    def _():
        o_ref[...] = xp_ref[...]

    @pl.when(i >= N_CTX_TOK // DEC_SEQ)
    def _():
        o_ref[...] = xs_ref[...] + pos_ref[...]


def _embed(xp, xs, pos):
    n_ctx_blk = N_CTX_TOK // DEC_SEQ
    return pl.pallas_call(
        _embed_kernel,
        grid=(T_ALL // DEC_SEQ,),
        in_specs=[
            pl.BlockSpec((DEC_SEQ, D_MODEL), lambda i: (jnp.minimum(i, n_ctx_blk - 1), 0)),
            pl.BlockSpec((DEC_SEQ, D_MODEL), lambda i: (jnp.maximum(i - n_ctx_blk, 0), 0)),
            pl.BlockSpec((DEC_SEQ, D_MODEL), lambda i: (0, 0)),
        ],
        out_specs=pl.BlockSpec((DEC_SEQ, D_MODEL), lambda i: (i, 0)),
        out_shape=jax.ShapeDtypeStruct((T_ALL, D_MODEL), F32),
        compiler_params=_cparams(1),
        name="embed",
    )(xp, xs, pos)


def _inproj_kernel(x_ref, mod_ref, wm_ref, wg_ref, om_ref, og_ref, u_scr, *, tm):
    i = pl.program_id(0)
    j = pl.program_id(1)

    @pl.when(j == 0)
    def _():
        r = _mod_row(i * tm)
        shift = mod_ref[pl.ds(r, 1), 0:D_MODEL]
        scale = mod_ref[pl.ds(r, 1), D_MODEL:2 * D_MODEL]
        u = (_ln(x_ref[...]) * (1.0 + scale) + shift).astype(BF16)
        u_scr[...] = u
        og_ref[...] = _dot(u, wg_ref[...])

    om_ref[...] = _dot(u_scr[...], wm_ref[...])


def _inproj(x, mod, w_main, w_small):
    tm, tn = TM_PROJ, D_MAIN // 2
    return pl.pallas_call(
        functools.partial(_inproj_kernel, tm=tm),
        grid=(T_ALL // tm, D_MAIN // tn),
        in_specs=[
            pl.BlockSpec((tm, D_MODEL), lambda i, j: (i, 0)),
            pl.BlockSpec((N_MOD_ROWS, 6 * D_MODEL), lambda i, j: (0, 0)),
            pl.BlockSpec((D_MODEL, tn), lambda i, j: (0, j)),
            pl.BlockSpec((D_MODEL, LANES), lambda i, j: (0, 0)),
        ],
        out_specs=[
            pl.BlockSpec((tm, tn), lambda i, j: (i, j)),
            pl.BlockSpec((tm, LANES), lambda i, j: (i, 0)),
        ],
        out_shape=[
            jax.ShapeDtypeStruct((T_ALL, D_MAIN), F32),
            jax.ShapeDtypeStruct((T_ALL, LANES), F32),
        ],
        scratch_shapes=[pltpu.VMEM((tm, D_MODEL), BF16)],
        compiler_params=_cparams(2),
        name="inproj",
    )(x, mod, w_main, w_small)


def _mlstm_kernel(*refs, seq, has_init, emit_state, layer):
    it = iter(refs)
    q_ref, k_ref, v_ref, o_ref, g_ref, gb_ref, ng_ref = (next(it) for _ in range(7))
    if has_init:
        c0_ref, n0_ref, m0_ref = (next(it) for _ in range(3))
    if emit_state and layer > 0:
        prev_refs = [next(it) for _ in range(3)]
    out_ref = next(it)
    if emit_state:
        co_ref, no_ref, mo_ref = (next(it) for _ in range(3))
    tg_scr, hf_scr, c_scr, n_scr, m_scr = (next(it) for _ in range(5))
    n_chunks = seq // CHUNK
    if emit_state and layer > 0:
        for prev_ref, st_ref in zip(prev_refs, (co_ref, no_ref, mo_ref)):
            st_ref[0, 0:layer] = prev_ref[0]

    y = g_ref[...] + gb_ref[...]
    lane = lax.broadcasted_iota(jnp.int32, y.shape, 1)
    is_forget = (lane < N_DIR * 2 * H_A) & ((lane & H_A) != 0)
    tg_scr[...] = jnp.where(is_forget, -_softplus(-y), y)

    def run_direction(d):
        for h in range(H_A):
            row = d * H_A + h
            if has_init:
                c_scr[h] = c0_ref[0, 0, d, h]
                n_scr[h:h + 1, :] = n0_ref[0, 0, row:row + 1, :]
                m_scr[h:h + 1, :] = m0_ref[0, 0, row:row + 1, :]
            else:
                c_scr[h] = jnp.zeros((DH_A, DH_A), F32)
                n_scr[h:h + 1, :] = jnp.zeros((1, DH_A), F32)
                m_scr[h:h + 1, :] = jnp.zeros((1, LANES), F32)

        mask = _tri_mask(d)
        tri = mask.astype(BF16)

        def body(ci, carry):
            c = ci if d == 0 else n_chunks - 1 - ci
            r0 = pl.multiple_of(c * CHUNK, CHUNK)
            rows = pl.ds(r0, CHUNK)
            tc = tg_scr[rows, :]
            cum = _tri_cumsum(tri, tc)
            cum_t = cum.T
            tc_t = tc.T
            for h in range(H_A):
                ci_col = d * 2 * H_A + h
                cf_col = ci_col + H_A
                cols = slice(h * DH_A, (h + 1) * DH_A)
                li_c = tc[:, ci_col:ci_col + 1]
                li_r = tc_t[ci_col:ci_col + 1, :]
                b_c = cum[:, cf_col:cf_col + 1]
                b_r = cum_t[cf_col:cf_col + 1, :]
                g = b_r[:, CHUNK - 1:CHUNK] if d == 0 else b_r[:, 0:1]
                qs = q_ref[rows, cols] * (DH_A ** -0.5)
                kf = k_ref[rows, cols]
                qb = qs.astype(BF16)
                kb = kf.astype(BF16)
                vb = v_ref[rows, cols].astype(BF16)
                c_st = c_scr[h]
                n_st = n_scr[h:h + 1, :]
                m_st = m_scr[h:h + 1, 0:1]

                dm = jnp.where(mask, b_c - b_r + li_r, -jnp.inf)
                inter = b_c + m_st
                m_t = jnp.maximum(inter, jnp.max(dm, axis=-1, keepdims=True))
                s = _dotg(qb, kb, _NT) * jnp.exp(dm - m_t)
                w_int = jnp.exp(inter - m_t)
                num = w_int * _dot(qb, c_st.astype(BF16)) + _dot(s.astype(BF16), vb)
                den = w_int * jnp.sum(qs * n_st, axis=-1, keepdims=True) + jnp.sum(s, axis=-1, keepdims=True)
                hh = num / jnp.maximum(jnp.abs(den), jnp.exp(-m_t))

                a_c = g - b_c + li_c
                a_r = g - b_r + li_r
                m_new = jnp.maximum(g + m_st, jnp.max(a_r, axis=-1, keepdims=True))
                w_prev = jnp.exp(g + m_st - m_new)
                kw = jnp.exp(a_c - m_new) * kf
                c_scr[h] = w_prev * c_st + _dotg(kw.astype(BF16), vb, _TN)
                n_scr[h:h + 1, :] = w_prev * n_st + jnp.sum(kw, axis=0, keepdims=True)
                m_scr[h:h + 1, :] = jnp.broadcast_to(m_new, (1, LANES))

                if d == 0:
                    hf_scr[rows, cols] = hh
                else:
                    hn = _ln(hf_scr[rows, cols] + hh) * ng_ref[:, cols]
                    out_ref[rows, cols] = (hn * _sigmoid(o_ref[rows, cols])).astype(out_ref.dtype)
            return carry

        lax.fori_loop(0, n_chunks, body, 0)

        if emit_state:
            for h in range(H_A):
                row = d * H_A + h
                co_ref[0, layer, d, h] = c_scr[h]
                no_ref[0, layer, row:row + 1, :] = n_scr[h:h + 1, :]
                mo_ref[0, layer, row:row + 1, :] = m_scr[h:h + 1, :]

    run_direction(0)
    run_direction(1)


def _mlstm(proj, gates, gate_bias, norm_g, seq, n_seq, row_off, layer, init=None, prev=None):
    blk0 = row_off // seq
    has_init = init is not None
    emit_state = not has_init
    col = lambda c: pl.BlockSpec((seq, W_A), lambda s: (s + blk0, c))
    st_shapes = [(N_DIR, H_A, DH_A, DH_A), (N_DIR * H_A, DH_A), (N_DIR * H_A, LANES)]

    def st_specs(n_layers, first):
        return [pl.BlockSpec((1, n_layers) + shp, lambda s, nd=len(shp): (s, first) + (0,) * nd) for shp in st_shapes]

    in_specs = [col(0), col(1), col(2), col(3),
                pl.BlockSpec((seq, LANES), lambda s: (s + blk0, 0)),
                pl.BlockSpec((1, LANES), lambda s: (0, 0)),
                pl.BlockSpec((1, W_A), lambda s: (0, 0))]
    args = [proj, proj, proj, proj, gates, gate_bias, norm_g]
    if has_init:
        in_specs += st_specs(1, layer)
        args += list(init)
    if emit_state and layer > 0:
        in_specs += st_specs(layer, 0)
        args += list(prev)
    out_specs = [pl.BlockSpec((seq, W_A), lambda s: (s, 0))]
    out_shape = [jax.ShapeDtypeStruct((n_seq * seq, W_A), BF16)]
    if emit_state:
        out_specs += st_specs(layer + 1, 0)
        out_shape += [jax.ShapeDtypeStruct((n_seq, layer + 1) + shp, F32) for shp in st_shapes]
    return pl.pallas_call(
        functools.partial(_mlstm_kernel, seq=seq, has_init=has_init, emit_state=emit_state, layer=layer),
        grid=(n_seq,),
        in_specs=in_specs,
        out_specs=out_specs,
        out_shape=out_shape,
        scratch_shapes=[pltpu.VMEM((seq, LANES), F32),
                        pltpu.VMEM((seq, W_A), F32),
                        pltpu.VMEM((H_A, DH_A, DH_A), F32),
                        pltpu.VMEM((8, DH_A), F32),
                        pltpu.VMEM((8, LANES), F32)],
        compiler_params=_cparams(1),
        name="mlstm_lat" if has_init else "mlstm_ctx",
    )(*args)


_DT_COL0 = N_DIR * 2 * H_A
_HG = H_B // G_B
_WG = _HG * P_B


def _ssd_kernel(*refs, seq, has_init, emit_state, layer):
    it = iter(refs)
    (z_ref, x_ref, b_ref, c_ref, g_ref, gb_ref, alog_ref, cw_ref, cb_ref, dsk_ref, ng_ref) = (
        next(it) for _ in range(11))
    if has_init:
        h0_ref = next(it)
    if emit_state and layer > 0:
        prev_ref = next(it)
    out_ref = next(it)
    if emit_state:
        ho_ref = next(it)
    pad_scr, xbc_scr, dt_scr, yf_scr, h_scr = (next(it) for _ in range(5))
    n_chunks = seq // CHUNK
    pad = 8
    if emit_state and layer > 0:
        ho_ref[0, 0:layer] = prev_ref[0]

    pad_scr[0:pad, :] = jnp.zeros((pad, W_XBC), F32)
    pad_scr[pad + seq:2 * pad + seq, :] = jnp.zeros((pad, W_XBC), F32)
    pad_scr[pad:pad + seq, 0:W_B] = x_ref[...]
    pad_scr[pad:pad + seq, W_B:W_B + G_B * N_B] = b_ref[...]
    pad_scr[pad:pad + seq, W_B + G_B * N_B:W_XBC] = c_ref[...]
    for c in range(n_chunks):
        for lb in range(W_XBC // 256):
            cols = slice(lb * 256, (lb + 1) * 256)
            acc = cb_ref[:, cols]
            for k in range(SSM_CONV):
                r = c * CHUNK + pad - SSM_CONV // 2 + k
                acc = acc + cw_ref[k:k + 1, cols] * pad_scr[r:r + CHUNK, cols]
            xbc_scr[c * CHUNK:(c + 1) * CHUNK, cols] = _silu(acc)

    dt_scr[...] = _softplus(g_ref[...] + gb_ref[...])
    a_row = -jnp.exp(alog_ref[...])

    lane_blk = lax.broadcasted_iota(jnp.int32, (CHUNK, _WG), 1) // P_B
    row_blk = lax.broadcasted_iota(jnp.int32, (_WG, 1), 0) // P_B

    def run_direction(d):
        for g in range(G_B):
            for hh in range(_HG):
                head = g * _HG + hh
                if has_init:
                    h_scr[g, hh * P_B:(hh + 1) * P_B, :] = h0_ref[0, 0, d, head]
                else:
                    h_scr[g, hh * P_B:(hh + 1) * P_B, :] = jnp.zeros((P_B, N_B), F32)

        mask = _tri_mask(d)
        tri = mask.astype(BF16)

        def body(ci, carry):
            c = ci if d == 0 else n_chunks - 1 - ci
            r0 = pl.multiple_of(c * CHUNK, CHUNK)
            rows = pl.ds(r0, CHUNK)
            dtc = dt_scr[rows, :]
            cum = _tri_cumsum(tri, dtc * a_row)
            cum_t = cum.T
            dt_t = dtc.T
            for g in range(G_B):
                xg = xbc_scr[rows, g * _WG:(g + 1) * _WG]
                xgb = xg.astype(BF16)
                bg = xbc_scr[rows, W_B + g * N_B:W_B + (g + 1) * N_B].astype(BF16)
                cg = xbc_scr[rows, W_B + G_B * N_B + g * N_B:W_B + G_B * N_B + (g + 1) * N_B].astype(BF16)
                h_st = h_scr[g]
                cb = _dotg(cg, bg, _NT)
                ch = _dotg(cg, h_st.astype(BF16), _NT)
                y = jnp.zeros((CHUNK, _WG), F32)
                e_full = jnp.zeros((CHUNK, _WG), F32)
                w_full = jnp.zeros((CHUNK, _WG), F32)
                decay = jnp.zeros((_WG, 1), F32)
                for hh in range(_HG):
                    colh = _DT_COL0 + d * H_B + g * _HG + hh
                    b_c = cum[:, colh:colh + 1]
                    b_r = cum_t[colh:colh + 1, :]
                    dt_c = dtc[:, colh:colh + 1]
                    dt_r = dt_t[colh:colh + 1, :]
                    gt = b_r[:, CHUNK - 1:CHUNK] if d == 0 else b_r[:, 0:1]
                    seg = jnp.where(mask, b_c - b_r, -jnp.inf)
                    sm = cb * jnp.exp(seg) * dt_r
                    sel = lane_blk == hh
                    y = jnp.where(sel, _dot(sm.astype(BF16), xgb), y)
                    e_full = jnp.where(sel, jnp.exp(b_c), e_full)
                    w_full = jnp.where(sel, jnp.exp(gt - b_c) * dt_c, w_full)
                    decay = jnp.where(row_blk == hh, jnp.exp(gt), decay)
                y = y + e_full * ch
                h_scr[g] = decay * h_st + _dotg((w_full * xg).astype(BF16), bg, _TN)
                gcols = slice(g * _WG, (g + 1) * _WG)
                if d == 0:
                    yf_scr[rows, gcols] = y
                else:
                    yf_scr[rows, gcols] = yf_scr[rows, gcols] + y + dsk_ref[:, gcols] * xg
            if d == 1:
                yz = yf_scr[rows, :] * _silu(z_ref[rows, :])
                rms = lax.rsqrt(jnp.mean(yz * yz, axis=-1, keepdims=True) + EPS)
                out_ref[rows, :] = (yz * rms * ng_ref[...]).astype(out_ref.dtype)
            return carry

        lax.fori_loop(0, n_chunks, body, 0)

        if emit_state:
            for g in range(G_B):
                for hh in range(_HG):
                    ho_ref[0, layer, d, g * _HG + hh] = h_scr[g, hh * P_B:(hh + 1) * P_B, :]

    run_direction(0)
    run_direction(1)


def _ssd(proj, gates, gate_bias, alog_row, conv_w, conv_b, d_row, norm_g, seq, n_seq, row_off, layer,
         init=None, prev=None):
    blk0 = row_off // seq
    has_init = init is not None
    emit_state = not has_init
    zc = W_A * 4 // W_B
    bc = (4 * W_A + 2 * W_B) // (G_B * N_B)
    const = lambda shape: pl.BlockSpec(shape, lambda s: (0,) * len(shape))
    st_spec = lambda n_layers, first: pl.BlockSpec((1, n_layers, N_DIR, H_B, P_B, N_B),
                                                   lambda s: (s, first, 0, 0, 0, 0))
    in_specs = [pl.BlockSpec((seq, W_B), lambda s: (s + blk0, zc)),
                pl.BlockSpec((seq, W_B), lambda s: (s + blk0, zc + 1)),
                pl.BlockSpec((seq, G_B * N_B), lambda s: (s + blk0, bc)),
                pl.BlockSpec((seq, G_B * N_B), lambda s: (s + blk0, bc + 1)),
                pl.BlockSpec((seq, LANES), lambda s: (s + blk0, 0)),
                const((1, LANES)), const((1, LANES)), const((8, W_XBC)), const((1, W_XBC)),
                const((1, W_B)), const((1, W_B))]
    args = [proj, proj, proj, proj, gates, gate_bias, alog_row, conv_w, conv_b, d_row, norm_g]
    if has_init:
        in_specs.append(st_spec(1, layer))
        args.append(init)
    if emit_state and layer > 0:
        in_specs.append(st_spec(layer, 0))
        args.append(prev)
    out_specs = [pl.BlockSpec((seq, W_B), lambda s: (s, 0))]
    out_shape = [jax.ShapeDtypeStruct((n_seq * seq, W_B), BF16)]
    if emit_state:
        out_specs.append(st_spec(layer + 1, 0))
        out_shape.append(jax.ShapeDtypeStruct((n_seq, layer + 1, N_DIR, H_B, P_B, N_B), F32))
    return pl.pallas_call(
        functools.partial(_ssd_kernel, seq=seq, has_init=has_init, emit_state=emit_state, layer=layer),
        grid=(n_seq,),
        in_specs=in_specs,
        out_specs=out_specs,
        out_shape=out_shape,
        scratch_shapes=[pltpu.VMEM((seq + 16, W_XBC), F32),
                        pltpu.VMEM((seq, W_XBC), F32),
                        pltpu.VMEM((seq, LANES), F32),
                        pltpu.VMEM((seq, W_B), F32),
                        pltpu.VMEM((G_B, _WG, N_B), F32)],
        compiler_params=_cparams(1),
        name="ssd_lat" if has_init else "ssd_ctx",
    )(*args)


def _conf_kernel(a_ref, ap_ref, an_ref, g_ref, gp_ref, gn_ref, w_ref, b_ref, lg_ref, lb_ref, out_ref, pad_scr):
    i = pl.program_id(0)
    n_ctx_blk = N_CTX_TOK // CONV_ROWS
    blk_per_lat = DEC_SEQ // CONV_ROWS
    j = (i - n_ctx_blk) % blk_per_lat
    is_ctx = i < n_ctx_blk
    keep_prev = jnp.where(is_ctx | (j == 0), 0.0, 1.0)
    keep_next = jnp.where(is_ctx | (j == blk_per_lat - 1), 0.0, 1.0)
    pad_scr[0:HALO, :] = ap_ref[...] * _sigmoid(gp_ref[...]) * keep_prev
    pad_scr[HALO:HALO + CONV_ROWS, :] = a_ref[...] * _sigmoid(g_ref[...])
    pad_scr[HALO + CONV_ROWS:2 * HALO + CONV_ROWS, :] = an_ref[...] * _sigmoid(gn_ref[...]) * keep_next
    rc = 64
    for c in range(CONV_ROWS // rc):
        acc = jnp.broadcast_to(b_ref[...], (rc, W_C))
        for k in range(CONV_W):
            r = c * rc + HALO - CONV_W // 2 + k
            acc = acc + w_ref[k:k + 1, :] * pad_scr[r:r + rc, :]
        u = _ln(acc) * lg_ref[...] + lb_ref[...]
        out_ref[c * rc:(c + 1) * rc, :] = _silu(u).astype(out_ref.dtype)


def _conf(proj, dw_w, dw_b, ln_g, ln_b):
    assert SEQ == CONV_ROWS and DEC_SEQ % CONV_ROWS == 0
    ac = (4 * W_A + W_B + W_XBC) // W_C
    hb = CONV_ROWS // HALO
    n_halo = T_ALL // HALO
    const = lambda shape: pl.BlockSpec(shape, lambda i: (0,) * len(shape))

    def specs(c):
        return [pl.BlockSpec((CONV_ROWS, W_C), lambda i: (i, c)),
                pl.BlockSpec((HALO, W_C), lambda i: (jnp.maximum(i * hb - 1, 0), c)),
                pl.BlockSpec((HALO, W_C), lambda i: (jnp.minimum((i + 1) * hb, n_halo - 1), c))]

    return pl.pallas_call(
        _conf_kernel,
        grid=(T_ALL // CONV_ROWS,),
        in_specs=specs(ac) + specs(ac + 1) + [const((32, W_C)), const((1, W_C)), const((1, W_C)), const((1, W_C))],
        out_specs=pl.BlockSpec((CONV_ROWS, W_C), lambda i: (i, 0)),
        out_shape=jax.ShapeDtypeStruct((T_ALL, W_C), BF16),
        scratch_shapes=[pltpu.VMEM((CONV_ROWS + 2 * HALO, W_C), F32)],
        compiler_params=_cparams(1),
        name="conf",
    )(proj, proj, proj, proj, proj, proj, dw_w, dw_b, ln_g, ln_b)


_PAIRS = [(a, b) for a in range(EXPERTS_PER_GROUP) for b in range(a + 1, EXPERTS_PER_GROUP)]
N_CLASSES = N_EGROUPS * len(_PAIRS)
D_U2X = D_MODEL + LANES
N_MOE_ROWS = T_ALL + N_CLASSES * TM_MOE
N_MOE_TILES = N_MOE_ROWS // TM_MOE


def _outproj_kernel(hac_ref, hal_ref, hbc_ref, hbl_ref, hc_ref, x_ref, mod_ref, wo_ref, pg_ref, pb_ref,
                    wr_ref, br_ref, x1_ref, u2_ref, rt_ref, *, tm):
    i = pl.program_id(0)
    r = _mod_row(i * tm)
    gate1 = mod_ref[pl.ds(r, 1), 2 * D_MODEL:3 * D_MODEL]
    shift2 = mod_ref[pl.ds(r, 1), 3 * D_MODEL:4 * D_MODEL]
    scale2 = mod_ref[pl.ds(r, 1), 4 * D_MODEL:5 * D_MODEL]
    is_ctx = i < N_CTX_TOK // tm
    ha = jnp.where(is_ctx, hac_ref[...], hal_ref[...])
    hb = jnp.where(is_ctx, hbc_ref[...], hbl_ref[...])
    mix = (_dot(ha, wo_ref[0:W_A, :]) + _dot(hb, wo_ref[W_A:W_A + W_B, :])
           + _dot(hc_ref[...], wo_ref[W_A + W_B:D_MIX, :]))
    x1 = _ln(ALPHA * x_ref[...] + gate1 * mix) * pg_ref[...] + pb_ref[...]
    x1_ref[...] = x1
    u2 = _ln(x1) * (1.0 + scale2) + shift2
    u2_ref[:, 0:D_MODEL] = u2

    logits = lax.dot_general(wr_ref[...], u2, _NT, precision=lax.Precision.HIGHEST,
                             preferred_element_type=F32) + br_ref[...]
    ex = jnp.exp(logits - jnp.max(logits, axis=0, keepdims=True))
    probs = ex / jnp.sum(ex, axis=0, keepdims=True)
    scores = []
    for g in range(N_EGROUPS):
        p = [probs[g * EXPERTS_PER_GROUP + e:g * EXPERTS_PER_GROUP + e + 1, :] for e in range(EXPERTS_PER_GROUP)]
        best = p[0] + p[1]
        for a in range(EXPERTS_PER_GROUP):
            for b in range(a + 1, EXPERTS_PER_GROUP):
                if (a, b) != (0, 1):
                    best = jnp.maximum(best, p[a] + p[b])
        scores.append(best)
    gmax = functools.reduce(jnp.maximum, scores)
    sel = jnp.full(gmax.shape, N_EGROUPS - 1, jnp.int32)
    for g in range(N_EGROUPS - 2, -1, -1):
        sel = jnp.where(scores[g] == gmax, g, sel)
    eidx = lax.broadcasted_iota(jnp.int32, probs.shape, 0)
    pm = jnp.where((eidx // EXPERTS_PER_GROUP) == sel, probs, -jnp.inf)
    p1 = jnp.max(pm, axis=0, keepdims=True)
    i1 = jnp.min(jnp.where(pm == p1, eidx, N_EXPERTS), axis=0, keepdims=True)
    pm2 = jnp.where(eidx == i1, -jnp.inf, pm)
    p2 = jnp.max(pm2, axis=0, keepdims=True)
    i2 = jnp.min(jnp.where(pm2 == p2, eidx, N_EXPERTS), axis=0, keepdims=True)
    den = p1 + p2
    first_lo = i1 < i2
    w_lo = jnp.where(first_lo, p1, p2) / den
    w_hi = jnp.where(first_lo, p2, p1) / den
    a = jnp.minimum(i1, i2) - sel * EXPERTS_PER_GROUP
    b = jnp.maximum(i1, i2) - sel * EXPERTS_PER_GROUP
    pair = jnp.zeros_like(a)
    for k, (pa, pb_) in enumerate(_PAIRS):
        pair = jnp.where((a == pa) & (b == pb_), k, pair)
    cls = sel * len(_PAIRS) + pair
    rt_ref[...] = jnp.broadcast_to(cls, rt_ref.shape)
    wrow = lax.broadcasted_iota(jnp.int32, (LANES, tm), 0)
    wt = jnp.where(wrow == 0, w_lo, jnp.where(wrow == 1, w_hi, 0.0))
    u2_ref[:, D_MODEL:D_U2X] = wt.T


def _outproj(ha_c, ha_l, hb_c, hb_l, hc, x, mod, w_o, pg, pb, w_rt, b_r):
    tm = TM_PROJ
    n_ctx_blk = N_CTX_TOK // tm
    const = lambda shape: pl.BlockSpec(shape, lambda i: (0,) * len(shape))
    rows = lambda w: pl.BlockSpec((tm, w), lambda i: (i, 0))
    ctx_rows = lambda w: pl.BlockSpec((tm, w), lambda i: (jnp.minimum(i, n_ctx_blk - 1), 0))
    lat_rows = lambda w: pl.BlockSpec((tm, w), lambda i: (jnp.maximum(i - n_ctx_blk, 0), 0))
    return pl.pallas_call(
        functools.partial(_outproj_kernel, tm=tm),
        grid=(T_ALL // tm,),
        in_specs=[ctx_rows(W_A), lat_rows(W_A), ctx_rows(W_B), lat_rows(W_B),
                  rows(W_C), rows(D_MODEL), const((N_MOD_ROWS, 6 * D_MODEL)),
                  const((D_MIX, D_MODEL)), const((1, D_MODEL)), const((1, D_MODEL)),
                  const((N_EXPERTS, D_MODEL)), const((N_EXPERTS, 1))],
        out_specs=[rows(D_MODEL), rows(D_U2X), pl.BlockSpec((8, tm), lambda i: (0, i))],
        out_shape=[jax.ShapeDtypeStruct((T_ALL, D_MODEL), F32),
                   jax.ShapeDtypeStruct((T_ALL, D_U2X), F32),
                   jax.ShapeDtypeStruct((8, T_ALL), jnp.int32)],
        compiler_params=_cparams(1),
        name="outproj",
    )(ha_c, ha_l, hb_c, hb_l, hc, x, mod, w_o, pg, pb, w_rt, b_r)


N_Y_ROWS = T_ALL + 2 * TM_MOE


def _moe_kernel(tok_ref, dst_ref, tlo_ref, thi_ref, nused_ref,
                u_hbm, wgl_ref, wul_ref, wdl_ref, wgh_ref, wuh_ref, wdh_ref, y_hbm,
                xbuf0, xbuf1, ybuf0, ybuf1, gsem, ssem):
    del tlo_ref, thi_ref
    i = pl.program_id(0)
    n_used = nused_ref[0]

    def gather_rows(t, xb, sem):
        for r in range(TM_MOE):
            pltpu.make_async_copy(u_hbm.at[pl.ds(tok_ref[t * TM_MOE + r], 1), :], xb.at[pl.ds(r, 1), :], sem).start()

    def scatter_rows(t, yb, sem):
        for r in range(TM_MOE):
            pltpu.make_async_copy(yb.at[pl.ds(r, 1), :], y_hbm.at[pl.ds(dst_ref[(t + 1) * TM_MOE + r], 1), :],
                                  sem).start()

    def wait_gather(xb, sem):
        pltpu.make_async_copy(u_hbm.at[pl.ds(0, TM_MOE), :], xb, sem).wait()

    def wait_scatter(yb, sem):
        pltpu.make_async_copy(yb, y_hbm.at[pl.ds(0, TM_MOE), :], sem).wait()

    def step(s, xb, xb_next, yb, yb_prev):
        @pl.when(i > 0)
        def _():
            wait_scatter(yb, ssem.at[s])

        wait_gather(xb, gsem.at[s])
        xx = xb[...]
        x = xx[:, 0:D_MODEL].astype(BF16)
        w_lo = xx[:, D_MODEL:D_MODEL + 1]
        w_hi = xx[:, D_MODEL + 1:D_MODEL + 2]
        gather_rows(jnp.minimum(i + 1, n_used - 1), xb_next, gsem.at[1 - s])
        scatter_rows(i - 1, yb_prev, ssem.at[1 - s])

        def ffn(wg_ref, wu_ref, wd_ref, w):
            act = _silu(_dot(x, wg_ref[0, 0].astype(BF16))) * _dot(x, wu_ref[0, 0].astype(BF16)) * w
            return _dot(act.astype(BF16), wd_ref[0, 0].astype(BF16))

        yb[...] = ffn(wgl_ref, wul_ref, wdl_ref, w_lo) + ffn(wgh_ref, wuh_ref, wdh_ref, w_hi)

        @pl.when(i == n_used - 1)
        def _():
            scatter_rows(i, yb, ssem.at[s])
            wait_scatter(yb, ssem.at[s])
            wait_scatter(yb_prev, ssem.at[1 - s])
            wait_gather(xb_next, gsem.at[1 - s])

    @pl.when(i == 0)
    def _():
        ybuf1[...] = jnp.zeros(ybuf1.shape, F32)
        fill = pltpu.make_async_copy(ybuf1, y_hbm.at[pl.ds(T_ALL, TM_MOE), :], ssem.at[0])
        fill.start()
        fill.wait()
        gather_rows(0, xbuf0, gsem.at[0])

    @pl.when((i < n_used) & (i % 2 == 0))
    def _():
        step(0, xbuf0, xbuf1, ybuf0, ybuf1)

    @pl.when((i < n_used) & (i % 2 == 1))
    def _():
        step(1, xbuf1, xbuf0, ybuf1, ybuf0)


def _moe(u2x, row_tok, row_dst, tile_lo, tile_hi, n_used, w_gate, w_up, w_down, layer):
    lo = lambda shape: pl.BlockSpec(shape, lambda i, tok, dst, tlo, thi, nused: (layer, tlo[i], 0, 0))
    hi = lambda shape: pl.BlockSpec(shape, lambda i, tok, dst, tlo, thi, nused: (layer, thi[i], 0, 0))
    up_shape = (1, 1, D_MODEL, D_FF_EXPERT)
    down_shape = (1, 1, D_FF_EXPERT, D_MODEL)
    grid_spec = pltpu.PrefetchScalarGridSpec(
        num_scalar_prefetch=5,
        grid=(N_MOE_TILES,),
        in_specs=[pl.BlockSpec(memory_space=pl.ANY),
                  lo(up_shape), lo(up_shape), lo(down_shape),
                  hi(up_shape), hi(up_shape), hi(down_shape)],
        out_specs=pl.BlockSpec(memory_space=pl.ANY),
        scratch_shapes=[pltpu.VMEM((TM_MOE, D_U2X), F32), pltpu.VMEM((TM_MOE, D_U2X), F32),
                        pltpu.VMEM((TM_MOE, D_MODEL), F32), pltpu.VMEM((TM_MOE, D_MODEL), F32),
                        pltpu.SemaphoreType.DMA((2,)),
                        pltpu.SemaphoreType.DMA((2,))],
    )
    return pl.pallas_call(
        _moe_kernel,
        grid_spec=grid_spec,
        out_shape=jax.ShapeDtypeStruct((N_Y_ROWS, D_MODEL), F32),
        compiler_params=pltpu.CompilerParams(dimension_semantics=("arbitrary",), vmem_limit_bytes=MOE_VMEM_LIMIT,
                                             has_side_effects=True),
        name="moe",
    )(row_tok, row_dst, tile_lo, tile_hi, n_used, u2x, w_gate, w_up, w_down, w_gate, w_up, w_down)


def _route_tables(cls):
    order = jnp.argsort(cls, stable=True).astype(jnp.int32)
    cids = jnp.arange(N_CLASSES, dtype=jnp.int32)
    counts = jnp.sum((cls[:, None] == cids[None, :]).astype(jnp.int32), axis=0)
    offs = jnp.cumsum(counts) - counts
    padded = (counts + TM_MOE - 1) // TM_MOE * TM_MOE
    ends = jnp.cumsum(padded)
    offs_p = ends - padded
    n_used = ends[-1] // TM_MOE
    tile_start = jnp.arange(N_MOE_TILES, dtype=jnp.int32) * TM_MOE
    tile_cls = jnp.sum((ends[None, :] <= jnp.minimum(tile_start, ends[-1] - TM_MOE)[:, None]).astype(jnp.int32),
                       axis=1)
    row = jnp.arange(N_MOE_ROWS, dtype=jnp.int32)
    row_cls = jnp.repeat(tile_cls, TM_MOE)
    k = row - offs_p[row_cls]
    valid = (k < counts[row_cls]) & (row < ends[-1])
    src = offs[row_cls] + jnp.clip(k, 0, jnp.maximum(counts[row_cls] - 1, 0))
    row_tok = order[jnp.clip(src, 0, T_ALL - 1)]
    spare = T_ALL + ((row // TM_MOE) % 2) * TM_MOE + row % TM_MOE
    row_dst = jnp.where(valid, row_tok, spare)
    row_dst = jnp.concatenate([T_ALL + TM_MOE + jnp.arange(TM_MOE, dtype=jnp.int32), row_dst])
    grp = tile_cls // len(_PAIRS)
    pair = tile_cls % len(_PAIRS)
    tile_lo = grp * EXPERTS_PER_GROUP + jnp.array([p[0] for p in _PAIRS], jnp.int32)[pair]
    tile_hi = grp * EXPERTS_PER_GROUP + jnp.array([p[1] for p in _PAIRS], jnp.int32)[pair]
    i32 = lambda v: v.astype(jnp.int32)
    return i32(row_tok), i32(row_dst), i32(tile_lo), i32(tile_hi), i32(n_used).reshape(1)


def _final_kernel(x1_ref, y_ref, mod_ref, pg_ref, pb_ref, *o_refs, tm):
    i = pl.program_id(0)
    r = _mod_row(i * tm)
    gate2 = mod_ref[pl.ds(r, 1), 5 * D_MODEL:6 * D_MODEL]
    out = _ln(ALPHA * x1_ref[...] + gate2 * y_ref[...]) * pg_ref[...] + pb_ref[...]
    if len(o_refs) == 1:
        o_refs[0][...] = out
    else:
        @pl.when(i < N_CTX_TOK // tm)
        def _():
            o_refs[0][...] = out

        @pl.when(i >= N_CTX_TOK // tm)
        def _():
            o_refs[1][...] = out


def _final(x1, y, mod, pg, pb, split):
    tm = TM_PROJ
    n_ctx_blk = N_CTX_TOK // tm
    const = lambda shape: pl.BlockSpec(shape, lambda i: (0,) * len(shape))
    rows = pl.BlockSpec((tm, D_MODEL), lambda i: (i, 0))
    if split:
        out_specs = [pl.BlockSpec((tm, D_MODEL), lambda i: (jnp.minimum(i, n_ctx_blk - 1), 0)),
                     pl.BlockSpec((tm, D_MODEL), lambda i: (jnp.maximum(i - n_ctx_blk, 0), 0))]
        out_shape = [jax.ShapeDtypeStruct((N_CTX_TOK, D_MODEL), F32), jax.ShapeDtypeStruct((N_LAT_TOK, D_MODEL), F32)]
    else:
        out_specs = [rows]
        out_shape = [jax.ShapeDtypeStruct((T_ALL, D_MODEL), F32)]
    return pl.pallas_call(
        functools.partial(_final_kernel, tm=tm),
        grid=(T_ALL // tm,),
        in_specs=[rows, rows, const((N_MOD_ROWS, 6 * D_MODEL)), const((1, D_MODEL)), const((1, D_MODEL))],
        out_specs=out_specs,
        out_shape=out_shape,
        compiler_params=_cparams(1),
        name="final",
    )(x1, y, mod, pg, pb)


def _grid_pos(n_tok):
    rows = n_tok // GRID_W
    r, col = jnp.meshgrid(jnp.arange(rows, dtype=F32), jnp.arange(GRID_W, dtype=F32), indexing='ij')
    quarter = D_MODEL // 4
    omega = 1.0 / (10000.0 ** (jnp.arange(quarter, dtype=F32) / quarter))

    def emb(p):
        ang = p.reshape(-1)[:, None] * omega[None, :]
        return jnp.concatenate([jnp.sin(ang), jnp.cos(ang)], axis=-1)

    return jnp.concatenate([emb(r), emb(col)], axis=-1)


def _pad_lanes(v, start):
    v = v.reshape(1, -1).astype(F32)
    return jnp.pad(v, ((0, 0), (start, LANES - start - v.shape[1])))


def kernel(x_prompt, x_sample, state_mlstm_C, state_mlstm_n, state_mlstm_m, state_ssd, c, c_ctx, w_in, w_o, mlstm_b_i, mlstm_b_f, mlstm_norm_g, ssd_conv_w, ssd_conv_b, ssd_dt_bias, ssd_A_log, ssd_D, ssd_norm_g, conv_dw_w, conv_dw_b, conv_ln_g, conv_ln_b, w_ada, b_ada, post1_g, post1_b, post2_g, post2_b, w_router, b_router, w_e_gate, w_e_up, w_e_down):
    cvec = jnp.concatenate([c_ctx[None, :], c, jnp.zeros((N_MOD_ROWS - 1 - DEC_BATCH, D_MODEL), F32)], axis=0)
    mod_all = _ada(cvec, w_ada, b_ada)
    x = _embed(x_prompt.reshape(N_CTX_TOK, D_MODEL), x_sample.reshape(N_LAT_TOK, D_MODEL), _grid_pos(DEC_SEQ))
    w_rt = w_router.T
    b_r = b_router.reshape(N_EXPERTS, 1)

    a_end = 4 * W_A + N_DIR * 2 * H_A
    b_end = a_end + W_B + W_XBC + N_DIR * H_B
    init = (state_mlstm_C,
            state_mlstm_n.reshape(DEC_BATCH, DEPTH, N_DIR * H_A, DH_A),
            jnp.broadcast_to(state_mlstm_m.reshape(DEC_BATCH, DEPTH, N_DIR * H_A, 1),
                             (DEC_BATCH, DEPTH, N_DIR * H_A, LANES)))
    st_c = st_n = st_m = st_h = None
    for l in range(DEPTH):
        w = w_in[l]
        w_main = jnp.concatenate([w[:, 0:4 * W_A], w[:, a_end:a_end + W_B + W_XBC], w[:, b_end:]], axis=1).astype(BF16)
        w_small = jnp.concatenate([w[:, 4 * W_A:a_end], w[:, b_end - N_DIR * H_B:b_end],
                                   jnp.zeros((D_MODEL, LANES - _DT_COL0 - N_DIR * H_B), F32)], axis=1).astype(BF16)
        gate_bias = (_pad_lanes(jnp.stack([mlstm_b_i[l], mlstm_b_f[l]], axis=1), 0)
                     + _pad_lanes(ssd_dt_bias[l], _DT_COL0))
        alog_row = _pad_lanes(ssd_A_log[l], _DT_COL0)
        mod = mod_all[l]

        proj, gates = _inproj(x, mod, w_main, w_small)

        m_norm = mlstm_norm_g[l].reshape(1, W_A)
        ha_c, st_c, st_n, st_m = _mlstm(proj, gates, gate_bias, m_norm, SEQ, BATCH, 0, l, prev=(st_c, st_n, st_m))
        (ha_l,) = _mlstm(proj, gates, gate_bias, m_norm, DEC_SEQ, DEC_BATCH, N_CTX_TOK, l, init=init)

        cw = jnp.pad(ssd_conv_w[l], ((0, 8 - SSM_CONV), (0, 0)))
        cb = ssd_conv_b[l].reshape(1, W_XBC)
        d_row = jnp.repeat(ssd_D[l], P_B).reshape(1, W_B)
        s_norm = ssd_norm_g[l].reshape(1, W_B)
        hb_c, st_h = _ssd(proj, gates, gate_bias, alog_row, cw, cb, d_row, s_norm, SEQ, BATCH, 0, l, prev=st_h)
        (hb_l,) = _ssd(proj, gates, gate_bias, alog_row, cw, cb, d_row, s_norm, DEC_SEQ, DEC_BATCH, N_CTX_TOK, l,
                       init=state_ssd)

        hc = _conf(proj, jnp.pad(conv_dw_w[l], ((0, 32 - CONV_W), (0, 0))), conv_dw_b[l].reshape(1, W_C),
                   conv_ln_g[l].reshape(1, W_C), conv_ln_b[l].reshape(1, W_C))

        x1, u2x, route = _outproj(ha_c, ha_l, hb_c, hb_l, hc, x, mod, w_o[l].astype(BF16), post1_g[l].reshape(1, D_MODEL),
                                  post1_b[l].reshape(1, D_MODEL), w_rt, b_r)
        row_tok, row_dst, tile_lo, tile_hi, n_used = _route_tables(route[0])
        y = _moe(u2x, row_tok, row_dst, tile_lo, tile_hi, n_used, w_e_gate, w_e_up, w_e_down, l)
        outs = _final(x1, y, mod, post2_g[l].reshape(1, D_MODEL), post2_b[l].reshape(1, D_MODEL), l == DEPTH - 1)
        x = outs[0]

    y_prompt = outs[0].reshape(BATCH, SEQ, D_MODEL)
    y_sample = outs[1].reshape(DEC_BATCH, DEC_SEQ, D_MODEL)
    return (y_prompt, y_sample, st_c, st_n.reshape(BATCH, DEPTH, N_DIR, H_A, DH_A),
            st_m[:, :, :, 0].reshape(BATCH, DEPTH, N_DIR, H_A), st_h)
```

```python
import functools

import jax
import jax.numpy as jnp
from jax import lax
from jax.experimental import pallas as pl
from jax.experimental.pallas import tpu as pltpu

D_MODEL = 1024
BATCH = 32
SEQ = 256
DEPTH = 2
DEC_BATCH = 2
DEC_SEQ = 1024
GRID_W = 64
N_DIR = 2
CHUNK = 128
H_A = 4
DH_A = 128
W_A = H_A * DH_A
H_B = 8
P_B = 64
W_B = H_B * P_B
G_B = 2
N_B = 128
W_XBC = W_B + 2 * G_B * N_B
SSM_CONV = 3
W_C = 512
CONV_W = 31
D_MIX = W_A + W_B + W_C
N_EXPERTS = 16
N_EGROUPS = 4
EXPERTS_PER_GROUP = N_EXPERTS // N_EGROUPS
D_FF_EXPERT = 512
ALPHA = (2 * DEPTH) ** 0.25
EPS = 1e-5
F32 = jnp.float32
BF16 = jnp.bfloat16

N_CTX_TOK = BATCH * SEQ
N_LAT_TOK = DEC_BATCH * DEC_SEQ
T_ALL = N_CTX_TOK + N_LAT_TOK
N_MOD_ROWS = 8
D_MAIN = 4 * W_A + W_B + W_XBC + 2 * W_C
LANES = 128
HALO = 16
CONV_ROWS = 256
TM_PROJ = 512
TM_MOE = 256
VMEM_LIMIT = 48 * 1024 * 1024
MOE_VMEM_LIMIT = 56 * 1024 * 1024

_NT = (((1,), (1,)), ((), ()))
_TN = (((0,), (0,)), ((), ()))


def _ln(x):
    mu = jnp.mean(x, axis=-1, keepdims=True)
    xc = x - mu
    var = jnp.mean(xc * xc, axis=-1, keepdims=True)
    return xc * lax.rsqrt(var + EPS)


def _sigmoid(x):
    return 1.0 / (1.0 + jnp.exp(-x))


def _silu(x):
    return x * _sigmoid(x)


def _softplus(x):
    return jnp.maximum(x, 0.0) + jnp.log1p(jnp.exp(-jnp.abs(x)))


def _dot(a, b):
    return jnp.dot(a, b, preferred_element_type=F32)


def _dotg(a, b, dims):
    return lax.dot_general(a, b, dims, preferred_element_type=F32)


def _tri_cumsum(tri, x):
    hi = x.astype(BF16)
    r1 = x - hi.astype(F32)
    mid = r1.astype(BF16)
    lo = (r1 - mid.astype(F32)).astype(BF16)
    return _dot(tri, hi) + _dot(tri, mid) + _dot(tri, lo)


def _tri_mask(d):
    row = lax.broadcasted_iota(jnp.int32, (CHUNK, CHUNK), 0)
    col = lax.broadcasted_iota(jnp.int32, (CHUNK, CHUNK), 1)
    return (row >= col) if d == 0 else (row <= col)


def _mod_row(row_start):
    return jnp.where(row_start < N_CTX_TOK, 0, 1 + (row_start - N_CTX_TOK) // DEC_SEQ)


def _cparams(n_axes):
    return pltpu.CompilerParams(dimension_semantics=("arbitrary",) * n_axes, vmem_limit_bytes=VMEM_LIMIT)


def _ada_kernel(c_ref, w_ref, b_ref, o_ref):
    o_ref[0] = _dot(_silu(c_ref[...]), w_ref[0]) + b_ref[0]


def _ada(cvec, w_ada, b_ada):
    tn = 1536
    return pl.pallas_call(
        _ada_kernel,
        grid=(DEPTH, 6 * D_MODEL // tn),
        in_specs=[
            pl.BlockSpec((N_MOD_ROWS, D_MODEL), lambda l, j: (0, 0)),
            pl.BlockSpec((1, D_MODEL, tn), lambda l, j: (l, 0, j)),
            pl.BlockSpec((1, 1, tn), lambda l, j: (l, 0, j)),
        ],
        out_specs=pl.BlockSpec((1, N_MOD_ROWS, tn), lambda l, j: (l, 0, j)),
        out_shape=jax.ShapeDtypeStruct((DEPTH, N_MOD_ROWS, 6 * D_MODEL), F32),
        compiler_params=_cparams(2),
        name="ada",
    )(cvec, w_ada, b_ada.reshape(DEPTH, 1, 6 * D_MODEL))


def _embed_kernel(xp_ref, xs_ref, pos_ref, o_ref):
    i = pl.program_id(0)

    @pl.when(i < N_CTX_TOK // DEC_SEQ)
    def _():
        o_ref[...] = xp_ref[...]

    @pl.when(i >= N_CTX_TOK // DEC_SEQ)
    def _():
        o_ref[...] = xs_ref[...] + pos_ref[...]


def _embed(xp, xs, pos):
    n_ctx_blk = N_CTX_TOK // DEC_SEQ
    return pl.pallas_call(
        _embed_kernel,
        grid=(T_ALL // DEC_SEQ,),
        in_specs=[
            pl.BlockSpec((DEC_SEQ, D_MODEL), lambda i: (jnp.minimum(i, n_ctx_blk - 1), 0)),
            pl.BlockSpec((DEC_SEQ, D_MODEL), lambda i: (jnp.maximum(i - n_ctx_blk, 0), 0)),
            pl.BlockSpec((DEC_SEQ, D_MODEL), lambda i: (0, 0)),
        ],
        out_specs=pl.BlockSpec((DEC_SEQ, D_MODEL), lambda i: (i, 0)),
        out_shape=jax.ShapeDtypeStruct((T_ALL, D_MODEL), F32),
        compiler_params=_cparams(1),
        name="embed",
    )(xp, xs, pos)


def _inproj_kernel(x_ref, mod_ref, wm_ref, wg_ref, om_ref, og_ref, u_scr, *, tm):
    i = pl.program_id(0)
    j = pl.program_id(1)

    @pl.when(j == 0)
    def _():
        r = _mod_row(i * tm)
        shift = mod_ref[pl.ds(r, 1), 0:D_MODEL]
        scale = mod_ref[pl.ds(r, 1), D_MODEL:2 * D_MODEL]
        u = (_ln(x_ref[...]) * (1.0 + scale) + shift).astype(BF16)
        u_scr[...] = u
        og_ref[...] = _dot(u, wg_ref[...])

    om_ref[...] = _dot(u_scr[...], wm_ref[...])


def _inproj(x, mod, w_main, w_small):
    tm, tn = TM_PROJ, D_MAIN // 2
    return pl.pallas_call(
        functools.partial(_inproj_kernel, tm=tm),
        grid=(T_ALL // tm, D_MAIN // tn),
        in_specs=[
            pl.BlockSpec((tm, D_MODEL), lambda i, j: (i, 0)),
            pl.BlockSpec((N_MOD_ROWS, 6 * D_MODEL), lambda i, j: (0, 0)),
            pl.BlockSpec((D_MODEL, tn), lambda i, j: (0, j)),
            pl.BlockSpec((D_MODEL, LANES), lambda i, j: (0, 0)),
        ],
        out_specs=[
            pl.BlockSpec((tm, tn), lambda i, j: (i, j)),
            pl.BlockSpec((tm, LANES), lambda i, j: (i, 0)),
        ],
        out_shape=[
            jax.ShapeDtypeStruct((T_ALL, D_MAIN), F32),
            jax.ShapeDtypeStruct((T_ALL, LANES), F32),
        ],
        scratch_shapes=[pltpu.VMEM((tm, D_MODEL), BF16)],
        compiler_params=_cparams(2),
        name="inproj",
    )(x, mod, w_main, w_small)


def _mlstm_kernel(*refs, seq, has_init, emit_state, layer):
    it = iter(refs)
    q_ref, k_ref, v_ref, o_ref, g_ref, gb_ref, ng_ref = (next(it) for _ in range(7))
    if has_init:
        c0_ref, n0_ref, m0_ref = (next(it) for _ in range(3))
    if emit_state and layer > 0:
        prev_refs = [next(it) for _ in range(3)]
    out_ref = next(it)
    if emit_state:
        co_ref, no_ref, mo_ref = (next(it) for _ in range(3))
    tg_scr, hf_scr, c_scr, n_scr, m_scr = (next(it) for _ in range(5))
    n_chunks = seq // CHUNK
    if emit_state and layer > 0:
        for prev_ref, st_ref in zip(prev_refs, (co_ref, no_ref, mo_ref)):
            st_ref[0, 0:layer] = prev_ref[0]

    y = g_ref[...] + gb_ref[...]
    lane = lax.broadcasted_iota(jnp.int32, y.shape, 1)
    is_forget = (lane < N_DIR * 2 * H_A) & ((lane & H_A) != 0)
    tg_scr[...] = jnp.where(is_forget, -_softplus(-y), y)

    def run_direction(d):
        for h in range(H_A):
            row = d * H_A + h
            if has_init:
                c_scr[h] = c0_ref[0, 0, d, h]
                n_scr[h:h + 1, :] = n0_ref[0, 0, row:row + 1, :]
                m_scr[h:h + 1, :] = m0_ref[0, 0, row:row + 1, :]
            else:
                c_scr[h] = jnp.zeros((DH_A, DH_A), F32)
                n_scr[h:h + 1, :] = jnp.zeros((1, DH_A), F32)
                m_scr[h:h + 1, :] = jnp.zeros((1, LANES), F32)

        mask = _tri_mask(d)
        tri = mask.astype(BF16)

        def body(ci, carry):
            c = ci if d == 0 else n_chunks - 1 - ci
            r0 = pl.multiple_of(c * CHUNK, CHUNK)
            rows = pl.ds(r0, CHUNK)
            tc = tg_scr[rows, :]
            cum = _tri_cumsum(tri, tc)
            cum_t = cum.T
            tc_t = tc.T
            for h in range(H_A):
                ci_col = d * 2 * H_A + h
                cf_col = ci_col + H_A
                cols = slice(h * DH_A, (h + 1) * DH_A)
                li_c = tc[:, ci_col:ci_col + 1]
                li_r = tc_t[ci_col:ci_col + 1, :]
                b_c = cum[:, cf_col:cf_col + 1]
                b_r = cum_t[cf_col:cf_col + 1, :]
                g = b_r[:, CHUNK - 1:CHUNK] if d == 0 else b_r[:, 0:1]
                qs = q_ref[rows, cols] * (DH_A ** -0.5)
                kf = k_ref[rows, cols]
                qb = qs.astype(BF16)
                kb = kf.astype(BF16)
                vb = v_ref[rows, cols].astype(BF16)
                c_st = c_scr[h]
                n_st = n_scr[h:h + 1, :]
                m_st = m_scr[h:h + 1, 0:1]

                dm = jnp.where(mask, b_c - b_r + li_r, -jnp.inf)
                inter = b_c + m_st
                m_t = jnp.maximum(inter, jnp.max(dm, axis=-1, keepdims=True))
                s = _dotg(qb, kb, _NT) * jnp.exp(dm - m_t)
                w_int = jnp.exp(inter - m_t)
                num = w_int * _dot(qb, c_st.astype(BF16)) + _dot(s.astype(BF16), vb)
                den = w_int * jnp.sum(qs * n_st, axis=-1, keepdims=True) + jnp.sum(s, axis=-1, keepdims=True)
                hh = num / jnp.maximum(jnp.abs(den), jnp.exp(-m_t))

                a_c = g - b_c + li_c
                a_r = g - b_r + li_r
                m_new = jnp.maximum(g + m_st, jnp.max(a_r, axis=-1, keepdims=True))
                w_prev = jnp.exp(g + m_st - m_new)
                kw = jnp.exp(a_c - m_new) * kf
                c_scr[h] = w_prev * c_st + _dotg(kw.astype(BF16), vb, _TN)
                n_scr[h:h + 1, :] = w_prev * n_st + jnp.sum(kw, axis=0, keepdims=True)
                m_scr[h:h + 1, :] = jnp.broadcast_to(m_new, (1, LANES))

                if d == 0:
                    hf_scr[rows, cols] = hh
                else:
                    hn = _ln(hf_scr[rows, cols] + hh) * ng_ref[:, cols]
                    out_ref[rows, cols] = (hn * _sigmoid(o_ref[rows, cols])).astype(out_ref.dtype)
            return carry

        lax.fori_loop(0, n_chunks, body, 0)

        if emit_state:
            for h in range(H_A):
                row = d * H_A + h
                co_ref[0, layer, d, h] = c_scr[h]
                no_ref[0, layer, row:row + 1, :] = n_scr[h:h + 1, :]
                mo_ref[0, layer, row:row + 1, :] = m_scr[h:h + 1, :]

    run_direction(0)
    run_direction(1)


def _mlstm(proj, gates, gate_bias, norm_g, seq, n_seq, row_off, layer, init=None, prev=None):
    blk0 = row_off // seq
    has_init = init is not None
    emit_state = not has_init
    col = lambda c: pl.BlockSpec((seq, W_A), lambda s: (s + blk0, c))
    st_shapes = [(N_DIR, H_A, DH_A, DH_A), (N_DIR * H_A, DH_A), (N_DIR * H_A, LANES)]

    def st_specs(n_layers, first):
        return [pl.BlockSpec((1, n_layers) + shp, lambda s, nd=len(shp): (s, first) + (0,) * nd) for shp in st_shapes]

    in_specs = [col(0), col(1), col(2), col(3),
                pl.BlockSpec((seq, LANES), lambda s: (s + blk0, 0)),
                pl.BlockSpec((1, LANES), lambda s: (0, 0)),
                pl.BlockSpec((1, W_A), lambda s: (0, 0))]
    args = [proj, proj, proj, proj, gates, gate_bias, norm_g]
    if has_init:
        in_specs += st_specs(1, layer)
        args += list(init)
    if emit_state and layer > 0:
        in_specs += st_specs(layer, 0)
        args += list(prev)
    out_specs = [pl.BlockSpec((seq, W_A), lambda s: (s, 0))]
    out_shape = [jax.ShapeDtypeStruct((n_seq * seq, W_A), BF16)]
    if emit_state:
        out_specs += st_specs(layer + 1, 0)
        out_shape += [jax.ShapeDtypeStruct((n_seq, layer + 1) + shp, F32) for shp in st_shapes]
    return pl.pallas_call(
        functools.partial(_mlstm_kernel, seq=seq, has_init=has_init, emit_state=emit_state, layer=layer),
        grid=(n_seq,),
        in_specs=in_specs,
        out_specs=out_specs,
        out_shape=out_shape,
        scratch_shapes=[pltpu.VMEM((seq, LANES), F32),
                        pltpu.VMEM((seq, W_A), F32),
                        pltpu.VMEM((H_A, DH_A, DH_A), F32),
                        pltpu.VMEM((8, DH_A), F32),
                        pltpu.VMEM((8, LANES), F32)],
        compiler_params=_cparams(1),
        name="mlstm_lat" if has_init else "mlstm_ctx",
    )(*args)


_DT_COL0 = N_DIR * 2 * H_A
_HG = H_B // G_B
_WG = _HG * P_B


def _ssd_kernel(*refs, seq, has_init, emit_state, layer):
    it = iter(refs)
    (z_ref, x_ref, b_ref, c_ref, g_ref, gb_ref, alog_ref, cw_ref, cb_ref, dsk_ref, ng_ref) = (
        next(it) for _ in range(11))
    if has_init:
        h0_ref = next(it)
    if emit_state and layer > 0:
        prev_ref = next(it)
    out_ref = next(it)
    if emit_state:
        ho_ref = next(it)
    pad_scr, xbc_scr, dt_scr, yf_scr, h_scr = (next(it) for _ in range(5))
    n_chunks = seq // CHUNK
    pad = 8
    if emit_state and layer > 0:
        ho_ref[0, 0:layer] = prev_ref[0]

    pad_scr[0:pad, :] = jnp.zeros((pad, W_XBC), F32)
    pad_scr[pad + seq:2 * pad + seq, :] = jnp.zeros((pad, W_XBC), F32)
    pad_scr[pad:pad + seq, 0:W_B] = x_ref[...]
    pad_scr[pad:pad + seq, W_B:W_B + G_B * N_B] = b_ref[...]
    pad_scr[pad:pad + seq, W_B + G_B * N_B:W_XBC] = c_ref[...]
    for c in range(n_chunks):
        for lb in range(W_XBC // 256):
            cols = slice(lb * 256, (lb + 1) * 256)
            acc = cb_ref[:, cols]
            for k in range(SSM_CONV):
                r = c * CHUNK + pad - SSM_CONV // 2 + k
                acc = acc + cw_ref[k:k + 1, cols] * pad_scr[r:r + CHUNK, cols]
            xbc_scr[c * CHUNK:(c + 1) * CHUNK, cols] = _silu(acc)

    dt_scr[...] = _softplus(g_ref[...] + gb_ref[...])
    a_row = -jnp.exp(alog_ref[...])

    lane_blk = lax.broadcasted_iota(jnp.int32, (CHUNK, _WG), 1) // P_B
    row_blk = lax.broadcasted_iota(jnp.int32, (_WG, 1), 0) // P_B

    def run_direction(d):
        for g in range(G_B):
            for hh in range(_HG):
                head = g * _HG + hh
                if has_init:
                    h_scr[g, hh * P_B:(hh + 1) * P_B, :] = h0_ref[0, 0, d, head]
                else:
                    h_scr[g, hh * P_B:(hh + 1) * P_B, :] = jnp.zeros((P_B, N_B), F32)

        mask = _tri_mask(d)
        tri = mask.astype(BF16)

        def body(ci, carry):
            c = ci if d == 0 else n_chunks - 1 - ci
            r0 = pl.multiple_of(c * CHUNK, CHUNK)
            rows = pl.ds(r0, CHUNK)
            dtc = dt_scr[rows, :]
            cum = _tri_cumsum(tri, dtc * a_row)
            cum_t = cum.T
            dt_t = dtc.T
            for g in range(G_B):
                xg = xbc_scr[rows, g * _WG:(g + 1) * _WG]
                xgb = xg.astype(BF16)
                bg = xbc_scr[rows, W_B + g * N_B:W_B + (g + 1) * N_B].astype(BF16)
                cg = xbc_scr[rows, W_B + G_B * N_B + g * N_B:W_B + G_B * N_B + (g + 1) * N_B].astype(BF16)
                h_st = h_scr[g]
                cb = _dotg(cg, bg, _NT)
                ch = _dotg(cg, h_st.astype(BF16), _NT)
                y = jnp.zeros((CHUNK, _WG), F32)
                e_full = jnp.zeros((CHUNK, _WG), F32)
                w_full = jnp.zeros((CHUNK, _WG), F32)
                decay = jnp.zeros((_WG, 1), F32)
                for hh in range(_HG):
                    colh = _DT_COL0 + d * H_B + g * _HG + hh
                    b_c = cum[:, colh:colh + 1]
                    b_r = cum_t[colh:colh + 1, :]
                    dt_c = dtc[:, colh:colh + 1]
                    dt_r = dt_t[colh:colh + 1, :]
                    gt = b_r[:, CHUNK - 1:CHUNK] if d == 0 else b_r[:, 0:1]
                    seg = jnp.where(mask, b_c - b_r, -jnp.inf)
                    sm = cb * jnp.exp(seg) * dt_r
                    sel = lane_blk == hh
                    y = jnp.where(sel, _dot(sm.astype(BF16), xgb), y)
                    e_full = jnp.where(sel, jnp.exp(b_c), e_full)
                    w_full = jnp.where(sel, jnp.exp(gt - b_c) * dt_c, w_full)
                    decay = jnp.where(row_blk == hh, jnp.exp(gt), decay)
                y = y + e_full * ch
                h_scr[g] = decay * h_st + _dotg((w_full * xg).astype(BF16), bg, _TN)
                gcols = slice(g * _WG, (g + 1) * _WG)
                if d == 0:
                    yf_scr[rows, gcols] = y
                else:
                    yf_scr[rows, gcols] = yf_scr[rows, gcols] + y + dsk_ref[:, gcols] * xg
            if d == 1:
                yz = yf_scr[rows, :] * _silu(z_ref[rows, :])
                rms = lax.rsqrt(jnp.mean(yz * yz, axis=-1, keepdims=True) + EPS)
                out_ref[rows, :] = (yz * rms * ng_ref[...]).astype(out_ref.dtype)
            return carry

        lax.fori_loop(0, n_chunks, body, 0)

        if emit_state:
            for g in range(G_B):
                for hh in range(_HG):
                    ho_ref[0, layer, d, g * _HG + hh] = h_scr[g, hh * P_B:(hh + 1) * P_B, :]

    run_direction(0)
    run_direction(1)


def _ssd(proj, gates, gate_bias, alog_row, conv_w, conv_b, d_row, norm_g, seq, n_seq, row_off, layer,
         init=None, prev=None):
    blk0 = row_off // seq
    has_init = init is not None
    emit_state = not has_init
    zc = W_A * 4 // W_B
    bc = (4 * W_A + 2 * W_B) // (G_B * N_B)
    const = lambda shape: pl.BlockSpec(shape, lambda s: (0,) * len(shape))
    st_spec = lambda n_layers, first: pl.BlockSpec((1, n_layers, N_DIR, H_B, P_B, N_B),
                                                   lambda s: (s, first, 0, 0, 0, 0))
    in_specs = [pl.BlockSpec((seq, W_B), lambda s: (s + blk0, zc)),
                pl.BlockSpec((seq, W_B), lambda s: (s + blk0, zc + 1)),
                pl.BlockSpec((seq, G_B * N_B), lambda s: (s + blk0, bc)),
                pl.BlockSpec((seq, G_B * N_B), lambda s: (s + blk0, bc + 1)),
                pl.BlockSpec((seq, LANES), lambda s: (s + blk0, 0)),
                const((1, LANES)), const((1, LANES)), const((8, W_XBC)), const((1, W_XBC)),
                const((1, W_B)), const((1, W_B))]
    args = [proj, proj, proj, proj, gates, gate_bias, alog_row, conv_w, conv_b, d_row, norm_g]
    if has_init:
        in_specs.append(st_spec(1, layer))
        args.append(init)
    if emit_state and layer > 0:
        in_specs.append(st_spec(layer, 0))
        args.append(prev)
    out_specs = [pl.BlockSpec((seq, W_B), lambda s: (s, 0))]
    out_shape = [jax.ShapeDtypeStruct((n_seq * seq, W_B), BF16)]
    if emit_state:
        out_specs.append(st_spec(layer + 1, 0))
        out_shape.append(jax.ShapeDtypeStruct((n_seq, layer + 1, N_DIR, H_B, P_B, N_B), F32))
    return pl.pallas_call(
        functools.partial(_ssd_kernel, seq=seq, has_init=has_init, emit_state=emit_state, layer=layer),
        grid=(n_seq,),
        in_specs=in_specs,
        out_specs=out_specs,
        out_shape=out_shape,
        scratch_shapes=[pltpu.VMEM((seq + 16, W_XBC), F32),
                        pltpu.VMEM((seq, W_XBC), F32),
                        pltpu.VMEM((seq, LANES), F32),
                        pltpu.VMEM((seq, W_B), F32),
                        pltpu.VMEM((G_B, _WG, N_B), F32)],
        compiler_params=_cparams(1),
        name="ssd_lat" if has_init else "ssd_ctx",
    )(*args)


def _conf_kernel(a_ref, ap_ref, an_ref, g_ref, gp_ref, gn_ref, w_ref, b_ref, lg_ref, lb_ref, out_ref, pad_scr):
    i = pl.program_id(0)
    n_ctx_blk = N_CTX_TOK // CONV_ROWS
    blk_per_lat = DEC_SEQ // CONV_ROWS
    j = (i - n_ctx_blk) % blk_per_lat
    is_ctx = i < n_ctx_blk
    keep_prev = jnp.where(is_ctx | (j == 0), 0.0, 1.0)
    keep_next = jnp.where(is_ctx | (j == blk_per_lat - 1), 0.0, 1.0)
    pad_scr[0:HALO, :] = ap_ref[...] * _sigmoid(gp_ref[...]) * keep_prev
    pad_scr[HALO:HALO + CONV_ROWS, :] = a_ref[...] * _sigmoid(g_ref[...])
    pad_scr[HALO + CONV_ROWS:2 * HALO + CONV_ROWS, :] = an_ref[...] * _sigmoid(gn_ref[...]) * keep_next
    rc = 64
    for c in range(CONV_ROWS // rc):
        acc = jnp.broadcast_to(b_ref[...], (rc, W_C))
        for k in range(CONV_W):
            r = c * rc + HALO - CONV_W // 2 + k
            acc = acc + w_ref[k:k + 1, :] * pad_scr[r:r + rc, :]
        u = _ln(acc) * lg_ref[...] + lb_ref[...]
        out_ref[c * rc:(c + 1) * rc, :] = _silu(u).astype(out_ref.dtype)


def _conf(proj, dw_w, dw_b, ln_g, ln_b):
    assert SEQ == CONV_ROWS and DEC_SEQ % CONV_ROWS == 0
    ac = (4 * W_A + W_B + W_XBC) // W_C
    hb = CONV_ROWS // HALO
    n_halo = T_ALL // HALO
    const = lambda shape: pl.BlockSpec(shape, lambda i: (0,) * len(shape))

    def specs(c):
        return [pl.BlockSpec((CONV_ROWS, W_C), lambda i: (i, c)),
                pl.BlockSpec((HALO, W_C), lambda i: (jnp.maximum(i * hb - 1, 0), c)),
                pl.BlockSpec((HALO, W_C), lambda i: (jnp.minimum((i + 1) * hb, n_halo - 1), c))]

    return pl.pallas_call(
        _conf_kernel,
        grid=(T_ALL // CONV_ROWS,),
        in_specs=specs(ac) + specs(ac + 1) + [const((32, W_C)), const((1, W_C)), const((1, W_C)), const((1, W_C))],
        out_specs=pl.BlockSpec((CONV_ROWS, W_C), lambda i: (i, 0)),
        out_shape=jax.ShapeDtypeStruct((T_ALL, W_C), BF16),
        scratch_shapes=[pltpu.VMEM((CONV_ROWS + 2 * HALO, W_C), F32)],
        compiler_params=_cparams(1),
        name="conf",
    )(proj, proj, proj, proj, proj, proj, dw_w, dw_b, ln_g, ln_b)


_PAIRS = [(a, b) for a in range(EXPERTS_PER_GROUP) for b in range(a + 1, EXPERTS_PER_GROUP)]
N_CLASSES = N_EGROUPS * len(_PAIRS)
N_FCHUNK = D_MODEL // LANES


def _store_token_tiles(ref, val):
    n = val.shape[0]
    for c in range(N_FCHUNK):
        ref[pl.ds(c, n, stride=N_FCHUNK), :] = val[:, c * LANES:(c + 1) * LANES]


def _load_token_tiles(ref, n):
    return jnp.concatenate([ref[pl.ds(c, n, stride=N_FCHUNK), :] for c in range(N_FCHUNK)], axis=1)
N_MOE_ROWS = T_ALL + N_CLASSES * TM_MOE
N_MOE_TILES = N_MOE_ROWS // TM_MOE


def _outproj_kernel(hac_ref, hal_ref, hbc_ref, hbl_ref, hc_ref, x_ref, mod_ref, wo_ref, pg_ref, pb_ref,
                    wr_ref, br_ref, x1_ref, u2_ref, wc_ref, rt_ref, *, tm):
    i = pl.program_id(0)
    r = _mod_row(i * tm)
    gate1 = mod_ref[pl.ds(r, 1), 2 * D_MODEL:3 * D_MODEL]
    shift2 = mod_ref[pl.ds(r, 1), 3 * D_MODEL:4 * D_MODEL]
    scale2 = mod_ref[pl.ds(r, 1), 4 * D_MODEL:5 * D_MODEL]
    is_ctx = i < N_CTX_TOK // tm
    ha = jnp.where(is_ctx, hac_ref[...], hal_ref[...])
    hb = jnp.where(is_ctx, hbc_ref[...], hbl_ref[...])
    mix = (_dot(ha, wo_ref[0:W_A, :]) + _dot(hb, wo_ref[W_A:W_A + W_B, :])
           + _dot(hc_ref[...], wo_ref[W_A + W_B:D_MIX, :]))
    x1 = _ln(ALPHA * x_ref[...] + gate1 * mix) * pg_ref[...] + pb_ref[...]
    x1_ref[...] = x1
    u2 = _ln(x1) * (1.0 + scale2) + shift2
    _store_token_tiles(u2_ref, u2)

    logits = lax.dot_general(wr_ref[...], u2, _NT, precision=lax.Precision.HIGHEST,
                             preferred_element_type=F32) + br_ref[...]
    ex = jnp.exp(logits - jnp.max(logits, axis=0, keepdims=True))
    probs = ex / jnp.sum(ex, axis=0, keepdims=True)
    scores = []
    for g in range(N_EGROUPS):
        p = [probs[g * EXPERTS_PER_GROUP + e:g * EXPERTS_PER_GROUP + e + 1, :] for e in range(EXPERTS_PER_GROUP)]
        best = p[0] + p[1]
        for a in range(EXPERTS_PER_GROUP):
            for b in range(a + 1, EXPERTS_PER_GROUP):
                if (a, b) != (0, 1):
                    best = jnp.maximum(best, p[a] + p[b])
        scores.append(best)
    gmax = functools.reduce(jnp.maximum, scores)
    sel = jnp.full(gmax.shape, N_EGROUPS - 1, jnp.int32)
    for g in range(N_EGROUPS - 2, -1, -1):
        sel = jnp.where(scores[g] == gmax, g, sel)
    eidx = lax.broadcasted_iota(jnp.int32, probs.shape, 0)
    pm = jnp.where((eidx // EXPERTS_PER_GROUP) == sel, probs, -jnp.inf)
    p1 = jnp.max(pm, axis=0, keepdims=True)
    i1 = jnp.min(jnp.where(pm == p1, eidx, N_EXPERTS), axis=0, keepdims=True)
    pm2 = jnp.where(eidx == i1, -jnp.inf, pm)
    p2 = jnp.max(pm2, axis=0, keepdims=True)
    i2 = jnp.min(jnp.where(pm2 == p2, eidx, N_EXPERTS), axis=0, keepdims=True)
    den = p1 + p2
    first_lo = i1 < i2
    w_lo = jnp.where(first_lo, p1, p2) / den
    w_hi = jnp.where(first_lo, p2, p1) / den
    a = jnp.minimum(i1, i2) - sel * EXPERTS_PER_GROUP
    b = jnp.maximum(i1, i2) - sel * EXPERTS_PER_GROUP
    pair = jnp.zeros_like(a)
    for k, (pa, pb_) in enumerate(_PAIRS):
        pair = jnp.where((a == pa) & (b == pb_), k, pair)
    cls = sel * len(_PAIRS) + pair
    rt_ref[...] = jnp.broadcast_to(cls, rt_ref.shape)
    wrow = lax.broadcasted_iota(jnp.int32, (LANES, tm), 0)
    wt = jnp.where(wrow == 0, w_lo, jnp.where(wrow == 1, w_hi, 0.0))
    wc_ref[...] = wt.T


def _outproj(ha_c, ha_l, hb_c, hb_l, hc, x, mod, w_o, pg, pb, w_rt, b_r):
    tm = TM_PROJ
    n_ctx_blk = N_CTX_TOK // tm
    const = lambda shape: pl.BlockSpec(shape, lambda i: (0,) * len(shape))
    rows = lambda w: pl.BlockSpec((tm, w), lambda i: (i, 0))
    ctx_rows = lambda w: pl.BlockSpec((tm, w), lambda i: (jnp.minimum(i, n_ctx_blk - 1), 0))
    lat_rows = lambda w: pl.BlockSpec((tm, w), lambda i: (jnp.maximum(i - n_ctx_blk, 0), 0))
    return pl.pallas_call(
        functools.partial(_outproj_kernel, tm=tm),
        grid=(T_ALL // tm,),
        in_specs=[ctx_rows(W_A), lat_rows(W_A), ctx_rows(W_B), lat_rows(W_B),
                  rows(W_C), rows(D_MODEL), const((N_MOD_ROWS, 6 * D_MODEL)),
                  const((D_MIX, D_MODEL)), const((1, D_MODEL)), const((1, D_MODEL)),
                  const((N_EXPERTS, D_MODEL)), const((N_EXPERTS, 1))],
        out_specs=[rows(D_MODEL), pl.BlockSpec((tm * N_FCHUNK, LANES), lambda i: (i, 0)), rows(LANES),
                   pl.BlockSpec((8, tm), lambda i: (0, i))],
        out_shape=[jax.ShapeDtypeStruct((T_ALL, D_MODEL), F32),
                   jax.ShapeDtypeStruct((T_ALL * N_FCHUNK, LANES), F32),
                   jax.ShapeDtypeStruct((T_ALL, LANES), F32),
                   jax.ShapeDtypeStruct((8, T_ALL), jnp.int32)],
        compiler_params=_cparams(1),
        name="outproj",
    )(ha_c, ha_l, hb_c, hb_l, hc, x, mod, w_o, pg, pb, w_rt, b_r)


N_Y_ROWS = T_ALL + 2 * TM_MOE


def _moe_kernel(tok_ref, dst_ref, tlo_ref, thi_ref, nused_ref,
                u_hbm, wrow_ref, wgl_ref, wul_ref, wdl_ref, wgh_ref, wuh_ref, wdh_ref, y_hbm,
                xbuf0, xbuf1, ybuf0, ybuf1, gsem, ssem):
    del tlo_ref, thi_ref
    i = pl.program_id(0)
    n_used = nused_ref[0]
    tile_rows = TM_MOE * N_FCHUNK

    def token_tile(ref, tok):
        return ref.at[pl.ds(pl.multiple_of(tok * N_FCHUNK, N_FCHUNK), N_FCHUNK), :]

    def gather_rows(t, xb, sem):
        for r in range(TM_MOE):
            pltpu.make_async_copy(token_tile(u_hbm, tok_ref[t * TM_MOE + r]), token_tile(xb, r), sem).start()

    def scatter_rows(t, yb, sem):
        for r in range(TM_MOE):
            pltpu.make_async_copy(token_tile(yb, r), token_tile(y_hbm, dst_ref[(t + 1) * TM_MOE + r]), sem).start()

    def wait_gather(xb, sem):
        pltpu.make_async_copy(u_hbm.at[pl.ds(0, tile_rows), :], xb, sem).wait()

    def wait_scatter(yb, sem):
        pltpu.make_async_copy(yb, y_hbm.at[pl.ds(0, tile_rows), :], sem).wait()

    def step(s, xb, xb_next, yb, yb_prev):
        @pl.when(i > 0)
        def _():
            wait_scatter(yb, ssem.at[s])

        wait_gather(xb, gsem.at[s])
        x = _load_token_tiles(xb, TM_MOE).astype(BF16)
        w_lo = wrow_ref[:, 0:1]
        w_hi = wrow_ref[:, 1:2]
        gather_rows(jnp.minimum(i + 1, n_used - 1), xb_next, gsem.at[1 - s])
        scatter_rows(i - 1, yb_prev, ssem.at[1 - s])

        def ffn(wg_ref, wu_ref, wd_ref, w):
            act = _silu(_dot(x, wg_ref[0, 0].astype(BF16))) * _dot(x, wu_ref[0, 0].astype(BF16)) * w
            return _dot(act.astype(BF16), wd_ref[0, 0].astype(BF16))

        _store_token_tiles(yb, ffn(wgl_ref, wul_ref, wdl_ref, w_lo) + ffn(wgh_ref, wuh_ref, wdh_ref, w_hi))

        @pl.when(i == n_used - 1)
        def _():
            scatter_rows(i, yb, ssem.at[s])
            wait_scatter(yb, ssem.at[s])
            wait_scatter(yb_prev, ssem.at[1 - s])
            wait_gather(xb_next, gsem.at[1 - s])

    @pl.when(i == 0)
    def _():
        ybuf1[...] = jnp.zeros(ybuf1.shape, F32)
        fill = pltpu.make_async_copy(ybuf1, y_hbm.at[pl.ds(T_ALL * N_FCHUNK, tile_rows), :], ssem.at[0])
        fill.start()
        fill.wait()
        gather_rows(0, xbuf0, gsem.at[0])

    @pl.when((i < n_used) & (i % 2 == 0))
    def _():
        step(0, xbuf0, xbuf1, ybuf0, ybuf1)

    @pl.when((i < n_used) & (i % 2 == 1))
    def _():
        step(1, xbuf1, xbuf0, ybuf1, ybuf0)


def _moe(u2t, w_sorted, row_tok, row_dst, tile_lo, tile_hi, n_used, w_gate, w_up, w_down, layer):
    lo = lambda shape: pl.BlockSpec(shape, lambda i, tok, dst, tlo, thi, nused: (layer, tlo[i], 0, 0))
    hi = lambda shape: pl.BlockSpec(shape, lambda i, tok, dst, tlo, thi, nused: (layer, thi[i], 0, 0))
    up_shape = (1, 1, D_MODEL, D_FF_EXPERT)
    down_shape = (1, 1, D_FF_EXPERT, D_MODEL)
    tile_buf = pltpu.VMEM((TM_MOE * N_FCHUNK, LANES), F32)
    grid_spec = pltpu.PrefetchScalarGridSpec(
        num_scalar_prefetch=5,
        grid=(N_MOE_TILES,),
        in_specs=[pl.BlockSpec(memory_space=pl.ANY),
                  pl.BlockSpec((TM_MOE, LANES), lambda i, *_: (i, 0)),
                  lo(up_shape), lo(up_shape), lo(down_shape),
                  hi(up_shape), hi(up_shape), hi(down_shape)],
        out_specs=pl.BlockSpec(memory_space=pl.ANY),
        scratch_shapes=[tile_buf, tile_buf, tile_buf, tile_buf,
                        pltpu.SemaphoreType.DMA((2,)),
                        pltpu.SemaphoreType.DMA((2,))],
    )
    return pl.pallas_call(
        _moe_kernel,
        grid_spec=grid_spec,
        out_shape=jax.ShapeDtypeStruct((N_Y_ROWS * N_FCHUNK, LANES), F32),
        compiler_params=pltpu.CompilerParams(dimension_semantics=("arbitrary",), vmem_limit_bytes=MOE_VMEM_LIMIT,
                                             has_side_effects=True),
        name="moe",
    )(row_tok, row_dst, tile_lo, tile_hi, n_used, u2t, w_sorted, w_gate, w_up, w_down, w_gate, w_up, w_down)


def _route_tables(cls):
    order = jnp.argsort(cls, stable=True).astype(jnp.int32)
    cids = jnp.arange(N_CLASSES, dtype=jnp.int32)
    counts = jnp.sum((cls[:, None] == cids[None, :]).astype(jnp.int32), axis=0)
    offs = jnp.cumsum(counts) - counts
    padded = (counts + TM_MOE - 1) // TM_MOE * TM_MOE
    ends = jnp.cumsum(padded)
    offs_p = ends - padded
    n_used = ends[-1] // TM_MOE
    tile_start = jnp.arange(N_MOE_TILES, dtype=jnp.int32) * TM_MOE
    tile_cls = jnp.sum((ends[None, :] <= jnp.minimum(tile_start, ends[-1] - TM_MOE)[:, None]).astype(jnp.int32),
                       axis=1)
    row = jnp.arange(N_MOE_ROWS, dtype=jnp.int32)
    row_cls = jnp.repeat(tile_cls, TM_MOE)
    k = row - offs_p[row_cls]
    valid = (k < counts[row_cls]) & (row < ends[-1])
    src = offs[row_cls] + jnp.clip(k, 0, jnp.maximum(counts[row_cls] - 1, 0))
    row_tok = order[jnp.clip(src, 0, T_ALL - 1)]
    spare = T_ALL + ((row // TM_MOE) % 2) * TM_MOE + row % TM_MOE
    row_dst = jnp.where(valid, row_tok, spare)
    row_dst = jnp.concatenate([T_ALL + TM_MOE + jnp.arange(TM_MOE, dtype=jnp.int32), row_dst])
    grp = tile_cls // len(_PAIRS)
    pair = tile_cls % len(_PAIRS)
    tile_lo = grp * EXPERTS_PER_GROUP + jnp.array([p[0] for p in _PAIRS], jnp.int32)[pair]
    tile_hi = grp * EXPERTS_PER_GROUP + jnp.array([p[1] for p in _PAIRS], jnp.int32)[pair]
    i32 = lambda v: v.astype(jnp.int32)
    return i32(row_tok), i32(row_dst), i32(tile_lo), i32(tile_hi), i32(n_used).reshape(1)


def _final_kernel(x1_ref, y_ref, mod_ref, pg_ref, pb_ref, *o_refs, tm):
    i = pl.program_id(0)
    r = _mod_row(i * tm)
    gate2 = mod_ref[pl.ds(r, 1), 5 * D_MODEL:6 * D_MODEL]
    out = _ln(ALPHA * x1_ref[...] + gate2 * _load_token_tiles(y_ref, tm)) * pg_ref[...] + pb_ref[...]
    if len(o_refs) == 1:
        o_refs[0][...] = out
    else:
        @pl.when(i < N_CTX_TOK // tm)
        def _():
            o_refs[0][...] = out

        @pl.when(i >= N_CTX_TOK // tm)
        def _():
            o_refs[1][...] = out


def _final(x1, y, mod, pg, pb, split):
    tm = TM_PROJ
    n_ctx_blk = N_CTX_TOK // tm
    const = lambda shape: pl.BlockSpec(shape, lambda i: (0,) * len(shape))
    rows = pl.BlockSpec((tm, D_MODEL), lambda i: (i, 0))
    if split:
        out_specs = [pl.BlockSpec((tm, D_MODEL), lambda i: (jnp.minimum(i, n_ctx_blk - 1), 0)),
                     pl.BlockSpec((tm, D_MODEL), lambda i: (jnp.maximum(i - n_ctx_blk, 0), 0))]
        out_shape = [jax.ShapeDtypeStruct((N_CTX_TOK, D_MODEL), F32), jax.ShapeDtypeStruct((N_LAT_TOK, D_MODEL), F32)]
    else:
        out_specs = [rows]
        out_shape = [jax.ShapeDtypeStruct((T_ALL, D_MODEL), F32)]
    return pl.pallas_call(
        functools.partial(_final_kernel, tm=tm),
        grid=(T_ALL // tm,),
        in_specs=[rows, pl.BlockSpec((tm * N_FCHUNK, LANES), lambda i: (i, 0)),
                  const((N_MOD_ROWS, 6 * D_MODEL)), const((1, D_MODEL)), const((1, D_MODEL))],
        out_specs=out_specs,
        out_shape=out_shape,
        compiler_params=_cparams(1),
        name="final",
    )(x1, y, mod, pg, pb)


def _grid_pos(n_tok):
    rows = n_tok // GRID_W
    r, col = jnp.meshgrid(jnp.arange(rows, dtype=F32), jnp.arange(GRID_W, dtype=F32), indexing='ij')
    quarter = D_MODEL // 4
    omega = 1.0 / (10000.0 ** (jnp.arange(quarter, dtype=F32) / quarter))

    def emb(p):
        ang = p.reshape(-1)[:, None] * omega[None, :]
        return jnp.concatenate([jnp.sin(ang), jnp.cos(ang)], axis=-1)

    return jnp.concatenate([emb(r), emb(col)], axis=-1)


def _pad_lanes(v, start):
    v = v.reshape(1, -1).astype(F32)
    return jnp.pad(v, ((0, 0), (start, LANES - start - v.shape[1])))


def kernel(x_prompt, x_sample, state_mlstm_C, state_mlstm_n, state_mlstm_m, state_ssd, c, c_ctx, w_in, w_o, mlstm_b_i, mlstm_b_f, mlstm_norm_g, ssd_conv_w, ssd_conv_b, ssd_dt_bias, ssd_A_log, ssd_D, ssd_norm_g, conv_dw_w, conv_dw_b, conv_ln_g, conv_ln_b, w_ada, b_ada, post1_g, post1_b, post2_g, post2_b, w_router, b_router, w_e_gate, w_e_up, w_e_down):
    cvec = jnp.concatenate([c_ctx[None, :], c, jnp.zeros((N_MOD_ROWS - 1 - DEC_BATCH, D_MODEL), F32)], axis=0)
    mod_all = _ada(cvec, w_ada, b_ada)
    x = _embed(x_prompt.reshape(N_CTX_TOK, D_MODEL), x_sample.reshape(N_LAT_TOK, D_MODEL), _grid_pos(DEC_SEQ))
    w_rt = w_router.T
    b_r = b_router.reshape(N_EXPERTS, 1)

    a_end = 4 * W_A + N_DIR * 2 * H_A
    b_end = a_end + W_B + W_XBC + N_DIR * H_B
    init = (state_mlstm_C,
            state_mlstm_n.reshape(DEC_BATCH, DEPTH, N_DIR * H_A, DH_A),
            jnp.broadcast_to(state_mlstm_m.reshape(DEC_BATCH, DEPTH, N_DIR * H_A, 1),
                             (DEC_BATCH, DEPTH, N_DIR * H_A, LANES)))
    st_c = st_n = st_m = st_h = None
    for l in range(DEPTH):
        w = w_in[l]
        w_main = jnp.concatenate([w[:, 0:4 * W_A], w[:, a_end:a_end + W_B + W_XBC], w[:, b_end:]], axis=1).astype(BF16)
        w_small = jnp.concatenate([w[:, 4 * W_A:a_end], w[:, b_end - N_DIR * H_B:b_end],
                                   jnp.zeros((D_MODEL, LANES - _DT_COL0 - N_DIR * H_B), F32)], axis=1).astype(BF16)
        gate_bias = (_pad_lanes(jnp.stack([mlstm_b_i[l], mlstm_b_f[l]], axis=1), 0)
                     + _pad_lanes(ssd_dt_bias[l], _DT_COL0))
        alog_row = _pad_lanes(ssd_A_log[l], _DT_COL0)
        mod = mod_all[l]

        proj, gates = _inproj(x, mod, w_main, w_small)

        m_norm = mlstm_norm_g[l].reshape(1, W_A)
        ha_c, st_c, st_n, st_m = _mlstm(proj, gates, gate_bias, m_norm, SEQ, BATCH, 0, l, prev=(st_c, st_n, st_m))
        (ha_l,) = _mlstm(proj, gates, gate_bias, m_norm, DEC_SEQ, DEC_BATCH, N_CTX_TOK, l, init=init)

        cw = jnp.pad(ssd_conv_w[l], ((0, 8 - SSM_CONV), (0, 0)))
        cb = ssd_conv_b[l].reshape(1, W_XBC)
        d_row = jnp.repeat(ssd_D[l], P_B).reshape(1, W_B)
        s_norm = ssd_norm_g[l].reshape(1, W_B)
        hb_c, st_h = _ssd(proj, gates, gate_bias, alog_row, cw, cb, d_row, s_norm, SEQ, BATCH, 0, l, prev=st_h)
        (hb_l,) = _ssd(proj, gates, gate_bias, alog_row, cw, cb, d_row, s_norm, DEC_SEQ, DEC_BATCH, N_CTX_TOK, l,
                       init=state_ssd)

        hc = _conf(proj, jnp.pad(conv_dw_w[l], ((0, 32 - CONV_W), (0, 0))), conv_dw_b[l].reshape(1, W_C),
                   conv_ln_g[l].reshape(1, W_C), conv_ln_b[l].reshape(1, W_C))

        x1, u2t, wcol, route = _outproj(ha_c, ha_l, hb_c, hb_l, hc, x, mod, w_o[l].astype(BF16), post1_g[l].reshape(1, D_MODEL),
                                  post1_b[l].reshape(1, D_MODEL), w_rt, b_r)
        row_tok, row_dst, tile_lo, tile_hi, n_used = _route_tables(route[0])
        y = _moe(u2t, wcol[row_tok], row_tok, row_dst, tile_lo, tile_hi, n_used, w_e_gate, w_e_up, w_e_down, l)
        outs = _final(x1, y, mod, post2_g[l].reshape(1, D_MODEL), post2_b[l].reshape(1, D_MODEL), l == DEPTH - 1)
        x = outs[0]

    y_prompt = outs[0].reshape(BATCH, SEQ, D_MODEL)
    y_sample = outs[1].reshape(DEC_BATCH, DEC_SEQ, D_MODEL)
    return (y_prompt, y_sample, st_c, st_n.reshape(BATCH, DEPTH, N_DIR, H_A, DH_A),
            st_m[:, :, :, 0].reshape(BATCH, DEPTH, N_DIR, H_A), st_h)
```

```python
import functools

import jax
import jax.numpy as jnp
from jax import lax
from jax.experimental import pallas as pl
from jax.experimental.pallas import tpu as pltpu

D_MODEL = 1024
BATCH = 32
SEQ = 256
DEPTH = 2
DEC_BATCH = 2
DEC_SEQ = 1024
GRID_W = 64
N_DIR = 2
CHUNK = 128
H_A = 4
DH_A = 128
W_A = H_A * DH_A
H_B = 8
P_B = 64
W_B = H_B * P_B
G_B = 2
N_B = 128
W_XBC = W_B + 2 * G_B * N_B
SSM_CONV = 3
W_C = 512
CONV_W = 31
D_MIX = W_A + W_B + W_C
N_EXPERTS = 16
N_EGROUPS = 4
EXPERTS_PER_GROUP = N_EXPERTS // N_EGROUPS
D_FF_EXPERT = 512
ALPHA = (2 * DEPTH) ** 0.25
EPS = 1e-5
F32 = jnp.float32
BF16 = jnp.bfloat16

N_CTX_TOK = BATCH * SEQ
N_LAT_TOK = DEC_BATCH * DEC_SEQ
T_ALL = N_CTX_TOK + N_LAT_TOK
N_MOD_ROWS = 8
D_MAIN = 4 * W_A + W_B + W_XBC + 2 * W_C
LANES = 128
HALO = 16
CONV_ROWS = 256
TM_PROJ = 512
TM_MOE = 256
VMEM_LIMIT = 48 * 1024 * 1024
MOE_VMEM_LIMIT = 56 * 1024 * 1024

_NT = (((1,), (1,)), ((), ()))
_TN = (((0,), (0,)), ((), ()))


def _ln(x):
    mu = jnp.mean(x, axis=-1, keepdims=True)
    xc = x - mu
    var = jnp.mean(xc * xc, axis=-1, keepdims=True)
    return xc * lax.rsqrt(var + EPS)


def _sigmoid(x):
    return 1.0 / (1.0 + jnp.exp(-x))


def _silu(x):
    return x * _sigmoid(x)


def _softplus(x):
    return jnp.maximum(x, 0.0) + jnp.log1p(jnp.exp(-jnp.abs(x)))


def _dot(a, b):
    return jnp.dot(a, b, preferred_element_type=F32)


def _dotg(a, b, dims):
    return lax.dot_general(a, b, dims, preferred_element_type=F32)


def _tri_cumsum(tri, x):
    hi = x.astype(BF16)
    r1 = x - hi.astype(F32)
    mid = r1.astype(BF16)
    lo = (r1 - mid.astype(F32)).astype(BF16)
    return _dot(tri, hi) + _dot(tri, mid) + _dot(tri, lo)


def _tri_mask(d):
    row = lax.broadcasted_iota(jnp.int32, (CHUNK, CHUNK), 0)
    col = lax.broadcasted_iota(jnp.int32, (CHUNK, CHUNK), 1)
    return (row >= col) if d == 0 else (row <= col)


def _mod_row(row_start):
    return jnp.where(row_start < N_CTX_TOK, 0, 1 + (row_start - N_CTX_TOK) // DEC_SEQ)


def _cparams(n_axes):
    return pltpu.CompilerParams(dimension_semantics=("arbitrary",) * n_axes, vmem_limit_bytes=VMEM_LIMIT)


def _ada_kernel(c_ref, w_ref, b_ref, o_ref):
    o_ref[0] = _dot(_silu(c_ref[...]), w_ref[0]) + b_ref[0]


def _ada(cvec, w_ada, b_ada):
    tn = 1536
    return pl.pallas_call(
        _ada_kernel,
        grid=(DEPTH, 6 * D_MODEL // tn),
        in_specs=[
            pl.BlockSpec((N_MOD_ROWS, D_MODEL), lambda l, j: (0, 0)),
            pl.BlockSpec((1, D_MODEL, tn), lambda l, j: (l, 0, j)),
            pl.BlockSpec((1, 1, tn), lambda l, j: (l, 0, j)),
        ],
        out_specs=pl.BlockSpec((1, N_MOD_ROWS, tn), lambda l, j: (l, 0, j)),
        out_shape=jax.ShapeDtypeStruct((DEPTH, N_MOD_ROWS, 6 * D_MODEL), F32),
        compiler_params=_cparams(2),
        name="ada",
    )(cvec, w_ada, b_ada.reshape(DEPTH, 1, 6 * D_MODEL))


def _embed_kernel(xp_ref, xs_ref, pos_ref, o_ref):
    i = pl.program_id(0)

    @pl.when(i < N_CTX_TOK // DEC_SEQ)
    def _():
        o_ref[...] = xp_ref[...]

    @pl.when(i >= N_CTX_TOK // DEC_SEQ)
    def _():
        o_ref[...] = xs_ref[...] + pos_ref[...]


def _embed(xp, xs, pos):
    n_ctx_blk = N_CTX_TOK // DEC_SEQ
    return pl.pallas_call(
        _embed_kernel,
        grid=(T_ALL // DEC_SEQ,),
        in_specs=[
            pl.BlockSpec((DEC_SEQ, D_MODEL), lambda i: (jnp.minimum(i, n_ctx_blk - 1), 0)),
            pl.BlockSpec((DEC_SEQ, D_MODEL), lambda i: (jnp.maximum(i - n_ctx_blk, 0), 0)),
            pl.BlockSpec((DEC_SEQ, D_MODEL), lambda i: (0, 0)),
        ],
        out_specs=pl.BlockSpec((DEC_SEQ, D_MODEL), lambda i: (i, 0)),
        out_shape=jax.ShapeDtypeStruct((T_ALL, D_MODEL), F32),
        compiler_params=_cparams(1),
        name="embed",
    )(xp, xs, pos)


def _inproj_kernel(x_ref, mod_ref, wm_ref, wg_ref, om_ref, og_ref, u_scr, *, tm):
    i = pl.program_id(0)
    j = pl.program_id(1)

    @pl.when(j == 0)
    def _():
        r = _mod_row(i * tm)
        shift = mod_ref[pl.ds(r, 1), 0:D_MODEL]
        scale = mod_ref[pl.ds(r, 1), D_MODEL:2 * D_MODEL]
        u = (_ln(x_ref[...]) * (1.0 + scale) + shift).astype(BF16)
        u_scr[...] = u
        og_ref[...] = _dot(u, wg_ref[...])

    om_ref[...] = _dot(u_scr[...], wm_ref[...])


def _inproj(x, mod, w_main, w_small):
    tm, tn = TM_PROJ, D_MAIN // 2
    return pl.pallas_call(
        functools.partial(_inproj_kernel, tm=tm),
        grid=(T_ALL // tm, D_MAIN // tn),
        in_specs=[
            pl.BlockSpec((tm, D_MODEL), lambda i, j: (i, 0)),
            pl.BlockSpec((N_MOD_ROWS, 6 * D_MODEL), lambda i, j: (0, 0)),
            pl.BlockSpec((D_MODEL, tn), lambda i, j: (0, j)),
            pl.BlockSpec((D_MODEL, LANES), lambda i, j: (0, 0)),
        ],
        out_specs=[
            pl.BlockSpec((tm, tn), lambda i, j: (i, j)),
            pl.BlockSpec((tm, LANES), lambda i, j: (i, 0)),
        ],
        out_shape=[
            jax.ShapeDtypeStruct((T_ALL, D_MAIN), F32),
            jax.ShapeDtypeStruct((T_ALL, LANES), F32),
        ],
        scratch_shapes=[pltpu.VMEM((tm, D_MODEL), BF16)],
        compiler_params=_cparams(2),
        name="inproj",
    )(x, mod, w_main, w_small)


def _mlstm_kernel(*refs, seq, has_init, emit_state, layer):
    it = iter(refs)
    q_ref, k_ref, v_ref, o_ref, g_ref, gb_ref, ng_ref = (next(it) for _ in range(7))
    if has_init:
        c0_ref, n0_ref, m0_ref = (next(it) for _ in range(3))
    if emit_state and layer > 0:
        prev_refs = [next(it) for _ in range(3)]
    out_ref = next(it)
    if emit_state:
        co_ref, no_ref, mo_ref = (next(it) for _ in range(3))
    tg_scr, hf_scr, c_scr, n_scr, m_scr = (next(it) for _ in range(5))
    n_chunks = seq // CHUNK
    if emit_state and layer > 0:
        for prev_ref, st_ref in zip(prev_refs, (co_ref, no_ref, mo_ref)):
            st_ref[0, 0:layer] = prev_ref[0]

    y = g_ref[...] + gb_ref[...]
    lane = lax.broadcasted_iota(jnp.int32, y.shape, 1)
    is_forget = (lane < N_DIR * 2 * H_A) & ((lane & H_A) != 0)
    tg_scr[...] = jnp.where(is_forget, -_softplus(-y), y)

    def run_direction(d):
        for h in range(H_A):
            row = d * H_A + h
            if has_init:
                c_scr[h] = c0_ref[0, 0, d, h]
                n_scr[h:h + 1, :] = n0_ref[0, 0, row:row + 1, :]
                m_scr[h:h + 1, :] = m0_ref[0, 0, row:row + 1, :]
            else:
                c_scr[h] = jnp.zeros((DH_A, DH_A), F32)
                n_scr[h:h + 1, :] = jnp.zeros((1, DH_A), F32)
                m_scr[h:h + 1, :] = jnp.zeros((1, LANES), F32)

        mask = _tri_mask(d)
        tri = mask.astype(BF16)

        def body(ci, carry):
            c = ci if d == 0 else n_chunks - 1 - ci
            r0 = pl.multiple_of(c * CHUNK, CHUNK)
            rows = pl.ds(r0, CHUNK)
            tc = tg_scr[rows, :]
            cum = _tri_cumsum(tri, tc)
            cum_t = cum.T
            tc_t = tc.T
            for h in range(H_A):
                ci_col = d * 2 * H_A + h
                cf_col = ci_col + H_A
                cols = slice(h * DH_A, (h + 1) * DH_A)
                li_c = tc[:, ci_col:ci_col + 1]
                li_r = tc_t[ci_col:ci_col + 1, :]
                b_c = cum[:, cf_col:cf_col + 1]
                b_r = cum_t[cf_col:cf_col + 1, :]
                g = b_r[:, CHUNK - 1:CHUNK] if d == 0 else b_r[:, 0:1]
                qs = q_ref[rows, cols] * (DH_A ** -0.5)
                kf = k_ref[rows, cols]
                qb = qs.astype(BF16)
                kb = kf.astype(BF16)
                vb = v_ref[rows, cols].astype(BF16)
                c_st = c_scr[h]
                n_st = n_scr[h:h + 1, :]
                m_st = m_scr[h:h + 1, 0:1]

                dm = jnp.where(mask, b_c - b_r + li_r, -jnp.inf)
                inter = b_c + m_st
                m_t = jnp.maximum(inter, jnp.max(dm, axis=-1, keepdims=True))
                s = _dotg(qb, kb, _NT) * jnp.exp(dm - m_t)
                w_int = jnp.exp(inter - m_t)
                num = w_int * _dot(qb, c_st.astype(BF16)) + _dot(s.astype(BF16), vb)
                den = w_int * jnp.sum(qs * n_st, axis=-1, keepdims=True) + jnp.sum(s, axis=-1, keepdims=True)
                hh = num / jnp.maximum(jnp.abs(den), jnp.exp(-m_t))

                a_c = g - b_c + li_c
                a_r = g - b_r + li_r
                m_new = jnp.maximum(g + m_st, jnp.max(a_r, axis=-1, keepdims=True))
                w_prev = jnp.exp(g + m_st - m_new)
                kw = jnp.exp(a_c - m_new) * kf
                c_scr[h] = w_prev * c_st + _dotg(kw.astype(BF16), vb, _TN)
                n_scr[h:h + 1, :] = w_prev * n_st + jnp.sum(kw, axis=0, keepdims=True)
                m_scr[h:h + 1, :] = jnp.broadcast_to(m_new, (1, LANES))

                if d == 0:
                    hf_scr[rows, cols] = hh
                else:
                    hn = _ln(hf_scr[rows, cols] + hh) * ng_ref[:, cols]
                    out_ref[rows, cols] = (hn * _sigmoid(o_ref[rows, cols])).astype(out_ref.dtype)
            return carry

        lax.fori_loop(0, n_chunks, body, 0)

        if emit_state:
            for h in range(H_A):
                row = d * H_A + h
                co_ref[0, layer, d, h] = c_scr[h]
                no_ref[0, layer, row:row + 1, :] = n_scr[h:h + 1, :]
                mo_ref[0, layer, row:row + 1, :] = m_scr[h:h + 1, :]

    run_direction(0)
    run_direction(1)


def _mlstm(proj, gates, gate_bias, norm_g, seq, n_seq, row_off, layer, init=None, prev=None):
    blk0 = row_off // seq
    has_init = init is not None
    emit_state = not has_init
    col = lambda c: pl.BlockSpec((seq, W_A), lambda s: (s + blk0, c))
    st_shapes = [(N_DIR, H_A, DH_A, DH_A), (N_DIR * H_A, DH_A), (N_DIR * H_A, LANES)]

    def st_specs(n_layers, first):
        return [pl.BlockSpec((1, n_layers) + shp, lambda s, nd=len(shp): (s, first) + (0,) * nd) for shp in st_shapes]

    in_specs = [col(0), col(1), col(2), col(3),
                pl.BlockSpec((seq, LANES), lambda s: (s + blk0, 0)),
                pl.BlockSpec((1, LANES), lambda s: (0, 0)),
                pl.BlockSpec((1, W_A), lambda s: (0, 0))]
    args = [proj, proj, proj, proj, gates, gate_bias, norm_g]
    if has_init:
        in_specs += st_specs(1, layer)
        args += list(init)
    if emit_state and layer > 0:
        in_specs += st_specs(layer, 0)
        args += list(prev)
    out_specs = [pl.BlockSpec((seq, W_A), lambda s: (s, 0))]
    out_shape = [jax.ShapeDtypeStruct((n_seq * seq, W_A), BF16)]
    if emit_state:
        out_specs += st_specs(layer + 1, 0)
        out_shape += [jax.ShapeDtypeStruct((n_seq, layer + 1) + shp, F32) for shp in st_shapes]
    return pl.pallas_call(
        functools.partial(_mlstm_kernel, seq=seq, has_init=has_init, emit_state=emit_state, layer=layer),
        grid=(n_seq,),
        in_specs=in_specs,
        out_specs=out_specs,
        out_shape=out_shape,
        scratch_shapes=[pltpu.VMEM((seq, LANES), F32),
                        pltpu.VMEM((seq, W_A), F32),
                        pltpu.VMEM((H_A, DH_A, DH_A), F32),
                        pltpu.VMEM((8, DH_A), F32),
                        pltpu.VMEM((8, LANES), F32)],
        compiler_params=_cparams(1),
        name="mlstm_lat" if has_init else "mlstm_ctx",
    )(*args)


_DT_COL0 = N_DIR * 2 * H_A
_HG = H_B // G_B
_WG = _HG * P_B


def _ssd_kernel(*refs, seq, has_init, emit_state, layer):
    it = iter(refs)
    (z_ref, x_ref, b_ref, c_ref, g_ref, gb_ref, alog_ref, cw_ref, cb_ref, dsk_ref, ng_ref) = (
        next(it) for _ in range(11))
    if has_init:
        h0_ref = next(it)
    if emit_state and layer > 0:
        prev_ref = next(it)
    out_ref = next(it)
    if emit_state:
        ho_ref = next(it)
    pad_scr, xbc_scr, dt_scr, yf_scr, h_scr = (next(it) for _ in range(5))
    n_chunks = seq // CHUNK
    pad = 8
    if emit_state and layer > 0:
        ho_ref[0, 0:layer] = prev_ref[0]

    pad_scr[0:pad, :] = jnp.zeros((pad, W_XBC), F32)
    pad_scr[pad + seq:2 * pad + seq, :] = jnp.zeros((pad, W_XBC), F32)
    pad_scr[pad:pad + seq, 0:W_B] = x_ref[...]
    pad_scr[pad:pad + seq, W_B:W_B + G_B * N_B] = b_ref[...]
    pad_scr[pad:pad + seq, W_B + G_B * N_B:W_XBC] = c_ref[...]
    for c in range(n_chunks):
        for lb in range(W_XBC // 256):
            cols = slice(lb * 256, (lb + 1) * 256)
            acc = cb_ref[:, cols]
            for k in range(SSM_CONV):
                r = c * CHUNK + pad - SSM_CONV // 2 + k
                acc = acc + cw_ref[k:k + 1, cols] * pad_scr[r:r + CHUNK, cols]
            xbc_scr[c * CHUNK:(c + 1) * CHUNK, cols] = _silu(acc)

    dt_scr[...] = _softplus(g_ref[...] + gb_ref[...])
    a_row = -jnp.exp(alog_ref[...])

    lane_blk = lax.broadcasted_iota(jnp.int32, (CHUNK, _WG), 1) // P_B
    row_blk = lax.broadcasted_iota(jnp.int32, (_WG, 1), 0) // P_B

    def run_direction(d):
        for g in range(G_B):
            for hh in range(_HG):
                head = g * _HG + hh
                if has_init:
                    h_scr[g, hh * P_B:(hh + 1) * P_B, :] = h0_ref[0, 0, d, head]
                else:
                    h_scr[g, hh * P_B:(hh + 1) * P_B, :] = jnp.zeros((P_B, N_B), F32)

        mask = _tri_mask(d)
        tri = mask.astype(BF16)

        def body(ci, carry):
            c = ci if d == 0 else n_chunks - 1 - ci
            r0 = pl.multiple_of(c * CHUNK, CHUNK)
            rows = pl.ds(r0, CHUNK)
            dtc = dt_scr[rows, :]
            cum = _tri_cumsum(tri, dtc * a_row)
            cum_t = cum.T
            dt_t = dtc.T
            for g in range(G_B):
                xg = xbc_scr[rows, g * _WG:(g + 1) * _WG]
                xgb = xg.astype(BF16)
                bg = xbc_scr[rows, W_B + g * N_B:W_B + (g + 1) * N_B].astype(BF16)
                cg = xbc_scr[rows, W_B + G_B * N_B + g * N_B:W_B + G_B * N_B + (g + 1) * N_B].astype(BF16)
                h_st = h_scr[g]
                cb = _dotg(cg, bg, _NT)
                ch = _dotg(cg, h_st.astype(BF16), _NT)
                y = jnp.zeros((CHUNK, _WG), F32)
                e_full = jnp.zeros((CHUNK, _WG), F32)
                w_full = jnp.zeros((CHUNK, _WG), F32)
                decay = jnp.zeros((_WG, 1), F32)
                for hh in range(_HG):
                    colh = _DT_COL0 + d * H_B + g * _HG + hh
                    b_c = cum[:, colh:colh + 1]
                    b_r = cum_t[colh:colh + 1, :]
                    dt_c = dtc[:, colh:colh + 1]
                    dt_r = dt_t[colh:colh + 1, :]
                    gt = b_r[:, CHUNK - 1:CHUNK] if d == 0 else b_r[:, 0:1]
                    seg = jnp.where(mask, b_c - b_r, -jnp.inf)
                    sm = cb * jnp.exp(seg) * dt_r
                    sel = lane_blk == hh
                    y = jnp.where(sel, _dot(sm.astype(BF16), xgb), y)
                    e_full = jnp.where(sel, jnp.exp(b_c), e_full)
                    w_full = jnp.where(sel, jnp.exp(gt - b_c) * dt_c, w_full)
                    decay = jnp.where(row_blk == hh, jnp.exp(gt), decay)
                y = y + e_full * ch
                h_scr[g] = decay * h_st + _dotg((w_full * xg).astype(BF16), bg, _TN)
                gcols = slice(g * _WG, (g + 1) * _WG)
                if d == 0:
                    yf_scr[rows, gcols] = y
                else:
                    yf_scr[rows, gcols] = yf_scr[rows, gcols] + y + dsk_ref[:, gcols] * xg
            if d == 1:
                yz = yf_scr[rows, :] * _silu(z_ref[rows, :])
                rms = lax.rsqrt(jnp.mean(yz * yz, axis=-1, keepdims=True) + EPS)
                out_ref[rows, :] = (yz * rms * ng_ref[...]).astype(out_ref.dtype)
            return carry

        lax.fori_loop(0, n_chunks, body, 0)

        if emit_state:
            for g in range(G_B):
                for hh in range(_HG):
                    ho_ref[0, layer, d, g * _HG + hh] = h_scr[g, hh * P_B:(hh + 1) * P_B, :]

    run_direction(0)
    run_direction(1)


def _ssd(proj, gates, gate_bias, alog_row, conv_w, conv_b, d_row, norm_g, seq, n_seq, row_off, layer,
         init=None, prev=None):
    blk0 = row_off // seq
    has_init = init is not None
    emit_state = not has_init
    zc = W_A * 4 // W_B
    bc = (4 * W_A + 2 * W_B) // (G_B * N_B)
    const = lambda shape: pl.BlockSpec(shape, lambda s: (0,) * len(shape))
    st_spec = lambda n_layers, first: pl.BlockSpec((1, n_layers, N_DIR, H_B, P_B, N_B),
                                                   lambda s: (s, first, 0, 0, 0, 0))
    in_specs = [pl.BlockSpec((seq, W_B), lambda s: (s + blk0, zc)),
                pl.BlockSpec((seq, W_B), lambda s: (s + blk0, zc + 1)),
                pl.BlockSpec((seq, G_B * N_B), lambda s: (s + blk0, bc)),
                pl.BlockSpec((seq, G_B * N_B), lambda s: (s + blk0, bc + 1)),
                pl.BlockSpec((seq, LANES), lambda s: (s + blk0, 0)),
                const((1, LANES)), const((1, LANES)), const((8, W_XBC)), const((1, W_XBC)),
                const((1, W_B)), const((1, W_B))]
    args = [proj, proj, proj, proj, gates, gate_bias, alog_row, conv_w, conv_b, d_row, norm_g]
    if has_init:
        in_specs.append(st_spec(1, layer))
        args.append(init)
    if emit_state and layer > 0:
        in_specs.append(st_spec(layer, 0))
        args.append(prev)
    out_specs = [pl.BlockSpec((seq, W_B), lambda s: (s, 0))]
    out_shape = [jax.ShapeDtypeStruct((n_seq * seq, W_B), BF16)]
    if emit_state:
        out_specs.append(st_spec(layer + 1, 0))
        out_shape.append(jax.ShapeDtypeStruct((n_seq, layer + 1, N_DIR, H_B, P_B, N_B), F32))
    return pl.pallas_call(
        functools.partial(_ssd_kernel, seq=seq, has_init=has_init, emit_state=emit_state, layer=layer),
        grid=(n_seq,),
        in_specs=in_specs,
        out_specs=out_specs,
        out_shape=out_shape,
        scratch_shapes=[pltpu.VMEM((seq + 16, W_XBC), F32),
                        pltpu.VMEM((seq, W_XBC), F32),
                        pltpu.VMEM((seq, LANES), F32),
                        pltpu.VMEM((seq, W_B), F32),
                        pltpu.VMEM((G_B, _WG, N_B), F32)],
        compiler_params=_cparams(1),
        name="ssd_lat" if has_init else "ssd_ctx",
    )(*args)


def _conf_kernel(a_ref, ap_ref, an_ref, g_ref, gp_ref, gn_ref, w_ref, b_ref, lg_ref, lb_ref, out_ref, pad_scr):
    i = pl.program_id(0)
    n_ctx_blk = N_CTX_TOK // CONV_ROWS
    blk_per_lat = DEC_SEQ // CONV_ROWS
    j = (i - n_ctx_blk) % blk_per_lat
    is_ctx = i < n_ctx_blk
    keep_prev = jnp.where(is_ctx | (j == 0), 0.0, 1.0)
    keep_next = jnp.where(is_ctx | (j == blk_per_lat - 1), 0.0, 1.0)
    pad_scr[0:HALO, :] = ap_ref[...] * _sigmoid(gp_ref[...]) * keep_prev
    pad_scr[HALO:HALO + CONV_ROWS, :] = a_ref[...] * _sigmoid(g_ref[...])
    pad_scr[HALO + CONV_ROWS:2 * HALO + CONV_ROWS, :] = an_ref[...] * _sigmoid(gn_ref[...]) * keep_next
    rc = 64
    for c in range(CONV_ROWS // rc):
        acc = jnp.broadcast_to(b_ref[...], (rc, W_C))
        for k in range(CONV_W):
            r = c * rc + HALO - CONV_W // 2 + k
            acc = acc + w_ref[k:k + 1, :] * pad_scr[r:r + rc, :]
        u = _ln(acc) * lg_ref[...] + lb_ref[...]
        out_ref[c * rc:(c + 1) * rc, :] = _silu(u).astype(out_ref.dtype)


def _conf(proj, dw_w, dw_b, ln_g, ln_b):
    assert SEQ == CONV_ROWS and DEC_SEQ % CONV_ROWS == 0
    ac = (4 * W_A + W_B + W_XBC) // W_C
    hb = CONV_ROWS // HALO
    n_halo = T_ALL // HALO
    const = lambda shape: pl.BlockSpec(shape, lambda i: (0,) * len(shape))

    def specs(c):
        return [pl.BlockSpec((CONV_ROWS, W_C), lambda i: (i, c)),
                pl.BlockSpec((HALO, W_C), lambda i: (jnp.maximum(i * hb - 1, 0), c)),
                pl.BlockSpec((HALO, W_C), lambda i: (jnp.minimum((i + 1) * hb, n_halo - 1), c))]

    return pl.pallas_call(
        _conf_kernel,
        grid=(T_ALL // CONV_ROWS,),
        in_specs=specs(ac) + specs(ac + 1) + [const((32, W_C)), const((1, W_C)), const((1, W_C)), const((1, W_C))],
        out_specs=pl.BlockSpec((CONV_ROWS, W_C), lambda i: (i, 0)),
        out_shape=jax.ShapeDtypeStruct((T_ALL, W_C), BF16),
        scratch_shapes=[pltpu.VMEM((CONV_ROWS + 2 * HALO, W_C), F32)],
        compiler_params=_cparams(1),
        name="conf",
    )(proj, proj, proj, proj, proj, proj, dw_w, dw_b, ln_g, ln_b)


_PAIRS = [(a, b) for a in range(EXPERTS_PER_GROUP) for b in range(a + 1, EXPERTS_PER_GROUP)]
N_CLASSES = N_EGROUPS * len(_PAIRS)
N_FCHUNK = D_MODEL // LANES


def _store_token_tiles(ref, val):
    n = val.shape[0]
    for c in range(N_FCHUNK):
        ref[pl.ds(c, n, stride=N_FCHUNK), :] = val[:, c * LANES:(c + 1) * LANES]


def _load_token_tiles(ref, n):
    return jnp.concatenate([ref[pl.ds(c, n, stride=N_FCHUNK), :] for c in range(N_FCHUNK)], axis=1)
N_MOE_ROWS = T_ALL + N_CLASSES * TM_MOE
N_MOE_TILES = N_MOE_ROWS // TM_MOE


def _outproj_kernel(hac_ref, hal_ref, hbc_ref, hbl_ref, hc_ref, x_ref, mod_ref, wo_ref, pg_ref, pb_ref,
                    wr_ref, br_ref, x1_ref, u2_ref, rt_ref, *, tm):
    i = pl.program_id(0)
    r = _mod_row(i * tm)
    gate1 = mod_ref[pl.ds(r, 1), 2 * D_MODEL:3 * D_MODEL]
    shift2 = mod_ref[pl.ds(r, 1), 3 * D_MODEL:4 * D_MODEL]
    scale2 = mod_ref[pl.ds(r, 1), 4 * D_MODEL:5 * D_MODEL]
    is_ctx = i < N_CTX_TOK // tm
    ha = jnp.where(is_ctx, hac_ref[...], hal_ref[...])
    hb = jnp.where(is_ctx, hbc_ref[...], hbl_ref[...])
    mix = (_dot(ha, wo_ref[0:W_A, :]) + _dot(hb, wo_ref[W_A:W_A + W_B, :])
           + _dot(hc_ref[...], wo_ref[W_A + W_B:D_MIX, :]))
    x1 = _ln(ALPHA * x_ref[...] + gate1 * mix) * pg_ref[...] + pb_ref[...]
    x1_ref[...] = x1
    u2 = _ln(x1) * (1.0 + scale2) + shift2
    _store_token_tiles(u2_ref, u2)

    logits = lax.dot_general(wr_ref[...], u2, _NT, precision=lax.Precision.HIGHEST,
                             preferred_element_type=F32) + br_ref[...]
    ex = jnp.exp(logits - jnp.max(logits, axis=0, keepdims=True))
    probs = ex / jnp.sum(ex, axis=0, keepdims=True)
    scores = []
    for g in range(N_EGROUPS):
        p = [probs[g * EXPERTS_PER_GROUP + e:g * EXPERTS_PER_GROUP + e + 1, :] for e in range(EXPERTS_PER_GROUP)]
        best = p[0] + p[1]
        for a in range(EXPERTS_PER_GROUP):
            for b in range(a + 1, EXPERTS_PER_GROUP):
                if (a, b) != (0, 1):
                    best = jnp.maximum(best, p[a] + p[b])
        scores.append(best)
    gmax = functools.reduce(jnp.maximum, scores)
    sel = jnp.full(gmax.shape, N_EGROUPS - 1, jnp.int32)
    for g in range(N_EGROUPS - 2, -1, -1):
        sel = jnp.where(scores[g] == gmax, g, sel)
    eidx = lax.broadcasted_iota(jnp.int32, probs.shape, 0)
    pm = jnp.where((eidx // EXPERTS_PER_GROUP) == sel, probs, -jnp.inf)
    p1 = jnp.max(pm, axis=0, keepdims=True)
    i1 = jnp.min(jnp.where(pm == p1, eidx, N_EXPERTS), axis=0, keepdims=True)
    pm2 = jnp.where(eidx == i1, -jnp.inf, pm)
    p2 = jnp.max(pm2, axis=0, keepdims=True)
    i2 = jnp.min(jnp.where(pm2 == p2, eidx, N_EXPERTS), axis=0, keepdims=True)
    den = p1 + p2
    first_lo = i1 < i2
    w_lo = jnp.where(first_lo, p1, p2) / den
    w_hi = jnp.where(first_lo, p2, p1) / den
    a = jnp.minimum(i1, i2) - sel * EXPERTS_PER_GROUP
    b = jnp.maximum(i1, i2) - sel * EXPERTS_PER_GROUP
    pair = jnp.zeros_like(a)
    for k, (pa, pb_) in enumerate(_PAIRS):
        pair = jnp.where((a == pa) & (b == pb_), k, pair)
    cls = sel * len(_PAIRS) + pair
    rrow = lax.broadcasted_iota(jnp.int32, rt_ref.shape, 0)
    rt_ref[...] = jnp.where(rrow == 0, cls.astype(F32), jnp.where(rrow == 1, w_lo, jnp.where(rrow == 2, w_hi, 0.0)))


def _outproj(ha_c, ha_l, hb_c, hb_l, hc, x, mod, w_o, pg, pb, w_rt, b_r):
    tm = TM_PROJ
    n_ctx_blk = N_CTX_TOK // tm
    const = lambda shape: pl.BlockSpec(shape, lambda i: (0,) * len(shape))
    rows = lambda w: pl.BlockSpec((tm, w), lambda i: (i, 0))
    ctx_rows = lambda w: pl.BlockSpec((tm, w), lambda i: (jnp.minimum(i, n_ctx_blk - 1), 0))
    lat_rows = lambda w: pl.BlockSpec((tm, w), lambda i: (jnp.maximum(i - n_ctx_blk, 0), 0))
    return pl.pallas_call(
        functools.partial(_outproj_kernel, tm=tm),
        grid=(T_ALL // tm,),
        in_specs=[ctx_rows(W_A), lat_rows(W_A), ctx_rows(W_B), lat_rows(W_B),
                  rows(W_C), rows(D_MODEL), const((N_MOD_ROWS, 6 * D_MODEL)),
                  const((D_MIX, D_MODEL)), const((1, D_MODEL)), const((1, D_MODEL)),
                  const((N_EXPERTS, D_MODEL)), const((N_EXPERTS, 1))],
        out_specs=[rows(D_MODEL), pl.BlockSpec((tm * N_FCHUNK, LANES), lambda i: (i, 0)),
                   pl.BlockSpec((8, tm), lambda i: (0, i))],
        out_shape=[jax.ShapeDtypeStruct((T_ALL, D_MODEL), F32),
                   jax.ShapeDtypeStruct((T_ALL * N_FCHUNK, LANES), F32),
                   jax.ShapeDtypeStruct((8, T_ALL), F32)],
        compiler_params=_cparams(1),
        name="outproj",
    )(ha_c, ha_l, hb_c, hb_l, hc, x, mod, w_o, pg, pb, w_rt, b_r)


N_Y_ROWS = T_ALL + 2 * TM_MOE


def _moe_kernel(tok_ref, dst_ref, tlo_ref, thi_ref, nused_ref,
                u_hbm, wrow_ref, wgl_ref, wul_ref, wdl_ref, wgh_ref, wuh_ref, wdh_ref, y_hbm,
                xbuf0, xbuf1, ybuf0, ybuf1, gsem, ssem):
    del tlo_ref, thi_ref
    i = pl.program_id(0)
    n_used = nused_ref[0]
    tile_rows = TM_MOE * N_FCHUNK

    def token_tile(ref, tok):
        return ref.at[pl.ds(pl.multiple_of(tok * N_FCHUNK, N_FCHUNK), N_FCHUNK), :]

    def gather_rows(t, xb, sem):
        for r in range(TM_MOE):
            pltpu.make_async_copy(token_tile(u_hbm, tok_ref[t * TM_MOE + r]), token_tile(xb, r), sem).start()

    def scatter_rows(t, yb, sem):
        for r in range(TM_MOE):
            pltpu.make_async_copy(token_tile(yb, r), token_tile(y_hbm, dst_ref[(t + 1) * TM_MOE + r]), sem).start()

    def wait_gather(xb, sem):
        pltpu.make_async_copy(u_hbm.at[pl.ds(0, tile_rows), :], xb, sem).wait()

    def wait_scatter(yb, sem):
        pltpu.make_async_copy(yb, y_hbm.at[pl.ds(0, tile_rows), :], sem).wait()

    def step(s, xb, xb_next, yb, yb_prev):
        @pl.when(i > 0)
        def _():
            wait_scatter(yb, ssem.at[s])

        wait_gather(xb, gsem.at[s])

        @pl.when(n_used > 0)
        def _():
            gather_rows(jnp.minimum(i + 1, n_used - 1), xb_next, gsem.at[1 - s])
            scatter_rows(i - 1, yb_prev, ssem.at[1 - s])

        x = _load_token_tiles(xb, TM_MOE).astype(BF16)
        w_lo = wrow_ref[:, 0:1]
        w_hi = wrow_ref[:, 1:2]

        def ffn(wg_ref, wu_ref, wd_ref, w):
            act = _silu(_dot(x, wg_ref[0, 0].astype(BF16))) * _dot(x, wu_ref[0, 0].astype(BF16)) * w
            return _dot(act.astype(BF16), wd_ref[0, 0].astype(BF16))

        _store_token_tiles(yb, ffn(wgl_ref, wul_ref, wdl_ref, w_lo) + ffn(wgh_ref, wuh_ref, wdh_ref, w_hi))

        @pl.when(i == n_used - 1)
        def _():
            scatter_rows(i, yb, ssem.at[s])
            wait_scatter(yb, ssem.at[s])
            wait_scatter(yb_prev, ssem.at[1 - s])
            wait_gather(xb_next, gsem.at[1 - s])

    @pl.when(i == 0)
    def _():
        ybuf1[...] = jnp.zeros(ybuf1.shape, F32)
        fill = pltpu.make_async_copy(ybuf1, y_hbm.at[pl.ds(T_ALL * N_FCHUNK, tile_rows), :], ssem.at[0])
        fill.start()
        fill.wait()
        gather_rows(0, xbuf0, gsem.at[0])

    @pl.when((i < n_used) & (i % 2 == 0))
    def _():
        step(0, xbuf0, xbuf1, ybuf0, ybuf1)

    @pl.when((i < n_used) & (i % 2 == 1))
    def _():
        step(1, xbuf1, xbuf0, ybuf1, ybuf0)


def _moe(u2t, w_sorted, row_tok, row_dst, tile_lo, tile_hi, n_used, w_gate, w_up, w_down, layer):
    lo = lambda shape: pl.BlockSpec(shape, lambda i, tok, dst, tlo, thi, nused: (layer, tlo[i], 0, 0))
    hi = lambda shape: pl.BlockSpec(shape, lambda i, tok, dst, tlo, thi, nused: (layer, thi[i], 0, 0))
    up_shape = (1, 1, D_MODEL, D_FF_EXPERT)
    down_shape = (1, 1, D_FF_EXPERT, D_MODEL)
    tile_buf = pltpu.VMEM((TM_MOE * N_FCHUNK, LANES), F32)
    grid_spec = pltpu.PrefetchScalarGridSpec(
        num_scalar_prefetch=5,
        grid=(N_MOE_TILES,),
        in_specs=[pl.BlockSpec(memory_space=pl.ANY),
                  pl.BlockSpec((TM_MOE, 2), lambda i, *_: (i, 0)),
                  lo(up_shape), lo(up_shape), lo(down_shape),
                  hi(up_shape), hi(up_shape), hi(down_shape)],
        out_specs=pl.BlockSpec(memory_space=pl.ANY),
        scratch_shapes=[tile_buf, tile_buf, tile_buf, tile_buf,
                        pltpu.SemaphoreType.DMA((2,)),
                        pltpu.SemaphoreType.DMA((2,))],
    )
    return pl.pallas_call(
        _moe_kernel,
        grid_spec=grid_spec,
        out_shape=jax.ShapeDtypeStruct((N_Y_ROWS * N_FCHUNK, LANES), F32),
        compiler_params=pltpu.CompilerParams(dimension_semantics=("arbitrary",), vmem_limit_bytes=MOE_VMEM_LIMIT,
                                             has_side_effects=True),
        name="moe",
    )(row_tok, row_dst, tile_lo, tile_hi, n_used, u2t, w_sorted, w_gate, w_up, w_down, w_gate, w_up, w_down)


def _route_tables(cls):
    order = jnp.argsort(cls, stable=True).astype(jnp.int32)
    cids = jnp.arange(N_CLASSES, dtype=jnp.int32)
    counts = jnp.sum((cls[:, None] == cids[None, :]).astype(jnp.int32), axis=0)
    offs = jnp.cumsum(counts) - counts
    padded = (counts + TM_MOE - 1) // TM_MOE * TM_MOE
    ends = jnp.cumsum(padded)
    offs_p = ends - padded
    n_used = ends[-1] // TM_MOE
    tile_start = jnp.arange(N_MOE_TILES, dtype=jnp.int32) * TM_MOE
    tile_cls = jnp.sum((ends[None, :] <= jnp.minimum(tile_start, ends[-1] - TM_MOE)[:, None]).astype(jnp.int32),
                       axis=1)
    row = jnp.arange(N_MOE_ROWS, dtype=jnp.int32)
    row_cls = jnp.repeat(tile_cls, TM_MOE)
    k = row - offs_p[row_cls]
    valid = (k < counts[row_cls]) & (row < ends[-1])
    src = offs[row_cls] + jnp.clip(k, 0, jnp.maximum(counts[row_cls] - 1, 0))
    row_tok = order[jnp.clip(src, 0, T_ALL - 1)]
    spare = T_ALL + ((row // TM_MOE) % 2) * TM_MOE + row % TM_MOE
    row_dst = jnp.where(valid, row_tok, spare)
    row_dst = jnp.concatenate([T_ALL + TM_MOE + jnp.arange(TM_MOE, dtype=jnp.int32), row_dst])
    grp = tile_cls // len(_PAIRS)
    pair = tile_cls % len(_PAIRS)
    tile_lo = grp * EXPERTS_PER_GROUP + jnp.array([p[0] for p in _PAIRS], jnp.int32)[pair]
    tile_hi = grp * EXPERTS_PER_GROUP + jnp.array([p[1] for p in _PAIRS], jnp.int32)[pair]
    i32 = lambda v: v.astype(jnp.int32)
    return i32(row_tok), i32(row_dst), i32(tile_lo), i32(tile_hi), i32(n_used).reshape(1)


def _final_kernel(x1_ref, y_ref, mod_ref, pg_ref, pb_ref, *o_refs, tm):
    i = pl.program_id(0)
    r = _mod_row(i * tm)
    gate2 = mod_ref[pl.ds(r, 1), 5 * D_MODEL:6 * D_MODEL]
    out = _ln(ALPHA * x1_ref[...] + gate2 * _load_token_tiles(y_ref, tm)) * pg_ref[...] + pb_ref[...]
    if len(o_refs) == 1:
        o_refs[0][...] = out
    else:
        @pl.when(i < N_CTX_TOK // tm)
        def _():
            o_refs[0][...] = out

        @pl.when(i >= N_CTX_TOK // tm)
        def _():
            o_refs[1][...] = out


def _final(x1, y, mod, pg, pb, split):
    tm = TM_PROJ
    n_ctx_blk = N_CTX_TOK // tm
    const = lambda shape: pl.BlockSpec(shape, lambda i: (0,) * len(shape))
    rows = pl.BlockSpec((tm, D_MODEL), lambda i: (i, 0))
    if split:
        out_specs = [pl.BlockSpec((tm, D_MODEL), lambda i: (jnp.minimum(i, n_ctx_blk - 1), 0)),
                     pl.BlockSpec((tm, D_MODEL), lambda i: (jnp.maximum(i - n_ctx_blk, 0), 0))]
        out_shape = [jax.ShapeDtypeStruct((N_CTX_TOK, D_MODEL), F32), jax.ShapeDtypeStruct((N_LAT_TOK, D_MODEL), F32)]
    else:
        out_specs = [rows]
        out_shape = [jax.ShapeDtypeStruct((T_ALL, D_MODEL), F32)]
    return pl.pallas_call(
        functools.partial(_final_kernel, tm=tm),
        grid=(T_ALL // tm,),
        in_specs=[rows, pl.BlockSpec((tm * N_FCHUNK, LANES), lambda i: (i, 0)),
                  const((N_MOD_ROWS, 6 * D_MODEL)), const((1, D_MODEL)), const((1, D_MODEL))],
        out_specs=out_specs,
        out_shape=out_shape,
        compiler_params=_cparams(1),
        name="final",
    )(x1, y, mod, pg, pb)


def _grid_pos(n_tok):
    rows = n_tok // GRID_W
    r, col = jnp.meshgrid(jnp.arange(rows, dtype=F32), jnp.arange(GRID_W, dtype=F32), indexing='ij')
    quarter = D_MODEL // 4
    omega = 1.0 / (10000.0 ** (jnp.arange(quarter, dtype=F32) / quarter))

    def emb(p):
        ang = p.reshape(-1)[:, None] * omega[None, :]
        return jnp.concatenate([jnp.sin(ang), jnp.cos(ang)], axis=-1)

    return jnp.concatenate([emb(r), emb(col)], axis=-1)


def _pad_lanes(v, start):
    v = v.reshape(1, -1).astype(F32)
    return jnp.pad(v, ((0, 0), (start, LANES - start - v.shape[1])))


def kernel(x_prompt, x_sample, state_mlstm_C, state_mlstm_n, state_mlstm_m, state_ssd, c, c_ctx, w_in, w_o, mlstm_b_i, mlstm_b_f, mlstm_norm_g, ssd_conv_w, ssd_conv_b, ssd_dt_bias, ssd_A_log, ssd_D, ssd_norm_g, conv_dw_w, conv_dw_b, conv_ln_g, conv_ln_b, w_ada, b_ada, post1_g, post1_b, post2_g, post2_b, w_router, b_router, w_e_gate, w_e_up, w_e_down):
    cvec = jnp.concatenate([c_ctx[None, :], c, jnp.zeros((N_MOD_ROWS - 1 - DEC_BATCH, D_MODEL), F32)], axis=0)
    mod_all = _ada(cvec, w_ada, b_ada)
    x = _embed(x_prompt.reshape(N_CTX_TOK, D_MODEL), x_sample.reshape(N_LAT_TOK, D_MODEL), _grid_pos(DEC_SEQ))
    w_rt = w_router.T
    b_r = b_router.reshape(N_EXPERTS, 1)

    a_end = 4 * W_A + N_DIR * 2 * H_A
    b_end = a_end + W_B + W_XBC + N_DIR * H_B
    init = (state_mlstm_C,
            state_mlstm_n.reshape(DEC_BATCH, DEPTH, N_DIR * H_A, DH_A),
            jnp.broadcast_to(state_mlstm_m.reshape(DEC_BATCH, DEPTH, N_DIR * H_A, 1),
                             (DEC_BATCH, DEPTH, N_DIR * H_A, LANES)))
    st_c = st_n = st_m = st_h = None
    for l in range(DEPTH):
        w = w_in[l]
        w_main = jnp.concatenate([w[:, 0:4 * W_A], w[:, a_end:a_end + W_B + W_XBC], w[:, b_end:]], axis=1).astype(BF16)
        w_small = jnp.concatenate([w[:, 4 * W_A:a_end], w[:, b_end - N_DIR * H_B:b_end],
                                   jnp.zeros((D_MODEL, LANES - _DT_COL0 - N_DIR * H_B), F32)], axis=1).astype(BF16)
        gate_bias = (_pad_lanes(jnp.stack([mlstm_b_i[l], mlstm_b_f[l]], axis=1), 0)
                     + _pad_lanes(ssd_dt_bias[l], _DT_COL0))
        alog_row = _pad_lanes(ssd_A_log[l], _DT_COL0)
        mod = mod_all[l]

        proj, gates = _inproj(x, mod, w_main, w_small)

        m_norm = mlstm_norm_g[l].reshape(1, W_A)
        ha_c, st_c, st_n, st_m = _mlstm(proj, gates, gate_bias, m_norm, SEQ, BATCH, 0, l, prev=(st_c, st_n, st_m))
        (ha_l,) = _mlstm(proj, gates, gate_bias, m_norm, DEC_SEQ, DEC_BATCH, N_CTX_TOK, l, init=init)

        cw = jnp.pad(ssd_conv_w[l], ((0, 8 - SSM_CONV), (0, 0)))
        cb = ssd_conv_b[l].reshape(1, W_XBC)
        d_row = jnp.repeat(ssd_D[l], P_B).reshape(1, W_B)
        s_norm = ssd_norm_g[l].reshape(1, W_B)
        hb_c, st_h = _ssd(proj, gates, gate_bias, alog_row, cw, cb, d_row, s_norm, SEQ, BATCH, 0, l, prev=st_h)
        (hb_l,) = _ssd(proj, gates, gate_bias, alog_row, cw, cb, d_row, s_norm, DEC_SEQ, DEC_BATCH, N_CTX_TOK, l,
                       init=state_ssd)

        hc = _conf(proj, jnp.pad(conv_dw_w[l], ((0, 32 - CONV_W), (0, 0))), conv_dw_b[l].reshape(1, W_C),
                   conv_ln_g[l].reshape(1, W_C), conv_ln_b[l].reshape(1, W_C))

        x1, u2t, route = _outproj(ha_c, ha_l, hb_c, hb_l, hc, x, mod, w_o[l].astype(BF16), post1_g[l].reshape(1, D_MODEL),
                                  post1_b[l].reshape(1, D_MODEL), w_rt, b_r)
        row_tok, row_dst, tile_lo, tile_hi, n_used = _route_tables(route[0].astype(jnp.int32))
        w_sorted = jnp.stack([route[1][row_tok], route[2][row_tok]], axis=1)
        y = _moe(u2t, w_sorted, row_tok, row_dst, tile_lo, tile_hi, n_used, w_e_gate, w_e_up, w_e_down, l)
        outs = _final(x1, y, mod, post2_g[l].reshape(1, D_MODEL), post2_b[l].reshape(1, D_MODEL), l == DEPTH - 1)
        x = outs[0]

    y_prompt = outs[0].reshape(BATCH, SEQ, D_MODEL)
    y_sample = outs[1].reshape(DEC_BATCH, DEC_SEQ, D_MODEL)
    return (y_prompt, y_sample, st_c, st_n.reshape(BATCH, DEPTH, N_DIR, H_A, DH_A),
            st_m[:, :, :, 0].reshape(BATCH, DEPTH, N_DIR, H_A), st_h)
```

```python
import functools

import jax
import jax.numpy as jnp
from jax import lax
from jax.experimental import pallas as pl
from jax.experimental.pallas import tpu as pltpu

D_MODEL = 1024
BATCH = 32
SEQ = 256
DEPTH = 2
DEC_BATCH = 2
DEC_SEQ = 1024
GRID_W = 64
N_DIR = 2
CHUNK = 128
H_A = 4
DH_A = 128
W_A = H_A * DH_A
H_B = 8
P_B = 64
W_B = H_B * P_B
G_B = 2
N_B = 128
W_XBC = W_B + 2 * G_B * N_B
SSM_CONV = 3
W_C = 512
CONV_W = 31
D_MIX = W_A + W_B + W_C
N_EXPERTS = 16
N_EGROUPS = 4
EXPERTS_PER_GROUP = N_EXPERTS // N_EGROUPS
D_FF_EXPERT = 512
ALPHA = (2 * DEPTH) ** 0.25
EPS = 1e-5
F32 = jnp.float32
BF16 = jnp.bfloat16

N_CTX_TOK = BATCH * SEQ
N_LAT_TOK = DEC_BATCH * DEC_SEQ
T_ALL = N_CTX_TOK + N_LAT_TOK
N_MOD_ROWS = 8
D_MAIN = W_A + W_B + W_XBC + 2 * W_C
D_T = 3 * W_A + 128
LANES = 128
HALO = 16
CONV_ROWS = 256
TM_PROJ = 512
TM_MOE = 256
VMEM_LIMIT = 48 * 1024 * 1024
MOE_VMEM_LIMIT = 56 * 1024 * 1024

_NT = (((1,), (1,)), ((), ()))
_TN = (((0,), (0,)), ((), ()))


def _ln(x):
    mu = jnp.mean(x, axis=-1, keepdims=True)
    xc = x - mu
    var = jnp.mean(xc * xc, axis=-1, keepdims=True)
    return xc * lax.rsqrt(var + EPS)


def _sigmoid(x):
    return 1.0 / (1.0 + jnp.exp(-x))


def _silu(x):
    return x * _sigmoid(x)


def _softplus(x):
    return jnp.maximum(x, 0.0) + jnp.log1p(jnp.exp(-jnp.abs(x)))


def _dot(a, b):
    return jnp.dot(a, b, preferred_element_type=F32)


def _dotg(a, b, dims):
    return lax.dot_general(a, b, dims, preferred_element_type=F32)


def _tri_cumsum(tri, x):
    hi = x.astype(BF16)
    r1 = x - hi.astype(F32)
    mid = r1.astype(BF16)
    lo = (r1 - mid.astype(F32)).astype(BF16)
    return _dot(tri, hi) + _dot(tri, mid) + _dot(tri, lo)


def _tri_mask(d):
    row = lax.broadcasted_iota(jnp.int32, (CHUNK, CHUNK), 0)
    col = lax.broadcasted_iota(jnp.int32, (CHUNK, CHUNK), 1)
    return (row >= col) if d == 0 else (row <= col)


def _mod_row(row_start):
    return jnp.where(row_start < N_CTX_TOK, 0, 1 + (row_start - N_CTX_TOK) // DEC_SEQ)


def _cparams(n_axes):
    return pltpu.CompilerParams(dimension_semantics=("arbitrary",) * n_axes, vmem_limit_bytes=VMEM_LIMIT)


def _ada_kernel(c_ref, w_ref, b_ref, o_ref):
    o_ref[0] = _dot(_silu(c_ref[...]), w_ref[0]) + b_ref[0]


def _ada(cvec, w_ada, b_ada):
    tn = 1536
    return pl.pallas_call(
        _ada_kernel,
        grid=(DEPTH, 6 * D_MODEL // tn),
        in_specs=[
            pl.BlockSpec((N_MOD_ROWS, D_MODEL), lambda l, j: (0, 0)),
            pl.BlockSpec((1, D_MODEL, tn), lambda l, j: (l, 0, j)),
            pl.BlockSpec((1, 1, tn), lambda l, j: (l, 0, j)),
        ],
        out_specs=pl.BlockSpec((1, N_MOD_ROWS, tn), lambda l, j: (l, 0, j)),
        out_shape=jax.ShapeDtypeStruct((DEPTH, N_MOD_ROWS, 6 * D_MODEL), F32),
        compiler_params=_cparams(2),
        name="ada",
    )(cvec, w_ada, b_ada.reshape(DEPTH, 1, 6 * D_MODEL))


def _embed_kernel(xp_ref, xs_ref, pos_ref, o_ref):
    i = pl.program_id(0)

    @pl.when(i < N_CTX_TOK // DEC_SEQ)
    def _():
        o_ref[...] = xp_ref[...]

    @pl.when(i >= N_CTX_TOK // DEC_SEQ)
    def _():
        o_ref[...] = xs_ref[...] + pos_ref[...]


def _embed(xp, xs, pos):
    n_ctx_blk = N_CTX_TOK // DEC_SEQ
    return pl.pallas_call(
        _embed_kernel,
        grid=(T_ALL // DEC_SEQ,),
        in_specs=[
            pl.BlockSpec((DEC_SEQ, D_MODEL), lambda i: (jnp.minimum(i, n_ctx_blk - 1), 0)),
            pl.BlockSpec((DEC_SEQ, D_MODEL), lambda i: (jnp.maximum(i - n_ctx_blk, 0), 0)),
            pl.BlockSpec((DEC_SEQ, D_MODEL), lambda i: (0, 0)),
        ],
        out_specs=pl.BlockSpec((DEC_SEQ, D_MODEL), lambda i: (i, 0)),
        out_shape=jax.ShapeDtypeStruct((T_ALL, D_MODEL), F32),
        compiler_params=_cparams(1),
        name="embed",
    )(xp, xs, pos)


def _inproj_kernel(x_ref, mod_ref, wm_ref, wg_ref, wt_ref, om_ref, og_ref, ot_ref, u_scr, *, tm):
    i = pl.program_id(0)
    j = pl.program_id(1)

    @pl.when(j == 0)
    def _():
        r = _mod_row(i * tm)
        shift = mod_ref[pl.ds(r, 1), 0:D_MODEL]
        scale = mod_ref[pl.ds(r, 1), D_MODEL:2 * D_MODEL]
        u = (_ln(x_ref[...]) * (1.0 + scale) + shift).astype(BF16)
        u_scr[...] = u
        og_ref[...] = _dot(u, wg_ref[...])
        ot_ref[...] = _dotg(wt_ref[...], u, _NT)

    om_ref[...] = _dot(u_scr[...], wm_ref[...])


def _inproj(x, mod, w_main, w_small, w_t):
    tm, tn = TM_PROJ, D_MAIN // 2
    return pl.pallas_call(
        functools.partial(_inproj_kernel, tm=tm),
        grid=(T_ALL // tm, D_MAIN // tn),
        in_specs=[
            pl.BlockSpec((tm, D_MODEL), lambda i, j: (i, 0)),
            pl.BlockSpec((N_MOD_ROWS, 6 * D_MODEL), lambda i, j: (0, 0)),
            pl.BlockSpec((D_MODEL, tn), lambda i, j: (0, j)),
            pl.BlockSpec((D_MODEL, LANES), lambda i, j: (0, 0)),
            pl.BlockSpec((D_T, D_MODEL), lambda i, j: (0, 0)),
        ],
        out_specs=[
            pl.BlockSpec((tm, tn), lambda i, j: (i, j)),
            pl.BlockSpec((tm, LANES), lambda i, j: (i, 0)),
            pl.BlockSpec((D_T, tm), lambda i, j: (0, i)),
        ],
        out_shape=[
            jax.ShapeDtypeStruct((T_ALL, D_MAIN), F32),
            jax.ShapeDtypeStruct((T_ALL, LANES), F32),
            jax.ShapeDtypeStruct((D_T, T_ALL), F32),
        ],
        scratch_shapes=[pltpu.VMEM((tm, D_MODEL), BF16)],
        compiler_params=_cparams(2),
        name="inproj",
    )(x, mod, w_main, w_small, w_t)


def _scan_max(x, lane, d):
    s = 1
    while s < CHUNK:
        if d == 0:
            x = jnp.where(lane >= s, jnp.maximum(x, pltpu.roll(x, s, axis=1)), x)
        else:
            x = jnp.where(lane < CHUNK - s, jnp.maximum(x, pltpu.roll(x, CHUNK - s, axis=1)), x)
        s *= 2
    return x


def _split_dot(x, tri):
    hi = x.astype(BF16)
    r1 = x - hi.astype(F32)
    mid = r1.astype(BF16)
    lo = (r1 - mid.astype(F32)).astype(BF16)
    return _dot(hi, tri) + _dot(mid, tri) + _dot(lo, tri)


def _mlstm_kernel(*refs, seq, has_init, emit_state, layer):
    it = iter(refs)
    k_ref, qt_ref, vt_ref, ot_ref, g_ref, gt_ref, gb_ref, gbt_ref, ng_ref = (next(it) for _ in range(9))
    if has_init:
        c0_ref, n0_ref, m0_ref = (next(it) for _ in range(3))
    if emit_state and layer > 0:
        prev_refs = [next(it) for _ in range(3)]
    out_ref = next(it)
    if emit_state:
        co_ref, no_ref, mo_ref = (next(it) for _ in range(3))
    tg_scr, tgt_scr, b_scr, beta_scr, pm_scr, cumc_scr, hf_scr, c_scr, n_scr, m_scr = (next(it) for _ in range(10))
    n_chunks = seq // CHUNK
    n_gate = N_DIR * 2 * H_A
    if emit_state and layer > 0:
        for prev_ref, st_ref in zip(prev_refs, (co_ref, no_ref, mo_ref)):
            st_ref[0, 0:layer] = prev_ref[0]

    y = g_ref[...] + gb_ref[...]
    lane = lax.broadcasted_iota(jnp.int32, y.shape, 1)
    tg_scr[...] = jnp.where((lane < n_gate) & ((lane & H_A) != 0), -_softplus(-y), y)
    yt = gt_ref[0:n_gate, :] + gbt_ref[0:n_gate, :]
    rowt = lax.broadcasted_iota(jnp.int32, yt.shape, 0)
    tgt_scr[...] = jnp.where((rowt & H_A) != 0, -_softplus(-yt), yt)

    lane8 = lax.broadcasted_iota(jnp.int32, (8, CHUNK), 1)
    row8 = lax.broadcasted_iota(jnp.int32, (8, LANES), 0)
    masks_t = [_tri_mask(1 - d) for d in range(N_DIR)]

    for d in range(N_DIR):
        tri_t = masks_t[d].astype(BF16)
        tri_c = _tri_mask(d).astype(BF16)
        rows = slice(d * 8, (d + 1) * 8)
        for c in range(n_chunks):
            cs = slice(c * CHUNK, (c + 1) * CHUNK)
            gr8 = tgt_scr[rows, cs]
            b8 = pltpu.roll(_split_dot(gr8, tri_t), H_A, axis=0)
            beta8 = gr8 - b8
            b_scr[rows, cs] = b8
            beta_scr[rows, cs] = beta8
            pm_scr[rows, cs] = _scan_max(beta8, lane8, d)
            cumc_scr[d, cs, :] = _tri_cumsum(tri_c, tg_scr[cs, :])

    def run_direction(d):
        rows = slice(d * 8, (d + 1) * 8)
        if has_init:
            for h in range(H_A):
                c_scr[d, h] = c0_ref[0, 0, d, h].T
            n0 = n0_ref[0, 0]
            m0 = m0_ref[0, 0]
            if d == 1:
                n0 = pltpu.roll(n0, H_A, axis=0)
                m0 = pltpu.roll(m0, H_A, axis=0)
            n_scr[rows, :] = jnp.where(row8 < H_A, n0, 0.0)
            m_scr[rows, :] = jnp.where(row8 < H_A, m0, 0.0)
        else:
            for h in range(H_A):
                c_scr[d, h] = jnp.zeros((DH_A, DH_A), F32)
            n_scr[rows, :] = jnp.zeros((8, DH_A), F32)
            m_scr[rows, :] = jnp.zeros((8, LANES), F32)

        mask_t = masks_t[d]
        pos = CHUNK - 1 if d == 0 else 0

        for ci in range(n_chunks):
            c = ci if d == 0 else n_chunks - 1 - ci
            cs = slice(c * CHUNK, (c + 1) * CHUNK)
            b8 = b_scr[rows, cs]
            beta8 = beta_scr[rows, cs]
            m8 = m_scr[rows, :]
            n8 = n_scr[rows, :]
            mu8 = jnp.maximum(m8, pm_scr[rows, cs])
            mu_last = mu8[:, pos:pos + 1]
            w_int8 = jnp.exp(m8 - mu8)
            emt8 = jnp.exp(-(b8 + mu8))
            w_tok8 = jnp.exp(beta8 - mu_last)
            w_prev8 = jnp.exp(m8 - mu_last)
            m_new8 = b8[:, pos:pos + 1] + mu_last
            n16b = jnp.concatenate([n8, jnp.zeros_like(n8)], axis=0).astype(BF16)
            w_tok16 = jnp.concatenate([w_tok8, jnp.zeros_like(w_tok8)], axis=0)

            gc = tg_scr[cs, :]
            cum_c = cumc_scr[d, cs, :]
            nk_acc = jnp.zeros((8, DH_A), F32)
            for h in range(H_A):
                hs = slice(h * DH_A, (h + 1) * DH_A)
                col_i = d * 2 * H_A + h
                col_f = col_i + H_A
                beta_c = gc[:, col_i:col_i + 1] - cum_c[:, col_f:col_f + 1]
                kb = k_ref[cs, hs].astype(BF16)
                qtb = (qt_ref[hs, cs] * (DH_A ** -0.5)).astype(BF16)
                vt = vt_ref[hs, cs]
                ct = c_scr[d, h]
                w_int_r = w_int8[h:h + 1, :]
                w_prev_r = w_prev8[h:h + 1, :]

                res = _dot(jnp.concatenate([kb, ct.astype(BF16), n16b], axis=0), qtb)
                st = res[0:CHUNK] * jnp.exp(jnp.where(mask_t, beta_c - mu8[h:h + 1, :], -jnp.inf))
                num = w_int_r * res[CHUNK:CHUNK + DH_A] + _dot(vt.astype(BF16), st.astype(BF16))
                den_r = (w_int_r * res[CHUNK + DH_A + h:CHUNK + DH_A + h + 1]
                         + jnp.sum(st, axis=0, keepdims=True))
                ht = num * (1.0 / jnp.maximum(jnp.abs(den_r), emt8[h:h + 1, :]))

                upd = _dot(jnp.concatenate([vt * w_tok8[h:h + 1, :], w_tok16], axis=0).astype(BF16), kb)
                c_scr[d, h] = w_prev_r * ct + upd[0:DH_A]
                nk_acc = jnp.where(row8 == h, upd[DH_A:DH_A + 8], nk_acc)

                if d == 0:
                    hf_scr[hs, cs] = ht
                else:
                    hsum = hf_scr[hs, cs] + ht
                    mean = jnp.sum(hsum, axis=0, keepdims=True) * (1.0 / DH_A)
                    xc = hsum - mean
                    var = jnp.sum(xc * xc, axis=0, keepdims=True) * (1.0 / DH_A)
                    hn = xc * lax.rsqrt(var + EPS) * ng_ref[hs, :]
                    out_ref[hs, cs] = (hn * _sigmoid(ot_ref[hs, cs])).astype(out_ref.dtype)
            n_scr[rows, :] = w_prev8 * n8 + nk_acc
            m_scr[rows, :] = jnp.where(row8 < H_A, jnp.broadcast_to(m_new8, (8, LANES)), 0.0)

        if emit_state:
            for h in range(H_A):
                co_ref[0, layer, d, h] = c_scr[d, h].T
            no_ref[0, layer, d * H_A:(d + 1) * H_A, :] = n_scr[d * 8:d * 8 + H_A, :]
            mo_ref[0, layer, d * H_A:(d + 1) * H_A, :] = m_scr[d * 8:d * 8 + H_A, :]

    run_direction(0)
    run_direction(1)


def _mlstm(proj, gates, proj_t, gate_bias, gate_bias_t, norm_g_rep, seq, n_seq, row_off, layer, init=None, prev=None):
    blk0 = row_off // seq
    has_init = init is not None
    emit_state = not has_init
    feat = lambda r: pl.BlockSpec((W_A, seq), lambda s: (r, s + blk0))
    const = lambda shape: pl.BlockSpec(shape, lambda s: (0,) * len(shape))
    st_shapes = [(N_DIR, H_A, DH_A, DH_A), (N_DIR * H_A, DH_A), (N_DIR * H_A, LANES)]

    def st_specs(n_layers, first):
        return [pl.BlockSpec((1, n_layers) + shp, lambda s, nd=len(shp): (s, first) + (0,) * nd) for shp in st_shapes]

    in_specs = [pl.BlockSpec((seq, W_A), lambda s: (s + blk0, 0)), feat(0), feat(1), feat(2),
                pl.BlockSpec((seq, LANES), lambda s: (s + blk0, 0)),
                pl.BlockSpec((LANES, seq), lambda s: (3 * W_A // LANES, s + blk0)),
                const((1, LANES)), const((LANES, 1)), const((W_A, LANES))]
    args = [proj, proj_t, proj_t, proj_t, gates, proj_t, gate_bias, gate_bias_t, norm_g_rep]
    if has_init:
        in_specs += st_specs(1, layer)
        args += list(init)
    if emit_state and layer > 0:
        in_specs += st_specs(layer, 0)
        args += list(prev)
    out_specs = [pl.BlockSpec((W_A, seq), lambda s: (0, s))]
    out_shape = [jax.ShapeDtypeStruct((W_A, n_seq * seq), BF16)]
    if emit_state:
        out_specs += st_specs(layer + 1, 0)
        out_shape += [jax.ShapeDtypeStruct((n_seq, layer + 1) + shp, F32) for shp in st_shapes]
    return pl.pallas_call(
        functools.partial(_mlstm_kernel, seq=seq, has_init=has_init, emit_state=emit_state, layer=layer),
        grid=(n_seq,),
        in_specs=in_specs,
        out_specs=out_specs,
        out_shape=out_shape,
        scratch_shapes=[pltpu.VMEM((seq, LANES), F32),
                        pltpu.VMEM((N_DIR * 8, seq), F32),
                        pltpu.VMEM((N_DIR * 8, seq), F32),
                        pltpu.VMEM((N_DIR * 8, seq), F32),
                        pltpu.VMEM((N_DIR * 8, seq), F32),
                        pltpu.VMEM((N_DIR, seq, LANES), F32),
                        pltpu.VMEM((W_A, seq), F32),
                        pltpu.VMEM((N_DIR, H_A, DH_A, DH_A), F32),
                        pltpu.VMEM((N_DIR * 8, DH_A), F32),
                        pltpu.VMEM((N_DIR * 8, LANES), F32)],
        compiler_params=_cparams(1),
        name="mlstm_lat" if has_init else "mlstm_ctx",
    )(*args)


_DT_COL0 = N_DIR * 2 * H_A
_HG = H_B // G_B
_WG = _HG * P_B


def _ssd_kernel(*refs, seq, has_init, emit_state, layer):
    it = iter(refs)
    (z_ref, x_ref, b_ref, c_ref, g_ref, gb_ref, alog_ref, cw_ref, cb_ref, dsk_ref, ng_ref) = (
        next(it) for _ in range(11))
    if has_init:
        h0_ref = next(it)
    if emit_state and layer > 0:
        prev_ref = next(it)
    out_ref = next(it)
    if emit_state:
        ho_ref = next(it)
    pad_scr, xbc_scr, dt_scr, yf_scr, h_scr = (next(it) for _ in range(5))
    n_chunks = seq // CHUNK
    pad = 8
    if emit_state and layer > 0:
        ho_ref[0, 0:layer] = prev_ref[0]

    pad_scr[0:pad, :] = jnp.zeros((pad, W_XBC), F32)
    pad_scr[pad + seq:2 * pad + seq, :] = jnp.zeros((pad, W_XBC), F32)
    pad_scr[pad:pad + seq, 0:W_B] = x_ref[...]
    pad_scr[pad:pad + seq, W_B:W_B + G_B * N_B] = b_ref[...]
    pad_scr[pad:pad + seq, W_B + G_B * N_B:W_XBC] = c_ref[...]
    for c in range(n_chunks):
        for lb in range(W_XBC // 256):
            cols = slice(lb * 256, (lb + 1) * 256)
            acc = cb_ref[:, cols]
            for k in range(SSM_CONV):
                r = c * CHUNK + pad - SSM_CONV // 2 + k
                acc = acc + cw_ref[k:k + 1, cols] * pad_scr[r:r + CHUNK, cols]
            xbc_scr[c * CHUNK:(c + 1) * CHUNK, cols] = _silu(acc)

    dt_scr[...] = _softplus(g_ref[...] + gb_ref[...])
    a_row = -jnp.exp(alog_ref[...])

    lane_blk = lax.broadcasted_iota(jnp.int32, (CHUNK, _WG), 1) // P_B
    row_blk = lax.broadcasted_iota(jnp.int32, (_WG, 1), 0) // P_B

    def run_direction(d):
        for g in range(G_B):
            for hh in range(_HG):
                head = g * _HG + hh
                if has_init:
                    h_scr[g, hh * P_B:(hh + 1) * P_B, :] = h0_ref[0, 0, d, head]
                else:
                    h_scr[g, hh * P_B:(hh + 1) * P_B, :] = jnp.zeros((P_B, N_B), F32)

        mask = _tri_mask(d)
        tri = mask.astype(BF16)

        def body(ci, carry):
            c = ci if d == 0 else n_chunks - 1 - ci
            r0 = pl.multiple_of(c * CHUNK, CHUNK)
            rows = pl.ds(r0, CHUNK)
            dtc = dt_scr[rows, :]
            cum = _tri_cumsum(tri, dtc * a_row)
            cum_t = cum.T
            dt_t = dtc.T
            for g in range(G_B):
                xg = xbc_scr[rows, g * _WG:(g + 1) * _WG]
                xgb = xg.astype(BF16)
                bg = xbc_scr[rows, W_B + g * N_B:W_B + (g + 1) * N_B].astype(BF16)
                cg = xbc_scr[rows, W_B + G_B * N_B + g * N_B:W_B + G_B * N_B + (g + 1) * N_B].astype(BF16)
                h_st = h_scr[g]
                cb = _dotg(cg, bg, _NT)
                ch = _dotg(cg, h_st.astype(BF16), _NT)
                y = jnp.zeros((CHUNK, _WG), F32)
                e_full = jnp.zeros((CHUNK, _WG), F32)
                w_full = jnp.zeros((CHUNK, _WG), F32)
                decay = jnp.zeros((_WG, 1), F32)
                for hh in range(_HG):
                    colh = _DT_COL0 + d * H_B + g * _HG + hh
                    b_c = cum[:, colh:colh + 1]
                    b_r = cum_t[colh:colh + 1, :]
                    dt_c = dtc[:, colh:colh + 1]
                    dt_r = dt_t[colh:colh + 1, :]
                    gt = b_r[:, CHUNK - 1:CHUNK] if d == 0 else b_r[:, 0:1]
                    seg = jnp.where(mask, b_c - b_r, -jnp.inf)
                    sm = cb * jnp.exp(seg) * dt_r
                    sel = lane_blk == hh
                    y = jnp.where(sel, _dot(sm.astype(BF16), xgb), y)
                    e_full = jnp.where(sel, jnp.exp(b_c), e_full)
                    w_full = jnp.where(sel, jnp.exp(gt - b_c) * dt_c, w_full)
                    decay = jnp.where(row_blk == hh, jnp.exp(gt), decay)
                y = y + e_full * ch
                h_scr[g] = decay * h_st + _dotg((w_full * xg).astype(BF16), bg, _TN)
                gcols = slice(g * _WG, (g + 1) * _WG)
                if d == 0:
                    yf_scr[rows, gcols] = y
                else:
                    yf_scr[rows, gcols] = yf_scr[rows, gcols] + y + dsk_ref[:, gcols] * xg
            if d == 1:
                yz = yf_scr[rows, :] * _silu(z_ref[rows, :])
                rms = lax.rsqrt(jnp.mean(yz * yz, axis=-1, keepdims=True) + EPS)
                out_ref[rows, :] = (yz * rms * ng_ref[...]).astype(out_ref.dtype)
            return carry

        lax.fori_loop(0, n_chunks, body, 0)

        if emit_state:
            for g in range(G_B):
                for hh in range(_HG):
                    ho_ref[0, layer, d, g * _HG + hh] = h_scr[g, hh * P_B:(hh + 1) * P_B, :]

    run_direction(0)
    run_direction(1)


def _ssd(proj, gates, gate_bias, alog_row, conv_w, conv_b, d_row, norm_g, seq, n_seq, row_off, layer,
         init=None, prev=None):
    blk0 = row_off // seq
    has_init = init is not None
    emit_state = not has_init
    zc = W_A // W_B
    bc = (W_A + 2 * W_B) // (G_B * N_B)
    const = lambda shape: pl.BlockSpec(shape, lambda s: (0,) * len(shape))
    st_spec = lambda n_layers, first: pl.BlockSpec((1, n_layers, N_DIR, H_B, P_B, N_B),
                                                   lambda s: (s, first, 0, 0, 0, 0))
    in_specs = [pl.BlockSpec((seq, W_B), lambda s: (s + blk0, zc)),
                pl.BlockSpec((seq, W_B), lambda s: (s + blk0, zc + 1)),
                pl.BlockSpec((seq, G_B * N_B), lambda s: (s + blk0, bc)),
                pl.BlockSpec((seq, G_B * N_B), lambda s: (s + blk0, bc + 1)),
                pl.BlockSpec((seq, LANES), lambda s: (s + blk0, 0)),
                const((1, LANES)), const((1, LANES)), const((8, W_XBC)), const((1, W_XBC)),
                const((1, W_B)), const((1, W_B))]
    args = [proj, proj, proj, proj, gates, gate_bias, alog_row, conv_w, conv_b, d_row, norm_g]
    if has_init:
        in_specs.append(st_spec(1, layer))
        args.append(init)
    if emit_state and layer > 0:
        in_specs.append(st_spec(layer, 0))
        args.append(prev)
    out_specs = [pl.BlockSpec((seq, W_B), lambda s: (s, 0))]
    out_shape = [jax.ShapeDtypeStruct((n_seq * seq, W_B), BF16)]
    if emit_state:
        out_specs.append(st_spec(layer + 1, 0))
        out_shape.append(jax.ShapeDtypeStruct((n_seq, layer + 1, N_DIR, H_B, P_B, N_B), F32))
    return pl.pallas_call(
        functools.partial(_ssd_kernel, seq=seq, has_init=has_init, emit_state=emit_state, layer=layer),
        grid=(n_seq,),
        in_specs=in_specs,
        out_specs=out_specs,
        out_shape=out_shape,
        scratch_shapes=[pltpu.VMEM((seq + 16, W_XBC), F32),
                        pltpu.VMEM((seq, W_XBC), F32),
                        pltpu.VMEM((seq, LANES), F32),
                        pltpu.VMEM((seq, W_B), F32),
                        pltpu.VMEM((G_B, _WG, N_B), F32)],
        compiler_params=_cparams(1),
        name="ssd_lat" if has_init else "ssd_ctx",
    )(*args)


def _conf_kernel(a_ref, ap_ref, an_ref, g_ref, gp_ref, gn_ref, w_ref, b_ref, lg_ref, lb_ref, out_ref, pad_scr):
    i = pl.program_id(0)
    n_ctx_blk = N_CTX_TOK // CONV_ROWS
    blk_per_lat = DEC_SEQ // CONV_ROWS
    j = (i - n_ctx_blk) % blk_per_lat
    is_ctx = i < n_ctx_blk
    keep_prev = jnp.where(is_ctx | (j == 0), 0.0, 1.0)
    keep_next = jnp.where(is_ctx | (j == blk_per_lat - 1), 0.0, 1.0)
    pad_scr[0:HALO, :] = ap_ref[...] * _sigmoid(gp_ref[...]) * keep_prev
    pad_scr[HALO:HALO + CONV_ROWS, :] = a_ref[...] * _sigmoid(g_ref[...])
    pad_scr[HALO + CONV_ROWS:2 * HALO + CONV_ROWS, :] = an_ref[...] * _sigmoid(gn_ref[...]) * keep_next
    rc = 64
    for c in range(CONV_ROWS // rc):
        acc = jnp.broadcast_to(b_ref[...], (rc, W_C))
        for k in range(CONV_W):
            r = c * rc + HALO - CONV_W // 2 + k
            acc = acc + w_ref[k:k + 1, :] * pad_scr[r:r + rc, :]
        u = _ln(acc) * lg_ref[...] + lb_ref[...]
        out_ref[c * rc:(c + 1) * rc, :] = _silu(u).astype(out_ref.dtype)


def _conf(proj, dw_w, dw_b, ln_g, ln_b):
    assert SEQ == CONV_ROWS and DEC_SEQ % CONV_ROWS == 0
    ac = (W_A + W_B + W_XBC) // W_C
    hb = CONV_ROWS // HALO
    n_halo = T_ALL // HALO
    const = lambda shape: pl.BlockSpec(shape, lambda i: (0,) * len(shape))

    def specs(c):
        return [pl.BlockSpec((CONV_ROWS, W_C), lambda i: (i, c)),
                pl.BlockSpec((HALO, W_C), lambda i: (jnp.maximum(i * hb - 1, 0), c)),
                pl.BlockSpec((HALO, W_C), lambda i: (jnp.minimum((i + 1) * hb, n_halo - 1), c))]

    return pl.pallas_call(
        _conf_kernel,
        grid=(T_ALL // CONV_ROWS,),
        in_specs=specs(ac) + specs(ac + 1) + [const((32, W_C)), const((1, W_C)), const((1, W_C)), const((1, W_C))],
        out_specs=pl.BlockSpec((CONV_ROWS, W_C), lambda i: (i, 0)),
        out_shape=jax.ShapeDtypeStruct((T_ALL, W_C), BF16),
        scratch_shapes=[pltpu.VMEM((CONV_ROWS + 2 * HALO, W_C), F32)],
        compiler_params=_cparams(1),
        name="conf",
    )(proj, proj, proj, proj, proj, proj, dw_w, dw_b, ln_g, ln_b)


_PAIRS = [(a, b) for a in range(EXPERTS_PER_GROUP) for b in range(a + 1, EXPERTS_PER_GROUP)]
N_CLASSES = N_EGROUPS * len(_PAIRS)
N_FCHUNK = D_MODEL // LANES


def _store_token_tiles(ref, val):
    n = val.shape[0]
    for c in range(N_FCHUNK):
        ref[pl.ds(c, n, stride=N_FCHUNK), :] = val[:, c * LANES:(c + 1) * LANES]


def _load_token_tiles(ref, n):
    return jnp.concatenate([ref[pl.ds(c, n, stride=N_FCHUNK), :] for c in range(N_FCHUNK)], axis=1)
N_MOE_ROWS = T_ALL + N_CLASSES * TM_MOE
N_MOE_TILES = N_MOE_ROWS // TM_MOE


def _outproj_kernel(hac_ref, hal_ref, hbc_ref, hbl_ref, hc_ref, x_ref, mod_ref, wo_ref, pg_ref, pb_ref,
                    wr_ref, br_ref, x1_ref, u2_ref, rt_ref, *, tm):
    i = pl.program_id(0)
    r = _mod_row(i * tm)
    gate1 = mod_ref[pl.ds(r, 1), 2 * D_MODEL:3 * D_MODEL]
    shift2 = mod_ref[pl.ds(r, 1), 3 * D_MODEL:4 * D_MODEL]
    scale2 = mod_ref[pl.ds(r, 1), 4 * D_MODEL:5 * D_MODEL]
    is_ctx = i < N_CTX_TOK // tm
    ha = jnp.where(is_ctx, hac_ref[...], hal_ref[...])
    hb = jnp.where(is_ctx, hbc_ref[...], hbl_ref[...])
    mix = (_dotg(ha, wo_ref[0:W_A, :], _TN) + _dot(hb, wo_ref[W_A:W_A + W_B, :])
           + _dot(hc_ref[...], wo_ref[W_A + W_B:D_MIX, :]))
    x1 = _ln(ALPHA * x_ref[...] + gate1 * mix) * pg_ref[...] + pb_ref[...]
    x1_ref[...] = x1
    u2 = _ln(x1) * (1.0 + scale2) + shift2
    _store_token_tiles(u2_ref, u2)

    logits = lax.dot_general(wr_ref[...], u2, _NT, precision=lax.Precision.HIGHEST,
                             preferred_element_type=F32) + br_ref[...]
    ex = jnp.exp(logits - jnp.max(logits, axis=0, keepdims=True))
    probs = ex / jnp.sum(ex, axis=0, keepdims=True)
    scores = []
    for g in range(N_EGROUPS):
        p = [probs[g * EXPERTS_PER_GROUP + e:g * EXPERTS_PER_GROUP + e + 1, :] for e in range(EXPERTS_PER_GROUP)]
        best = p[0] + p[1]
        for a in range(EXPERTS_PER_GROUP):
            for b in range(a + 1, EXPERTS_PER_GROUP):
                if (a, b) != (0, 1):
                    best = jnp.maximum(best, p[a] + p[b])
        scores.append(best)
    gmax = functools.reduce(jnp.maximum, scores)
    sel = jnp.full(gmax.shape, N_EGROUPS - 1, jnp.int32)
    for g in range(N_EGROUPS - 2, -1, -1):
        sel = jnp.where(scores[g] == gmax, g, sel)
    eidx = lax.broadcasted_iota(jnp.int32, probs.shape, 0)
    pm = jnp.where((eidx // EXPERTS_PER_GROUP) == sel, probs, -jnp.inf)
    p1 = jnp.max(pm, axis=0, keepdims=True)
    i1 = jnp.min(jnp.where(pm == p1, eidx, N_EXPERTS), axis=0, keepdims=True)
    pm2 = jnp.where(eidx == i1, -jnp.inf, pm)
    p2 = jnp.max(pm2, axis=0, keepdims=True)
    i2 = jnp.min(jnp.where(pm2 == p2, eidx, N_EXPERTS), axis=0, keepdims=True)
    den = p1 + p2
    first_lo = i1 < i2
    w_lo = jnp.where(first_lo, p1, p2) / den
    w_hi = jnp.where(first_lo, p2, p1) / den
    a = jnp.minimum(i1, i2) - sel * EXPERTS_PER_GROUP
    b = jnp.maximum(i1, i2) - sel * EXPERTS_PER_GROUP
    pair = jnp.zeros_like(a)
    for k, (pa, pb_) in enumerate(_PAIRS):
        pair = jnp.where((a == pa) & (b == pb_), k, pair)
    cls = sel * len(_PAIRS) + pair
    rrow = lax.broadcasted_iota(jnp.int32, rt_ref.shape, 0)
    rt_ref[...] = jnp.where(rrow == 0, cls.astype(F32), jnp.where(rrow == 1, w_lo, jnp.where(rrow == 2, w_hi, 0.0)))


def _outproj(ha_c, ha_l, hb_c, hb_l, hc, x, mod, w_o, pg, pb, w_rt, b_r):
    tm = TM_PROJ
    n_ctx_blk = N_CTX_TOK // tm
    const = lambda shape: pl.BlockSpec(shape, lambda i: (0,) * len(shape))
    rows = lambda w: pl.BlockSpec((tm, w), lambda i: (i, 0))
    ctx_rows = lambda w: pl.BlockSpec((tm, w), lambda i: (jnp.minimum(i, n_ctx_blk - 1), 0))
    lat_rows = lambda w: pl.BlockSpec((tm, w), lambda i: (jnp.maximum(i - n_ctx_blk, 0), 0))
    return pl.pallas_call(
        functools.partial(_outproj_kernel, tm=tm),
        grid=(T_ALL // tm,),
        in_specs=[pl.BlockSpec((W_A, tm), lambda i: (0, jnp.minimum(i, n_ctx_blk - 1))),
                  pl.BlockSpec((W_A, tm), lambda i: (0, jnp.maximum(i - n_ctx_blk, 0))),
                  ctx_rows(W_B), lat_rows(W_B),
                  rows(W_C), rows(D_MODEL), const((N_MOD_ROWS, 6 * D_MODEL)),
                  const((D_MIX, D_MODEL)), const((1, D_MODEL)), const((1, D_MODEL)),
                  const((N_EXPERTS, D_MODEL)), const((N_EXPERTS, 1))],
        out_specs=[rows(D_MODEL), pl.BlockSpec((tm * N_FCHUNK, LANES), lambda i: (i, 0)),
                   pl.BlockSpec((8, tm), lambda i: (0, i))],
        out_shape=[jax.ShapeDtypeStruct((T_ALL, D_MODEL), F32),
                   jax.ShapeDtypeStruct((T_ALL * N_FCHUNK, LANES), F32),
                   jax.ShapeDtypeStruct((8, T_ALL), F32)],
        compiler_params=_cparams(1),
        name="outproj",
    )(ha_c, ha_l, hb_c, hb_l, hc, x, mod, w_o, pg, pb, w_rt, b_r)


N_Y_ROWS = T_ALL + 2 * TM_MOE


def _moe_kernel(tok_ref, dst_ref, tlo_ref, thi_ref, nused_ref,
                u_hbm, wrow_ref, wgl_ref, wul_ref, wdl_ref, wgh_ref, wuh_ref, wdh_ref, y_hbm,
                xbuf0, xbuf1, ybuf0, ybuf1, gsem, ssem):
    del tlo_ref, thi_ref
    i = pl.program_id(0)
    n_used = nused_ref[0]
    tile_rows = TM_MOE * N_FCHUNK

    def token_tile(ref, tok):
        return ref.at[pl.ds(pl.multiple_of(tok * N_FCHUNK, N_FCHUNK), N_FCHUNK), :]

    def gather_rows(t, xb, sem):
        for r in range(TM_MOE):
            pltpu.make_async_copy(token_tile(u_hbm, tok_ref[t * TM_MOE + r]), token_tile(xb, r), sem).start()

    def scatter_rows(t, yb, sem):
        for r in range(TM_MOE):
            pltpu.make_async_copy(token_tile(yb, r), token_tile(y_hbm, dst_ref[(t + 1) * TM_MOE + r]), sem).start()

    def wait_gather(xb, sem):
        pltpu.make_async_copy(u_hbm.at[pl.ds(0, tile_rows), :], xb, sem).wait()

    def wait_scatter(yb, sem):
        pltpu.make_async_copy(yb, y_hbm.at[pl.ds(0, tile_rows), :], sem).wait()

    def step(s, xb, xb_next, yb, yb_prev):
        @pl.when(i > 0)
        def _():
            wait_scatter(yb, ssem.at[s])

        wait_gather(xb, gsem.at[s])

        @pl.when(n_used > 0)
        def _():
            gather_rows(jnp.minimum(i + 1, n_used - 1), xb_next, gsem.at[1 - s])
            scatter_rows(i - 1, yb_prev, ssem.at[1 - s])

        x = _load_token_tiles(xb, TM_MOE).astype(BF16)
        w_lo = wrow_ref[:, 0:1]
        w_hi = wrow_ref[:, 1:2]

        def ffn(wg_ref, wu_ref, wd_ref, w):
            act = _silu(_dot(x, wg_ref[0, 0].astype(BF16))) * _dot(x, wu_ref[0, 0].astype(BF16)) * w
            return _dot(act.astype(BF16), wd_ref[0, 0].astype(BF16))

        _store_token_tiles(yb, ffn(wgl_ref, wul_ref, wdl_ref, w_lo) + ffn(wgh_ref, wuh_ref, wdh_ref, w_hi))

        @pl.when(i == n_used - 1)
        def _():
            scatter_rows(i, yb, ssem.at[s])
            wait_scatter(yb, ssem.at[s])
            wait_scatter(yb_prev, ssem.at[1 - s])
            wait_gather(xb_next, gsem.at[1 - s])

    @pl.when(i == 0)
    def _():
        ybuf1[...] = jnp.zeros(ybuf1.shape, F32)
        fill = pltpu.make_async_copy(ybuf1, y_hbm.at[pl.ds(T_ALL * N_FCHUNK, tile_rows), :], ssem.at[0])
        fill.start()
        fill.wait()
        gather_rows(0, xbuf0, gsem.at[0])

    @pl.when((i < n_used) & (i % 2 == 0))
    def _():
        step(0, xbuf0, xbuf1, ybuf0, ybuf1)

    @pl.when((i < n_used) & (i % 2 == 1))
    def _():
        step(1, xbuf1, xbuf0, ybuf1, ybuf0)


def _moe(u2t, w_sorted, row_tok, row_dst, tile_lo, tile_hi, n_used, w_gate, w_up, w_down, layer):
    lo = lambda shape: pl.BlockSpec(shape, lambda i, tok, dst, tlo, thi, nused: (layer, tlo[i], 0, 0))
    hi = lambda shape: pl.BlockSpec(shape, lambda i, tok, dst, tlo, thi, nused: (layer, thi[i], 0, 0))
    up_shape = (1, 1, D_MODEL, D_FF_EXPERT)
    down_shape = (1, 1, D_FF_EXPERT, D_MODEL)
    tile_buf = pltpu.VMEM((TM_MOE * N_FCHUNK, LANES), F32)
    grid_spec = pltpu.PrefetchScalarGridSpec(
        num_scalar_prefetch=5,
        grid=(N_MOE_TILES,),
        in_specs=[pl.BlockSpec(memory_space=pl.ANY),
                  pl.BlockSpec((TM_MOE, 2), lambda i, *_: (i, 0)),
                  lo(up_shape), lo(up_shape), lo(down_shape),
                  hi(up_shape), hi(up_shape), hi(down_shape)],
        out_specs=pl.BlockSpec(memory_space=pl.ANY),
        scratch_shapes=[tile_buf, tile_buf, tile_buf, tile_buf,
                        pltpu.SemaphoreType.DMA((2,)),
                        pltpu.SemaphoreType.DMA((2,))],
    )
    return pl.pallas_call(
        _moe_kernel,
        grid_spec=grid_spec,
        out_shape=jax.ShapeDtypeStruct((N_Y_ROWS * N_FCHUNK, LANES), F32),
        compiler_params=pltpu.CompilerParams(dimension_semantics=("arbitrary",), vmem_limit_bytes=MOE_VMEM_LIMIT,
                                             has_side_effects=True),
        name="moe",
    )(row_tok, row_dst, tile_lo, tile_hi, n_used, u2t, w_sorted, w_gate, w_up, w_down, w_gate, w_up, w_down)


def _route_tables(cls):
    order = jnp.argsort(cls, stable=True).astype(jnp.int32)
    cids = jnp.arange(N_CLASSES, dtype=jnp.int32)
    counts = jnp.sum((cls[:, None] == cids[None, :]).astype(jnp.int32), axis=0)
    offs = jnp.cumsum(counts) - counts
    padded = (counts + TM_MOE - 1) // TM_MOE * TM_MOE
    ends = jnp.cumsum(padded)
    offs_p = ends - padded
    n_used = ends[-1] // TM_MOE
    tile_start = jnp.arange(N_MOE_TILES, dtype=jnp.int32) * TM_MOE
    tile_cls = jnp.sum((ends[None, :] <= jnp.minimum(tile_start, ends[-1] - TM_MOE)[:, None]).astype(jnp.int32),
                       axis=1)
    row = jnp.arange(N_MOE_ROWS, dtype=jnp.int32)
    row_cls = jnp.repeat(tile_cls, TM_MOE)
    k = row - offs_p[row_cls]
    valid = (k < counts[row_cls]) & (row < ends[-1])
    src = offs[row_cls] + jnp.clip(k, 0, jnp.maximum(counts[row_cls] - 1, 0))
    row_tok = order[jnp.clip(src, 0, T_ALL - 1)]
    spare = T_ALL + ((row // TM_MOE) % 2) * TM_MOE + row % TM_MOE
    row_dst = jnp.where(valid, row_tok, spare)
    row_dst = jnp.concatenate([T_ALL + TM_MOE + jnp.arange(TM_MOE, dtype=jnp.int32), row_dst])
    grp = tile_cls // len(_PAIRS)
    pair = tile_cls % len(_PAIRS)
    tile_lo = grp * EXPERTS_PER_GROUP + jnp.array([p[0] for p in _PAIRS], jnp.int32)[pair]
    tile_hi = grp * EXPERTS_PER_GROUP + jnp.array([p[1] for p in _PAIRS], jnp.int32)[pair]
    i32 = lambda v: v.astype(jnp.int32)
    return i32(row_tok), i32(row_dst), i32(tile_lo), i32(tile_hi), i32(n_used).reshape(1)


def _final_kernel(x1_ref, y_ref, mod_ref, pg_ref, pb_ref, *o_refs, tm):
    i = pl.program_id(0)
    r = _mod_row(i * tm)
    gate2 = mod_ref[pl.ds(r, 1), 5 * D_MODEL:6 * D_MODEL]
    out = _ln(ALPHA * x1_ref[...] + gate2 * _load_token_tiles(y_ref, tm)) * pg_ref[...] + pb_ref[...]
    if len(o_refs) == 1:
        o_refs[0][...] = out
    else:
        @pl.when(i < N_CTX_TOK // tm)
        def _():
            o_refs[0][...] = out

        @pl.when(i >= N_CTX_TOK // tm)
        def _():
            o_refs[1][...] = out


def _final(x1, y, mod, pg, pb, split):
    tm = TM_PROJ
    n_ctx_blk = N_CTX_TOK // tm
    const = lambda shape: pl.BlockSpec(shape, lambda i: (0,) * len(shape))
    rows = pl.BlockSpec((tm, D_MODEL), lambda i: (i, 0))
    if split:
        out_specs = [pl.BlockSpec((tm, D_MODEL), lambda i: (jnp.minimum(i, n_ctx_blk - 1), 0)),
                     pl.BlockSpec((tm, D_MODEL), lambda i: (jnp.maximum(i - n_ctx_blk, 0), 0))]
        out_shape = [jax.ShapeDtypeStruct((N_CTX_TOK, D_MODEL), F32), jax.ShapeDtypeStruct((N_LAT_TOK, D_MODEL), F32)]
    else:
        out_specs = [rows]
        out_shape = [jax.ShapeDtypeStruct((T_ALL, D_MODEL), F32)]
    return pl.pallas_call(
        functools.partial(_final_kernel, tm=tm),
        grid=(T_ALL // tm,),
        in_specs=[rows, pl.BlockSpec((tm * N_FCHUNK, LANES), lambda i: (i, 0)),
                  const((N_MOD_ROWS, 6 * D_MODEL)), const((1, D_MODEL)), const((1, D_MODEL))],
        out_specs=out_specs,
        out_shape=out_shape,
        compiler_params=_cparams(1),
        name="final",
    )(x1, y, mod, pg, pb)


def _grid_pos(n_tok):
    rows = n_tok // GRID_W
    r, col = jnp.meshgrid(jnp.arange(rows, dtype=F32), jnp.arange(GRID_W, dtype=F32), indexing='ij')
    quarter = D_MODEL // 4
    omega = 1.0 / (10000.0 ** (jnp.arange(quarter, dtype=F32) / quarter))

    def emb(p):
        ang = p.reshape(-1)[:, None] * omega[None, :]
        return jnp.concatenate([jnp.sin(ang), jnp.cos(ang)], axis=-1)

    return jnp.concatenate([emb(r), emb(col)], axis=-1)


def _pad_lanes(v, start):
    v = v.reshape(1, -1).astype(F32)
    return jnp.pad(v, ((0, 0), (start, LANES - start - v.shape[1])))


def kernel(x_prompt, x_sample, state_mlstm_C, state_mlstm_n, state_mlstm_m, state_ssd, c, c_ctx, w_in, w_o, mlstm_b_i, mlstm_b_f, mlstm_norm_g, ssd_conv_w, ssd_conv_b, ssd_dt_bias, ssd_A_log, ssd_D, ssd_norm_g, conv_dw_w, conv_dw_b, conv_ln_g, conv_ln_b, w_ada, b_ada, post1_g, post1_b, post2_g, post2_b, w_router, b_router, w_e_gate, w_e_up, w_e_down):
    cvec = jnp.concatenate([c_ctx[None, :], c, jnp.zeros((N_MOD_ROWS - 1 - DEC_BATCH, D_MODEL), F32)], axis=0)
    mod_all = _ada(cvec, w_ada, b_ada)
    x = _embed(x_prompt.reshape(N_CTX_TOK, D_MODEL), x_sample.reshape(N_LAT_TOK, D_MODEL), _grid_pos(DEC_SEQ))
    w_rt = w_router.T
    b_r = b_router.reshape(N_EXPERTS, 1)

    a_end = 4 * W_A + N_DIR * 2 * H_A
    b_end = a_end + W_B + W_XBC + N_DIR * H_B
    init = (state_mlstm_C,
            state_mlstm_n.reshape(DEC_BATCH, DEPTH, N_DIR * H_A, DH_A),
            jnp.broadcast_to(state_mlstm_m.reshape(DEC_BATCH, DEPTH, N_DIR * H_A, 1),
                             (DEC_BATCH, DEPTH, N_DIR * H_A, LANES)))
    st_c = st_n = st_m = st_h = None
    for l in range(DEPTH):
        w = w_in[l]
        w_main = jnp.concatenate([w[:, W_A:2 * W_A], w[:, a_end:a_end + W_B + W_XBC], w[:, b_end:]],
                                 axis=1).astype(BF16)
        w_small = jnp.concatenate([w[:, 4 * W_A:a_end], w[:, b_end - N_DIR * H_B:b_end],
                                   jnp.zeros((D_MODEL, LANES - _DT_COL0 - N_DIR * H_B), F32)], axis=1)
        w_t = jnp.concatenate([w[:, 0:W_A], w[:, 2 * W_A:4 * W_A], w_small], axis=1).T.astype(BF16)
        w_small = w_small.astype(BF16)
        gate_bias = (_pad_lanes(jnp.stack([mlstm_b_i[l], mlstm_b_f[l]], axis=1), 0)
                     + _pad_lanes(ssd_dt_bias[l], _DT_COL0))
        alog_row = _pad_lanes(ssd_A_log[l], _DT_COL0)
        mod = mod_all[l]

        proj, gates, proj_t = _inproj(x, mod, w_main, w_small, w_t)

        m_norm = jnp.broadcast_to(mlstm_norm_g[l].reshape(W_A, 1), (W_A, LANES))
        mlstm_args = (proj, gates, proj_t, gate_bias, gate_bias.reshape(LANES, 1), m_norm)
        ha_c, st_c, st_n, st_m = _mlstm(*mlstm_args, SEQ, BATCH, 0, l, prev=(st_c, st_n, st_m))
        (ha_l,) = _mlstm(*mlstm_args, DEC_SEQ, DEC_BATCH, N_CTX_TOK, l, init=init)

        cw = jnp.pad(ssd_conv_w[l], ((0, 8 - SSM_CONV), (0, 0)))
        cb = ssd_conv_b[l].reshape(1, W_XBC)
        d_row = jnp.repeat(ssd_D[l], P_B).reshape(1, W_B)
        s_norm = ssd_norm_g[l].reshape(1, W_B)
        hb_c, st_h = _ssd(proj, gates, gate_bias, alog_row, cw, cb, d_row, s_norm, SEQ, BATCH, 0, l, prev=st_h)
        (hb_l,) = _ssd(proj, gates, gate_bias, alog_row, cw, cb, d_row, s_norm, DEC_SEQ, DEC_BATCH, N_CTX_TOK, l,
                       init=state_ssd)

        hc = _conf(proj, jnp.pad(conv_dw_w[l], ((0, 32 - CONV_W), (0, 0))), conv_dw_b[l].reshape(1, W_C),
                   conv_ln_g[l].reshape(1, W_C), conv_ln_b[l].reshape(1, W_C))

        x1, u2t, route = _outproj(ha_c, ha_l, hb_c, hb_l, hc, x, mod, w_o[l].astype(BF16), post1_g[l].reshape(1, D_MODEL),
                                  post1_b[l].reshape(1, D_MODEL), w_rt, b_r)
        row_tok, row_dst, tile_lo, tile_hi, n_used = _route_tables(route[0].astype(jnp.int32))
        w_sorted = jnp.stack([route[1][row_tok], route[2][row_tok]], axis=1)
        y = _moe(u2t, w_sorted, row_tok, row_dst, tile_lo, tile_hi, n_used, w_e_gate, w_e_up, w_e_down, l)
        outs = _final(x1, y, mod, post2_g[l].reshape(1, D_MODEL), post2_b[l].reshape(1, D_MODEL), l == DEPTH - 1)
        x = outs[0]

    y_prompt = outs[0].reshape(BATCH, SEQ, D_MODEL)
    y_sample = outs[1].reshape(DEC_BATCH, DEC_SEQ, D_MODEL)
    return (y_prompt, y_sample, st_c, st_n.reshape(BATCH, DEPTH, N_DIR, H_A, DH_A),
            st_m[:, :, :, 0].reshape(BATCH, DEPTH, N_DIR, H_A), st_h)
```

```python
import functools

import jax
import jax.numpy as jnp
from jax import lax
from jax.experimental import pallas as pl
from jax.experimental.pallas import tpu as pltpu

D_MODEL = 1024
BATCH = 32
SEQ = 256
DEPTH = 2
DEC_BATCH = 2
DEC_SEQ = 1024
GRID_W = 64
N_DIR = 2
CHUNK = 128
H_A = 4
DH_A = 128
W_A = H_A * DH_A
H_B = 8
P_B = 64
W_B = H_B * P_B
G_B = 2
N_B = 128
W_XBC = W_B + 2 * G_B * N_B
SSM_CONV = 3
W_C = 512
CONV_W = 31
D_MIX = W_A + W_B + W_C
N_EXPERTS = 16
N_EGROUPS = 4
EXPERTS_PER_GROUP = N_EXPERTS // N_EGROUPS
D_FF_EXPERT = 512
ALPHA = (2 * DEPTH) ** 0.25
EPS = 1e-5
F32 = jnp.float32
BF16 = jnp.bfloat16

N_CTX_TOK = BATCH * SEQ
N_LAT_TOK = DEC_BATCH * DEC_SEQ
T_ALL = N_CTX_TOK + N_LAT_TOK
N_MOD_ROWS = 8
D_MAIN = W_A + 2 * W_C + G_B * N_B
D_T = 3 * W_A + 2 * W_B + G_B * N_B + 128
LANES = 128
HALO = 16
CONV_ROWS = 256
TM_PROJ = 512
TM_MOE = 256
VMEM_LIMIT = 48 * 1024 * 1024
MOE_VMEM_LIMIT = 56 * 1024 * 1024

_NT = (((1,), (1,)), ((), ()))
_TN = (((0,), (0,)), ((), ()))


def _ln(x):
    mu = jnp.mean(x, axis=-1, keepdims=True)
    xc = x - mu
    var = jnp.mean(xc * xc, axis=-1, keepdims=True)
    return xc * lax.rsqrt(var + EPS)


def _sigmoid(x):
    return 1.0 / (1.0 + jnp.exp(-x))


def _silu(x):
    return x * _sigmoid(x)


def _softplus(x):
    return jnp.maximum(x, 0.0) + jnp.log1p(jnp.exp(-jnp.abs(x)))


def _dot(a, b):
    return jnp.dot(a, b, preferred_element_type=F32)


def _dotg(a, b, dims):
    return lax.dot_general(a, b, dims, preferred_element_type=F32)


def _tri_cumsum(tri, x):
    hi = x.astype(BF16)
    r1 = x - hi.astype(F32)
    mid = r1.astype(BF16)
    lo = (r1 - mid.astype(F32)).astype(BF16)
    return _dot(tri, hi) + _dot(tri, mid) + _dot(tri, lo)


def _tri_mask(d):
    row = lax.broadcasted_iota(jnp.int32, (CHUNK, CHUNK), 0)
    col = lax.broadcasted_iota(jnp.int32, (CHUNK, CHUNK), 1)
    return (row >= col) if d == 0 else (row <= col)


def _mod_row(row_start):
    return jnp.where(row_start < N_CTX_TOK, 0, 1 + (row_start - N_CTX_TOK) // DEC_SEQ)


def _cparams(n_axes):
    return pltpu.CompilerParams(dimension_semantics=("arbitrary",) * n_axes, vmem_limit_bytes=VMEM_LIMIT)


def _ada_kernel(c_ref, w_ref, b_ref, o_ref):
    o_ref[0] = _dot(_silu(c_ref[...]), w_ref[0]) + b_ref[0]


def _ada(cvec, w_ada, b_ada):
    tn = 1536
    return pl.pallas_call(
        _ada_kernel,
        grid=(DEPTH, 6 * D_MODEL // tn),
        in_specs=[
            pl.BlockSpec((N_MOD_ROWS, D_MODEL), lambda l, j: (0, 0)),
            pl.BlockSpec((1, D_MODEL, tn), lambda l, j: (l, 0, j)),
            pl.BlockSpec((1, 1, tn), lambda l, j: (l, 0, j)),
        ],
        out_specs=pl.BlockSpec((1, N_MOD_ROWS, tn), lambda l, j: (l, 0, j)),
        out_shape=jax.ShapeDtypeStruct((DEPTH, N_MOD_ROWS, 6 * D_MODEL), F32),
        compiler_params=_cparams(2),
        name="ada",
    )(cvec, w_ada, b_ada.reshape(DEPTH, 1, 6 * D_MODEL))


def _embed_kernel(xp_ref, xs_ref, pos_ref, o_ref):
    i = pl.program_id(0)

    @pl.when(i < N_CTX_TOK // DEC_SEQ)
    def _():
        o_ref[...] = xp_ref[...]

    @pl.when(i >= N_CTX_TOK // DEC_SEQ)
    def _():
        o_ref[...] = xs_ref[...] + pos_ref[...]


def _embed(xp, xs, pos):
    n_ctx_blk = N_CTX_TOK // DEC_SEQ
    return pl.pallas_call(
        _embed_kernel,
        grid=(T_ALL // DEC_SEQ,),
        in_specs=[
            pl.BlockSpec((DEC_SEQ, D_MODEL), lambda i: (jnp.minimum(i, n_ctx_blk - 1), 0)),
            pl.BlockSpec((DEC_SEQ, D_MODEL), lambda i: (jnp.maximum(i - n_ctx_blk, 0), 0)),
            pl.BlockSpec((DEC_SEQ, D_MODEL), lambda i: (0, 0)),
        ],
        out_specs=pl.BlockSpec((DEC_SEQ, D_MODEL), lambda i: (i, 0)),
        out_shape=jax.ShapeDtypeStruct((T_ALL, D_MODEL), F32),
        compiler_params=_cparams(1),
        name="embed",
    )(xp, xs, pos)


def _inproj_kernel(x_ref, mod_ref, wm_ref, wg_ref, wt_ref, om_ref, og_ref, ot_ref, u_scr, *, tm):
    i = pl.program_id(0)
    j = pl.program_id(1)

    @pl.when(j == 0)
    def _():
        r = _mod_row(i * tm)
        shift = mod_ref[pl.ds(r, 1), 0:D_MODEL]
        scale = mod_ref[pl.ds(r, 1), D_MODEL:2 * D_MODEL]
        u = (_ln(x_ref[...]) * (1.0 + scale) + shift).astype(BF16)
        u_scr[...] = u
        og_ref[...] = _dot(u, wg_ref[...])
        ot_ref[...] = _dotg(wt_ref[...], u, _NT)

    om_ref[...] = _dot(u_scr[...], wm_ref[...])


def _inproj(x, mod, w_main, w_small, w_t):
    tm, tn = TM_PROJ, D_MAIN // 2
    return pl.pallas_call(
        functools.partial(_inproj_kernel, tm=tm),
        grid=(T_ALL // tm, D_MAIN // tn),
        in_specs=[
            pl.BlockSpec((tm, D_MODEL), lambda i, j: (i, 0)),
            pl.BlockSpec((N_MOD_ROWS, 6 * D_MODEL), lambda i, j: (0, 0)),
            pl.BlockSpec((D_MODEL, tn), lambda i, j: (0, j)),
            pl.BlockSpec((D_MODEL, LANES), lambda i, j: (0, 0)),
            pl.BlockSpec((D_T, D_MODEL), lambda i, j: (0, 0)),
        ],
        out_specs=[
            pl.BlockSpec((tm, tn), lambda i, j: (i, j)),
            pl.BlockSpec((tm, LANES), lambda i, j: (i, 0)),
            pl.BlockSpec((D_T, tm), lambda i, j: (0, i)),
        ],
        out_shape=[
            jax.ShapeDtypeStruct((T_ALL, D_MAIN), F32),
            jax.ShapeDtypeStruct((T_ALL, LANES), F32),
            jax.ShapeDtypeStruct((D_T, T_ALL), F32),
        ],
        scratch_shapes=[pltpu.VMEM((tm, D_MODEL), BF16)],
        compiler_params=_cparams(2),
        name="inproj",
    )(x, mod, w_main, w_small, w_t)


def _scan_max(x, lane, d):
    s = 1
    while s < CHUNK:
        if d == 0:
            x = jnp.where(lane >= s, jnp.maximum(x, pltpu.roll(x, s, axis=1)), x)
        else:
            x = jnp.where(lane < CHUNK - s, jnp.maximum(x, pltpu.roll(x, CHUNK - s, axis=1)), x)
        s *= 2
    return x


def _split_dot(x, tri):
    hi = x.astype(BF16)
    r1 = x - hi.astype(F32)
    mid = r1.astype(BF16)
    lo = (r1 - mid.astype(F32)).astype(BF16)
    return _dot(hi, tri) + _dot(mid, tri) + _dot(lo, tri)


def _mlstm_kernel(*refs, seq, has_init, emit_state, layer):
    it = iter(refs)
    k_ref, qt_ref, vt_ref, ot_ref, g_ref, gt_ref, gb_ref, gbt_ref, ng_ref = (next(it) for _ in range(9))
    if has_init:
        c0_ref, n0_ref, m0_ref = (next(it) for _ in range(3))
    if emit_state and layer > 0:
        prev_refs = [next(it) for _ in range(3)]
    out_ref = next(it)
    if emit_state:
        co_ref, no_ref, mo_ref = (next(it) for _ in range(3))
    tg_scr, tgt_scr, b_scr, beta_scr, pm_scr, cumc_scr, hf_scr, c_scr, n_scr, m_scr = (next(it) for _ in range(10))
    n_chunks = seq // CHUNK
    n_gate = N_DIR * 2 * H_A
    if emit_state and layer > 0:
        for prev_ref, st_ref in zip(prev_refs, (co_ref, no_ref, mo_ref)):
            st_ref[0, 0:layer] = prev_ref[0]

    y = g_ref[...] + gb_ref[...]
    lane = lax.broadcasted_iota(jnp.int32, y.shape, 1)
    tg_scr[...] = jnp.where((lane < n_gate) & ((lane & H_A) != 0), -_softplus(-y), y)
    yt = gt_ref[0:n_gate, :] + gbt_ref[0:n_gate, :]
    rowt = lax.broadcasted_iota(jnp.int32, yt.shape, 0)
    tgt_scr[...] = jnp.where((rowt & H_A) != 0, -_softplus(-yt), yt)

    lane8 = lax.broadcasted_iota(jnp.int32, (8, CHUNK), 1)
    row8 = lax.broadcasted_iota(jnp.int32, (8, LANES), 0)
    masks_t = [_tri_mask(1 - d) for d in range(N_DIR)]

    for d in range(N_DIR):
        tri_t = masks_t[d].astype(BF16)
        tri_c = _tri_mask(d).astype(BF16)
        rows = slice(d * 8, (d + 1) * 8)
        for c in range(n_chunks):
            cs = slice(c * CHUNK, (c + 1) * CHUNK)
            gr8 = tgt_scr[rows, cs]
            b8 = pltpu.roll(_split_dot(gr8, tri_t), H_A, axis=0)
            beta8 = gr8 - b8
            b_scr[rows, cs] = b8
            beta_scr[rows, cs] = beta8
            pm_scr[rows, cs] = _scan_max(beta8, lane8, d)
            cumc_scr[d, cs, :] = _tri_cumsum(tri_c, tg_scr[cs, :])

    def run_direction(d):
        rows = slice(d * 8, (d + 1) * 8)
        if has_init:
            for h in range(H_A):
                c_scr[d, h] = c0_ref[0, 0, d, h].T
            n0 = n0_ref[0, 0]
            m0 = m0_ref[0, 0]
            if d == 1:
                n0 = pltpu.roll(n0, H_A, axis=0)
                m0 = pltpu.roll(m0, H_A, axis=0)
            n_scr[rows, :] = jnp.where(row8 < H_A, n0, 0.0)
            m_scr[rows, :] = jnp.where(row8 < H_A, m0, 0.0)
        else:
            for h in range(H_A):
                c_scr[d, h] = jnp.zeros((DH_A, DH_A), F32)
            n_scr[rows, :] = jnp.zeros((8, DH_A), F32)
            m_scr[rows, :] = jnp.zeros((8, LANES), F32)

        mask_t = masks_t[d]
        pos = CHUNK - 1 if d == 0 else 0

        for ci in range(n_chunks):
            c = ci if d == 0 else n_chunks - 1 - ci
            cs = slice(c * CHUNK, (c + 1) * CHUNK)
            b8 = b_scr[rows, cs]
            beta8 = beta_scr[rows, cs]
            m8 = m_scr[rows, :]
            n8 = n_scr[rows, :]
            mu8 = jnp.maximum(m8, pm_scr[rows, cs])
            mu_last = mu8[:, pos:pos + 1]
            w_int8 = jnp.exp(m8 - mu8)
            emt8 = jnp.exp(-(b8 + mu8))
            w_tok8 = jnp.exp(beta8 - mu_last)
            w_prev8 = jnp.exp(m8 - mu_last)
            m_new8 = b8[:, pos:pos + 1] + mu_last
            n16b = jnp.concatenate([n8, jnp.zeros_like(n8)], axis=0).astype(BF16)
            w_tok16 = jnp.concatenate([w_tok8, jnp.zeros_like(w_tok8)], axis=0)

            gc = tg_scr[cs, :]
            cum_c = cumc_scr[d, cs, :]
            nk_acc = jnp.zeros((8, DH_A), F32)
            for h in range(H_A):
                hs = slice(h * DH_A, (h + 1) * DH_A)
                col_i = d * 2 * H_A + h
                col_f = col_i + H_A
                beta_c = gc[:, col_i:col_i + 1] - cum_c[:, col_f:col_f + 1]
                kb = k_ref[cs, hs].astype(BF16)
                qtb = (qt_ref[hs, cs] * (DH_A ** -0.5)).astype(BF16)
                vt = vt_ref[hs, cs]
                ct = c_scr[d, h]
                w_int_r = w_int8[h:h + 1, :]
                w_prev_r = w_prev8[h:h + 1, :]

                res = _dot(jnp.concatenate([kb, ct.astype(BF16), n16b], axis=0), qtb)
                st = res[0:CHUNK] * jnp.exp(jnp.where(mask_t, beta_c - mu8[h:h + 1, :], -jnp.inf))
                num = w_int_r * res[CHUNK:CHUNK + DH_A] + _dot(vt.astype(BF16), st.astype(BF16))
                den_r = (w_int_r * res[CHUNK + DH_A + h:CHUNK + DH_A + h + 1]
                         + jnp.sum(st, axis=0, keepdims=True))
                ht = num * (1.0 / jnp.maximum(jnp.abs(den_r), emt8[h:h + 1, :]))

                upd = _dot(jnp.concatenate([vt * w_tok8[h:h + 1, :], w_tok16], axis=0).astype(BF16), kb)
                c_scr[d, h] = w_prev_r * ct + upd[0:DH_A]
                nk_acc = jnp.where(row8 == h, upd[DH_A:DH_A + 8], nk_acc)

                if d == 0:
                    hf_scr[hs, cs] = ht
                else:
                    hsum = hf_scr[hs, cs] + ht
                    mean = jnp.sum(hsum, axis=0, keepdims=True) * (1.0 / DH_A)
                    xc = hsum - mean
                    var = jnp.sum(xc * xc, axis=0, keepdims=True) * (1.0 / DH_A)
                    hn = xc * lax.rsqrt(var + EPS) * ng_ref[hs, :]
                    out_ref[hs, cs] = (hn * _sigmoid(ot_ref[hs, cs])).astype(out_ref.dtype)
            n_scr[rows, :] = w_prev8 * n8 + nk_acc
            m_scr[rows, :] = jnp.where(row8 < H_A, jnp.broadcast_to(m_new8, (8, LANES)), 0.0)

        if emit_state:
            for h in range(H_A):
                co_ref[0, layer, d, h] = c_scr[d, h].T
            no_ref[0, layer, d * H_A:(d + 1) * H_A, :] = n_scr[d * 8:d * 8 + H_A, :]
            mo_ref[0, layer, d * H_A:(d + 1) * H_A, :] = m_scr[d * 8:d * 8 + H_A, :]

    run_direction(0)
    run_direction(1)


def _mlstm(proj, gates, proj_t, gate_bias, gate_bias_t, norm_g_rep, seq, n_seq, row_off, layer, init=None, prev=None):
    blk0 = row_off // seq
    has_init = init is not None
    emit_state = not has_init
    feat = lambda r: pl.BlockSpec((W_A, seq), lambda s: (r, s + blk0))
    const = lambda shape: pl.BlockSpec(shape, lambda s: (0,) * len(shape))
    st_shapes = [(N_DIR, H_A, DH_A, DH_A), (N_DIR * H_A, DH_A), (N_DIR * H_A, LANES)]

    def st_specs(n_layers, first):
        return [pl.BlockSpec((1, n_layers) + shp, lambda s, nd=len(shp): (s, first) + (0,) * nd) for shp in st_shapes]

    in_specs = [pl.BlockSpec((seq, W_A), lambda s: (s + blk0, 0)), feat(0), feat(1), feat(2),
                pl.BlockSpec((seq, LANES), lambda s: (s + blk0, 0)),
                pl.BlockSpec((LANES, seq), lambda s: (D_T // LANES - 1, s + blk0)),
                const((1, LANES)), const((LANES, 1)), const((W_A, LANES))]
    args = [proj, proj_t, proj_t, proj_t, gates, proj_t, gate_bias, gate_bias_t, norm_g_rep]
    if has_init:
        in_specs += st_specs(1, layer)
        args += list(init)
    if emit_state and layer > 0:
        in_specs += st_specs(layer, 0)
        args += list(prev)
    out_specs = [pl.BlockSpec((W_A, seq), lambda s: (0, s))]
    out_shape = [jax.ShapeDtypeStruct((W_A, n_seq * seq), BF16)]
    if emit_state:
        out_specs += st_specs(layer + 1, 0)
        out_shape += [jax.ShapeDtypeStruct((n_seq, layer + 1) + shp, F32) for shp in st_shapes]
    return pl.pallas_call(
        functools.partial(_mlstm_kernel, seq=seq, has_init=has_init, emit_state=emit_state, layer=layer),
        grid=(n_seq,),
        in_specs=in_specs,
        out_specs=out_specs,
        out_shape=out_shape,
        scratch_shapes=[pltpu.VMEM((seq, LANES), F32),
                        pltpu.VMEM((N_DIR * 8, seq), F32),
                        pltpu.VMEM((N_DIR * 8, seq), F32),
                        pltpu.VMEM((N_DIR * 8, seq), F32),
                        pltpu.VMEM((N_DIR * 8, seq), F32),
                        pltpu.VMEM((N_DIR, seq, LANES), F32),
                        pltpu.VMEM((W_A, seq), F32),
                        pltpu.VMEM((N_DIR, H_A, DH_A, DH_A), F32),
                        pltpu.VMEM((N_DIR * 8, DH_A), F32),
                        pltpu.VMEM((N_DIR * 8, LANES), F32)],
        compiler_params=_cparams(1),
        name="mlstm_lat" if has_init else "mlstm_ctx",
    )(*args)


_DT_COL0 = N_DIR * 2 * H_A
_HG = H_B // G_B
_WG = _HG * P_B


def _conv3_lanes(ref, w_ref, b_ref, out_scr, seq):
    lane = lax.broadcasted_iota(jnp.int32, (CHUNK, seq), 1)
    for r in range(ref.shape[0] // CHUNK):
        rs = slice(r * CHUNK, (r + 1) * CHUNK)
        cur = ref[rs, :]
        prev = jnp.where(lane == 0, 0.0, pltpu.roll(cur, 1, axis=1))
        nxt = jnp.where(lane == seq - 1, 0.0, pltpu.roll(cur, seq - 1, axis=1))
        for c in range(seq // CHUNK):
            cs = slice(c * CHUNK, (c + 1) * CHUNK)
            acc = (w_ref[0, rs, :] * prev[:, cs] + w_ref[1, rs, :] * cur[:, cs] + w_ref[2, rs, :] * nxt[:, cs]
                   + b_ref[rs, :])
            out_scr[rs, cs] = _silu(acc).astype(out_scr.dtype)


def _ssd_kernel(*refs, seq, has_init, emit_state, layer):
    it = iter(refs)
    (zt_ref, xt_ref, ct_ref, b_ref, g_ref, gt_ref, gb_ref, gbt_ref, alog_ref, alogt_ref,
     cwx_ref, cbx_ref, cwc_ref, cbc_ref, cwb_ref, cbb_ref, dsk_ref, ng_ref) = (next(it) for _ in range(18))
    if has_init:
        h0_ref = next(it)
    if emit_state and layer > 0:
        prev_ref = next(it)
    out_ref = next(it)
    if emit_state:
        ho_ref = next(it)
    pad_scr, xs_scr, cs_scr, bs_scr, dtr_scr, br_scr, cumc_scr, yt_scr, h_scr = (next(it) for _ in range(9))
    n_chunks = seq // CHUNK
    pad = 8
    if emit_state and layer > 0:
        ho_ref[0, 0:layer] = prev_ref[0]

    _conv3_lanes(xt_ref, cwx_ref, cbx_ref, xs_scr, seq)
    _conv3_lanes(ct_ref, cwc_ref, cbc_ref, cs_scr, seq)
    nb = G_B * N_B
    pad_scr[0:pad, :] = jnp.zeros((pad, nb), F32)
    pad_scr[pad + seq:2 * pad + seq, :] = jnp.zeros((pad, nb), F32)
    pad_scr[pad:pad + seq, :] = b_ref[...]
    for c in range(n_chunks):
        acc = cbb_ref[...]
        for k in range(SSM_CONV):
            r = c * CHUNK + pad - SSM_CONV // 2 + k
            acc = acc + cwb_ref[k:k + 1, :] * pad_scr[r:r + CHUNK, :]
        bs_scr[c * CHUNK:(c + 1) * CHUNK, :] = _silu(acc).astype(bs_scr.dtype)

    lac = _softplus(g_ref[...] + gb_ref[...]) * (-jnp.exp(alog_ref[...]))
    r0 = _DT_COL0
    dtr = _softplus(gt_ref[r0:r0 + N_DIR * H_B, :] + gbt_ref[r0:r0 + N_DIR * H_B, :])
    dtr_scr[...] = dtr
    lar = dtr * (-jnp.exp(alogt_ref[r0:r0 + N_DIR * H_B, :]))
    masks_t = [_tri_mask(1 - d) for d in range(N_DIR)]
    for d in range(N_DIR):
        tri_t = masks_t[d].astype(BF16)
        tri_c = _tri_mask(d).astype(BF16)
        rows = slice(d * H_B, (d + 1) * H_B)
        for c in range(n_chunks):
            cs = slice(c * CHUNK, (c + 1) * CHUNK)
            br_scr[rows, cs] = _split_dot(lar[rows, cs], tri_t)
            cumc_scr[d, cs, :] = _tri_cumsum(tri_c, lac[cs, :])

    row_blk = lax.broadcasted_iota(jnp.int32, (_WG, 1), 0) // P_B

    def run_direction(d):
        for g in range(G_B):
            for hh in range(_HG):
                if has_init:
                    h_scr[d, g, hh * P_B:(hh + 1) * P_B, :] = h0_ref[0, 0, d, g * _HG + hh]
                else:
                    h_scr[d, g, hh * P_B:(hh + 1) * P_B, :] = jnp.zeros((P_B, N_B), F32)

        mask_t = masks_t[d]
        pos = CHUNK - 1 if d == 0 else 0

        for ci in range(n_chunks):
            c = ci if d == 0 else n_chunks - 1 - ci
            cs = slice(c * CHUNK, (c + 1) * CHUNK)
            cum_c = cumc_scr[d, cs, :]
            for g in range(G_B):
                bg = bs_scr[cs, g * N_B:(g + 1) * N_B]
                ctg = cs_scr[g * N_B:(g + 1) * N_B, cs]
                h_st = h_scr[d, g]
                res = _dot(jnp.concatenate([bg, h_st.astype(BF16)], axis=0), ctg)
                cbt = res[0:CHUNK]
                inter = res[CHUNK:CHUNK + _WG]
                xw = []
                decay = jnp.zeros((_WG, 1), F32)
                for hh in range(_HG):
                    head = g * _HG + hh
                    r = d * H_B + head
                    b_r = br_scr[r:r + 1, cs]
                    dt_r = dtr_scr[r:r + 1, cs]
                    b_c = cum_c[:, _DT_COL0 + r:_DT_COL0 + r + 1]
                    gt = b_r[:, pos:pos + 1]
                    ps = slice(head * P_B, (head + 1) * P_B)
                    xh = xs_scr[ps, cs]
                    st = (cbt * jnp.exp(jnp.where(mask_t, b_r - b_c, -jnp.inf))).astype(BF16)
                    y = _dot((xh * dt_r).astype(BF16), st) + jnp.exp(b_r) * inter[hh * P_B:(hh + 1) * P_B]
                    xw.append(xh * (jnp.exp(gt - b_r) * dt_r))
                    decay = jnp.where(row_blk == hh, jnp.exp(gt), decay)
                    if d == 0:
                        yt_scr[ps, cs] = y
                    else:
                        yt_scr[ps, cs] = yt_scr[ps, cs] + y + dsk_ref[ps, :] * xh
                h_scr[d, g] = decay * h_st + _dot(jnp.concatenate(xw, axis=0).astype(BF16), bg)
            if d == 1:
                yz = yt_scr[:, cs] * _silu(zt_ref[:, cs])
                rms = lax.rsqrt(jnp.sum(yz * yz, axis=0, keepdims=True) * (1.0 / W_B) + EPS)
                out_ref[:, cs] = (yz * rms * ng_ref[...]).astype(out_ref.dtype)

        if emit_state:
            for g in range(G_B):
                for hh in range(_HG):
                    ho_ref[0, layer, d, g * _HG + hh] = h_scr[d, g, hh * P_B:(hh + 1) * P_B, :]

    run_direction(0)
    run_direction(1)


def _ssd(proj, gates, proj_t, gate_bias, gate_bias_t, alog_row, alog_col, conv, d_rep, norm_g_rep,
         seq, n_seq, row_off, layer, init=None, prev=None):
    blk0 = row_off // seq
    has_init = init is not None
    emit_state = not has_init
    nb = G_B * N_B
    zr = 3 * W_A // W_B
    cr = (3 * W_A + 2 * W_B) // nb
    gr = (3 * W_A + 2 * W_B + nb) // LANES
    bc = (W_A + 2 * W_C) // nb
    const = lambda shape: pl.BlockSpec(shape, lambda s: (0,) * len(shape))
    st_spec = lambda n_layers, first: pl.BlockSpec((1, n_layers, N_DIR, H_B, P_B, N_B),
                                                   lambda s: (s, first, 0, 0, 0, 0))
    in_specs = [pl.BlockSpec((W_B, seq), lambda s: (zr, s + blk0)),
                pl.BlockSpec((W_B, seq), lambda s: (zr + 1, s + blk0)),
                pl.BlockSpec((nb, seq), lambda s: (cr, s + blk0)),
                pl.BlockSpec((seq, nb), lambda s: (s + blk0, bc)),
                pl.BlockSpec((seq, LANES), lambda s: (s + blk0, 0)),
                pl.BlockSpec((LANES, seq), lambda s: (gr, s + blk0)),
                const((1, LANES)), const((LANES, 1)), const((1, LANES)), const((LANES, 1)),
                const((SSM_CONV, W_B, LANES)), const((W_B, LANES)), const((SSM_CONV, nb, LANES)), const((nb, LANES)),
                const((8, nb)), const((1, nb)), const((W_B, LANES)), const((W_B, LANES))]
    args = [proj_t, proj_t, proj_t, proj, gates, proj_t, gate_bias, gate_bias_t, alog_row, alog_col,
            *conv, d_rep, norm_g_rep]
    if has_init:
        in_specs.append(st_spec(1, layer))
        args.append(init)
    if emit_state and layer > 0:
        in_specs.append(st_spec(layer, 0))
        args.append(prev)
    out_specs = [pl.BlockSpec((W_B, seq), lambda s: (0, s))]
    out_shape = [jax.ShapeDtypeStruct((W_B, n_seq * seq), BF16)]
    if emit_state:
        out_specs.append(st_spec(layer + 1, 0))
        out_shape.append(jax.ShapeDtypeStruct((n_seq, layer + 1, N_DIR, H_B, P_B, N_B), F32))
    return pl.pallas_call(
        functools.partial(_ssd_kernel, seq=seq, has_init=has_init, emit_state=emit_state, layer=layer),
        grid=(n_seq,),
        in_specs=in_specs,
        out_specs=out_specs,
        out_shape=out_shape,
        scratch_shapes=[pltpu.VMEM((seq + 16, nb), F32),
                        pltpu.VMEM((W_B, seq), F32),
                        pltpu.VMEM((nb, seq), BF16),
                        pltpu.VMEM((seq, nb), BF16),
                        pltpu.VMEM((N_DIR * H_B, seq), F32),
                        pltpu.VMEM((N_DIR * H_B, seq), F32),
                        pltpu.VMEM((N_DIR, seq, LANES), F32),
                        pltpu.VMEM((W_B, seq), F32),
                        pltpu.VMEM((N_DIR, G_B, _WG, N_B), F32)],
        compiler_params=_cparams(1),
        name="ssd_lat" if has_init else "ssd_ctx",
    )(*args)


def _conf_kernel(a_ref, ap_ref, an_ref, g_ref, gp_ref, gn_ref, w_ref, b_ref, lg_ref, lb_ref, out_ref, pad_scr):
    i = pl.program_id(0)
    n_ctx_blk = N_CTX_TOK // CONV_ROWS
    blk_per_lat = DEC_SEQ // CONV_ROWS
    j = (i - n_ctx_blk) % blk_per_lat
    is_ctx = i < n_ctx_blk
    keep_prev = jnp.where(is_ctx | (j == 0), 0.0, 1.0)
    keep_next = jnp.where(is_ctx | (j == blk_per_lat - 1), 0.0, 1.0)
    pad_scr[0:HALO, :] = ap_ref[...] * _sigmoid(gp_ref[...]) * keep_prev
    pad_scr[HALO:HALO + CONV_ROWS, :] = a_ref[...] * _sigmoid(g_ref[...])
    pad_scr[HALO + CONV_ROWS:2 * HALO + CONV_ROWS, :] = an_ref[...] * _sigmoid(gn_ref[...]) * keep_next
    rc = 64
    for c in range(CONV_ROWS // rc):
        acc = jnp.broadcast_to(b_ref[...], (rc, W_C))
        for k in range(CONV_W):
            r = c * rc + HALO - CONV_W // 2 + k
            acc = acc + w_ref[k:k + 1, :] * pad_scr[r:r + rc, :]
        u = _ln(acc) * lg_ref[...] + lb_ref[...]
        out_ref[c * rc:(c + 1) * rc, :] = _silu(u).astype(out_ref.dtype)


def _conf(proj, dw_w, dw_b, ln_g, ln_b):
    assert SEQ == CONV_ROWS and DEC_SEQ % CONV_ROWS == 0
    ac = W_A // W_C
    hb = CONV_ROWS // HALO
    n_halo = T_ALL // HALO
    const = lambda shape: pl.BlockSpec(shape, lambda i: (0,) * len(shape))

    def specs(c):
        return [pl.BlockSpec((CONV_ROWS, W_C), lambda i: (i, c)),
                pl.BlockSpec((HALO, W_C), lambda i: (jnp.maximum(i * hb - 1, 0), c)),
                pl.BlockSpec((HALO, W_C), lambda i: (jnp.minimum((i + 1) * hb, n_halo - 1), c))]

    return pl.pallas_call(
        _conf_kernel,
        grid=(T_ALL // CONV_ROWS,),
        in_specs=specs(ac) + specs(ac + 1) + [const((32, W_C)), const((1, W_C)), const((1, W_C)), const((1, W_C))],
        out_specs=pl.BlockSpec((CONV_ROWS, W_C), lambda i: (i, 0)),
        out_shape=jax.ShapeDtypeStruct((T_ALL, W_C), BF16),
        scratch_shapes=[pltpu.VMEM((CONV_ROWS + 2 * HALO, W_C), F32)],
        compiler_params=_cparams(1),
        name="conf",
    )(proj, proj, proj, proj, proj, proj, dw_w, dw_b, ln_g, ln_b)


_PAIRS = [(a, b) for a in range(EXPERTS_PER_GROUP) for b in range(a + 1, EXPERTS_PER_GROUP)]
N_CLASSES = N_EGROUPS * len(_PAIRS)
N_FCHUNK = D_MODEL // LANES


def _store_token_tiles(ref, val):
    n = val.shape[0]
    for c in range(N_FCHUNK):
        ref[pl.ds(c, n, stride=N_FCHUNK), :] = val[:, c * LANES:(c + 1) * LANES]


def _load_token_tiles(ref, n):
    return jnp.concatenate([ref[pl.ds(c, n, stride=N_FCHUNK), :] for c in range(N_FCHUNK)], axis=1)
N_MOE_ROWS = T_ALL + N_CLASSES * TM_MOE
N_MOE_TILES = N_MOE_ROWS // TM_MOE


def _outproj_kernel(hac_ref, hal_ref, hbc_ref, hbl_ref, hc_ref, x_ref, mod_ref, wo_ref, pg_ref, pb_ref,
                    wr_ref, br_ref, x1_ref, u2_ref, rt_ref, *, tm):
    i = pl.program_id(0)
    r = _mod_row(i * tm)
    gate1 = mod_ref[pl.ds(r, 1), 2 * D_MODEL:3 * D_MODEL]
    shift2 = mod_ref[pl.ds(r, 1), 3 * D_MODEL:4 * D_MODEL]
    scale2 = mod_ref[pl.ds(r, 1), 4 * D_MODEL:5 * D_MODEL]
    is_ctx = i < N_CTX_TOK // tm
    ha = jnp.where(is_ctx, hac_ref[...], hal_ref[...])
    hb = jnp.where(is_ctx, hbc_ref[...], hbl_ref[...])
    mix = (_dotg(ha, wo_ref[0:W_A, :], _TN) + _dotg(hb, wo_ref[W_A:W_A + W_B, :], _TN)
           + _dot(hc_ref[...], wo_ref[W_A + W_B:D_MIX, :]))
    x1 = _ln(ALPHA * x_ref[...] + gate1 * mix) * pg_ref[...] + pb_ref[...]
    x1_ref[...] = x1
    u2 = _ln(x1) * (1.0 + scale2) + shift2
    _store_token_tiles(u2_ref, u2)

    logits = lax.dot_general(wr_ref[...], u2, _NT, precision=lax.Precision.HIGHEST,
                             preferred_element_type=F32) + br_ref[...]
    ex = jnp.exp(logits - jnp.max(logits, axis=0, keepdims=True))
    probs = ex / jnp.sum(ex, axis=0, keepdims=True)
    scores = []
    for g in range(N_EGROUPS):
        p = [probs[g * EXPERTS_PER_GROUP + e:g * EXPERTS_PER_GROUP + e + 1, :] for e in range(EXPERTS_PER_GROUP)]
        best = p[0] + p[1]
        for a in range(EXPERTS_PER_GROUP):
            for b in range(a + 1, EXPERTS_PER_GROUP):
                if (a, b) != (0, 1):
                    best = jnp.maximum(best, p[a] + p[b])
        scores.append(best)
    gmax = functools.reduce(jnp.maximum, scores)
    sel = jnp.full(gmax.shape, N_EGROUPS - 1, jnp.int32)
    for g in range(N_EGROUPS - 2, -1, -1):
        sel = jnp.where(scores[g] == gmax, g, sel)
    eidx = lax.broadcasted_iota(jnp.int32, probs.shape, 0)
    pm = jnp.where((eidx // EXPERTS_PER_GROUP) == sel, probs, -jnp.inf)
    p1 = jnp.max(pm, axis=0, keepdims=True)
    i1 = jnp.min(jnp.where(pm == p1, eidx, N_EXPERTS), axis=0, keepdims=True)
    pm2 = jnp.where(eidx == i1, -jnp.inf, pm)
    p2 = jnp.max(pm2, axis=0, keepdims=True)
    i2 = jnp.min(jnp.where(pm2 == p2, eidx, N_EXPERTS), axis=0, keepdims=True)
    den = p1 + p2
    first_lo = i1 < i2
    w_lo = jnp.where(first_lo, p1, p2) / den
    w_hi = jnp.where(first_lo, p2, p1) / den
    a = jnp.minimum(i1, i2) - sel * EXPERTS_PER_GROUP
    b = jnp.maximum(i1, i2) - sel * EXPERTS_PER_GROUP
    pair = jnp.zeros_like(a)
    for k, (pa, pb_) in enumerate(_PAIRS):
        pair = jnp.where((a == pa) & (b == pb_), k, pair)
    cls = sel * len(_PAIRS) + pair
    rrow = lax.broadcasted_iota(jnp.int32, rt_ref.shape, 0)
    rt_ref[...] = jnp.where(rrow == 0, cls.astype(F32), jnp.where(rrow == 1, w_lo, jnp.where(rrow == 2, w_hi, 0.0)))


def _outproj(ha_c, ha_l, hb_c, hb_l, hc, x, mod, w_o, pg, pb, w_rt, b_r):
    tm = TM_PROJ
    n_ctx_blk = N_CTX_TOK // tm
    const = lambda shape: pl.BlockSpec(shape, lambda i: (0,) * len(shape))
    rows = lambda w: pl.BlockSpec((tm, w), lambda i: (i, 0))
    ctx_feat = lambda w: pl.BlockSpec((w, tm), lambda i: (0, jnp.minimum(i, n_ctx_blk - 1)))
    lat_feat = lambda w: pl.BlockSpec((w, tm), lambda i: (0, jnp.maximum(i - n_ctx_blk, 0)))
    return pl.pallas_call(
        functools.partial(_outproj_kernel, tm=tm),
        grid=(T_ALL // tm,),
        in_specs=[ctx_feat(W_A), lat_feat(W_A), ctx_feat(W_B), lat_feat(W_B),
                  rows(W_C), rows(D_MODEL), const((N_MOD_ROWS, 6 * D_MODEL)),
                  const((D_MIX, D_MODEL)), const((1, D_MODEL)), const((1, D_MODEL)),
                  const((N_EXPERTS, D_MODEL)), const((N_EXPERTS, 1))],
        out_specs=[rows(D_MODEL), pl.BlockSpec((tm * N_FCHUNK, LANES), lambda i: (i, 0)),
                   pl.BlockSpec((8, tm), lambda i: (0, i))],
        out_shape=[jax.ShapeDtypeStruct((T_ALL, D_MODEL), F32),
                   jax.ShapeDtypeStruct((T_ALL * N_FCHUNK, LANES), F32),
                   jax.ShapeDtypeStruct((8, T_ALL), F32)],
        compiler_params=_cparams(1),
        name="outproj",
    )(ha_c, ha_l, hb_c, hb_l, hc, x, mod, w_o, pg, pb, w_rt, b_r)


N_Y_ROWS = T_ALL + 2 * TM_MOE


def _moe_kernel(tok_ref, dst_ref, tlo_ref, thi_ref, nused_ref,
                u_hbm, wrow_ref, wgl_ref, wul_ref, wdl_ref, wgh_ref, wuh_ref, wdh_ref, y_hbm,
                xbuf0, xbuf1, ybuf0, ybuf1, gsem, ssem):
    del tlo_ref, thi_ref
    i = pl.program_id(0)
    n_used = nused_ref[0]
    tile_rows = TM_MOE * N_FCHUNK

    def token_tile(ref, tok):
        return ref.at[pl.ds(pl.multiple_of(tok * N_FCHUNK, N_FCHUNK), N_FCHUNK), :]

    def gather_rows(t, xb, sem):
        for r in range(TM_MOE):
            pltpu.make_async_copy(token_tile(u_hbm, tok_ref[t * TM_MOE + r]), token_tile(xb, r), sem).start()

    def scatter_rows(t, yb, sem):
        for r in range(TM_MOE):
            pltpu.make_async_copy(token_tile(yb, r), token_tile(y_hbm, dst_ref[(t + 1) * TM_MOE + r]), sem).start()

    def wait_gather(xb, sem):
        pltpu.make_async_copy(u_hbm.at[pl.ds(0, tile_rows), :], xb, sem).wait()

    def wait_scatter(yb, sem):
        pltpu.make_async_copy(yb, y_hbm.at[pl.ds(0, tile_rows), :], sem).wait()

    def step(s, xb, xb_next, yb, yb_prev):
        @pl.when(i > 0)
        def _():
            wait_scatter(yb, ssem.at[s])

        wait_gather(xb, gsem.at[s])

        @pl.when(n_used > 0)
        def _():
            gather_rows(jnp.minimum(i + 1, n_used - 1), xb_next, gsem.at[1 - s])
            scatter_rows(i - 1, yb_prev, ssem.at[1 - s])

        x = _load_token_tiles(xb, TM_MOE).astype(BF16)
        w_lo = wrow_ref[:, 0:1]
        w_hi = wrow_ref[:, 1:2]

        def ffn(wg_ref, wu_ref, wd_ref, w):
            act = _silu(_dot(x, wg_ref[0, 0].astype(BF16))) * _dot(x, wu_ref[0, 0].astype(BF16)) * w
            return _dot(act.astype(BF16), wd_ref[0, 0].astype(BF16))

        _store_token_tiles(yb, ffn(wgl_ref, wul_ref, wdl_ref, w_lo) + ffn(wgh_ref, wuh_ref, wdh_ref, w_hi))

        @pl.when(i == n_used - 1)
        def _():
            scatter_rows(i, yb, ssem.at[s])
            wait_scatter(yb, ssem.at[s])
            wait_scatter(yb_prev, ssem.at[1 - s])
            wait_gather(xb_next, gsem.at[1 - s])

    @pl.when(i == 0)
    def _():
        ybuf1[...] = jnp.zeros(ybuf1.shape, F32)
        fill = pltpu.make_async_copy(ybuf1, y_hbm.at[pl.ds(T_ALL * N_FCHUNK, tile_rows), :], ssem.at[0])
        fill.start()
        fill.wait()
        gather_rows(0, xbuf0, gsem.at[0])

    @pl.when((i < n_used) & (i % 2 == 0))
    def _():
        step(0, xbuf0, xbuf1, ybuf0, ybuf1)

    @pl.when((i < n_used) & (i % 2 == 1))
    def _():
        step(1, xbuf1, xbuf0, ybuf1, ybuf0)


def _moe(u2t, w_sorted, row_tok, row_dst, tile_lo, tile_hi, n_used, w_gate, w_up, w_down, layer):
    lo = lambda shape: pl.BlockSpec(shape, lambda i, tok, dst, tlo, thi, nused: (layer, tlo[i], 0, 0))
    hi = lambda shape: pl.BlockSpec(shape, lambda i, tok, dst, tlo, thi, nused: (layer, thi[i], 0, 0))
    up_shape = (1, 1, D_MODEL, D_FF_EXPERT)
    down_shape = (1, 1, D_FF_EXPERT, D_MODEL)
    tile_buf = pltpu.VMEM((TM_MOE * N_FCHUNK, LANES), F32)
    grid_spec = pltpu.PrefetchScalarGridSpec(
        num_scalar_prefetch=5,
        grid=(N_MOE_TILES,),
        in_specs=[pl.BlockSpec(memory_space=pl.ANY),
                  pl.BlockSpec((TM_MOE, 2), lambda i, *_: (i, 0)),
                  lo(up_shape), lo(up_shape), lo(down_shape),
                  hi(up_shape), hi(up_shape), hi(down_shape)],
        out_specs=pl.BlockSpec(memory_space=pl.ANY),
        scratch_shapes=[tile_buf, tile_buf, tile_buf, tile_buf,
                        pltpu.SemaphoreType.DMA((2,)),
                        pltpu.SemaphoreType.DMA((2,))],
    )
    return pl.pallas_call(
        _moe_kernel,
        grid_spec=grid_spec,
        out_shape=jax.ShapeDtypeStruct((N_Y_ROWS * N_FCHUNK, LANES), F32),
        compiler_params=pltpu.CompilerParams(dimension_semantics=("arbitrary",), vmem_limit_bytes=MOE_VMEM_LIMIT,
                                             has_side_effects=True),
        name="moe",
    )(row_tok, row_dst, tile_lo, tile_hi, n_used, u2t, w_sorted, w_gate, w_up, w_down, w_gate, w_up, w_down)


def _route_tables(cls):
    order = jnp.argsort(cls, stable=True).astype(jnp.int32)
    cids = jnp.arange(N_CLASSES, dtype=jnp.int32)
    counts = jnp.sum((cls[:, None] == cids[None, :]).astype(jnp.int32), axis=0)
    offs = jnp.cumsum(counts) - counts
    padded = (counts + TM_MOE - 1) // TM_MOE * TM_MOE
    ends = jnp.cumsum(padded)
    offs_p = ends - padded
    n_used = ends[-1] // TM_MOE
    tile_start = jnp.arange(N_MOE_TILES, dtype=jnp.int32) * TM_MOE
    tile_cls = jnp.sum((ends[None, :] <= jnp.minimum(tile_start, ends[-1] - TM_MOE)[:, None]).astype(jnp.int32),
                       axis=1)
    row = jnp.arange(N_MOE_ROWS, dtype=jnp.int32)
    row_cls = jnp.repeat(tile_cls, TM_MOE)
    k = row - offs_p[row_cls]
    valid = (k < counts[row_cls]) & (row < ends[-1])
    src = offs[row_cls] + jnp.clip(k, 0, jnp.maximum(counts[row_cls] - 1, 0))
    row_tok = order[jnp.clip(src, 0, T_ALL - 1)]
    spare = T_ALL + ((row // TM_MOE) % 2) * TM_MOE + row % TM_MOE
    row_dst = jnp.where(valid, row_tok, spare)
    row_dst = jnp.concatenate([T_ALL + TM_MOE + jnp.arange(TM_MOE, dtype=jnp.int32), row_dst])
    grp = tile_cls // len(_PAIRS)
    pair = tile_cls % len(_PAIRS)
    tile_lo = grp * EXPERTS_PER_GROUP + jnp.array([p[0] for p in _PAIRS], jnp.int32)[pair]
    tile_hi = grp * EXPERTS_PER_GROUP + jnp.array([p[1] for p in _PAIRS], jnp.int32)[pair]
    i32 = lambda v: v.astype(jnp.int32)
    return i32(row_tok), i32(row_dst), i32(tile_lo), i32(tile_hi), i32(n_used).reshape(1)


def _final_kernel(x1_ref, y_ref, mod_ref, pg_ref, pb_ref, *o_refs, tm):
    i = pl.program_id(0)
    r = _mod_row(i * tm)
    gate2 = mod_ref[pl.ds(r, 1), 5 * D_MODEL:6 * D_MODEL]
    out = _ln(ALPHA * x1_ref[...] + gate2 * _load_token_tiles(y_ref, tm)) * pg_ref[...] + pb_ref[...]
    if len(o_refs) == 1:
        o_refs[0][...] = out
    else:
        @pl.when(i < N_CTX_TOK // tm)
        def _():
            o_refs[0][...] = out

        @pl.when(i >= N_CTX_TOK // tm)
        def _():
            o_refs[1][...] = out


def _final(x1, y, mod, pg, pb, split):
    tm = TM_PROJ
    n_ctx_blk = N_CTX_TOK // tm
    const = lambda shape: pl.BlockSpec(shape, lambda i: (0,) * len(shape))
    rows = pl.BlockSpec((tm, D_MODEL), lambda i: (i, 0))
    if split:
        out_specs = [pl.BlockSpec((tm, D_MODEL), lambda i: (jnp.minimum(i, n_ctx_blk - 1), 0)),
                     pl.BlockSpec((tm, D_MODEL), lambda i: (jnp.maximum(i - n_ctx_blk, 0), 0))]
        out_shape = [jax.ShapeDtypeStruct((N_CTX_TOK, D_MODEL), F32), jax.ShapeDtypeStruct((N_LAT_TOK, D_MODEL), F32)]
    else:
        out_specs = [rows]
        out_shape = [jax.ShapeDtypeStruct((T_ALL, D_MODEL), F32)]
    return pl.pallas_call(
        functools.partial(_final_kernel, tm=tm),
        grid=(T_ALL // tm,),
        in_specs=[rows, pl.BlockSpec((tm * N_FCHUNK, LANES), lambda i: (i, 0)),
                  const((N_MOD_ROWS, 6 * D_MODEL)), const((1, D_MODEL)), const((1, D_MODEL))],
        out_specs=out_specs,
        out_shape=out_shape,
        compiler_params=_cparams(1),
        name="final",
    )(x1, y, mod, pg, pb)


def _grid_pos(n_tok):
    rows = n_tok // GRID_W
    r, col = jnp.meshgrid(jnp.arange(rows, dtype=F32), jnp.arange(GRID_W, dtype=F32), indexing='ij')
    quarter = D_MODEL // 4
    omega = 1.0 / (10000.0 ** (jnp.arange(quarter, dtype=F32) / quarter))

    def emb(p):
        ang = p.reshape(-1)[:, None] * omega[None, :]
        return jnp.concatenate([jnp.sin(ang), jnp.cos(ang)], axis=-1)

    return jnp.concatenate([emb(r), emb(col)], axis=-1)


def _pad_lanes(v, start):
    v = v.reshape(1, -1).astype(F32)
    return jnp.pad(v, ((0, 0), (start, LANES - start - v.shape[1])))


def kernel(x_prompt, x_sample, state_mlstm_C, state_mlstm_n, state_mlstm_m, state_ssd, c, c_ctx, w_in, w_o, mlstm_b_i, mlstm_b_f, mlstm_norm_g, ssd_conv_w, ssd_conv_b, ssd_dt_bias, ssd_A_log, ssd_D, ssd_norm_g, conv_dw_w, conv_dw_b, conv_ln_g, conv_ln_b, w_ada, b_ada, post1_g, post1_b, post2_g, post2_b, w_router, b_router, w_e_gate, w_e_up, w_e_down):
    cvec = jnp.concatenate([c_ctx[None, :], c, jnp.zeros((N_MOD_ROWS - 1 - DEC_BATCH, D_MODEL), F32)], axis=0)
    mod_all = _ada(cvec, w_ada, b_ada)
    x = _embed(x_prompt.reshape(N_CTX_TOK, D_MODEL), x_sample.reshape(N_LAT_TOK, D_MODEL), _grid_pos(DEC_SEQ))
    w_rt = w_router.T
    b_r = b_router.reshape(N_EXPERTS, 1)

    a_end = 4 * W_A + N_DIR * 2 * H_A
    b_end = a_end + W_B + W_XBC + N_DIR * H_B
    init = (state_mlstm_C,
            state_mlstm_n.reshape(DEC_BATCH, DEPTH, N_DIR * H_A, DH_A),
            jnp.broadcast_to(state_mlstm_m.reshape(DEC_BATCH, DEPTH, N_DIR * H_A, 1),
                             (DEC_BATCH, DEPTH, N_DIR * H_A, LANES)))
    st_c = st_n = st_m = st_h = None
    for l in range(DEPTH):
        w = w_in[l]
        xbc0 = a_end + W_B
        w_main = jnp.concatenate([w[:, W_A:2 * W_A], w[:, b_end:], w[:, xbc0 + W_B:xbc0 + W_B + G_B * N_B]],
                                 axis=1).astype(BF16)
        w_small = jnp.concatenate([w[:, 4 * W_A:a_end], w[:, b_end - N_DIR * H_B:b_end],
                                   jnp.zeros((D_MODEL, LANES - _DT_COL0 - N_DIR * H_B), F32)], axis=1)
        w_t = jnp.concatenate([w[:, 0:W_A], w[:, 2 * W_A:4 * W_A], w[:, a_end:xbc0 + W_B],
                               w[:, xbc0 + W_B + G_B * N_B:xbc0 + W_XBC], w_small],
                              axis=1).T.astype(BF16)
        w_small = w_small.astype(BF16)
        gate_bias = (_pad_lanes(jnp.stack([mlstm_b_i[l], mlstm_b_f[l]], axis=1), 0)
                     + _pad_lanes(ssd_dt_bias[l], _DT_COL0))
        alog_row = _pad_lanes(ssd_A_log[l], _DT_COL0)
        mod = mod_all[l]

        proj, gates, proj_t = _inproj(x, mod, w_main, w_small, w_t)

        m_norm = jnp.broadcast_to(mlstm_norm_g[l].reshape(W_A, 1), (W_A, LANES))
        mlstm_args = (proj, gates, proj_t, gate_bias, gate_bias.reshape(LANES, 1), m_norm)
        ha_c, st_c, st_n, st_m = _mlstm(*mlstm_args, SEQ, BATCH, 0, l, prev=(st_c, st_n, st_m))
        (ha_l,) = _mlstm(*mlstm_args, DEC_SEQ, DEC_BATCH, N_CTX_TOK, l, init=init)

        nb = G_B * N_B
        rep = lambda v: jnp.broadcast_to(v[..., None], v.shape + (LANES,))
        cw, cb = ssd_conv_w[l], ssd_conv_b[l]
        conv = (rep(cw[:, 0:W_B]), rep(cb[0:W_B]), rep(cw[:, W_B + nb:W_XBC]), rep(cb[W_B + nb:W_XBC]),
                jnp.pad(cw[:, W_B:W_B + nb], ((0, 8 - SSM_CONV), (0, 0))), cb[W_B:W_B + nb].reshape(1, nb))
        ssd_args = (proj, gates, proj_t, gate_bias, gate_bias.reshape(LANES, 1), alog_row, alog_row.reshape(LANES, 1),
                    conv, rep(jnp.repeat(ssd_D[l], P_B)), rep(ssd_norm_g[l]))
        hb_c, st_h = _ssd(*ssd_args, SEQ, BATCH, 0, l, prev=st_h)
        (hb_l,) = _ssd(*ssd_args, DEC_SEQ, DEC_BATCH, N_CTX_TOK, l, init=state_ssd)

        hc = _conf(proj, jnp.pad(conv_dw_w[l], ((0, 32 - CONV_W), (0, 0))), conv_dw_b[l].reshape(1, W_C),
                   conv_ln_g[l].reshape(1, W_C), conv_ln_b[l].reshape(1, W_C))

        x1, u2t, route = _outproj(ha_c, ha_l, hb_c, hb_l, hc, x, mod, w_o[l].astype(BF16), post1_g[l].reshape(1, D_MODEL),
                                  post1_b[l].reshape(1, D_MODEL), w_rt, b_r)
        row_tok, row_dst, tile_lo, tile_hi, n_used = _route_tables(route[0].astype(jnp.int32))
        w_sorted = jnp.stack([route[1][row_tok], route[2][row_tok]], axis=1)
        y = _moe(u2t, w_sorted, row_tok, row_dst, tile_lo, tile_hi, n_used, w_e_gate, w_e_up, w_e_down, l)
        outs = _final(x1, y, mod, post2_g[l].reshape(1, D_MODEL), post2_b[l].reshape(1, D_MODEL), l == DEPTH - 1)
        x = outs[0]

    y_prompt = outs[0].reshape(BATCH, SEQ, D_MODEL)
    y_sample = outs[1].reshape(DEC_BATCH, DEC_SEQ, D_MODEL)
    return (y_prompt, y_sample, st_c, st_n.reshape(BATCH, DEPTH, N_DIR, H_A, DH_A),
            st_m[:, :, :, 0].reshape(BATCH, DEPTH, N_DIR, H_A), st_h)
```

```python
import functools

import jax
import jax.numpy as jnp
from jax import lax
from jax.experimental import pallas as pl
from jax.experimental.pallas import tpu as pltpu

D_MODEL = 1024
BATCH = 32
SEQ = 256
DEPTH = 2
DEC_BATCH = 2
DEC_SEQ = 1024
GRID_W = 64
N_DIR = 2
CHUNK = 128
H_A = 4
DH_A = 128
W_A = H_A * DH_A
H_B = 8
P_B = 64
W_B = H_B * P_B
G_B = 2
N_B = 128
W_XBC = W_B + 2 * G_B * N_B
SSM_CONV = 3
W_C = 512
CONV_W = 31
D_MIX = W_A + W_B + W_C
N_EXPERTS = 16
N_EGROUPS = 4
EXPERTS_PER_GROUP = N_EXPERTS // N_EGROUPS
D_FF_EXPERT = 512
ALPHA = (2 * DEPTH) ** 0.25
EPS = 1e-5
F32 = jnp.float32
BF16 = jnp.bfloat16

N_CTX_TOK = BATCH * SEQ
N_LAT_TOK = DEC_BATCH * DEC_SEQ
T_ALL = N_CTX_TOK + N_LAT_TOK
N_MOD_ROWS = 8
D_MAIN = W_A + 2 * W_C + G_B * N_B
D_T = 3 * W_A + 2 * W_B + G_B * N_B + 128
LANES = 128
HALO = 16
CONV_ROWS = 256
TM_PROJ = 512
TM_MOE = 256
VMEM_LIMIT = 48 * 1024 * 1024
MOE_VMEM_LIMIT = 56 * 1024 * 1024

_NT = (((1,), (1,)), ((), ()))
_TN = (((0,), (0,)), ((), ()))


def _ln(x):
    mu = jnp.mean(x, axis=-1, keepdims=True)
    xc = x - mu
    var = jnp.mean(xc * xc, axis=-1, keepdims=True)
    return xc * lax.rsqrt(var + EPS)


def _sigmoid(x):
    return 1.0 / (1.0 + jnp.exp(-x))


def _silu(x):
    return x * _sigmoid(x)


def _softplus(x):
    return jnp.maximum(x, 0.0) + jnp.log1p(jnp.exp(-jnp.abs(x)))


def _dot(a, b):
    return jnp.dot(a, b, preferred_element_type=F32)


def _dotg(a, b, dims):
    return lax.dot_general(a, b, dims, preferred_element_type=F32)


def _tri_cumsum(tri, x):
    hi = x.astype(BF16)
    r1 = x - hi.astype(F32)
    mid = r1.astype(BF16)
    lo = (r1 - mid.astype(F32)).astype(BF16)
    return _dot(tri, hi) + _dot(tri, mid) + _dot(tri, lo)


def _tri_mask(d):
    row = lax.broadcasted_iota(jnp.int32, (CHUNK, CHUNK), 0)
    col = lax.broadcasted_iota(jnp.int32, (CHUNK, CHUNK), 1)
    return (row >= col) if d == 0 else (row <= col)


def _mod_row(row_start):
    return jnp.where(row_start < N_CTX_TOK, 0, 1 + (row_start - N_CTX_TOK) // DEC_SEQ)


def _cparams(n_axes):
    return pltpu.CompilerParams(dimension_semantics=("arbitrary",) * n_axes, vmem_limit_bytes=VMEM_LIMIT)


def _ada_kernel(c_ref, w_ref, b_ref, o_ref):
    o_ref[0] = _dot(_silu(c_ref[...]), w_ref[0]) + b_ref[0]


def _ada(cvec, w_ada, b_ada):
    tn = 1536
    return pl.pallas_call(
        _ada_kernel,
        grid=(DEPTH, 6 * D_MODEL // tn),
        in_specs=[
            pl.BlockSpec((N_MOD_ROWS, D_MODEL), lambda l, j: (0, 0)),
            pl.BlockSpec((1, D_MODEL, tn), lambda l, j: (l, 0, j)),
            pl.BlockSpec((1, 1, tn), lambda l, j: (l, 0, j)),
        ],
        out_specs=pl.BlockSpec((1, N_MOD_ROWS, tn), lambda l, j: (l, 0, j)),
        out_shape=jax.ShapeDtypeStruct((DEPTH, N_MOD_ROWS, 6 * D_MODEL), F32),
        compiler_params=_cparams(2),
        name="ada",
    )(cvec, w_ada, b_ada.reshape(DEPTH, 1, 6 * D_MODEL))


def _embed_kernel(xp_ref, xs_ref, pos_ref, o_ref):
    i = pl.program_id(0)

    @pl.when(i < N_CTX_TOK // DEC_SEQ)
    def _():
        o_ref[...] = xp_ref[...]

    @pl.when(i >= N_CTX_TOK // DEC_SEQ)
    def _():
        o_ref[...] = xs_ref[...] + pos_ref[...]


def _embed(xp, xs, pos):
    n_ctx_blk = N_CTX_TOK // DEC_SEQ
    return pl.pallas_call(
        _embed_kernel,
        grid=(T_ALL // DEC_SEQ,),
        in_specs=[
            pl.BlockSpec((DEC_SEQ, D_MODEL), lambda i: (jnp.minimum(i, n_ctx_blk - 1), 0)),
            pl.BlockSpec((DEC_SEQ, D_MODEL), lambda i: (jnp.maximum(i - n_ctx_blk, 0), 0)),
            pl.BlockSpec((DEC_SEQ, D_MODEL), lambda i: (0, 0)),
        ],
        out_specs=pl.BlockSpec((DEC_SEQ, D_MODEL), lambda i: (i, 0)),
        out_shape=jax.ShapeDtypeStruct((T_ALL, D_MODEL), F32),
        compiler_params=_cparams(1),
        name="embed",
    )(xp, xs, pos)


def _inproj_kernel(x_ref, mod_ref, wm_ref, wg_ref, wt_ref, om_ref, og_ref, ot_ref, u_scr, *, tm):
    i = pl.program_id(0)
    j = pl.program_id(1)

    @pl.when(j == 0)
    def _():
        r = _mod_row(i * tm)
        shift = mod_ref[pl.ds(r, 1), 0:D_MODEL]
        scale = mod_ref[pl.ds(r, 1), D_MODEL:2 * D_MODEL]
        u = (_ln(x_ref[...]) * (1.0 + scale) + shift).astype(BF16)
        u_scr[...] = u
        og_ref[...] = _dot(u, wg_ref[...])
        ot_ref[...] = _dotg(wt_ref[...], u, _NT)

    om_ref[...] = _dot(u_scr[...], wm_ref[...])


def _inproj(x, mod, w_main, w_small, w_t):
    tm, tn = TM_PROJ, D_MAIN // 2
    return pl.pallas_call(
        functools.partial(_inproj_kernel, tm=tm),
        grid=(T_ALL // tm, D_MAIN // tn),
        in_specs=[
            pl.BlockSpec((tm, D_MODEL), lambda i, j: (i, 0)),
            pl.BlockSpec((N_MOD_ROWS, 6 * D_MODEL), lambda i, j: (0, 0)),
            pl.BlockSpec((D_MODEL, tn), lambda i, j: (0, j)),
            pl.BlockSpec((D_MODEL, LANES), lambda i, j: (0, 0)),
            pl.BlockSpec((D_T, D_MODEL), lambda i, j: (0, 0)),
        ],
        out_specs=[
            pl.BlockSpec((tm, tn), lambda i, j: (i, j)),
            pl.BlockSpec((tm, LANES), lambda i, j: (i, 0)),
            pl.BlockSpec((D_T, tm), lambda i, j: (0, i)),
        ],
        out_shape=[
            jax.ShapeDtypeStruct((T_ALL, D_MAIN), F32),
            jax.ShapeDtypeStruct((T_ALL, LANES), F32),
            jax.ShapeDtypeStruct((D_T, T_ALL), F32),
        ],
        scratch_shapes=[pltpu.VMEM((tm, D_MODEL), BF16)],
        compiler_params=_cparams(2),
        name="inproj",
    )(x, mod, w_main, w_small, w_t)


def _scan_max(x, lane, d):
    s = 1
    while s < CHUNK:
        if d == 0:
            x = jnp.where(lane >= s, jnp.maximum(x, pltpu.roll(x, s, axis=1)), x)
        else:
            x = jnp.where(lane < CHUNK - s, jnp.maximum(x, pltpu.roll(x, CHUNK - s, axis=1)), x)
        s *= 2
    return x


def _split_dot(x, tri):
    hi = x.astype(BF16)
    r1 = x - hi.astype(F32)
    mid = r1.astype(BF16)
    lo = (r1 - mid.astype(F32)).astype(BF16)
    return _dot(hi, tri) + _dot(mid, tri) + _dot(lo, tri)


def _mlstm_kernel(*refs, seq, has_init, emit_state, layer):
    it = iter(refs)
    k_ref, qt_ref, vt_ref, ot_ref, g_ref, gt_ref, gb_ref, gbt_ref, ng_ref = (next(it) for _ in range(9))
    if has_init:
        c0_ref, n0_ref, m0_ref = (next(it) for _ in range(3))
    if emit_state and layer > 0:
        prev_refs = [next(it) for _ in range(3)]
    out_ref = next(it)
    if emit_state:
        co_ref, no_ref, mo_ref = (next(it) for _ in range(3))
    tg_scr, tgt_scr, b_scr, beta_scr, pm_scr, cumc_scr, hf_scr, c_scr, n_scr, m_scr = (next(it) for _ in range(10))
    n_chunks = seq // CHUNK
    n_gate = N_DIR * 2 * H_A
    if emit_state and layer > 0:
        for prev_ref, st_ref in zip(prev_refs, (co_ref, no_ref, mo_ref)):
            st_ref[0, 0:layer] = prev_ref[0]

    y = g_ref[...] + gb_ref[...]
    lane = lax.broadcasted_iota(jnp.int32, y.shape, 1)
    tg_scr[...] = jnp.where((lane < n_gate) & ((lane & H_A) != 0), -_softplus(-y), y)
    yt = gt_ref[0:n_gate, :] + gbt_ref[0:n_gate, :]
    rowt = lax.broadcasted_iota(jnp.int32, yt.shape, 0)
    tgt_scr[...] = jnp.where((rowt & H_A) != 0, -_softplus(-yt), yt)

    lane8 = lax.broadcasted_iota(jnp.int32, (8, CHUNK), 1)
    row8 = lax.broadcasted_iota(jnp.int32, (8, LANES), 0)
    masks_t = [_tri_mask(1 - d) for d in range(N_DIR)]

    for d in range(N_DIR):
        tri_t = masks_t[d].astype(BF16)
        tri_c = _tri_mask(d).astype(BF16)
        rows = slice(d * 8, (d + 1) * 8)
        for c in range(n_chunks):
            cs = slice(c * CHUNK, (c + 1) * CHUNK)
            gr8 = tgt_scr[rows, cs]
            b8 = pltpu.roll(_split_dot(gr8, tri_t), H_A, axis=0)
            beta8 = gr8 - b8
            b_scr[rows, cs] = b8
            beta_scr[rows, cs] = beta8
            pm_scr[rows, cs] = _scan_max(beta8, lane8, d)
            cumc_scr[d, cs, :] = _tri_cumsum(tri_c, tg_scr[cs, :])

    def run_direction(d):
        rows = slice(d * 8, (d + 1) * 8)
        if has_init:
            for h in range(H_A):
                c_scr[d, h] = c0_ref[0, 0, d, h].T
            n0 = n0_ref[0, 0]
            m0 = m0_ref[0, 0]
            if d == 1:
                n0 = pltpu.roll(n0, H_A, axis=0)
                m0 = pltpu.roll(m0, H_A, axis=0)
            n_scr[rows, :] = jnp.where(row8 < H_A, n0, 0.0)
            m_scr[rows, :] = jnp.where(row8 < H_A, m0, 0.0)
        else:
            for h in range(H_A):
                c_scr[d, h] = jnp.zeros((DH_A, DH_A), F32)
            n_scr[rows, :] = jnp.zeros((8, DH_A), F32)
            m_scr[rows, :] = jnp.zeros((8, LANES), F32)

        mask_t = masks_t[d]
        pos = CHUNK - 1 if d == 0 else 0

        for ci in range(n_chunks):
            c = ci if d == 0 else n_chunks - 1 - ci
            cs = slice(c * CHUNK, (c + 1) * CHUNK)
            b8 = b_scr[rows, cs]
            beta8 = beta_scr[rows, cs]
            m8 = m_scr[rows, :]
            n8 = n_scr[rows, :]
            mu8 = jnp.maximum(m8, pm_scr[rows, cs])
            mu_last = mu8[:, pos:pos + 1]
            w_int8 = jnp.exp(m8 - mu8)
            emt8 = jnp.exp(-(b8 + mu8))
            w_tok8 = jnp.exp(beta8 - mu_last)
            w_prev8 = jnp.exp(m8 - mu_last)
            m_new8 = b8[:, pos:pos + 1] + mu_last
            n16b = jnp.concatenate([n8, jnp.zeros_like(n8)], axis=0).astype(BF16)
            w_tok16 = jnp.concatenate([w_tok8, jnp.zeros_like(w_tok8)], axis=0)

            gc = tg_scr[cs, :]
            cum_c = cumc_scr[d, cs, :]
            nk_acc = jnp.zeros((8, DH_A), F32)
            for h in range(H_A):
                hs = slice(h * DH_A, (h + 1) * DH_A)
                col_i = d * 2 * H_A + h
                col_f = col_i + H_A
                beta_c = gc[:, col_i:col_i + 1] - cum_c[:, col_f:col_f + 1]
                kb = k_ref[cs, hs].astype(BF16)
                qtb = (qt_ref[hs, cs] * (DH_A ** -0.5)).astype(BF16)
                vt = vt_ref[hs, cs]
                ct = c_scr[d, h]
                w_int_r = w_int8[h:h + 1, :]
                w_prev_r = w_prev8[h:h + 1, :]

                res = _dot(jnp.concatenate([kb, ct.astype(BF16), n16b], axis=0), qtb)
                st = res[0:CHUNK] * jnp.exp(jnp.where(mask_t, beta_c - mu8[h:h + 1, :], -jnp.inf))
                num = w_int_r * res[CHUNK:CHUNK + DH_A] + _dot(vt.astype(BF16), st.astype(BF16))
                den_r = (w_int_r * res[CHUNK + DH_A + h:CHUNK + DH_A + h + 1]
                         + jnp.sum(st, axis=0, keepdims=True))
                ht = num * (1.0 / jnp.maximum(jnp.abs(den_r), emt8[h:h + 1, :]))

                upd = _dot(jnp.concatenate([vt * w_tok8[h:h + 1, :], w_tok16], axis=0).astype(BF16), kb)
                c_scr[d, h] = w_prev_r * ct + upd[0:DH_A]
                nk_acc = jnp.where(row8 == h, upd[DH_A:DH_A + 8], nk_acc)

                if d == 0:
                    hf_scr[hs, cs] = ht
                else:
                    hsum = hf_scr[hs, cs] + ht
                    mean = jnp.sum(hsum, axis=0, keepdims=True) * (1.0 / DH_A)
                    xc = hsum - mean
                    var = jnp.sum(xc * xc, axis=0, keepdims=True) * (1.0 / DH_A)
                    hn = xc * lax.rsqrt(var + EPS) * ng_ref[hs, :]
                    out_ref[hs, cs] = (hn * _sigmoid(ot_ref[hs, cs])).astype(out_ref.dtype)
            n_scr[rows, :] = w_prev8 * n8 + nk_acc
            m_scr[rows, :] = jnp.where(row8 < H_A, jnp.broadcast_to(m_new8, (8, LANES)), 0.0)

        if emit_state:
            for h in range(H_A):
                co_ref[0, layer, d, h] = c_scr[d, h].T
            no_ref[0, layer, d * H_A:(d + 1) * H_A, :] = n_scr[d * 8:d * 8 + H_A, :]
            mo_ref[0, layer, d * H_A:(d + 1) * H_A, :] = m_scr[d * 8:d * 8 + H_A, :]

    run_direction(0)
    run_direction(1)


def _mlstm(proj, gates, proj_t, gate_bias, gate_bias_t, norm_g_rep, seq, n_seq, row_off, layer, init=None, prev=None):
    blk0 = row_off // seq
    has_init = init is not None
    emit_state = not has_init
    feat = lambda r: pl.BlockSpec((W_A, seq), lambda s: (r, s + blk0))
    const = lambda shape: pl.BlockSpec(shape, lambda s: (0,) * len(shape))
    st_shapes = [(N_DIR, H_A, DH_A, DH_A), (N_DIR * H_A, DH_A), (N_DIR * H_A, LANES)]

    def st_specs(n_layers, first):
        return [pl.BlockSpec((1, n_layers) + shp, lambda s, nd=len(shp): (s, first) + (0,) * nd) for shp in st_shapes]

    in_specs = [pl.BlockSpec((seq, W_A), lambda s: (s + blk0, 0)), feat(0), feat(1), feat(2),
                pl.BlockSpec((seq, LANES), lambda s: (s + blk0, 0)),
                pl.BlockSpec((LANES, seq), lambda s: (D_T // LANES - 1, s + blk0)),
                const((1, LANES)), const((LANES, 1)), const((W_A, LANES))]
    args = [proj, proj_t, proj_t, proj_t, gates, proj_t, gate_bias, gate_bias_t, norm_g_rep]
    if has_init:
        in_specs += st_specs(1, layer)
        args += list(init)
    if emit_state and layer > 0:
        in_specs += st_specs(layer, 0)
        args += list(prev)
    out_specs = [pl.BlockSpec((W_A, seq), lambda s: (0, s))]
    out_shape = [jax.ShapeDtypeStruct((W_A, n_seq * seq), BF16)]
    if emit_state:
        out_specs += st_specs(layer + 1, 0)
        out_shape += [jax.ShapeDtypeStruct((n_seq, layer + 1) + shp, F32) for shp in st_shapes]
    return pl.pallas_call(
        functools.partial(_mlstm_kernel, seq=seq, has_init=has_init, emit_state=emit_state, layer=layer),
        grid=(n_seq,),
        in_specs=in_specs,
        out_specs=out_specs,
        out_shape=out_shape,
        scratch_shapes=[pltpu.VMEM((seq, LANES), F32),
                        pltpu.VMEM((N_DIR * 8, seq), F32),
                        pltpu.VMEM((N_DIR * 8, seq), F32),
                        pltpu.VMEM((N_DIR * 8, seq), F32),
                        pltpu.VMEM((N_DIR * 8, seq), F32),
                        pltpu.VMEM((N_DIR, seq, LANES), F32),
                        pltpu.VMEM((W_A, seq), F32),
                        pltpu.VMEM((N_DIR, H_A, DH_A, DH_A), F32),
                        pltpu.VMEM((N_DIR * 8, DH_A), F32),
                        pltpu.VMEM((N_DIR * 8, LANES), F32)],
        compiler_params=_cparams(1),
        name="mlstm_lat" if has_init else "mlstm_ctx",
    )(*args)


_DT_COL0 = N_DIR * 2 * H_A
_HG = H_B // G_B
_WG = _HG * P_B


def _conv3_lanes(ref, w_ref, b_ref, out_scr, seq):
    lane = lax.broadcasted_iota(jnp.int32, (CHUNK, seq), 1)
    for r in range(ref.shape[0] // CHUNK):
        rs = slice(r * CHUNK, (r + 1) * CHUNK)
        cur = ref[rs, :]
        prev = jnp.where(lane == 0, 0.0, pltpu.roll(cur, 1, axis=1))
        nxt = jnp.where(lane == seq - 1, 0.0, pltpu.roll(cur, seq - 1, axis=1))
        for c in range(seq // CHUNK):
            cs = slice(c * CHUNK, (c + 1) * CHUNK)
            acc = (w_ref[0, rs, :] * prev[:, cs] + w_ref[1, rs, :] * cur[:, cs] + w_ref[2, rs, :] * nxt[:, cs]
                   + b_ref[rs, :])
            out_scr[rs, cs] = _silu(acc).astype(out_scr.dtype)


def _ssd_kernel(*refs, seq, has_init, emit_state, layer):
    it = iter(refs)
    (zt_ref, xt_ref, ct_ref, b_ref, g_ref, gt_ref, gb_ref, gbt_ref, alog_ref, alogt_ref,
     cwx_ref, cbx_ref, cwc_ref, cbc_ref, cwb_ref, cbb_ref, dsk_ref, ng_ref) = (next(it) for _ in range(18))
    if has_init:
        h0_ref = next(it)
    if emit_state and layer > 0:
        prev_ref = next(it)
    out_ref = next(it)
    if emit_state:
        ho_ref = next(it)
    pad_scr, xs_scr, cs_scr, bs_scr, dtr_scr, br_scr, cumc_scr, yt_scr, h_scr = (next(it) for _ in range(9))
    n_chunks = seq // CHUNK
    pad = 8
    if emit_state and layer > 0:
        ho_ref[0, 0:layer] = prev_ref[0]

    _conv3_lanes(xt_ref, cwx_ref, cbx_ref, xs_scr, seq)
    _conv3_lanes(ct_ref, cwc_ref, cbc_ref, cs_scr, seq)
    nb = G_B * N_B
    pad_scr[0:pad, :] = jnp.zeros((pad, nb), F32)
    pad_scr[pad + seq:2 * pad + seq, :] = jnp.zeros((pad, nb), F32)
    pad_scr[pad:pad + seq, :] = b_ref[...]
    for c in range(n_chunks):
        acc = cbb_ref[...]
        for k in range(SSM_CONV):
            r = c * CHUNK + pad - SSM_CONV // 2 + k
            acc = acc + cwb_ref[k:k + 1, :] * pad_scr[r:r + CHUNK, :]
        bs_scr[c * CHUNK:(c + 1) * CHUNK, :] = _silu(acc).astype(bs_scr.dtype)

    lac = _softplus(g_ref[...] + gb_ref[...]) * (-jnp.exp(alog_ref[...]))
    r0 = _DT_COL0
    dtr = _softplus(gt_ref[r0:r0 + N_DIR * H_B, :] + gbt_ref[r0:r0 + N_DIR * H_B, :])
    dtr_scr[...] = dtr
    lar = dtr * (-jnp.exp(alogt_ref[r0:r0 + N_DIR * H_B, :]))
    masks_t = [_tri_mask(1 - d) for d in range(N_DIR)]
    for d in range(N_DIR):
        tri_t = masks_t[d].astype(BF16)
        tri_c = _tri_mask(d).astype(BF16)
        rows = slice(d * H_B, (d + 1) * H_B)
        for c in range(n_chunks):
            cs = slice(c * CHUNK, (c + 1) * CHUNK)
            br_scr[rows, cs] = _split_dot(lar[rows, cs], tri_t)
            cumc_scr[d, cs, :] = _tri_cumsum(tri_c, lac[cs, :])

    row_blk = lax.broadcasted_iota(jnp.int32, (_WG, 1), 0) // P_B

    def run_direction(d):
        for g in range(G_B):
            for hh in range(_HG):
                if has_init:
                    h_scr[d, g, hh * P_B:(hh + 1) * P_B, :] = h0_ref[0, 0, d, g * _HG + hh]
                else:
                    h_scr[d, g, hh * P_B:(hh + 1) * P_B, :] = jnp.zeros((P_B, N_B), F32)

        mask_t = masks_t[d]
        pos = CHUNK - 1 if d == 0 else 0

        for ci in range(n_chunks):
            c = ci if d == 0 else n_chunks - 1 - ci
            cs = slice(c * CHUNK, (c + 1) * CHUNK)
            cum_c = cumc_scr[d, cs, :]
            for g in range(G_B):
                bg = bs_scr[cs, g * N_B:(g + 1) * N_B]
                ctg = cs_scr[g * N_B:(g + 1) * N_B, cs]
                h_st = h_scr[d, g]
                res = _dot(jnp.concatenate([bg, h_st.astype(BF16)], axis=0), ctg)
                cbt = res[0:CHUNK]
                inter = res[CHUNK:CHUNK + _WG]
                xw = []
                decay = jnp.zeros((_WG, 1), F32)
                for hh in range(_HG):
                    head = g * _HG + hh
                    r = d * H_B + head
                    b_r = br_scr[r:r + 1, cs]
                    dt_r = dtr_scr[r:r + 1, cs]
                    b_c = cum_c[:, _DT_COL0 + r:_DT_COL0 + r + 1]
                    gt = b_r[:, pos:pos + 1]
                    ps = slice(head * P_B, (head + 1) * P_B)
                    xh = xs_scr[ps, cs]
                    st = (cbt * jnp.exp(jnp.where(mask_t, b_r - b_c, -jnp.inf))).astype(BF16)
                    y = _dot((xh * dt_r).astype(BF16), st) + jnp.exp(b_r) * inter[hh * P_B:(hh + 1) * P_B]
                    xw.append(xh * (jnp.exp(gt - b_r) * dt_r))
                    decay = jnp.where(row_blk == hh, jnp.exp(gt), decay)
                    if d == 0:
                        yt_scr[ps, cs] = y
                    else:
                        yt_scr[ps, cs] = yt_scr[ps, cs] + y + dsk_ref[ps, :] * xh
                h_scr[d, g] = decay * h_st + _dot(jnp.concatenate(xw, axis=0).astype(BF16), bg)
            if d == 1:
                yz = yt_scr[:, cs] * _silu(zt_ref[:, cs])
                rms = lax.rsqrt(jnp.sum(yz * yz, axis=0, keepdims=True) * (1.0 / W_B) + EPS)
                out_ref[:, cs] = (yz * rms * ng_ref[...]).astype(out_ref.dtype)

        if emit_state:
            for g in range(G_B):
                for hh in range(_HG):
                    ho_ref[0, layer, d, g * _HG + hh] = h_scr[d, g, hh * P_B:(hh + 1) * P_B, :]

    run_direction(0)
    run_direction(1)


def _ssd(proj, gates, proj_t, gate_bias, gate_bias_t, alog_row, alog_col, conv, d_rep, norm_g_rep,
         seq, n_seq, row_off, layer, init=None, prev=None):
    blk0 = row_off // seq
    has_init = init is not None
    emit_state = not has_init
    nb = G_B * N_B
    zr = 3 * W_A // W_B
    cr = (3 * W_A + 2 * W_B) // nb
    gr = (3 * W_A + 2 * W_B + nb) // LANES
    bc = (W_A + 2 * W_C) // nb
    const = lambda shape: pl.BlockSpec(shape, lambda s: (0,) * len(shape))
    st_spec = lambda n_layers, first: pl.BlockSpec((1, n_layers, N_DIR, H_B, P_B, N_B),
                                                   lambda s: (s, first, 0, 0, 0, 0))
    in_specs = [pl.BlockSpec((W_B, seq), lambda s: (zr, s + blk0)),
                pl.BlockSpec((W_B, seq), lambda s: (zr + 1, s + blk0)),
                pl.BlockSpec((nb, seq), lambda s: (cr, s + blk0)),
                pl.BlockSpec((seq, nb), lambda s: (s + blk0, bc)),
                pl.BlockSpec((seq, LANES), lambda s: (s + blk0, 0)),
                pl.BlockSpec((LANES, seq), lambda s: (gr, s + blk0)),
                const((1, LANES)), const((LANES, 1)), const((1, LANES)), const((LANES, 1)),
                const((SSM_CONV, W_B, LANES)), const((W_B, LANES)), const((SSM_CONV, nb, LANES)), const((nb, LANES)),
                const((8, nb)), const((1, nb)), const((W_B, LANES)), const((W_B, LANES))]
    args = [proj_t, proj_t, proj_t, proj, gates, proj_t, gate_bias, gate_bias_t, alog_row, alog_col,
            *conv, d_rep, norm_g_rep]
    if has_init:
        in_specs.append(st_spec(1, layer))
        args.append(init)
    if emit_state and layer > 0:
        in_specs.append(st_spec(layer, 0))
        args.append(prev)
    out_specs = [pl.BlockSpec((W_B, seq), lambda s: (0, s))]
    out_shape = [jax.ShapeDtypeStruct((W_B, n_seq * seq), BF16)]
    if emit_state:
        out_specs.append(st_spec(layer + 1, 0))
        out_shape.append(jax.ShapeDtypeStruct((n_seq, layer + 1, N_DIR, H_B, P_B, N_B), F32))
    return pl.pallas_call(
        functools.partial(_ssd_kernel, seq=seq, has_init=has_init, emit_state=emit_state, layer=layer),
        grid=(n_seq,),
        in_specs=in_specs,
        out_specs=out_specs,
        out_shape=out_shape,
        scratch_shapes=[pltpu.VMEM((seq + 16, nb), F32),
                        pltpu.VMEM((W_B, seq), F32),
                        pltpu.VMEM((nb, seq), BF16),
                        pltpu.VMEM((seq, nb), BF16),
                        pltpu.VMEM((N_DIR * H_B, seq), F32),
                        pltpu.VMEM((N_DIR * H_B, seq), F32),
                        pltpu.VMEM((N_DIR, seq, LANES), F32),
                        pltpu.VMEM((W_B, seq), F32),
                        pltpu.VMEM((N_DIR, G_B, _WG, N_B), F32)],
        compiler_params=_cparams(1),
        name="ssd_lat" if has_init else "ssd_ctx",
    )(*args)


def _conf_kernel(a_ref, ap_ref, an_ref, g_ref, gp_ref, gn_ref, w_ref, b_ref, lg_ref, lb_ref, out_ref, pad_scr):
    i = pl.program_id(0)
    n_ctx_blk = N_CTX_TOK // CONV_ROWS
    blk_per_lat = DEC_SEQ // CONV_ROWS
    j = (i - n_ctx_blk) % blk_per_lat
    is_ctx = i < n_ctx_blk
    keep_prev = jnp.where(is_ctx | (j == 0), 0.0, 1.0)
    keep_next = jnp.where(is_ctx | (j == blk_per_lat - 1), 0.0, 1.0)
    pad_scr[0:HALO, :] = ap_ref[...] * _sigmoid(gp_ref[...]) * keep_prev
    pad_scr[HALO:HALO + CONV_ROWS, :] = a_ref[...] * _sigmoid(g_ref[...])
    pad_scr[HALO + CONV_ROWS:2 * HALO + CONV_ROWS, :] = an_ref[...] * _sigmoid(gn_ref[...]) * keep_next
    rc = 64
    for c in range(CONV_ROWS // rc):
        acc = jnp.broadcast_to(b_ref[...], (rc, W_C))
        for k in range(CONV_W):
            r = c * rc + HALO - CONV_W // 2 + k
            acc = acc + w_ref[k:k + 1, :] * pad_scr[r:r + rc, :]
        u = _ln(acc) * lg_ref[...] + lb_ref[...]
        out_ref[c * rc:(c + 1) * rc, :] = _silu(u).astype(out_ref.dtype)


def _conf(proj, dw_w, dw_b, ln_g, ln_b):
    assert SEQ == CONV_ROWS and DEC_SEQ % CONV_ROWS == 0
    ac = W_A // W_C
    hb = CONV_ROWS // HALO
    n_halo = T_ALL // HALO
    const = lambda shape: pl.BlockSpec(shape, lambda i: (0,) * len(shape))

    def specs(c):
        return [pl.BlockSpec((CONV_ROWS, W_C), lambda i: (i, c)),
                pl.BlockSpec((HALO, W_C), lambda i: (jnp.maximum(i * hb - 1, 0), c)),
                pl.BlockSpec((HALO, W_C), lambda i: (jnp.minimum((i + 1) * hb, n_halo - 1), c))]

    return pl.pallas_call(
        _conf_kernel,
        grid=(T_ALL // CONV_ROWS,),
        in_specs=specs(ac) + specs(ac + 1) + [const((32, W_C)), const((1, W_C)), const((1, W_C)), const((1, W_C))],
        out_specs=pl.BlockSpec((CONV_ROWS, W_C), lambda i: (i, 0)),
        out_shape=jax.ShapeDtypeStruct((T_ALL, W_C), BF16),
        scratch_shapes=[pltpu.VMEM((CONV_ROWS + 2 * HALO, W_C), F32)],
        compiler_params=_cparams(1),
        name="conf",
    )(proj, proj, proj, proj, proj, proj, dw_w, dw_b, ln_g, ln_b)


_PAIRS = [(a, b) for a in range(EXPERTS_PER_GROUP) for b in range(a + 1, EXPERTS_PER_GROUP)]
N_CLASSES = N_EGROUPS * len(_PAIRS)
N_FCHUNK = D_MODEL // LANES
N_IN_ROWS = 2 * N_FCHUNK


def _store_token_tiles(ref, val):
    n = val.shape[0]
    for c in range(N_FCHUNK):
        ref[pl.ds(c, n, stride=N_FCHUNK), :] = val[:, c * LANES:(c + 1) * LANES]


def _load_token_tiles(ref, n):
    return jnp.concatenate([ref[pl.ds(c, n, stride=N_FCHUNK), :] for c in range(N_FCHUNK)], axis=1)
N_MOE_ROWS = T_ALL + N_CLASSES * TM_MOE
N_MOE_TILES = N_MOE_ROWS // TM_MOE


def _outproj_kernel(hac_ref, hal_ref, hbc_ref, hbl_ref, hc_ref, x_ref, mod_ref, wo_ref, pg_ref, pb_ref,
                    wr_ref, br_ref, x1_ref, u2_ref, rt_ref, *, tm):
    i = pl.program_id(0)
    r = _mod_row(i * tm)
    gate1 = mod_ref[pl.ds(r, 1), 2 * D_MODEL:3 * D_MODEL]
    shift2 = mod_ref[pl.ds(r, 1), 3 * D_MODEL:4 * D_MODEL]
    scale2 = mod_ref[pl.ds(r, 1), 4 * D_MODEL:5 * D_MODEL]
    is_ctx = i < N_CTX_TOK // tm
    ha = jnp.where(is_ctx, hac_ref[...], hal_ref[...])
    hb = jnp.where(is_ctx, hbc_ref[...], hbl_ref[...])
    mix = (_dotg(ha, wo_ref[0:W_A, :], _TN) + _dotg(hb, wo_ref[W_A:W_A + W_B, :], _TN)
           + _dot(hc_ref[...], wo_ref[W_A + W_B:D_MIX, :]))
    x1 = _ln(ALPHA * x_ref[...] + gate1 * mix) * pg_ref[...] + pb_ref[...]
    x1_ref[...] = x1
    u2 = _ln(x1) * (1.0 + scale2) + shift2
    for c in range(N_FCHUNK):
        u2_ref[pl.ds(c, tm, stride=N_IN_ROWS), :] = u2[:, c * LANES:(c + 1) * LANES]

    logits = lax.dot_general(wr_ref[...], u2, _NT, precision=lax.Precision.HIGHEST,
                             preferred_element_type=F32) + br_ref[...]
    ex = jnp.exp(logits - jnp.max(logits, axis=0, keepdims=True))
    probs = ex / jnp.sum(ex, axis=0, keepdims=True)
    scores = []
    for g in range(N_EGROUPS):
        p = [probs[g * EXPERTS_PER_GROUP + e:g * EXPERTS_PER_GROUP + e + 1, :] for e in range(EXPERTS_PER_GROUP)]
        best = p[0] + p[1]
        for a in range(EXPERTS_PER_GROUP):
            for b in range(a + 1, EXPERTS_PER_GROUP):
                if (a, b) != (0, 1):
                    best = jnp.maximum(best, p[a] + p[b])
        scores.append(best)
    gmax = functools.reduce(jnp.maximum, scores)
    sel = jnp.full(gmax.shape, N_EGROUPS - 1, jnp.int32)
    for g in range(N_EGROUPS - 2, -1, -1):
        sel = jnp.where(scores[g] == gmax, g, sel)
    eidx = lax.broadcasted_iota(jnp.int32, probs.shape, 0)
    pm = jnp.where((eidx // EXPERTS_PER_GROUP) == sel, probs, -jnp.inf)
    p1 = jnp.max(pm, axis=0, keepdims=True)
    i1 = jnp.min(jnp.where(pm == p1, eidx, N_EXPERTS), axis=0, keepdims=True)
    pm2 = jnp.where(eidx == i1, -jnp.inf, pm)
    p2 = jnp.max(pm2, axis=0, keepdims=True)
    i2 = jnp.min(jnp.where(pm2 == p2, eidx, N_EXPERTS), axis=0, keepdims=True)
    den = p1 + p2
    first_lo = i1 < i2
    w_lo = jnp.where(first_lo, p1, p2) / den
    w_hi = jnp.where(first_lo, p2, p1) / den
    a = jnp.minimum(i1, i2) - sel * EXPERTS_PER_GROUP
    b = jnp.maximum(i1, i2) - sel * EXPERTS_PER_GROUP
    pair = jnp.zeros_like(a)
    for k, (pa, pb_) in enumerate(_PAIRS):
        pair = jnp.where((a == pa) & (b == pb_), k, pair)
    cls = sel * len(_PAIRS) + pair
    rt_ref[...] = jnp.broadcast_to(cls, rt_ref.shape)
    wrow = lax.broadcasted_iota(jnp.int32, (LANES, tm), 0)
    wt = jnp.where(wrow == 0, w_lo, jnp.where(wrow == 1, w_hi, 0.0))
    u2_ref[pl.ds(N_FCHUNK, tm, stride=N_IN_ROWS), :] = wt.T
    for c in range(N_FCHUNK + 1, N_IN_ROWS):
        u2_ref[pl.ds(c, tm, stride=N_IN_ROWS), :] = jnp.zeros((tm, LANES), F32)


def _outproj(ha_c, ha_l, hb_c, hb_l, hc, x, mod, w_o, pg, pb, w_rt, b_r):
    tm = TM_PROJ
    n_ctx_blk = N_CTX_TOK // tm
    const = lambda shape: pl.BlockSpec(shape, lambda i: (0,) * len(shape))
    rows = lambda w: pl.BlockSpec((tm, w), lambda i: (i, 0))
    ctx_feat = lambda w: pl.BlockSpec((w, tm), lambda i: (0, jnp.minimum(i, n_ctx_blk - 1)))
    lat_feat = lambda w: pl.BlockSpec((w, tm), lambda i: (0, jnp.maximum(i - n_ctx_blk, 0)))
    return pl.pallas_call(
        functools.partial(_outproj_kernel, tm=tm),
        grid=(T_ALL // tm,),
        in_specs=[ctx_feat(W_A), lat_feat(W_A), ctx_feat(W_B), lat_feat(W_B),
                  rows(W_C), rows(D_MODEL), const((N_MOD_ROWS, 6 * D_MODEL)),
                  const((D_MIX, D_MODEL)), const((1, D_MODEL)), const((1, D_MODEL)),
                  const((N_EXPERTS, D_MODEL)), const((N_EXPERTS, 1))],
        out_specs=[rows(D_MODEL), pl.BlockSpec((tm * N_IN_ROWS, LANES), lambda i: (i, 0)),
                   pl.BlockSpec((8, tm), lambda i: (0, i))],
        out_shape=[jax.ShapeDtypeStruct((T_ALL, D_MODEL), F32),
                   jax.ShapeDtypeStruct((T_ALL * N_IN_ROWS, LANES), F32),
                   jax.ShapeDtypeStruct((8, T_ALL), jnp.int32)],
        compiler_params=_cparams(1),
        name="outproj",
    )(ha_c, ha_l, hb_c, hb_l, hc, x, mod, w_o, pg, pb, w_rt, b_r)


N_Y_ROWS = T_ALL + 2 * TM_MOE


def _moe_kernel(tok_ref, tstart_ref, nval_ref, tlo_ref, thi_ref, nused_ref,
                u_hbm, wgl_ref, wul_ref, wdl_ref, wgh_ref, wuh_ref, wdh_ref, y_hbm,
                xbuf0, xbuf1, ybuf0, ybuf1, gsem, ssem):
    del tlo_ref, thi_ref
    i = pl.program_id(0)
    n_used = nused_ref[0]
    tile_rows = TM_MOE * N_FCHUNK

    def token_tile(ref, tok, rows=N_FCHUNK):
        return ref.at[pl.ds(pl.multiple_of(tok * rows, rows), rows), :]

    def gather_rows(t, xb, sem):
        base = tstart_ref[t]
        for r in range(TM_MOE):
            pltpu.make_async_copy(token_tile(u_hbm, tok_ref[base + r], N_IN_ROWS), token_tile(xb, r, N_IN_ROWS),
                                  sem).start()

    def scatter_rows(t, n_valid, yb, sem, spare):
        base = tstart_ref[t]
        for r in range(TM_MOE):
            dst = jnp.where(r < n_valid, tok_ref[base + r], spare + r)
            pltpu.make_async_copy(token_tile(yb, r), token_tile(y_hbm, dst), sem).start()

    def load_tile(xb):
        x = jnp.concatenate([xb[pl.ds(c, TM_MOE, stride=N_IN_ROWS), :] for c in range(N_FCHUNK)], axis=1)
        w = xb[pl.ds(N_FCHUNK, TM_MOE, stride=N_IN_ROWS), :]
        return x.astype(BF16), w[:, 0:1], w[:, 1:2]

    def wait_gather(xb, sem):
        pltpu.make_async_copy(u_hbm.at[pl.ds(0, TM_MOE * N_IN_ROWS), :], xb, sem).wait()

    def wait_scatter(yb, sem):
        pltpu.make_async_copy(yb, y_hbm.at[pl.ds(0, tile_rows), :], sem).wait()

    def step(s, xb, xb_next, yb, yb_prev):
        @pl.when(i > 0)
        def _():
            wait_scatter(yb, ssem.at[s])

        wait_gather(xb, gsem.at[s])

        @pl.when(n_used > 0)
        def _():
            gather_rows(jnp.minimum(i + 1, n_used - 1), xb_next, gsem.at[1 - s])
            prev = jnp.maximum(i - 1, 0)
            scatter_rows(prev, jnp.where(i > 0, nval_ref[prev], 0), yb_prev, ssem.at[1 - s],
                         T_ALL + (1 - s) * TM_MOE)

        x, w_lo, w_hi = load_tile(xb)

        def ffn(wg_ref, wu_ref, wd_ref, w):
            act = _silu(_dot(x, wg_ref[0, 0].astype(BF16))) * _dot(x, wu_ref[0, 0].astype(BF16)) * w
            return _dot(act.astype(BF16), wd_ref[0, 0].astype(BF16))

        _store_token_tiles(yb, ffn(wgl_ref, wul_ref, wdl_ref, w_lo) + ffn(wgh_ref, wuh_ref, wdh_ref, w_hi))

        @pl.when(i == n_used - 1)
        def _():
            scatter_rows(i, nval_ref[i], yb, ssem.at[s], T_ALL + s * TM_MOE)
            wait_scatter(yb, ssem.at[s])
            wait_scatter(yb_prev, ssem.at[1 - s])
            wait_gather(xb_next, gsem.at[1 - s])

    @pl.when(i == 0)
    def _():
        ybuf1[...] = jnp.zeros(ybuf1.shape, F32)
        fill = pltpu.make_async_copy(ybuf1, y_hbm.at[pl.ds(T_ALL * N_FCHUNK, tile_rows), :], ssem.at[0])
        fill.start()
        fill.wait()
        gather_rows(0, xbuf0, gsem.at[0])

    @pl.when((i < n_used) & (i % 2 == 0))
    def _():
        step(0, xbuf0, xbuf1, ybuf0, ybuf1)

    @pl.when((i < n_used) & (i % 2 == 1))
    def _():
        step(1, xbuf1, xbuf0, ybuf1, ybuf0)


def _moe(u2t, tok_sorted, tile_start, n_valid, tile_lo, tile_hi, n_used, w_gate, w_up, w_down, layer):
    lo = lambda shape: pl.BlockSpec(shape, lambda i, tok, ts, nv, tlo, thi, nused: (layer, tlo[i], 0, 0))
    hi = lambda shape: pl.BlockSpec(shape, lambda i, tok, ts, nv, tlo, thi, nused: (layer, thi[i], 0, 0))
    up_shape = (1, 1, D_MODEL, D_FF_EXPERT)
    down_shape = (1, 1, D_FF_EXPERT, D_MODEL)
    in_buf = pltpu.VMEM((TM_MOE * N_IN_ROWS, LANES), F32)
    out_buf = pltpu.VMEM((TM_MOE * N_FCHUNK, LANES), F32)
    grid_spec = pltpu.PrefetchScalarGridSpec(
        num_scalar_prefetch=6,
        grid=(N_MOE_TILES,),
        in_specs=[pl.BlockSpec(memory_space=pl.ANY),
                  lo(up_shape), lo(up_shape), lo(down_shape),
                  hi(up_shape), hi(up_shape), hi(down_shape)],
        out_specs=pl.BlockSpec(memory_space=pl.ANY),
        scratch_shapes=[in_buf, in_buf, out_buf, out_buf,
                        pltpu.SemaphoreType.DMA((2,)),
                        pltpu.SemaphoreType.DMA((2,))],
    )
    return pl.pallas_call(
        _moe_kernel,
        grid_spec=grid_spec,
        out_shape=jax.ShapeDtypeStruct((N_Y_ROWS * N_FCHUNK, LANES), F32),
        compiler_params=pltpu.CompilerParams(dimension_semantics=("arbitrary",), vmem_limit_bytes=MOE_VMEM_LIMIT,
                                             has_side_effects=True),
        name="moe",
    )(tok_sorted, tile_start, n_valid, tile_lo, tile_hi, n_used, u2t, w_gate, w_up, w_down, w_gate, w_up, w_down)


def _route_tables(cls):
    _, tok_sorted = lax.sort((cls, jnp.arange(T_ALL, dtype=jnp.int32)), num_keys=1)
    cids = jnp.arange(N_CLASSES, dtype=jnp.int32)
    counts = jnp.sum((cls[:, None] == cids[None, :]).astype(jnp.int32), axis=0)
    offs = jnp.cumsum(counts) - counts
    padded = (counts + TM_MOE - 1) // TM_MOE * TM_MOE
    ends = jnp.cumsum(padded)
    offs_p = ends - padded
    n_used = ends[-1] // TM_MOE
    tile_start = jnp.arange(N_MOE_TILES, dtype=jnp.int32) * TM_MOE
    tile_cls = jnp.sum((ends[None, :] <= jnp.minimum(tile_start, ends[-1] - TM_MOE)[:, None]).astype(jnp.int32),
                       axis=1)
    onehot = (tile_cls[:, None] == cids[None, :]).astype(jnp.int32)
    pick = lambda table: jnp.sum(onehot * table[None, :], axis=1)
    k = tile_start - pick(offs_p)
    n_valid = jnp.where(tile_start < ends[-1], jnp.clip(pick(counts) - k, 0, TM_MOE), 0)
    tile_first = jnp.clip(pick(offs) + k, 0, T_ALL - 1)
    tile_lo = pick(cids // len(_PAIRS) * EXPERTS_PER_GROUP + jnp.array([p[0] for p in _PAIRS] * N_EGROUPS, jnp.int32))
    tile_hi = pick(cids // len(_PAIRS) * EXPERTS_PER_GROUP + jnp.array([p[1] for p in _PAIRS] * N_EGROUPS, jnp.int32))
    tok_sorted = jnp.concatenate([tok_sorted, jnp.zeros((TM_MOE,), jnp.int32)])
    i32 = lambda v: v.astype(jnp.int32)
    return tok_sorted, i32(tile_first), i32(n_valid), i32(tile_lo), i32(tile_hi), i32(n_used).reshape(1)


def _final_kernel(x1_ref, y_ref, mod_ref, pg_ref, pb_ref, *o_refs, tm):
    i = pl.program_id(0)
    r = _mod_row(i * tm)
    gate2 = mod_ref[pl.ds(r, 1), 5 * D_MODEL:6 * D_MODEL]
    out = _ln(ALPHA * x1_ref[...] + gate2 * _load_token_tiles(y_ref, tm)) * pg_ref[...] + pb_ref[...]
    if len(o_refs) == 1:
        o_refs[0][...] = out
    else:
        @pl.when(i < N_CTX_TOK // tm)
        def _():
            o_refs[0][...] = out

        @pl.when(i >= N_CTX_TOK // tm)
        def _():
            o_refs[1][...] = out


def _final(x1, y, mod, pg, pb, split):
    tm = TM_PROJ
    n_ctx_blk = N_CTX_TOK // tm
    const = lambda shape: pl.BlockSpec(shape, lambda i: (0,) * len(shape))
    rows = pl.BlockSpec((tm, D_MODEL), lambda i: (i, 0))
    if split:
        out_specs = [pl.BlockSpec((tm, D_MODEL), lambda i: (jnp.minimum(i, n_ctx_blk - 1), 0)),
                     pl.BlockSpec((tm, D_MODEL), lambda i: (jnp.maximum(i - n_ctx_blk, 0), 0))]
        out_shape = [jax.ShapeDtypeStruct((N_CTX_TOK, D_MODEL), F32), jax.ShapeDtypeStruct((N_LAT_TOK, D_MODEL), F32)]
    else:
        out_specs = [rows]
        out_shape = [jax.ShapeDtypeStruct((T_ALL, D_MODEL), F32)]
    return pl.pallas_call(
        functools.partial(_final_kernel, tm=tm),
        grid=(T_ALL // tm,),
        in_specs=[rows, pl.BlockSpec((tm * N_FCHUNK, LANES), lambda i: (i, 0)),
                  const((N_MOD_ROWS, 6 * D_MODEL)), const((1, D_MODEL)), const((1, D_MODEL))],
        out_specs=out_specs,
        out_shape=out_shape,
        compiler_params=_cparams(1),
        name="final",
    )(x1, y, mod, pg, pb)


def _grid_pos(n_tok):
    rows = n_tok // GRID_W
    r, col = jnp.meshgrid(jnp.arange(rows, dtype=F32), jnp.arange(GRID_W, dtype=F32), indexing='ij')
    quarter = D_MODEL // 4
    omega = 1.0 / (10000.0 ** (jnp.arange(quarter, dtype=F32) / quarter))

    def emb(p):
        ang = p.reshape(-1)[:, None] * omega[None, :]
        return jnp.concatenate([jnp.sin(ang), jnp.cos(ang)], axis=-1)

    return jnp.concatenate([emb(r), emb(col)], axis=-1)


def _pad_lanes(v, start):
    v = v.reshape(1, -1).astype(F32)
    return jnp.pad(v, ((0, 0), (start, LANES - start - v.shape[1])))


def kernel(x_prompt, x_sample, state_mlstm_C, state_mlstm_n, state_mlstm_m, state_ssd, c, c_ctx, w_in, w_o, mlstm_b_i, mlstm_b_f, mlstm_norm_g, ssd_conv_w, ssd_conv_b, ssd_dt_bias, ssd_A_log, ssd_D, ssd_norm_g, conv_dw_w, conv_dw_b, conv_ln_g, conv_ln_b, w_ada, b_ada, post1_g, post1_b, post2_g, post2_b, w_router, b_router, w_e_gate, w_e_up, w_e_down):
    cvec = jnp.concatenate([c_ctx[None, :], c, jnp.zeros((N_MOD_ROWS - 1 - DEC_BATCH, D_MODEL), F32)], axis=0)
    mod_all = _ada(cvec, w_ada, b_ada)
    x = _embed(x_prompt.reshape(N_CTX_TOK, D_MODEL), x_sample.reshape(N_LAT_TOK, D_MODEL), _grid_pos(DEC_SEQ))
    w_rt = w_router.T
    b_r = b_router.reshape(N_EXPERTS, 1)

    a_end = 4 * W_A + N_DIR * 2 * H_A
    b_end = a_end + W_B + W_XBC + N_DIR * H_B
    init = (state_mlstm_C,
            state_mlstm_n.reshape(DEC_BATCH, DEPTH, N_DIR * H_A, DH_A),
            jnp.broadcast_to(state_mlstm_m.reshape(DEC_BATCH, DEPTH, N_DIR * H_A, 1),
                             (DEC_BATCH, DEPTH, N_DIR * H_A, LANES)))
    st_c = st_n = st_m = st_h = None
    for l in range(DEPTH):
        w = w_in[l]
        xbc0 = a_end + W_B
        w_main = jnp.concatenate([w[:, W_A:2 * W_A], w[:, b_end:], w[:, xbc0 + W_B:xbc0 + W_B + G_B * N_B]],
                                 axis=1).astype(BF16)
        w_small = jnp.concatenate([w[:, 4 * W_A:a_end], w[:, b_end - N_DIR * H_B:b_end],
                                   jnp.zeros((D_MODEL, LANES - _DT_COL0 - N_DIR * H_B), F32)], axis=1)
        w_t = jnp.concatenate([w[:, 0:W_A], w[:, 2 * W_A:4 * W_A], w[:, a_end:xbc0 + W_B],
                               w[:, xbc0 + W_B + G_B * N_B:xbc0 + W_XBC], w_small],
                              axis=1).T.astype(BF16)
        w_small = w_small.astype(BF16)
        gate_bias = (_pad_lanes(jnp.stack([mlstm_b_i[l], mlstm_b_f[l]], axis=1), 0)
                     + _pad_lanes(ssd_dt_bias[l], _DT_COL0))
        alog_row = _pad_lanes(ssd_A_log[l], _DT_COL0)
        mod = mod_all[l]

        proj, gates, proj_t = _inproj(x, mod, w_main, w_small, w_t)

        m_norm = jnp.broadcast_to(mlstm_norm_g[l].reshape(W_A, 1), (W_A, LANES))
        mlstm_args = (proj, gates, proj_t, gate_bias, gate_bias.reshape(LANES, 1), m_norm)
        ha_c, st_c, st_n, st_m = _mlstm(*mlstm_args, SEQ, BATCH, 0, l, prev=(st_c, st_n, st_m))
        (ha_l,) = _mlstm(*mlstm_args, DEC_SEQ, DEC_BATCH, N_CTX_TOK, l, init=init)

        nb = G_B * N_B
        rep = lambda v: jnp.broadcast_to(v[..., None], v.shape + (LANES,))
        cw, cb = ssd_conv_w[l], ssd_conv_b[l]
        conv = (rep(cw[:, 0:W_B]), rep(cb[0:W_B]), rep(cw[:, W_B + nb:W_XBC]), rep(cb[W_B + nb:W_XBC]),
                jnp.pad(cw[:, W_B:W_B + nb], ((0, 8 - SSM_CONV), (0, 0))), cb[W_B:W_B + nb].reshape(1, nb))
        ssd_args = (proj, gates, proj_t, gate_bias, gate_bias.reshape(LANES, 1), alog_row, alog_row.reshape(LANES, 1),
                    conv, rep(jnp.repeat(ssd_D[l], P_B)), rep(ssd_norm_g[l]))
        hb_c, st_h = _ssd(*ssd_args, SEQ, BATCH, 0, l, prev=st_h)
        (hb_l,) = _ssd(*ssd_args, DEC_SEQ, DEC_BATCH, N_CTX_TOK, l, init=state_ssd)

        hc = _conf(proj, jnp.pad(conv_dw_w[l], ((0, 32 - CONV_W), (0, 0))), conv_dw_b[l].reshape(1, W_C),
                   conv_ln_g[l].reshape(1, W_C), conv_ln_b[l].reshape(1, W_C))

        x1, u2t, route = _outproj(ha_c, ha_l, hb_c, hb_l, hc, x, mod, w_o[l].astype(BF16), post1_g[l].reshape(1, D_MODEL),
                                  post1_b[l].reshape(1, D_MODEL), w_rt, b_r)
        y = _moe(u2t, *_route_tables(route[0]), w_e_gate, w_e_up, w_e_down, l)
        outs = _final(x1, y, mod, post2_g[l].reshape(1, D_MODEL), post2_b[l].reshape(1, D_MODEL), l == DEPTH - 1)
        x = outs[0]

    y_prompt = outs[0].reshape(BATCH, SEQ, D_MODEL)
    y_sample = outs[1].reshape(DEC_BATCH, DEC_SEQ, D_MODEL)
    return (y_prompt, y_sample, st_c, st_n.reshape(BATCH, DEPTH, N_DIR, H_A, DH_A),
            st_m[:, :, :, 0].reshape(BATCH, DEPTH, N_DIR, H_A), st_h)
```

```python
import functools

import jax
import jax.numpy as jnp
from jax import lax
from jax.experimental import pallas as pl
from jax.experimental.pallas import tpu as pltpu

D_MODEL = 1024
BATCH = 32
SEQ = 256
DEPTH = 2
DEC_BATCH = 2
DEC_SEQ = 1024
GRID_W = 64
N_DIR = 2
CHUNK = 128
H_A = 4
DH_A = 128
W_A = H_A * DH_A
H_B = 8
P_B = 64
W_B = H_B * P_B
G_B = 2
N_B = 128
W_XBC = W_B + 2 * G_B * N_B
SSM_CONV = 3
W_C = 512
CONV_W = 31
D_MIX = W_A + W_B + W_C
N_EXPERTS = 16
N_EGROUPS = 4
EXPERTS_PER_GROUP = N_EXPERTS // N_EGROUPS
D_FF_EXPERT = 512
ALPHA = (2 * DEPTH) ** 0.25
EPS = 1e-5
F32 = jnp.float32
BF16 = jnp.bfloat16

N_CTX_TOK = BATCH * SEQ
N_LAT_TOK = DEC_BATCH * DEC_SEQ
T_ALL = N_CTX_TOK + N_LAT_TOK
N_MOD_ROWS = 8
D_MAIN = W_A + 2 * W_C + G_B * N_B
D_T = 3 * W_A + 2 * W_B + G_B * N_B + 128
LANES = 128
HALO = 16
CONV_ROWS = 256
TM_PROJ = 512
TM_MOE = 256
VMEM_LIMIT = 48 * 1024 * 1024
MOE_VMEM_LIMIT = 56 * 1024 * 1024

_NT = (((1,), (1,)), ((), ()))
_TN = (((0,), (0,)), ((), ()))


def _ln(x):
    mu = jnp.mean(x, axis=-1, keepdims=True)
    xc = x - mu
    var = jnp.mean(xc * xc, axis=-1, keepdims=True)
    return xc * lax.rsqrt(var + EPS)


def _sigmoid(x):
    return 1.0 / (1.0 + jnp.exp(-x))


def _silu(x):
    return x * _sigmoid(x)


def _softplus(x):
    return jnp.maximum(x, 0.0) + jnp.log1p(jnp.exp(-jnp.abs(x)))


def _dot(a, b):
    return jnp.dot(a, b, preferred_element_type=F32)


def _dotg(a, b, dims):
    return lax.dot_general(a, b, dims, preferred_element_type=F32)


def _tri_cumsum(tri, x):
    hi = x.astype(BF16)
    r1 = x - hi.astype(F32)
    mid = r1.astype(BF16)
    lo = (r1 - mid.astype(F32)).astype(BF16)
    return _dot(tri, hi) + _dot(tri, mid) + _dot(tri, lo)


def _tri_mask(d):
    row = lax.broadcasted_iota(jnp.int32, (CHUNK, CHUNK), 0)
    col = lax.broadcasted_iota(jnp.int32, (CHUNK, CHUNK), 1)
    return (row >= col) if d == 0 else (row <= col)


def _mod_row(row_start):
    return jnp.where(row_start < N_CTX_TOK, 0, 1 + (row_start - N_CTX_TOK) // DEC_SEQ)


def _cparams(n_axes):
    return pltpu.CompilerParams(dimension_semantics=("arbitrary",) * n_axes, vmem_limit_bytes=VMEM_LIMIT)


def _ada_kernel(c_ref, w_ref, b_ref, o_ref):
    o_ref[0] = _dot(_silu(c_ref[...]), w_ref[0]) + b_ref[0]


def _ada(cvec, w_ada, b_ada):
    tn = 1536
    return pl.pallas_call(
        _ada_kernel,
        grid=(DEPTH, 6 * D_MODEL // tn),
        in_specs=[
            pl.BlockSpec((N_MOD_ROWS, D_MODEL), lambda l, j: (0, 0)),
            pl.BlockSpec((1, D_MODEL, tn), lambda l, j: (l, 0, j)),
            pl.BlockSpec((1, 1, tn), lambda l, j: (l, 0, j)),
        ],
        out_specs=pl.BlockSpec((1, N_MOD_ROWS, tn), lambda l, j: (l, 0, j)),
        out_shape=jax.ShapeDtypeStruct((DEPTH, N_MOD_ROWS, 6 * D_MODEL), F32),
        compiler_params=_cparams(2),
        name="ada",
    )(cvec, w_ada, b_ada.reshape(DEPTH, 1, 6 * D_MODEL))


def _embed_kernel(xp_ref, xs_ref, pos_ref, o_ref):
    i = pl.program_id(0)

    @pl.when(i < N_CTX_TOK // DEC_SEQ)
    def _():
        o_ref[...] = xp_ref[...]

    @pl.when(i >= N_CTX_TOK // DEC_SEQ)
    def _():
        o_ref[...] = xs_ref[...] + pos_ref[...]


def _embed(xp, xs, pos):
    n_ctx_blk = N_CTX_TOK // DEC_SEQ
    return pl.pallas_call(
        _embed_kernel,
        grid=(T_ALL // DEC_SEQ,),
        in_specs=[
            pl.BlockSpec((DEC_SEQ, D_MODEL), lambda i: (jnp.minimum(i, n_ctx_blk - 1), 0)),
            pl.BlockSpec((DEC_SEQ, D_MODEL), lambda i: (jnp.maximum(i - n_ctx_blk, 0), 0)),
            pl.BlockSpec((DEC_SEQ, D_MODEL), lambda i: (0, 0)),
        ],
        out_specs=pl.BlockSpec((DEC_SEQ, D_MODEL), lambda i: (i, 0)),
        out_shape=jax.ShapeDtypeStruct((T_ALL, D_MODEL), F32),
        compiler_params=_cparams(1),
        name="embed",
    )(xp, xs, pos)


def _inproj_kernel(x_ref, mod_ref, wm_ref, wg_ref, wt_ref, om_ref, og_ref, ot_ref, u_scr, *, tm):
    i = pl.program_id(0)
    j = pl.program_id(1)

    @pl.when(j == 0)
    def _():
        r = _mod_row(i * tm)
        shift = mod_ref[pl.ds(r, 1), 0:D_MODEL]
        scale = mod_ref[pl.ds(r, 1), D_MODEL:2 * D_MODEL]
        u = (_ln(x_ref[...]) * (1.0 + scale) + shift).astype(BF16)
        u_scr[...] = u
        og_ref[...] = _dot(u, wg_ref[...])
        ot_ref[...] = _dotg(wt_ref[...], u, _NT)

    om_ref[...] = _dot(u_scr[...], wm_ref[...])


def _inproj(x, mod, w_main, w_small, w_t):
    tm, tn = TM_PROJ, D_MAIN // 2
    return pl.pallas_call(
        functools.partial(_inproj_kernel, tm=tm),
        grid=(T_ALL // tm, D_MAIN // tn),
        in_specs=[
            pl.BlockSpec((tm, D_MODEL), lambda i, j: (i, 0)),
            pl.BlockSpec((N_MOD_ROWS, 6 * D_MODEL), lambda i, j: (0, 0)),
            pl.BlockSpec((D_MODEL, tn), lambda i, j: (0, j)),
            pl.BlockSpec((D_MODEL, LANES), lambda i, j: (0, 0)),
            pl.BlockSpec((D_T, D_MODEL), lambda i, j: (0, 0)),
        ],
        out_specs=[
            pl.BlockSpec((tm, tn), lambda i, j: (i, j)),
            pl.BlockSpec((tm, LANES), lambda i, j: (i, 0)),
            pl.BlockSpec((D_T, tm), lambda i, j: (0, i)),
        ],
        out_shape=[
            jax.ShapeDtypeStruct((T_ALL, D_MAIN), F32),
            jax.ShapeDtypeStruct((T_ALL, LANES), F32),
            jax.ShapeDtypeStruct((D_T, T_ALL), F32),
        ],
        scratch_shapes=[pltpu.VMEM((tm, D_MODEL), BF16)],
        compiler_params=_cparams(2),
        name="inproj",
    )(x, mod, w_main, w_small, w_t)


def _scan_max(x, lane, d):
    s = 1
    while s < CHUNK:
        if d == 0:
            x = jnp.where(lane >= s, jnp.maximum(x, pltpu.roll(x, s, axis=1)), x)
        else:
            x = jnp.where(lane < CHUNK - s, jnp.maximum(x, pltpu.roll(x, CHUNK - s, axis=1)), x)
        s *= 2
    return x


def _split_dot(x, tri):
    hi = x.astype(BF16)
    r1 = x - hi.astype(F32)
    mid = r1.astype(BF16)
    lo = (r1 - mid.astype(F32)).astype(BF16)
    return _dot(hi, tri) + _dot(mid, tri) + _dot(lo, tri)


def _mlstm_kernel(*refs, seq, has_init, emit_state, layer):
    it = iter(refs)
    k_ref, qt_ref, vt_ref, ot_ref, g_ref, gt_ref, gb_ref, gbt_ref, ng_ref = (next(it) for _ in range(9))
    if has_init:
        c0_ref, n0_ref, m0_ref = (next(it) for _ in range(3))
    if emit_state and layer > 0:
        prev_refs = [next(it) for _ in range(3)]
    out_ref = next(it)
    if emit_state:
        co_ref, no_ref, mo_ref = (next(it) for _ in range(3))
    tg_scr, tgt_scr, b_scr, beta_scr, pm_scr, cumc_scr, hd_scr, c_scr, n_scr, m_scr = (next(it) for _ in range(10))
    n_chunks = seq // CHUNK
    n_gate = N_DIR * 2 * H_A
    if emit_state and layer > 0:
        for prev_ref, st_ref in zip(prev_refs, (co_ref, no_ref, mo_ref)):
            st_ref[0, 0:layer] = prev_ref[0]

    y = g_ref[...] + gb_ref[...]
    lane = lax.broadcasted_iota(jnp.int32, y.shape, 1)
    tg_scr[...] = jnp.where((lane < n_gate) & ((lane & H_A) != 0), -_softplus(-y), y)
    yt = gt_ref[0:n_gate, :] + gbt_ref[0:n_gate, :]
    rowt = lax.broadcasted_iota(jnp.int32, yt.shape, 0)
    tgt_scr[...] = jnp.where((rowt & H_A) != 0, -_softplus(-yt), yt)

    lane8 = lax.broadcasted_iota(jnp.int32, (8, CHUNK), 1)
    row8 = lax.broadcasted_iota(jnp.int32, (8, LANES), 0)
    masks_t = [_tri_mask(1 - d) for d in range(N_DIR)]

    for d in range(N_DIR):
        tri_t = masks_t[d].astype(BF16)
        tri_c = _tri_mask(d).astype(BF16)
        rows = slice(d * 8, (d + 1) * 8)
        for c in range(n_chunks):
            cs = slice(c * CHUNK, (c + 1) * CHUNK)
            gr8 = tgt_scr[rows, cs]
            b8 = pltpu.roll(_split_dot(gr8, tri_t), H_A, axis=0)
            beta8 = gr8 - b8
            b_scr[rows, cs] = b8
            beta_scr[rows, cs] = beta8
            pm_scr[rows, cs] = _scan_max(beta8, lane8, d)
            cumc_scr[d, cs, :] = _tri_cumsum(tri_c, tg_scr[cs, :])

    pos = (CHUNK - 1, 0)

    for d in range(N_DIR):
        rows = slice(d * 8, (d + 1) * 8)
        if has_init:
            for h in range(H_A):
                c_scr[d, h] = c0_ref[0, 0, d, h].T
            n0 = n0_ref[0, 0]
            m0 = m0_ref[0, 0]
            if d == 1:
                n0 = pltpu.roll(n0, H_A, axis=0)
                m0 = pltpu.roll(m0, H_A, axis=0)
            n_scr[rows, :] = jnp.where(row8 < H_A, n0, 0.0)
            m_scr[rows, :] = jnp.where(row8 < H_A, m0, 0.0)
        else:
            for h in range(H_A):
                c_scr[d, h] = jnp.zeros((DH_A, DH_A), F32)
            n_scr[rows, :] = jnp.zeros((8, DH_A), F32)
            m_scr[rows, :] = jnp.zeros((8, LANES), F32)

    def chunk_prep(d, c):
        rows = slice(d * 8, (d + 1) * 8)
        cs = slice(c * CHUNK, (c + 1) * CHUNK)
        b8 = b_scr[rows, cs]
        beta8 = beta_scr[rows, cs]
        m8 = m_scr[rows, :]
        n8 = n_scr[rows, :]
        mu8 = jnp.maximum(m8, pm_scr[rows, cs])
        mu_last = mu8[:, pos[d]:pos[d] + 1]
        w_tok8 = jnp.exp(beta8 - mu_last)
        return dict(cs=cs, rows=rows, n8=n8, mu8=mu8, w_int8=jnp.exp(m8 - mu8), emt8=jnp.exp(-(b8 + mu8)),
                    w_tok8=w_tok8, w_prev8=jnp.exp(m8 - mu_last), m_new8=b8[:, pos[d]:pos[d] + 1] + mu_last,
                    n16b=jnp.concatenate([n8, jnp.zeros_like(n8)], axis=0).astype(BF16),
                    w_tok16=jnp.concatenate([w_tok8, jnp.zeros_like(w_tok8)], axis=0),
                    gc=tg_scr[cs, :], cum_c=cumc_scr[d, cs, :])

    def head_step(d, h, p):
        cs = p["cs"]
        hs = slice(h * DH_A, (h + 1) * DH_A)
        col_i = d * 2 * H_A + h
        col_f = col_i + H_A
        beta_c = p["gc"][:, col_i:col_i + 1] - p["cum_c"][:, col_f:col_f + 1]
        kb = k_ref[cs, hs].astype(BF16)
        qtb = (qt_ref[hs, cs] * (DH_A ** -0.5)).astype(BF16)
        vt = vt_ref[hs, cs]
        ct = c_scr[d, h]
        w_int_r = p["w_int8"][h:h + 1, :]

        res = _dot(jnp.concatenate([kb, ct.astype(BF16), p["n16b"]], axis=0), qtb)
        st = res[0:CHUNK] * jnp.exp(jnp.where(masks_t[d], beta_c - p["mu8"][h:h + 1, :], -jnp.inf))
        num = w_int_r * res[CHUNK:CHUNK + DH_A] + _dot(vt.astype(BF16), st.astype(BF16))
        den_r = (w_int_r * res[CHUNK + DH_A + h:CHUNK + DH_A + h + 1]
                 + jnp.sum(st, axis=0, keepdims=True))
        hd_scr[d, hs, cs] = num * (1.0 / jnp.maximum(jnp.abs(den_r), p["emt8"][h:h + 1, :]))

        upd = _dot(jnp.concatenate([vt * p["w_tok8"][h:h + 1, :], p["w_tok16"]], axis=0).astype(BF16), kb)
        c_scr[d, h] = p["w_prev8"][h:h + 1, :] * ct + upd[0:DH_A]
        return upd[DH_A:DH_A + 8]

    for ci in range(n_chunks):
        preps = [chunk_prep(0, ci), chunk_prep(1, n_chunks - 1 - ci)]
        nk_acc = [jnp.zeros((8, DH_A), F32) for _ in range(N_DIR)]
        for h in range(H_A):
            for d in range(N_DIR):
                nk_acc[d] = jnp.where(row8 == h, head_step(d, h, preps[d]), nk_acc[d])
        for d in range(N_DIR):
            p = preps[d]
            n_scr[p["rows"], :] = p["w_prev8"] * p["n8"] + nk_acc[d]
            m_scr[p["rows"], :] = jnp.where(row8 < H_A, jnp.broadcast_to(p["m_new8"], (8, LANES)), 0.0)

    if emit_state:
        for d in range(N_DIR):
            for h in range(H_A):
                co_ref[0, layer, d, h] = c_scr[d, h].T
            no_ref[0, layer, d * H_A:(d + 1) * H_A, :] = n_scr[d * 8:d * 8 + H_A, :]
            mo_ref[0, layer, d * H_A:(d + 1) * H_A, :] = m_scr[d * 8:d * 8 + H_A, :]

    for c in range(n_chunks):
        cs = slice(c * CHUNK, (c + 1) * CHUNK)
        for h in range(H_A):
            hs = slice(h * DH_A, (h + 1) * DH_A)
            hsum = hd_scr[0, hs, cs] + hd_scr[1, hs, cs]
            mean = jnp.sum(hsum, axis=0, keepdims=True) * (1.0 / DH_A)
            xc = hsum - mean
            var = jnp.sum(xc * xc, axis=0, keepdims=True) * (1.0 / DH_A)
            hn = xc * lax.rsqrt(var + EPS) * ng_ref[hs, :]
            out_ref[hs, cs] = (hn * _sigmoid(ot_ref[hs, cs])).astype(out_ref.dtype)


def _mlstm(proj, gates, proj_t, gate_bias, gate_bias_t, norm_g_rep, seq, n_seq, row_off, layer, init=None, prev=None):
    blk0 = row_off // seq
    has_init = init is not None
    emit_state = not has_init
    feat = lambda r: pl.BlockSpec((W_A, seq), lambda s: (r, s + blk0))
    const = lambda shape: pl.BlockSpec(shape, lambda s: (0,) * len(shape))
    st_shapes = [(N_DIR, H_A, DH_A, DH_A), (N_DIR * H_A, DH_A), (N_DIR * H_A, LANES)]

    def st_specs(n_layers, first):
        return [pl.BlockSpec((1, n_layers) + shp, lambda s, nd=len(shp): (s, first) + (0,) * nd) for shp in st_shapes]

    in_specs = [pl.BlockSpec((seq, W_A), lambda s: (s + blk0, 0)), feat(0), feat(1), feat(2),
                pl.BlockSpec((seq, LANES), lambda s: (s + blk0, 0)),
                pl.BlockSpec((LANES, seq), lambda s: (D_T // LANES - 1, s + blk0)),
                const((1, LANES)), const((LANES, 1)), const((W_A, LANES))]
    args = [proj, proj_t, proj_t, proj_t, gates, proj_t, gate_bias, gate_bias_t, norm_g_rep]
    if has_init:
        in_specs += st_specs(1, layer)
        args += list(init)
    if emit_state and layer > 0:
        in_specs += st_specs(layer, 0)
        args += list(prev)
    out_specs = [pl.BlockSpec((W_A, seq), lambda s: (0, s))]
    out_shape = [jax.ShapeDtypeStruct((W_A, n_seq * seq), BF16)]
    if emit_state:
        out_specs += st_specs(layer + 1, 0)
        out_shape += [jax.ShapeDtypeStruct((n_seq, layer + 1) + shp, F32) for shp in st_shapes]
    return pl.pallas_call(
        functools.partial(_mlstm_kernel, seq=seq, has_init=has_init, emit_state=emit_state, layer=layer),
        grid=(n_seq,),
        in_specs=in_specs,
        out_specs=out_specs,
        out_shape=out_shape,
        scratch_shapes=[pltpu.VMEM((seq, LANES), F32),
                        pltpu.VMEM((N_DIR * 8, seq), F32),
                        pltpu.VMEM((N_DIR * 8, seq), F32),
                        pltpu.VMEM((N_DIR * 8, seq), F32),
                        pltpu.VMEM((N_DIR * 8, seq), F32),
                        pltpu.VMEM((N_DIR, seq, LANES), F32),
                        pltpu.VMEM((N_DIR, W_A, seq), F32),
                        pltpu.VMEM((N_DIR, H_A, DH_A, DH_A), F32),
                        pltpu.VMEM((N_DIR * 8, DH_A), F32),
                        pltpu.VMEM((N_DIR * 8, LANES), F32)],
        compiler_params=_cparams(1),
        name="mlstm_lat" if has_init else "mlstm_ctx",
    )(*args)


_DT_COL0 = N_DIR * 2 * H_A
_HG = H_B // G_B
_WG = _HG * P_B


def _conv3_lanes(ref, w_ref, b_ref, out_scr, seq):
    lane = lax.broadcasted_iota(jnp.int32, (CHUNK, seq), 1)
    for r in range(ref.shape[0] // CHUNK):
        rs = slice(r * CHUNK, (r + 1) * CHUNK)
        cur = ref[rs, :]
        prev = jnp.where(lane == 0, 0.0, pltpu.roll(cur, 1, axis=1))
        nxt = jnp.where(lane == seq - 1, 0.0, pltpu.roll(cur, seq - 1, axis=1))
        for c in range(seq // CHUNK):
            cs = slice(c * CHUNK, (c + 1) * CHUNK)
            acc = (w_ref[0, rs, :] * prev[:, cs] + w_ref[1, rs, :] * cur[:, cs] + w_ref[2, rs, :] * nxt[:, cs]
                   + b_ref[rs, :])
            out_scr[rs, cs] = _silu(acc).astype(out_scr.dtype)


def _ssd_kernel(*refs, seq, has_init, emit_state, layer):
    it = iter(refs)
    (zt_ref, xt_ref, ct_ref, b_ref, g_ref, gt_ref, gb_ref, gbt_ref, alog_ref, alogt_ref,
     cwx_ref, cbx_ref, cwc_ref, cbc_ref, cwb_ref, cbb_ref, dsk_ref, ng_ref) = (next(it) for _ in range(18))
    if has_init:
        h0_ref = next(it)
    if emit_state and layer > 0:
        prev_ref = next(it)
    out_ref = next(it)
    if emit_state:
        ho_ref = next(it)
    pad_scr, xs_scr, cs_scr, bs_scr, dtr_scr, br_scr, cumc_scr, yt_scr, h_scr = (next(it) for _ in range(9))
    n_chunks = seq // CHUNK
    pad = 8
    if emit_state and layer > 0:
        ho_ref[0, 0:layer] = prev_ref[0]

    _conv3_lanes(xt_ref, cwx_ref, cbx_ref, xs_scr, seq)
    _conv3_lanes(ct_ref, cwc_ref, cbc_ref, cs_scr, seq)
    nb = G_B * N_B
    pad_scr[0:pad, :] = jnp.zeros((pad, nb), F32)
    pad_scr[pad + seq:2 * pad + seq, :] = jnp.zeros((pad, nb), F32)
    pad_scr[pad:pad + seq, :] = b_ref[...]
    for c in range(n_chunks):
        acc = cbb_ref[...]
        for k in range(SSM_CONV):
            r = c * CHUNK + pad - SSM_CONV // 2 + k
            acc = acc + cwb_ref[k:k + 1, :] * pad_scr[r:r + CHUNK, :]
        bs_scr[c * CHUNK:(c + 1) * CHUNK, :] = _silu(acc).astype(bs_scr.dtype)

    lac = _softplus(g_ref[...] + gb_ref[...]) * (-jnp.exp(alog_ref[...]))
    r0 = _DT_COL0
    dtr = _softplus(gt_ref[r0:r0 + N_DIR * H_B, :] + gbt_ref[r0:r0 + N_DIR * H_B, :])
    dtr_scr[...] = dtr
    lar = dtr * (-jnp.exp(alogt_ref[r0:r0 + N_DIR * H_B, :]))
    masks_t = [_tri_mask(1 - d) for d in range(N_DIR)]
    for d in range(N_DIR):
        tri_t = masks_t[d].astype(BF16)
        tri_c = _tri_mask(d).astype(BF16)
        rows = slice(d * H_B, (d + 1) * H_B)
        for c in range(n_chunks):
            cs = slice(c * CHUNK, (c + 1) * CHUNK)
            br_scr[rows, cs] = _split_dot(lar[rows, cs], tri_t)
            cumc_scr[d, cs, :] = _tri_cumsum(tri_c, lac[cs, :])

    row_blk = lax.broadcasted_iota(jnp.int32, (_WG, 1), 0) // P_B

    def run_direction(d):
        for g in range(G_B):
            for hh in range(_HG):
                if has_init:
                    h_scr[d, g, hh * P_B:(hh + 1) * P_B, :] = h0_ref[0, 0, d, g * _HG + hh]
                else:
                    h_scr[d, g, hh * P_B:(hh + 1) * P_B, :] = jnp.zeros((P_B, N_B), F32)

        mask_t = masks_t[d]
        pos = CHUNK - 1 if d == 0 else 0

        for ci in range(n_chunks):
            c = ci if d == 0 else n_chunks - 1 - ci
            cs = slice(c * CHUNK, (c + 1) * CHUNK)
            cum_c = cumc_scr[d, cs, :]
            for g in range(G_B):
                bg = bs_scr[cs, g * N_B:(g + 1) * N_B]
                ctg = cs_scr[g * N_B:(g + 1) * N_B, cs]
                h_st = h_scr[d, g]
                res = _dot(jnp.concatenate([bg, h_st.astype(BF16)], axis=0), ctg)
                cbt = res[0:CHUNK]
                inter = res[CHUNK:CHUNK + _WG]
                xw = []
                decay = jnp.zeros((_WG, 1), F32)
                for hh in range(_HG):
                    head = g * _HG + hh
                    r = d * H_B + head
                    b_r = br_scr[r:r + 1, cs]
                    dt_r = dtr_scr[r:r + 1, cs]
                    b_c = cum_c[:, _DT_COL0 + r:_DT_COL0 + r + 1]
                    gt = b_r[:, pos:pos + 1]
                    ps = slice(head * P_B, (head + 1) * P_B)
                    xh = xs_scr[ps, cs]
                    st = (cbt * jnp.exp(jnp.where(mask_t, b_r - b_c, -jnp.inf))).astype(BF16)
                    y = _dot((xh * dt_r).astype(BF16), st) + jnp.exp(b_r) * inter[hh * P_B:(hh + 1) * P_B]
                    xw.append(xh * (jnp.exp(gt - b_r) * dt_r))
                    decay = jnp.where(row_blk == hh, jnp.exp(gt), decay)
                    if d == 0:
                        yt_scr[ps, cs] = y
                    else:
                        yt_scr[ps, cs] = yt_scr[ps, cs] + y + dsk_ref[ps, :] * xh
                h_scr[d, g] = decay * h_st + _dot(jnp.concatenate(xw, axis=0).astype(BF16), bg)
            if d == 1:
                yz = yt_scr[:, cs] * _silu(zt_ref[:, cs])
                rms = lax.rsqrt(jnp.sum(yz * yz, axis=0, keepdims=True) * (1.0 / W_B) + EPS)
                out_ref[:, cs] = (yz * rms * ng_ref[...]).astype(out_ref.dtype)

        if emit_state:
            for g in range(G_B):
                for hh in range(_HG):
                    ho_ref[0, layer, d, g * _HG + hh] = h_scr[d, g, hh * P_B:(hh + 1) * P_B, :]

    run_direction(0)
    run_direction(1)


def _ssd(proj, gates, proj_t, gate_bias, gate_bias_t, alog_row, alog_col, conv, d_rep, norm_g_rep,
         seq, n_seq, row_off, layer, init=None, prev=None):
    blk0 = row_off // seq
    has_init = init is not None
    emit_state = not has_init
    nb = G_B * N_B
    zr = 3 * W_A // W_B
    cr = (3 * W_A + 2 * W_B) // nb
    gr = (3 * W_A + 2 * W_B + nb) // LANES
    bc = (W_A + 2 * W_C) // nb
    const = lambda shape: pl.BlockSpec(shape, lambda s: (0,) * len(shape))
    st_spec = lambda n_layers, first: pl.BlockSpec((1, n_layers, N_DIR, H_B, P_B, N_B),
                                                   lambda s: (s, first, 0, 0, 0, 0))
    in_specs = [pl.BlockSpec((W_B, seq), lambda s: (zr, s + blk0)),
                pl.BlockSpec((W_B, seq), lambda s: (zr + 1, s + blk0)),
                pl.BlockSpec((nb, seq), lambda s: (cr, s + blk0)),
                pl.BlockSpec((seq, nb), lambda s: (s + blk0, bc)),
                pl.BlockSpec((seq, LANES), lambda s: (s + blk0, 0)),
                pl.BlockSpec((LANES, seq), lambda s: (gr, s + blk0)),
                const((1, LANES)), const((LANES, 1)), const((1, LANES)), const((LANES, 1)),
                const((SSM_CONV, W_B, LANES)), const((W_B, LANES)), const((SSM_CONV, nb, LANES)), const((nb, LANES)),
                const((8, nb)), const((1, nb)), const((W_B, LANES)), const((W_B, LANES))]
    args = [proj_t, proj_t, proj_t, proj, gates, proj_t, gate_bias, gate_bias_t, alog_row, alog_col,
            *conv, d_rep, norm_g_rep]
    if has_init:
        in_specs.append(st_spec(1, layer))
        args.append(init)
    if emit_state and layer > 0:
        in_specs.append(st_spec(layer, 0))
        args.append(prev)
    out_specs = [pl.BlockSpec((W_B, seq), lambda s: (0, s))]
    out_shape = [jax.ShapeDtypeStruct((W_B, n_seq * seq), BF16)]
    if emit_state:
        out_specs.append(st_spec(layer + 1, 0))
        out_shape.append(jax.ShapeDtypeStruct((n_seq, layer + 1, N_DIR, H_B, P_B, N_B), F32))
    return pl.pallas_call(
        functools.partial(_ssd_kernel, seq=seq, has_init=has_init, emit_state=emit_state, layer=layer),
        grid=(n_seq,),
        in_specs=in_specs,
        out_specs=out_specs,
        out_shape=out_shape,
        scratch_shapes=[pltpu.VMEM((seq + 16, nb), F32),
                        pltpu.VMEM((W_B, seq), F32),
                        pltpu.VMEM((nb, seq), BF16),
                        pltpu.VMEM((seq, nb), BF16),
                        pltpu.VMEM((N_DIR * H_B, seq), F32),
                        pltpu.VMEM((N_DIR * H_B, seq), F32),
                        pltpu.VMEM((N_DIR, seq, LANES), F32),
                        pltpu.VMEM((W_B, seq), F32),
                        pltpu.VMEM((N_DIR, G_B, _WG, N_B), F32)],
        compiler_params=_cparams(1),
        name="ssd_lat" if has_init else "ssd_ctx",
    )(*args)


def _conf_kernel(a_ref, ap_ref, an_ref, g_ref, gp_ref, gn_ref, w_ref, b_ref, lg_ref, lb_ref, out_ref, pad_scr,
                 rot_scr):
    i = pl.program_id(0)
    n_ctx_blk = N_CTX_TOK // CONV_ROWS
    blk_per_lat = DEC_SEQ // CONV_ROWS
    j = (i - n_ctx_blk) % blk_per_lat
    is_ctx = i < n_ctx_blk
    keep_prev = jnp.where(is_ctx | (j == 0), 0.0, 1.0)
    keep_next = jnp.where(is_ctx | (j == blk_per_lat - 1), 0.0, 1.0)
    pad_scr[0:HALO, :] = ap_ref[...] * _sigmoid(gp_ref[...]) * keep_prev
    pad_scr[HALO:HALO + CONV_ROWS, :] = a_ref[...] * _sigmoid(g_ref[...])
    pad_scr[HALO + CONV_ROWS:2 * HALO + CONV_ROWS, :] = an_ref[...] * _sigmoid(gn_ref[...]) * keep_next
    n_rot = CONV_ROWS + 2 * HALO - 8
    for s in range(1, 8):
        rot_scr[s, :, :] = pad_scr[s:s + n_rot, :]
    rc = 64
    for c in range(CONV_ROWS // rc):
        acc = jnp.broadcast_to(b_ref[...], (rc, W_C))
        for k in range(CONV_W):
            off = HALO - CONV_W // 2 + k
            r = c * rc + off - off % 8
            src = pad_scr[r:r + rc, :] if off % 8 == 0 else rot_scr[off % 8, r:r + rc, :]
            acc = acc + w_ref[k:k + 1, :] * src
        u = _ln(acc) * lg_ref[...] + lb_ref[...]
        out_ref[c * rc:(c + 1) * rc, :] = _silu(u).astype(out_ref.dtype)


def _conf(proj, dw_w, dw_b, ln_g, ln_b):
    assert SEQ == CONV_ROWS and DEC_SEQ % CONV_ROWS == 0
    ac = W_A // W_C
    hb = CONV_ROWS // HALO
    n_halo = T_ALL // HALO
    const = lambda shape: pl.BlockSpec(shape, lambda i: (0,) * len(shape))

    def specs(c):
        return [pl.BlockSpec((CONV_ROWS, W_C), lambda i: (i, c)),
                pl.BlockSpec((HALO, W_C), lambda i: (jnp.maximum(i * hb - 1, 0), c)),
                pl.BlockSpec((HALO, W_C), lambda i: (jnp.minimum((i + 1) * hb, n_halo - 1), c))]

    return pl.pallas_call(
        _conf_kernel,
        grid=(T_ALL // CONV_ROWS,),
        in_specs=specs(ac) + specs(ac + 1) + [const((32, W_C)), const((1, W_C)), const((1, W_C)), const((1, W_C))],
        out_specs=pl.BlockSpec((CONV_ROWS, W_C), lambda i: (i, 0)),
        out_shape=jax.ShapeDtypeStruct((T_ALL, W_C), BF16),
        scratch_shapes=[pltpu.VMEM((CONV_ROWS + 2 * HALO, W_C), F32),
                        pltpu.VMEM((8, CONV_ROWS + 2 * HALO - 8, W_C), F32)],
        compiler_params=_cparams(1),
        name="conf",
    )(proj, proj, proj, proj, proj, proj, dw_w, dw_b, ln_g, ln_b)


_PAIRS = [(a, b) for a in range(EXPERTS_PER_GROUP) for b in range(a + 1, EXPERTS_PER_GROUP)]
N_CLASSES = N_EGROUPS * len(_PAIRS)
N_FCHUNK = D_MODEL // LANES
N_IN_ROWS = 2 * N_FCHUNK


def _store_token_tiles(ref, val):
    n = val.shape[0]
    for c in range(N_FCHUNK):
        ref[pl.ds(c, n, stride=N_FCHUNK), :] = val[:, c * LANES:(c + 1) * LANES]


def _load_token_tiles(ref, n):
    return jnp.concatenate([ref[pl.ds(c, n, stride=N_FCHUNK), :] for c in range(N_FCHUNK)], axis=1)
N_MOE_ROWS = T_ALL + N_CLASSES * TM_MOE
N_MOE_TILES = N_MOE_ROWS // TM_MOE


def _outproj_kernel(hac_ref, hal_ref, hbc_ref, hbl_ref, hc_ref, x_ref, mod_ref, wo_ref, pg_ref, pb_ref,
                    wr_ref, br_ref, x1_ref, u2_ref, rt_ref, *, tm):
    i = pl.program_id(0)
    r = _mod_row(i * tm)
    gate1 = mod_ref[pl.ds(r, 1), 2 * D_MODEL:3 * D_MODEL]
    shift2 = mod_ref[pl.ds(r, 1), 3 * D_MODEL:4 * D_MODEL]
    scale2 = mod_ref[pl.ds(r, 1), 4 * D_MODEL:5 * D_MODEL]
    is_ctx = i < N_CTX_TOK // tm
    ha = jnp.where(is_ctx, hac_ref[...], hal_ref[...])
    hb = jnp.where(is_ctx, hbc_ref[...], hbl_ref[...])
    mix = (_dotg(ha, wo_ref[0:W_A, :], _TN) + _dotg(hb, wo_ref[W_A:W_A + W_B, :], _TN)
           + _dot(hc_ref[...], wo_ref[W_A + W_B:D_MIX, :]))
    x1 = _ln(ALPHA * x_ref[...] + gate1 * mix) * pg_ref[...] + pb_ref[...]
    x1_ref[...] = x1
    u2 = _ln(x1) * (1.0 + scale2) + shift2
    for c in range(N_FCHUNK):
        u2_ref[pl.ds(c, tm, stride=N_IN_ROWS), :] = u2[:, c * LANES:(c + 1) * LANES]

    logits = lax.dot_general(wr_ref[...], u2, _NT, precision=lax.Precision.HIGHEST,
                             preferred_element_type=F32) + br_ref[...]
    ex = jnp.exp(logits - jnp.max(logits, axis=0, keepdims=True))
    probs = ex / jnp.sum(ex, axis=0, keepdims=True)
    scores = []
    for g in range(N_EGROUPS):
        p = [probs[g * EXPERTS_PER_GROUP + e:g * EXPERTS_PER_GROUP + e + 1, :] for e in range(EXPERTS_PER_GROUP)]
        best = p[0] + p[1]
        for a in range(EXPERTS_PER_GROUP):
            for b in range(a + 1, EXPERTS_PER_GROUP):
                if (a, b) != (0, 1):
                    best = jnp.maximum(best, p[a] + p[b])
        scores.append(best)
    gmax = functools.reduce(jnp.maximum, scores)
    sel = jnp.full(gmax.shape, N_EGROUPS - 1, jnp.int32)
    for g in range(N_EGROUPS - 2, -1, -1):
        sel = jnp.where(scores[g] == gmax, g, sel)
    eidx = lax.broadcasted_iota(jnp.int32, probs.shape, 0)
    pm = jnp.where((eidx // EXPERTS_PER_GROUP) == sel, probs, -jnp.inf)
    p1 = jnp.max(pm, axis=0, keepdims=True)
    i1 = jnp.min(jnp.where(pm == p1, eidx, N_EXPERTS), axis=0, keepdims=True)
    pm2 = jnp.where(eidx == i1, -jnp.inf, pm)
    p2 = jnp.max(pm2, axis=0, keepdims=True)
    i2 = jnp.min(jnp.where(pm2 == p2, eidx, N_EXPERTS), axis=0, keepdims=True)
    den = p1 + p2
    first_lo = i1 < i2
    w_lo = jnp.where(first_lo, p1, p2) / den
    w_hi = jnp.where(first_lo, p2, p1) / den
    a = jnp.minimum(i1, i2) - sel * EXPERTS_PER_GROUP
    b = jnp.maximum(i1, i2) - sel * EXPERTS_PER_GROUP
    pair = jnp.zeros_like(a)
    for k, (pa, pb_) in enumerate(_PAIRS):
        pair = jnp.where((a == pa) & (b == pb_), k, pair)
    cls = sel * len(_PAIRS) + pair
    rt_ref[...] = jnp.broadcast_to(cls, rt_ref.shape)
    wrow = lax.broadcasted_iota(jnp.int32, (LANES, tm), 0)
    wt = jnp.where(wrow == 0, w_lo, jnp.where(wrow == 1, w_hi, 0.0))
    u2_ref[pl.ds(N_FCHUNK, tm, stride=N_IN_ROWS), :] = wt.T
    for c in range(N_FCHUNK + 1, N_IN_ROWS):
        u2_ref[pl.ds(c, tm, stride=N_IN_ROWS), :] = jnp.zeros((tm, LANES), F32)


def _outproj(ha_c, ha_l, hb_c, hb_l, hc, x, mod, w_o, pg, pb, w_rt, b_r):
    tm = TM_PROJ
    n_ctx_blk = N_CTX_TOK // tm
    const = lambda shape: pl.BlockSpec(shape, lambda i: (0,) * len(shape))
    rows = lambda w: pl.BlockSpec((tm, w), lambda i: (i, 0))
    ctx_feat = lambda w: pl.BlockSpec((w, tm), lambda i: (0, jnp.minimum(i, n_ctx_blk - 1)))
    lat_feat = lambda w: pl.BlockSpec((w, tm), lambda i: (0, jnp.maximum(i - n_ctx_blk, 0)))
    return pl.pallas_call(
        functools.partial(_outproj_kernel, tm=tm),
        grid=(T_ALL // tm,),
        in_specs=[ctx_feat(W_A), lat_feat(W_A), ctx_feat(W_B), lat_feat(W_B),
                  rows(W_C), rows(D_MODEL), const((N_MOD_ROWS, 6 * D_MODEL)),
                  const((D_MIX, D_MODEL)), const((1, D_MODEL)), const((1, D_MODEL)),
                  const((N_EXPERTS, D_MODEL)), const((N_EXPERTS, 1))],
        out_specs=[rows(D_MODEL), pl.BlockSpec((tm * N_IN_ROWS, LANES), lambda i: (i, 0)),
                   pl.BlockSpec((8, tm), lambda i: (0, i))],
        out_shape=[jax.ShapeDtypeStruct((T_ALL, D_MODEL), F32),
                   jax.ShapeDtypeStruct((T_ALL * N_IN_ROWS, LANES), F32),
                   jax.ShapeDtypeStruct((8, T_ALL), jnp.int32)],
        compiler_params=_cparams(1),
        name="outproj",
    )(ha_c, ha_l, hb_c, hb_l, hc, x, mod, w_o, pg, pb, w_rt, b_r)


N_Y_ROWS = T_ALL + 2 * TM_MOE


def _moe_kernel(tok_ref, tstart_ref, nval_ref, tlo_ref, thi_ref, nused_ref,
                u_hbm, wgl_ref, wul_ref, wdl_ref, wgh_ref, wuh_ref, wdh_ref, y_hbm,
                xbuf0, xbuf1, ybuf0, ybuf1, gsem, ssem):
    del tlo_ref, thi_ref
    i = pl.program_id(0)
    n_used = nused_ref[0]
    tile_rows = TM_MOE * N_FCHUNK

    def token_tile(ref, tok, rows=N_FCHUNK):
        return ref.at[pl.ds(pl.multiple_of(tok * rows, rows), rows), :]

    def gather_rows(t, xb, sem):
        base = tstart_ref[t]
        for r in range(TM_MOE):
            pltpu.make_async_copy(token_tile(u_hbm, tok_ref[base + r], N_IN_ROWS), token_tile(xb, r, N_IN_ROWS),
                                  sem).start()

    def scatter_rows(t, n_valid, yb, sem, spare):
        base = tstart_ref[t]
        for r in range(TM_MOE):
            dst = jnp.where(r < n_valid, tok_ref[base + r], spare + r)
            pltpu.make_async_copy(token_tile(yb, r), token_tile(y_hbm, dst), sem).start()

    def load_tile(xb):
        x = jnp.concatenate([xb[pl.ds(c, TM_MOE, stride=N_IN_ROWS), :] for c in range(N_FCHUNK)], axis=1)
        w = xb[pl.ds(N_FCHUNK, TM_MOE, stride=N_IN_ROWS), :]
        return x.astype(BF16), w[:, 0:1], w[:, 1:2]

    def wait_gather(xb, sem):
        pltpu.make_async_copy(u_hbm.at[pl.ds(0, TM_MOE * N_IN_ROWS), :], xb, sem).wait()

    def wait_scatter(yb, sem):
        pltpu.make_async_copy(yb, y_hbm.at[pl.ds(0, tile_rows), :], sem).wait()

    def step(s, xb, xb_next, yb, yb_prev):
        @pl.when(i > 0)
        def _():
            wait_scatter(yb, ssem.at[s])

        wait_gather(xb, gsem.at[s])

        @pl.when(n_used > 0)
        def _():
            gather_rows(jnp.minimum(i + 1, n_used - 1), xb_next, gsem.at[1 - s])
            prev = jnp.maximum(i - 1, 0)
            scatter_rows(prev, jnp.where(i > 0, nval_ref[prev], 0), yb_prev, ssem.at[1 - s],
                         T_ALL + (1 - s) * TM_MOE)

        x, w_lo, w_hi = load_tile(xb)

        def ffn(wg_ref, wu_ref, wd_ref, w):
            act = _silu(_dot(x, wg_ref[0, 0].astype(BF16))) * _dot(x, wu_ref[0, 0].astype(BF16)) * w
            return _dot(act.astype(BF16), wd_ref[0, 0].astype(BF16))

        _store_token_tiles(yb, ffn(wgl_ref, wul_ref, wdl_ref, w_lo) + ffn(wgh_ref, wuh_ref, wdh_ref, w_hi))

        @pl.when(i == n_used - 1)
        def _():
            scatter_rows(i, nval_ref[i], yb, ssem.at[s], T_ALL + s * TM_MOE)
            wait_scatter(yb, ssem.at[s])
            wait_scatter(yb_prev, ssem.at[1 - s])
            wait_gather(xb_next, gsem.at[1 - s])

    @pl.when(i == 0)
    def _():
        ybuf1[...] = jnp.zeros(ybuf1.shape, F32)
        fill = pltpu.make_async_copy(ybuf1, y_hbm.at[pl.ds(T_ALL * N_FCHUNK, tile_rows), :], ssem.at[0])
        fill.start()
        fill.wait()
        gather_rows(0, xbuf0, gsem.at[0])

    @pl.when((i < n_used) & (i % 2 == 0))
    def _():
        step(0, xbuf0, xbuf1, ybuf0, ybuf1)

    @pl.when((i < n_used) & (i % 2 == 1))
    def _():
        step(1, xbuf1, xbuf0, ybuf1, ybuf0)


def _moe(u2t, tok_sorted, tile_start, n_valid, tile_lo, tile_hi, n_used, w_gate, w_up, w_down, layer):
    lo = lambda shape: pl.BlockSpec(shape, lambda i, tok, ts, nv, tlo, thi, nused: (layer, tlo[i], 0, 0))
    hi = lambda shape: pl.BlockSpec(shape, lambda i, tok, ts, nv, tlo, thi, nused: (layer, thi[i], 0, 0))
    up_shape = (1, 1, D_MODEL, D_FF_EXPERT)
    down_shape = (1, 1, D_FF_EXPERT, D_MODEL)
    in_buf = pltpu.VMEM((TM_MOE * N_IN_ROWS, LANES), F32)
    out_buf = pltpu.VMEM((TM_MOE * N_FCHUNK, LANES), F32)
    grid_spec = pltpu.PrefetchScalarGridSpec(
        num_scalar_prefetch=6,
        grid=(N_MOE_TILES,),
        in_specs=[pl.BlockSpec(memory_space=pl.ANY),
                  lo(up_shape), lo(up_shape), lo(down_shape),
                  hi(up_shape), hi(up_shape), hi(down_shape)],
        out_specs=pl.BlockSpec(memory_space=pl.ANY),
        scratch_shapes=[in_buf, in_buf, out_buf, out_buf,
                        pltpu.SemaphoreType.DMA((2,)),
                        pltpu.SemaphoreType.DMA((2,))],
    )
    return pl.pallas_call(
        _moe_kernel,
        grid_spec=grid_spec,
        out_shape=jax.ShapeDtypeStruct((N_Y_ROWS * N_FCHUNK, LANES), F32),
        compiler_params=pltpu.CompilerParams(dimension_semantics=("arbitrary",), vmem_limit_bytes=MOE_VMEM_LIMIT,
                                             has_side_effects=True),
        name="moe",
    )(tok_sorted, tile_start, n_valid, tile_lo, tile_hi, n_used, u2t, w_gate, w_up, w_down, w_gate, w_up, w_down)


def _route_tables(cls):
    _, tok_sorted = lax.sort((cls, jnp.arange(T_ALL, dtype=jnp.int32)), num_keys=1)
    cids = jnp.arange(N_CLASSES, dtype=jnp.int32)
    counts = jnp.sum((cls[:, None] == cids[None, :]).astype(jnp.int32), axis=0)
    offs = jnp.cumsum(counts) - counts
    padded = (counts + TM_MOE - 1) // TM_MOE * TM_MOE
    ends = jnp.cumsum(padded)
    offs_p = ends - padded
    n_used = ends[-1] // TM_MOE
    tile_start = jnp.arange(N_MOE_TILES, dtype=jnp.int32) * TM_MOE
    tile_cls = jnp.sum((ends[None, :] <= jnp.minimum(tile_start, ends[-1] - TM_MOE)[:, None]).astype(jnp.int32),
                       axis=1)
    onehot = (tile_cls[:, None] == cids[None, :]).astype(jnp.int32)
    pick = lambda table: jnp.sum(onehot * table[None, :], axis=1)
    k = tile_start - pick(offs_p)
    n_valid = jnp.where(tile_start < ends[-1], jnp.clip(pick(counts) - k, 0, TM_MOE), 0)
    tile_first = jnp.clip(pick(offs) + k, 0, T_ALL - 1)
    tile_lo = pick(cids // len(_PAIRS) * EXPERTS_PER_GROUP + jnp.array([p[0] for p in _PAIRS] * N_EGROUPS, jnp.int32))
    tile_hi = pick(cids // len(_PAIRS) * EXPERTS_PER_GROUP + jnp.array([p[1] for p in _PAIRS] * N_EGROUPS, jnp.int32))
    tok_sorted = jnp.concatenate([tok_sorted, jnp.zeros((TM_MOE,), jnp.int32)])
    i32 = lambda v: v.astype(jnp.int32)
    return tok_sorted, i32(tile_first), i32(n_valid), i32(tile_lo), i32(tile_hi), i32(n_used).reshape(1)


def _final_kernel(x1_ref, y_ref, mod_ref, pg_ref, pb_ref, *o_refs, tm):
    i = pl.program_id(0)
    r = _mod_row(i * tm)
    gate2 = mod_ref[pl.ds(r, 1), 5 * D_MODEL:6 * D_MODEL]
    out = _ln(ALPHA * x1_ref[...] + gate2 * _load_token_tiles(y_ref, tm)) * pg_ref[...] + pb_ref[...]
    if len(o_refs) == 1:
        o_refs[0][...] = out
    else:
        @pl.when(i < N_CTX_TOK // tm)
        def _():
            o_refs[0][...] = out

        @pl.when(i >= N_CTX_TOK // tm)
        def _():
            o_refs[1][...] = out


def _final(x1, y, mod, pg, pb, split):
    tm = TM_PROJ
    n_ctx_blk = N_CTX_TOK // tm
    const = lambda shape: pl.BlockSpec(shape, lambda i: (0,) * len(shape))
    rows = pl.BlockSpec((tm, D_MODEL), lambda i: (i, 0))
    if split:
        out_specs = [pl.BlockSpec((tm, D_MODEL), lambda i: (jnp.minimum(i, n_ctx_blk - 1), 0)),
                     pl.BlockSpec((tm, D_MODEL), lambda i: (jnp.maximum(i - n_ctx_blk, 0), 0))]
        out_shape = [jax.ShapeDtypeStruct((N_CTX_TOK, D_MODEL), F32), jax.ShapeDtypeStruct((N_LAT_TOK, D_MODEL), F32)]
    else:
        out_specs = [rows]
        out_shape = [jax.ShapeDtypeStruct((T_ALL, D_MODEL), F32)]
    return pl.pallas_call(
        functools.partial(_final_kernel, tm=tm),
        grid=(T_ALL // tm,),
        in_specs=[rows, pl.BlockSpec((tm * N_FCHUNK, LANES), lambda i: (i, 0)),
                  const((N_MOD_ROWS, 6 * D_MODEL)), const((1, D_MODEL)), const((1, D_MODEL))],
        out_specs=out_specs,
        out_shape=out_shape,
        compiler_params=_cparams(1),
        name="final",
    )(x1, y, mod, pg, pb)


def _grid_pos(n_tok):
    rows = n_tok // GRID_W
    r, col = jnp.meshgrid(jnp.arange(rows, dtype=F32), jnp.arange(GRID_W, dtype=F32), indexing='ij')
    quarter = D_MODEL // 4
    omega = 1.0 / (10000.0 ** (jnp.arange(quarter, dtype=F32) / quarter))

    def emb(p):
        ang = p.reshape(-1)[:, None] * omega[None, :]
        return jnp.concatenate([jnp.sin(ang), jnp.cos(ang)], axis=-1)

    return jnp.concatenate([emb(r), emb(col)], axis=-1)


def _pad_lanes(v, start):
    v = v.reshape(1, -1).astype(F32)
    return jnp.pad(v, ((0, 0), (start, LANES - start - v.shape[1])))


def kernel(x_prompt, x_sample, state_mlstm_C, state_mlstm_n, state_mlstm_m, state_ssd, c, c_ctx, w_in, w_o, mlstm_b_i, mlstm_b_f, mlstm_norm_g, ssd_conv_w, ssd_conv_b, ssd_dt_bias, ssd_A_log, ssd_D, ssd_norm_g, conv_dw_w, conv_dw_b, conv_ln_g, conv_ln_b, w_ada, b_ada, post1_g, post1_b, post2_g, post2_b, w_router, b_router, w_e_gate, w_e_up, w_e_down):
    cvec = jnp.concatenate([c_ctx[None, :], c, jnp.zeros((N_MOD_ROWS - 1 - DEC_BATCH, D_MODEL), F32)], axis=0)
    mod_all = _ada(cvec, w_ada, b_ada)
    x = _embed(x_prompt.reshape(N_CTX_TOK, D_MODEL), x_sample.reshape(N_LAT_TOK, D_MODEL), _grid_pos(DEC_SEQ))
    w_rt = w_router.T
    b_r = b_router.reshape(N_EXPERTS, 1)

    a_end = 4 * W_A + N_DIR * 2 * H_A
    b_end = a_end + W_B + W_XBC + N_DIR * H_B
    init = (state_mlstm_C,
            state_mlstm_n.reshape(DEC_BATCH, DEPTH, N_DIR * H_A, DH_A),
            jnp.broadcast_to(state_mlstm_m.reshape(DEC_BATCH, DEPTH, N_DIR * H_A, 1),
                             (DEC_BATCH, DEPTH, N_DIR * H_A, LANES)))
    st_c = st_n = st_m = st_h = None
    for l in range(DEPTH):
        w = w_in[l]
        xbc0 = a_end + W_B
        w_main = jnp.concatenate([w[:, W_A:2 * W_A], w[:, b_end:], w[:, xbc0 + W_B:xbc0 + W_B + G_B * N_B]],
                                 axis=1).astype(BF16)
        w_small = jnp.concatenate([w[:, 4 * W_A:a_end], w[:, b_end - N_DIR * H_B:b_end],
                                   jnp.zeros((D_MODEL, LANES - _DT_COL0 - N_DIR * H_B), F32)], axis=1)
        w_t = jnp.concatenate([w[:, 0:W_A], w[:, 2 * W_A:4 * W_A], w[:, a_end:xbc0 + W_B],
                               w[:, xbc0 + W_B + G_B * N_B:xbc0 + W_XBC], w_small],
                              axis=1).T.astype(BF16)
        w_small = w_small.astype(BF16)
        gate_bias = (_pad_lanes(jnp.stack([mlstm_b_i[l], mlstm_b_f[l]], axis=1), 0)
                     + _pad_lanes(ssd_dt_bias[l], _DT_COL0))
        alog_row = _pad_lanes(ssd_A_log[l], _DT_COL0)
        mod = mod_all[l]

        proj, gates, proj_t = _inproj(x, mod, w_main, w_small, w_t)

        m_norm = jnp.broadcast_to(mlstm_norm_g[l].reshape(W_A, 1), (W_A, LANES))
        mlstm_args = (proj, gates, proj_t, gate_bias, gate_bias.reshape(LANES, 1), m_norm)
        ha_c, st_c, st_n, st_m = _mlstm(*mlstm_args, SEQ, BATCH, 0, l, prev=(st_c, st_n, st_m))
        (ha_l,) = _mlstm(*mlstm_args, DEC_SEQ, DEC_BATCH, N_CTX_TOK, l, init=init)

        nb = G_B * N_B
        rep = lambda v: jnp.broadcast_to(v[..., None], v.shape + (LANES,))
        cw, cb = ssd_conv_w[l], ssd_conv_b[l]
        conv = (rep(cw[:, 0:W_B]), rep(cb[0:W_B]), rep(cw[:, W_B + nb:W_XBC]), rep(cb[W_B + nb:W_XBC]),
                jnp.pad(cw[:, W_B:W_B + nb], ((0, 8 - SSM_CONV), (0, 0))), cb[W_B:W_B + nb].reshape(1, nb))
        ssd_args = (proj, gates, proj_t, gate_bias, gate_bias.reshape(LANES, 1), alog_row, alog_row.reshape(LANES, 1),
                    conv, rep(jnp.repeat(ssd_D[l], P_B)), rep(ssd_norm_g[l]))
        hb_c, st_h = _ssd(*ssd_args, SEQ, BATCH, 0, l, prev=st_h)
        (hb_l,) = _ssd(*ssd_args, DEC_SEQ, DEC_BATCH, N_CTX_TOK, l, init=state_ssd)

        hc = _conf(proj, jnp.pad(conv_dw_w[l], ((0, 32 - CONV_W), (0, 0))), conv_dw_b[l].reshape(1, W_C),
                   conv_ln_g[l].reshape(1, W_C), conv_ln_b[l].reshape(1, W_C))

        x1, u2t, route = _outproj(ha_c, ha_l, hb_c, hb_l, hc, x, mod, w_o[l].astype(BF16), post1_g[l].reshape(1, D_MODEL),
                                  post1_b[l].reshape(1, D_MODEL), w_rt, b_r)
        y = _moe(u2t, *_route_tables(route[0]), w_e_gate, w_e_up, w_e_down, l)
        outs = _final(x1, y, mod, post2_g[l].reshape(1, D_MODEL), post2_b[l].reshape(1, D_MODEL), l == DEPTH - 1)
        x = outs[0]

    y_prompt = outs[0].reshape(BATCH, SEQ, D_MODEL)
    y_sample = outs[1].reshape(DEC_BATCH, DEC_SEQ, D_MODEL)
    return (y_prompt, y_sample, st_c, st_n.reshape(BATCH, DEPTH, N_DIR, H_A, DH_A),
            st_m[:, :, :, 0].reshape(BATCH, DEPTH, N_DIR, H_A), st_h)
```

```python
import functools

import jax
import jax.numpy as jnp
from jax import lax
from jax.experimental import pallas as pl
from jax.experimental.pallas import tpu as pltpu

D_MODEL = 1024
BATCH = 32
SEQ = 256
DEPTH = 2
DEC_BATCH = 2
DEC_SEQ = 1024
GRID_W = 64
N_DIR = 2
CHUNK = 128
H_A = 4
DH_A = 128
W_A = H_A * DH_A
H_B = 8
P_B = 64
W_B = H_B * P_B
G_B = 2
N_B = 128
W_XBC = W_B + 2 * G_B * N_B
SSM_CONV = 3
W_C = 512
CONV_W = 31
D_MIX = W_A + W_B + W_C
N_EXPERTS = 16
N_EGROUPS = 4
EXPERTS_PER_GROUP = N_EXPERTS // N_EGROUPS
D_FF_EXPERT = 512
ALPHA = (2 * DEPTH) ** 0.25
EPS = 1e-5
F32 = jnp.float32
BF16 = jnp.bfloat16

N_CTX_TOK = BATCH * SEQ
N_LAT_TOK = DEC_BATCH * DEC_SEQ
T_ALL = N_CTX_TOK + N_LAT_TOK
N_MOD_ROWS = 8
D_MAIN = W_A + 2 * W_C + G_B * N_B
D_T = 3 * W_A + 2 * W_B + G_B * N_B + 128
LANES = 128
HALO = 16
CONV_ROWS = 256
TM_PROJ = 512
TM_MOE = 256
VMEM_LIMIT = 48 * 1024 * 1024
BIG_VMEM_LIMIT = 56 * 1024 * 1024

_NT = (((1,), (1,)), ((), ()))
_TN = (((0,), (0,)), ((), ()))


def _ln(x):
    mu = jnp.mean(x, axis=-1, keepdims=True)
    xc = x - mu
    var = jnp.mean(xc * xc, axis=-1, keepdims=True)
    return xc * lax.rsqrt(var + EPS)


def _sigmoid(x):
    return 1.0 / (1.0 + jnp.exp(-x))


def _silu(x):
    return x * _sigmoid(x)


def _softplus(x):
    return jnp.maximum(x, 0.0) + jnp.log1p(jnp.exp(-jnp.abs(x)))


def _dot(a, b):
    return jnp.dot(a, b, preferred_element_type=F32)


def _dotg(a, b, dims):
    return lax.dot_general(a, b, dims, preferred_element_type=F32)


def _tri_cumsum(tri, x):
    hi = x.astype(BF16)
    r1 = x - hi.astype(F32)
    mid = r1.astype(BF16)
    lo = (r1 - mid.astype(F32)).astype(BF16)
    return _dot(tri, hi) + _dot(tri, mid) + _dot(tri, lo)


def _tri_mask(d):
    row = lax.broadcasted_iota(jnp.int32, (CHUNK, CHUNK), 0)
    col = lax.broadcasted_iota(jnp.int32, (CHUNK, CHUNK), 1)
    return (row >= col) if d == 0 else (row <= col)


def _mod_row(row_start):
    return jnp.where(row_start < N_CTX_TOK, 0, 1 + (row_start - N_CTX_TOK) // DEC_SEQ)


def _cparams(n_axes, vmem_limit=VMEM_LIMIT):
    return pltpu.CompilerParams(dimension_semantics=("arbitrary",) * n_axes, vmem_limit_bytes=vmem_limit)


def _ada_kernel(c_ref, w_ref, b_ref, o_ref):
    o_ref[0] = _dot(_silu(c_ref[...]), w_ref[0]) + b_ref[0]


def _ada(cvec, w_ada, b_ada):
    tn = 1536
    return pl.pallas_call(
        _ada_kernel,
        grid=(DEPTH, 6 * D_MODEL // tn),
        in_specs=[
            pl.BlockSpec((N_MOD_ROWS, D_MODEL), lambda l, j: (0, 0)),
            pl.BlockSpec((1, D_MODEL, tn), lambda l, j: (l, 0, j)),
            pl.BlockSpec((1, 1, tn), lambda l, j: (l, 0, j)),
        ],
        out_specs=pl.BlockSpec((1, N_MOD_ROWS, tn), lambda l, j: (l, 0, j)),
        out_shape=jax.ShapeDtypeStruct((DEPTH, N_MOD_ROWS, 6 * D_MODEL), F32),
        compiler_params=_cparams(2),
        name="ada",
    )(cvec, w_ada, b_ada.reshape(DEPTH, 1, 6 * D_MODEL))


def _embed_kernel(xp_ref, xs_ref, pos_ref, o_ref):
    i = pl.program_id(0)

    @pl.when(i < N_CTX_TOK // DEC_SEQ)
    def _():
        o_ref[...] = xp_ref[...]

    @pl.when(i >= N_CTX_TOK // DEC_SEQ)
    def _():
        o_ref[...] = xs_ref[...] + pos_ref[...]


def _embed(xp, xs, pos):
    n_ctx_blk = N_CTX_TOK // DEC_SEQ
    return pl.pallas_call(
        _embed_kernel,
        grid=(T_ALL // DEC_SEQ,),
        in_specs=[
            pl.BlockSpec((DEC_SEQ, D_MODEL), lambda i: (jnp.minimum(i, n_ctx_blk - 1), 0)),
            pl.BlockSpec((DEC_SEQ, D_MODEL), lambda i: (jnp.maximum(i - n_ctx_blk, 0), 0)),
            pl.BlockSpec((DEC_SEQ, D_MODEL), lambda i: (0, 0)),
        ],
        out_specs=pl.BlockSpec((DEC_SEQ, D_MODEL), lambda i: (i, 0)),
        out_shape=jax.ShapeDtypeStruct((T_ALL, D_MODEL), F32),
        compiler_params=_cparams(1),
        name="embed",
    )(xp, xs, pos)


def _inproj_kernel(x_ref, mod_ref, wm_ref, wg_ref, wt_ref, om_ref, og_ref, ot_ref, *, tm):
    r = _mod_row(pl.program_id(0) * tm)
    shift = mod_ref[pl.ds(r, 1), 0:D_MODEL]
    scale = mod_ref[pl.ds(r, 1), D_MODEL:2 * D_MODEL]
    u = (_ln(x_ref[...]) * (1.0 + scale) + shift).astype(BF16)
    om_ref[...] = _dot(u, wm_ref[...])
    og_ref[...] = _dot(u, wg_ref[...])
    ot_ref[...] = _dotg(wt_ref[...], u, _NT)


def _inproj(x, mod, w_main, w_small, w_t):
    tm = TM_PROJ
    const = lambda shape: pl.BlockSpec(shape, lambda i: (0, 0))
    return pl.pallas_call(
        functools.partial(_inproj_kernel, tm=tm),
        grid=(T_ALL // tm,),
        in_specs=[
            pl.BlockSpec((tm, D_MODEL), lambda i: (i, 0)),
            const((N_MOD_ROWS, 6 * D_MODEL)), const((D_MODEL, D_MAIN)), const((D_MODEL, LANES)),
            const((D_T, D_MODEL)),
        ],
        out_specs=[
            pl.BlockSpec((tm, D_MAIN), lambda i: (i, 0)),
            pl.BlockSpec((tm, LANES), lambda i: (i, 0)),
            pl.BlockSpec((D_T, tm), lambda i: (0, i)),
        ],
        out_shape=[
            jax.ShapeDtypeStruct((T_ALL, D_MAIN), F32),
            jax.ShapeDtypeStruct((T_ALL, LANES), F32),
            jax.ShapeDtypeStruct((D_T, T_ALL), F32),
        ],
        compiler_params=_cparams(1, BIG_VMEM_LIMIT),
        name="inproj",
    )(x, mod, w_main, w_small, w_t)


def _scan_max(x, lane, d):
    s = 1
    while s < CHUNK:
        if d == 0:
            x = jnp.where(lane >= s, jnp.maximum(x, pltpu.roll(x, s, axis=1)), x)
        else:
            x = jnp.where(lane < CHUNK - s, jnp.maximum(x, pltpu.roll(x, CHUNK - s, axis=1)), x)
        s *= 2
    return x


def _split_dot(x, tri):
    hi = x.astype(BF16)
    r1 = x - hi.astype(F32)
    mid = r1.astype(BF16)
    lo = (r1 - mid.astype(F32)).astype(BF16)
    return _dot(hi, tri) + _dot(mid, tri) + _dot(lo, tri)


def _mlstm_kernel(*refs, seq, has_init, emit_state, layer):
    it = iter(refs)
    k_ref, qt_ref, vt_ref, ot_ref, g_ref, gt_ref, gb_ref, gbt_ref, ng_ref = (next(it) for _ in range(9))
    if has_init:
        c0_ref, n0_ref, m0_ref = (next(it) for _ in range(3))
    if emit_state and layer > 0:
        prev_refs = [next(it) for _ in range(3)]
    out_ref = next(it)
    if emit_state:
        co_ref, no_ref, mo_ref = (next(it) for _ in range(3))
    tg_scr, tgt_scr, b_scr, beta_scr, pm_scr, cumc_scr, hd_scr, c_scr, n_scr, m_scr = (next(it) for _ in range(10))
    n_chunks = seq // CHUNK
    n_gate = N_DIR * 2 * H_A
    if emit_state and layer > 0:
        for prev_ref, st_ref in zip(prev_refs, (co_ref, no_ref, mo_ref)):
            st_ref[0, 0:layer] = prev_ref[0]

    y = g_ref[...] + gb_ref[...]
    lane = lax.broadcasted_iota(jnp.int32, y.shape, 1)
    tg_scr[...] = jnp.where((lane < n_gate) & ((lane & H_A) != 0), -_softplus(-y), y)
    yt = gt_ref[0:n_gate, :] + gbt_ref[0:n_gate, :]
    rowt = lax.broadcasted_iota(jnp.int32, yt.shape, 0)
    tgt_scr[...] = jnp.where((rowt & H_A) != 0, -_softplus(-yt), yt)

    lane8 = lax.broadcasted_iota(jnp.int32, (8, CHUNK), 1)
    row8 = lax.broadcasted_iota(jnp.int32, (8, LANES), 0)
    masks_t = [_tri_mask(1 - d) for d in range(N_DIR)]

    for d in range(N_DIR):
        tri_t = masks_t[d].astype(BF16)
        tri_c = _tri_mask(d).astype(BF16)
        rows = slice(d * 8, (d + 1) * 8)
        for c in range(n_chunks):
            cs = slice(c * CHUNK, (c + 1) * CHUNK)
            gr8 = tgt_scr[rows, cs]
            b8 = pltpu.roll(_split_dot(gr8, tri_t), H_A, axis=0)
            beta8 = gr8 - b8
            b_scr[rows, cs] = b8
            beta_scr[rows, cs] = beta8
            pm_scr[rows, cs] = _scan_max(beta8, lane8, d)
            cumc_scr[d, cs, :] = _tri_cumsum(tri_c, tg_scr[cs, :])

    pos = (CHUNK - 1, 0)

    for d in range(N_DIR):
        rows = slice(d * 8, (d + 1) * 8)
        if has_init:
            for h in range(H_A):
                c_scr[d, h] = c0_ref[0, 0, d, h].T
            n0 = n0_ref[0, 0]
            m0 = m0_ref[0, 0]
            if d == 1:
                n0 = pltpu.roll(n0, H_A, axis=0)
                m0 = pltpu.roll(m0, H_A, axis=0)
            n_scr[rows, :] = jnp.where(row8 < H_A, n0, 0.0)
            m_scr[rows, :] = jnp.where(row8 < H_A, m0, 0.0)
        else:
            for h in range(H_A):
                c_scr[d, h] = jnp.zeros((DH_A, DH_A), F32)
            n_scr[rows, :] = jnp.zeros((8, DH_A), F32)
            m_scr[rows, :] = jnp.zeros((8, LANES), F32)

    def chunk_prep(d, c):
        rows = slice(d * 8, (d + 1) * 8)
        cs = slice(c * CHUNK, (c + 1) * CHUNK)
        b8 = b_scr[rows, cs]
        beta8 = beta_scr[rows, cs]
        m8 = m_scr[rows, :]
        n8 = n_scr[rows, :]
        mu8 = jnp.maximum(m8, pm_scr[rows, cs])
        mu_last = mu8[:, pos[d]:pos[d] + 1]
        w_tok8 = jnp.exp(beta8 - mu_last)
        return dict(cs=cs, rows=rows, n8=n8, mu8=mu8, w_int8=jnp.exp(m8 - mu8), emt8=jnp.exp(-(b8 + mu8)),
                    w_tok8=w_tok8, w_prev8=jnp.exp(m8 - mu_last), m_new8=b8[:, pos[d]:pos[d] + 1] + mu_last,
                    n16b=jnp.concatenate([n8, jnp.zeros_like(n8)], axis=0).astype(BF16),
                    w_tok16=jnp.concatenate([w_tok8, jnp.zeros_like(w_tok8)], axis=0),
                    gc=tg_scr[cs, :], cum_c=cumc_scr[d, cs, :])

    def head_step(d, h, p):
        cs = p["cs"]
        hs = slice(h * DH_A, (h + 1) * DH_A)
        col_i = d * 2 * H_A + h
        col_f = col_i + H_A
        beta_c = p["gc"][:, col_i:col_i + 1] - p["cum_c"][:, col_f:col_f + 1]
        kb = k_ref[cs, hs].astype(BF16)
        qtb = (qt_ref[hs, cs] * (DH_A ** -0.5)).astype(BF16)
        vt = vt_ref[hs, cs]
        ct = c_scr[d, h]
        w_int_r = p["w_int8"][h:h + 1, :]

        res = _dot(jnp.concatenate([kb, ct.astype(BF16), p["n16b"]], axis=0), qtb)
        st = res[0:CHUNK] * jnp.exp(jnp.where(masks_t[d], beta_c - p["mu8"][h:h + 1, :], -jnp.inf))
        num = w_int_r * res[CHUNK:CHUNK + DH_A] + _dot(vt.astype(BF16), st.astype(BF16))
        den_r = (w_int_r * res[CHUNK + DH_A + h:CHUNK + DH_A + h + 1]
                 + jnp.sum(st, axis=0, keepdims=True))
        hd_scr[d, hs, cs] = num * (1.0 / jnp.maximum(jnp.abs(den_r), p["emt8"][h:h + 1, :]))

        upd = _dot(jnp.concatenate([vt * p["w_tok8"][h:h + 1, :], p["w_tok16"]], axis=0).astype(BF16), kb)
        c_scr[d, h] = p["w_prev8"][h:h + 1, :] * ct + upd[0:DH_A]
        return upd[DH_A:DH_A + 8]

    for ci in range(n_chunks):
        preps = [chunk_prep(0, ci), chunk_prep(1, n_chunks - 1 - ci)]
        nk_acc = [jnp.zeros((8, DH_A), F32) for _ in range(N_DIR)]
        for h in range(H_A):
            for d in range(N_DIR):
                nk_acc[d] = jnp.where(row8 == h, head_step(d, h, preps[d]), nk_acc[d])
        for d in range(N_DIR):
            p = preps[d]
            n_scr[p["rows"], :] = p["w_prev8"] * p["n8"] + nk_acc[d]
            m_scr[p["rows"], :] = jnp.where(row8 < H_A, jnp.broadcast_to(p["m_new8"], (8, LANES)), 0.0)

    if emit_state:
        for d in range(N_DIR):
            for h in range(H_A):
                co_ref[0, layer, d, h] = c_scr[d, h].T
            no_ref[0, layer, d * H_A:(d + 1) * H_A, :] = n_scr[d * 8:d * 8 + H_A, :]
            mo_ref[0, layer, d * H_A:(d + 1) * H_A, :] = m_scr[d * 8:d * 8 + H_A, :]

    for c in range(n_chunks):
        cs = slice(c * CHUNK, (c + 1) * CHUNK)
        for h in range(H_A):
            hs = slice(h * DH_A, (h + 1) * DH_A)
            hsum = hd_scr[0, hs, cs] + hd_scr[1, hs, cs]
            mean = jnp.sum(hsum, axis=0, keepdims=True) * (1.0 / DH_A)
            xc = hsum - mean
            var = jnp.sum(xc * xc, axis=0, keepdims=True) * (1.0 / DH_A)
            hn = xc * lax.rsqrt(var + EPS) * ng_ref[hs, :]
            out_ref[hs, cs] = (hn * _sigmoid(ot_ref[hs, cs])).astype(out_ref.dtype)


def _mlstm(proj, gates, proj_t, gate_bias, gate_bias_t, norm_g_rep, seq, n_seq, row_off, layer, init=None, prev=None):
    blk0 = row_off // seq
    has_init = init is not None
    emit_state = not has_init
    feat = lambda r: pl.BlockSpec((W_A, seq), lambda s: (r, s + blk0))
    const = lambda shape: pl.BlockSpec(shape, lambda s: (0,) * len(shape))
    st_shapes = [(N_DIR, H_A, DH_A, DH_A), (N_DIR * H_A, DH_A), (N_DIR * H_A, LANES)]

    def st_specs(n_layers, first):
        return [pl.BlockSpec((1, n_layers) + shp, lambda s, nd=len(shp): (s, first) + (0,) * nd) for shp in st_shapes]

    in_specs = [pl.BlockSpec((seq, W_A), lambda s: (s + blk0, 0)), feat(0), feat(1), feat(2),
                pl.BlockSpec((seq, LANES), lambda s: (s + blk0, 0)),
                pl.BlockSpec((LANES, seq), lambda s: (D_T // LANES - 1, s + blk0)),
                const((1, LANES)), const((LANES, 1)), const((W_A, LANES))]
    args = [proj, proj_t, proj_t, proj_t, gates, proj_t, gate_bias, gate_bias_t, norm_g_rep]
    if has_init:
        in_specs += st_specs(1, layer)
        args += list(init)
    if emit_state and layer > 0:
        in_specs += st_specs(layer, 0)
        args += list(prev)
    out_specs = [pl.BlockSpec((W_A, seq), lambda s: (0, s))]
    out_shape = [jax.ShapeDtypeStruct((W_A, n_seq * seq), BF16)]
    if emit_state:
        out_specs += st_specs(layer + 1, 0)
        out_shape += [jax.ShapeDtypeStruct((n_seq, layer + 1) + shp, F32) for shp in st_shapes]
    return pl.pallas_call(
        functools.partial(_mlstm_kernel, seq=seq, has_init=has_init, emit_state=emit_state, layer=layer),
        grid=(n_seq,),
        in_specs=in_specs,
        out_specs=out_specs,
        out_shape=out_shape,
        scratch_shapes=[pltpu.VMEM((seq, LANES), F32),
                        pltpu.VMEM((N_DIR * 8, seq), F32),
                        pltpu.VMEM((N_DIR * 8, seq), F32),
                        pltpu.VMEM((N_DIR * 8, seq), F32),
                        pltpu.VMEM((N_DIR * 8, seq), F32),
                        pltpu.VMEM((N_DIR, seq, LANES), F32),
                        pltpu.VMEM((N_DIR, W_A, seq), F32),
                        pltpu.VMEM((N_DIR, H_A, DH_A, DH_A), F32),
                        pltpu.VMEM((N_DIR * 8, DH_A), F32),
                        pltpu.VMEM((N_DIR * 8, LANES), F32)],
        compiler_params=_cparams(1),
        name="mlstm_lat" if has_init else "mlstm_ctx",
    )(*args)


_DT_COL0 = N_DIR * 2 * H_A
_HG = H_B // G_B
_WG = _HG * P_B


def _conv3_lanes(ref, w_ref, b_ref, out_scr, seq):
    lane = lax.broadcasted_iota(jnp.int32, (CHUNK, seq), 1)
    for r in range(ref.shape[0] // CHUNK):
        rs = slice(r * CHUNK, (r + 1) * CHUNK)
        cur = ref[rs, :]
        prev = jnp.where(lane == 0, 0.0, pltpu.roll(cur, 1, axis=1))
        nxt = jnp.where(lane == seq - 1, 0.0, pltpu.roll(cur, seq - 1, axis=1))
        for c in range(seq // CHUNK):
            cs = slice(c * CHUNK, (c + 1) * CHUNK)
            acc = (w_ref[0, rs, :] * prev[:, cs] + w_ref[1, rs, :] * cur[:, cs] + w_ref[2, rs, :] * nxt[:, cs]
                   + b_ref[rs, :])
            out_scr[rs, cs] = _silu(acc).astype(out_scr.dtype)


def _ssd_kernel(*refs, seq, has_init, emit_state, layer):
    it = iter(refs)
    (zt_ref, xt_ref, ct_ref, b_ref, g_ref, gt_ref, gb_ref, gbt_ref, alog_ref, alogt_ref,
     cwx_ref, cbx_ref, cwc_ref, cbc_ref, cwb_ref, cbb_ref, dsk_ref, ng_ref) = (next(it) for _ in range(18))
    if has_init:
        h0_ref = next(it)
    if emit_state and layer > 0:
        prev_ref = next(it)
    out_ref = next(it)
    if emit_state:
        ho_ref = next(it)
    pad_scr, xs_scr, cs_scr, bs_scr, dtr_scr, br_scr, cumc_scr, yt_scr, h_scr = (next(it) for _ in range(9))
    n_chunks = seq // CHUNK
    pad = 8
    if emit_state and layer > 0:
        ho_ref[0, 0:layer] = prev_ref[0]

    _conv3_lanes(xt_ref, cwx_ref, cbx_ref, xs_scr, seq)
    _conv3_lanes(ct_ref, cwc_ref, cbc_ref, cs_scr, seq)
    nb = G_B * N_B
    pad_scr[0:pad, :] = jnp.zeros((pad, nb), F32)
    pad_scr[pad + seq:2 * pad + seq, :] = jnp.zeros((pad, nb), F32)
    pad_scr[pad:pad + seq, :] = b_ref[...]
    for c in range(n_chunks):
        acc = cbb_ref[...]
        for k in range(SSM_CONV):
            r = c * CHUNK + pad - SSM_CONV // 2 + k
            acc = acc + cwb_ref[k:k + 1, :] * pad_scr[r:r + CHUNK, :]
        bs_scr[c * CHUNK:(c + 1) * CHUNK, :] = _silu(acc).astype(bs_scr.dtype)

    lac = _softplus(g_ref[...] + gb_ref[...]) * (-jnp.exp(alog_ref[...]))
    r0 = _DT_COL0
    dtr = _softplus(gt_ref[r0:r0 + N_DIR * H_B, :] + gbt_ref[r0:r0 + N_DIR * H_B, :])
    dtr_scr[...] = dtr
    lar = dtr * (-jnp.exp(alogt_ref[r0:r0 + N_DIR * H_B, :]))
    masks_t = [_tri_mask(1 - d) for d in range(N_DIR)]
    for d in range(N_DIR):
        tri_t = masks_t[d].astype(BF16)
        tri_c = _tri_mask(d).astype(BF16)
        rows = slice(d * H_B, (d + 1) * H_B)
        for c in range(n_chunks):
            cs = slice(c * CHUNK, (c + 1) * CHUNK)
            br_scr[rows, cs] = _split_dot(lar[rows, cs], tri_t)
            cumc_scr[d, cs, :] = _tri_cumsum(tri_c, lac[cs, :])

    row_blk = lax.broadcasted_iota(jnp.int32, (_WG, 1), 0) // P_B

    def run_direction(d):
        for g in range(G_B):
            for hh in range(_HG):
                if has_init:
                    h_scr[d, g, hh * P_B:(hh + 1) * P_B, :] = h0_ref[0, 0, d, g * _HG + hh]
                else:
                    h_scr[d, g, hh * P_B:(hh + 1) * P_B, :] = jnp.zeros((P_B, N_B), F32)

        mask_t = masks_t[d]
        pos = CHUNK - 1 if d == 0 else 0

        for ci in range(n_chunks):
            c = ci if d == 0 else n_chunks - 1 - ci
            cs = slice(c * CHUNK, (c + 1) * CHUNK)
            cum_c = cumc_scr[d, cs, :]
            for g in range(G_B):
                bg = bs_scr[cs, g * N_B:(g + 1) * N_B]
                ctg = cs_scr[g * N_B:(g + 1) * N_B, cs]
                h_st = h_scr[d, g]
                res = _dot(jnp.concatenate([bg, h_st.astype(BF16)], axis=0), ctg)
                cbt = res[0:CHUNK]
                inter = res[CHUNK:CHUNK + _WG]
                xw = []
                decay = jnp.zeros((_WG, 1), F32)
                for hh in range(_HG):
                    head = g * _HG + hh
                    r = d * H_B + head
                    b_r = br_scr[r:r + 1, cs]
                    dt_r = dtr_scr[r:r + 1, cs]
                    b_c = cum_c[:, _DT_COL0 + r:_DT_COL0 + r + 1]
                    gt = b_r[:, pos:pos + 1]
                    ps = slice(head * P_B, (head + 1) * P_B)
                    xh = xs_scr[ps, cs]
                    st = (cbt * jnp.exp(jnp.where(mask_t, b_r - b_c, -jnp.inf))).astype(BF16)
                    y = _dot((xh * dt_r).astype(BF16), st) + jnp.exp(b_r) * inter[hh * P_B:(hh + 1) * P_B]
                    xw.append(xh * (jnp.exp(gt - b_r) * dt_r))
                    decay = jnp.where(row_blk == hh, jnp.exp(gt), decay)
                    if d == 0:
                        yt_scr[ps, cs] = y
                    else:
                        yt_scr[ps, cs] = yt_scr[ps, cs] + y + dsk_ref[ps, :] * xh
                h_scr[d, g] = decay * h_st + _dot(jnp.concatenate(xw, axis=0).astype(BF16), bg)
            if d == 1:
                yz = yt_scr[:, cs] * _silu(zt_ref[:, cs])
                rms = lax.rsqrt(jnp.sum(yz * yz, axis=0, keepdims=True) * (1.0 / W_B) + EPS)
                out_ref[:, cs] = (yz * rms * ng_ref[...]).astype(out_ref.dtype)

        if emit_state:
            for g in range(G_B):
                for hh in range(_HG):
                    ho_ref[0, layer, d, g * _HG + hh] = h_scr[d, g, hh * P_B:(hh + 1) * P_B, :]

    run_direction(0)
    run_direction(1)


def _ssd(proj, gates, proj_t, gate_bias, gate_bias_t, alog_row, alog_col, conv, d_rep, norm_g_rep,
         seq, n_seq, row_off, layer, init=None, prev=None):
    blk0 = row_off // seq
    has_init = init is not None
    emit_state = not has_init
    nb = G_B * N_B
    zr = 3 * W_A // W_B
    cr = (3 * W_A + 2 * W_B) // nb
    gr = (3 * W_A + 2 * W_B + nb) // LANES
    bc = (W_A + 2 * W_C) // nb
    const = lambda shape: pl.BlockSpec(shape, lambda s: (0,) * len(shape))
    st_spec = lambda n_layers, first: pl.BlockSpec((1, n_layers, N_DIR, H_B, P_B, N_B),
                                                   lambda s: (s, first, 0, 0, 0, 0))
    in_specs = [pl.BlockSpec((W_B, seq), lambda s: (zr, s + blk0)),
                pl.BlockSpec((W_B, seq), lambda s: (zr + 1, s + blk0)),
                pl.BlockSpec((nb, seq), lambda s: (cr, s + blk0)),
                pl.BlockSpec((seq, nb), lambda s: (s + blk0, bc)),
                pl.BlockSpec((seq, LANES), lambda s: (s + blk0, 0)),
                pl.BlockSpec((LANES, seq), lambda s: (gr, s + blk0)),
                const((1, LANES)), const((LANES, 1)), const((1, LANES)), const((LANES, 1)),
                const((SSM_CONV, W_B, LANES)), const((W_B, LANES)), const((SSM_CONV, nb, LANES)), const((nb, LANES)),
                const((8, nb)), const((1, nb)), const((W_B, LANES)), const((W_B, LANES))]
    args = [proj_t, proj_t, proj_t, proj, gates, proj_t, gate_bias, gate_bias_t, alog_row, alog_col,
            *conv, d_rep, norm_g_rep]
    if has_init:
        in_specs.append(st_spec(1, layer))
        args.append(init)
    if emit_state and layer > 0:
        in_specs.append(st_spec(layer, 0))
        args.append(prev)
    out_specs = [pl.BlockSpec((W_B, seq), lambda s: (0, s))]
    out_shape = [jax.ShapeDtypeStruct((W_B, n_seq * seq), BF16)]
    if emit_state:
        out_specs.append(st_spec(layer + 1, 0))
        out_shape.append(jax.ShapeDtypeStruct((n_seq, layer + 1, N_DIR, H_B, P_B, N_B), F32))
    return pl.pallas_call(
        functools.partial(_ssd_kernel, seq=seq, has_init=has_init, emit_state=emit_state, layer=layer),
        grid=(n_seq,),
        in_specs=in_specs,
        out_specs=out_specs,
        out_shape=out_shape,
        scratch_shapes=[pltpu.VMEM((seq + 16, nb), F32),
                        pltpu.VMEM((W_B, seq), F32),
                        pltpu.VMEM((nb, seq), BF16),
                        pltpu.VMEM((seq, nb), BF16),
                        pltpu.VMEM((N_DIR * H_B, seq), F32),
                        pltpu.VMEM((N_DIR * H_B, seq), F32),
                        pltpu.VMEM((N_DIR, seq, LANES), F32),
                        pltpu.VMEM((W_B, seq), F32),
                        pltpu.VMEM((N_DIR, G_B, _WG, N_B), F32)],
        compiler_params=_cparams(1),
        name="ssd_lat" if has_init else "ssd_ctx",
    )(*args)


def _conf_kernel(a_ref, ap_ref, an_ref, g_ref, gp_ref, gn_ref, w_ref, b_ref, lg_ref, lb_ref, out_ref, pad_scr,
                 rot_scr):
    i = pl.program_id(0)
    n_ctx_blk = N_CTX_TOK // CONV_ROWS
    blk_per_lat = DEC_SEQ // CONV_ROWS
    j = (i - n_ctx_blk) % blk_per_lat
    is_ctx = i < n_ctx_blk
    keep_prev = jnp.where(is_ctx | (j == 0), 0.0, 1.0)
    keep_next = jnp.where(is_ctx | (j == blk_per_lat - 1), 0.0, 1.0)
    pad_scr[0:HALO, :] = ap_ref[...] * _sigmoid(gp_ref[...]) * keep_prev
    pad_scr[HALO:HALO + CONV_ROWS, :] = a_ref[...] * _sigmoid(g_ref[...])
    pad_scr[HALO + CONV_ROWS:2 * HALO + CONV_ROWS, :] = an_ref[...] * _sigmoid(gn_ref[...]) * keep_next
    n_rot = CONV_ROWS + 2 * HALO - 8
    for s in range(1, 8):
        rot_scr[s, :, :] = pad_scr[s:s + n_rot, :]
    rc = 64
    for c in range(CONV_ROWS // rc):
        acc = jnp.broadcast_to(b_ref[...], (rc, W_C))
        for k in range(CONV_W):
            off = HALO - CONV_W // 2 + k
            r = c * rc + off - off % 8
            src = pad_scr[r:r + rc, :] if off % 8 == 0 else rot_scr[off % 8, r:r + rc, :]
            acc = acc + w_ref[k:k + 1, :] * src
        u = _ln(acc) * lg_ref[...] + lb_ref[...]
        out_ref[c * rc:(c + 1) * rc, :] = _silu(u).astype(out_ref.dtype)


def _conf(proj, dw_w, dw_b, ln_g, ln_b):
    assert SEQ == CONV_ROWS and DEC_SEQ % CONV_ROWS == 0
    ac = W_A // W_C
    hb = CONV_ROWS // HALO
    n_halo = T_ALL // HALO
    const = lambda shape: pl.BlockSpec(shape, lambda i: (0,) * len(shape))

    def specs(c):
        return [pl.BlockSpec((CONV_ROWS, W_C), lambda i: (i, c)),
                pl.BlockSpec((HALO, W_C), lambda i: (jnp.maximum(i * hb - 1, 0), c)),
                pl.BlockSpec((HALO, W_C), lambda i: (jnp.minimum((i + 1) * hb, n_halo - 1), c))]

    return pl.pallas_call(
        _conf_kernel,
        grid=(T_ALL // CONV_ROWS,),
        in_specs=specs(ac) + specs(ac + 1) + [const((32, W_C)), const((1, W_C)), const((1, W_C)), const((1, W_C))],
        out_specs=pl.BlockSpec((CONV_ROWS, W_C), lambda i: (i, 0)),
        out_shape=jax.ShapeDtypeStruct((T_ALL, W_C), BF16),
        scratch_shapes=[pltpu.VMEM((CONV_ROWS + 2 * HALO, W_C), F32),
                        pltpu.VMEM((8, CONV_ROWS + 2 * HALO - 8, W_C), F32)],
        compiler_params=_cparams(1),
        name="conf",
    )(proj, proj, proj, proj, proj, proj, dw_w, dw_b, ln_g, ln_b)


_PAIRS = [(a, b) for a in range(EXPERTS_PER_GROUP) for b in range(a + 1, EXPERTS_PER_GROUP)]
N_CLASSES = N_EGROUPS * len(_PAIRS)
N_FCHUNK = D_MODEL // LANES
N_IN_ROWS = 2 * N_FCHUNK


def _store_token_tiles(ref, val):
    n = val.shape[0]
    for c in range(N_FCHUNK):
        ref[pl.ds(c, n, stride=N_FCHUNK), :] = val[:, c * LANES:(c + 1) * LANES]


def _load_token_tiles(ref, n):
    return jnp.concatenate([ref[pl.ds(c, n, stride=N_FCHUNK), :] for c in range(N_FCHUNK)], axis=1)
N_MOE_ROWS = T_ALL + N_CLASSES * TM_MOE
N_MOE_TILES = N_MOE_ROWS // TM_MOE


def _outproj_kernel(hac_ref, hal_ref, hbc_ref, hbl_ref, hc_ref, x_ref, mod_ref, wo_ref, pg_ref, pb_ref,
                    wr_ref, br_ref, x1_ref, u2_ref, rt_ref, *, tm):
    i = pl.program_id(0)
    r = _mod_row(i * tm)
    gate1 = mod_ref[pl.ds(r, 1), 2 * D_MODEL:3 * D_MODEL]
    shift2 = mod_ref[pl.ds(r, 1), 3 * D_MODEL:4 * D_MODEL]
    scale2 = mod_ref[pl.ds(r, 1), 4 * D_MODEL:5 * D_MODEL]
    is_ctx = i < N_CTX_TOK // tm
    ha = jnp.where(is_ctx, hac_ref[...], hal_ref[...])
    hb = jnp.where(is_ctx, hbc_ref[...], hbl_ref[...])
    mix = (_dotg(ha, wo_ref[0:W_A, :], _TN) + _dotg(hb, wo_ref[W_A:W_A + W_B, :], _TN)
           + _dot(hc_ref[...], wo_ref[W_A + W_B:D_MIX, :]))
    x1 = _ln(ALPHA * x_ref[...] + gate1 * mix) * pg_ref[...] + pb_ref[...]
    x1_ref[...] = x1
    u2 = _ln(x1) * (1.0 + scale2) + shift2
    for c in range(N_FCHUNK):
        u2_ref[pl.ds(c, tm, stride=N_IN_ROWS), :] = u2[:, c * LANES:(c + 1) * LANES]

    logits = lax.dot_general(wr_ref[...], u2, _NT, precision=lax.Precision.HIGHEST,
                             preferred_element_type=F32) + br_ref[...]
    ex = jnp.exp(logits - jnp.max(logits, axis=0, keepdims=True))
    probs = ex / jnp.sum(ex, axis=0, keepdims=True)
    scores = []
    for g in range(N_EGROUPS):
        p = [probs[g * EXPERTS_PER_GROUP + e:g * EXPERTS_PER_GROUP + e + 1, :] for e in range(EXPERTS_PER_GROUP)]
        best = p[0] + p[1]
        for a in range(EXPERTS_PER_GROUP):
            for b in range(a + 1, EXPERTS_PER_GROUP):
                if (a, b) != (0, 1):
                    best = jnp.maximum(best, p[a] + p[b])
        scores.append(best)
    gmax = functools.reduce(jnp.maximum, scores)
    sel = jnp.full(gmax.shape, N_EGROUPS - 1, jnp.int32)
    for g in range(N_EGROUPS - 2, -1, -1):
        sel = jnp.where(scores[g] == gmax, g, sel)
    eidx = lax.broadcasted_iota(jnp.int32, probs.shape, 0)
    pm = jnp.where((eidx // EXPERTS_PER_GROUP) == sel, probs, -jnp.inf)
    p1 = jnp.max(pm, axis=0, keepdims=True)
    i1 = jnp.min(jnp.where(pm == p1, eidx, N_EXPERTS), axis=0, keepdims=True)
    pm2 = jnp.where(eidx == i1, -jnp.inf, pm)
    p2 = jnp.max(pm2, axis=0, keepdims=True)
    i2 = jnp.min(jnp.where(pm2 == p2, eidx, N_EXPERTS), axis=0, keepdims=True)
    den = p1 + p2
    first_lo = i1 < i2
    w_lo = jnp.where(first_lo, p1, p2) / den
    w_hi = jnp.where(first_lo, p2, p1) / den
    a = jnp.minimum(i1, i2) - sel * EXPERTS_PER_GROUP
    b = jnp.maximum(i1, i2) - sel * EXPERTS_PER_GROUP
    pair = jnp.zeros_like(a)
    for k, (pa, pb_) in enumerate(_PAIRS):
        pair = jnp.where((a == pa) & (b == pb_), k, pair)
    cls = sel * len(_PAIRS) + pair
    rt_ref[...] = jnp.broadcast_to(cls, rt_ref.shape)
    wrow = lax.broadcasted_iota(jnp.int32, (LANES, tm), 0)
    wt = jnp.where(wrow == 0, w_lo, jnp.where(wrow == 1, w_hi, 0.0))
    u2_ref[pl.ds(N_FCHUNK, tm, stride=N_IN_ROWS), :] = wt.T
    for c in range(N_FCHUNK + 1, N_IN_ROWS):
        u2_ref[pl.ds(c, tm, stride=N_IN_ROWS), :] = jnp.zeros((tm, LANES), F32)


def _outproj(ha_c, ha_l, hb_c, hb_l, hc, x, mod, w_o, pg, pb, w_rt, b_r):
    tm = TM_PROJ
    n_ctx_blk = N_CTX_TOK // tm
    const = lambda shape: pl.BlockSpec(shape, lambda i: (0,) * len(shape))
    rows = lambda w: pl.BlockSpec((tm, w), lambda i: (i, 0))
    ctx_feat = lambda w: pl.BlockSpec((w, tm), lambda i: (0, jnp.minimum(i, n_ctx_blk - 1)))
    lat_feat = lambda w: pl.BlockSpec((w, tm), lambda i: (0, jnp.maximum(i - n_ctx_blk, 0)))
    return pl.pallas_call(
        functools.partial(_outproj_kernel, tm=tm),
        grid=(T_ALL // tm,),
        in_specs=[ctx_feat(W_A), lat_feat(W_A), ctx_feat(W_B), lat_feat(W_B),
                  rows(W_C), rows(D_MODEL), const((N_MOD_ROWS, 6 * D_MODEL)),
                  const((D_MIX, D_MODEL)), const((1, D_MODEL)), const((1, D_MODEL)),
                  const((N_EXPERTS, D_MODEL)), const((N_EXPERTS, 1))],
        out_specs=[rows(D_MODEL), pl.BlockSpec((tm * N_IN_ROWS, LANES), lambda i: (i, 0)),
                   pl.BlockSpec((8, tm), lambda i: (0, i))],
        out_shape=[jax.ShapeDtypeStruct((T_ALL, D_MODEL), F32),
                   jax.ShapeDtypeStruct((T_ALL * N_IN_ROWS, LANES), F32),
                   jax.ShapeDtypeStruct((8, T_ALL), jnp.int32)],
        compiler_params=_cparams(1),
        name="outproj",
    )(ha_c, ha_l, hb_c, hb_l, hc, x, mod, w_o, pg, pb, w_rt, b_r)


N_Y_ROWS = T_ALL + 2 * TM_MOE


def _moe_kernel(tok_ref, tstart_ref, nval_ref, tlo_ref, thi_ref, nused_ref,
                u_hbm, wgl_ref, wul_ref, wdl_ref, wgh_ref, wuh_ref, wdh_ref, y_hbm,
                xbuf0, xbuf1, ybuf0, ybuf1, gsem, ssem):
    del tlo_ref, thi_ref
    i = pl.program_id(0)
    n_used = nused_ref[0]
    tile_rows = TM_MOE * N_FCHUNK

    def token_tile(ref, tok, rows=N_FCHUNK):
        return ref.at[pl.ds(pl.multiple_of(tok * rows, rows), rows), :]

    def gather_rows(t, xb, sem):
        base = tstart_ref[t]
        for r in range(TM_MOE):
            pltpu.make_async_copy(token_tile(u_hbm, tok_ref[base + r], N_IN_ROWS), token_tile(xb, r, N_IN_ROWS),
                                  sem).start()

    def scatter_rows(t, n_valid, yb, sem, spare):
        base = tstart_ref[t]
        for r in range(TM_MOE):
            dst = jnp.where(r < n_valid, tok_ref[base + r], spare + r)
            pltpu.make_async_copy(token_tile(yb, r), token_tile(y_hbm, dst), sem).start()

    def load_tile(xb):
        x = jnp.concatenate([xb[pl.ds(c, TM_MOE, stride=N_IN_ROWS), :] for c in range(N_FCHUNK)], axis=1)
        w = xb[pl.ds(N_FCHUNK, TM_MOE, stride=N_IN_ROWS), :]
        return x.astype(BF16), w[:, 0:1], w[:, 1:2]

    def wait_gather(xb, sem):
        pltpu.make_async_copy(u_hbm.at[pl.ds(0, TM_MOE * N_IN_ROWS), :], xb, sem).wait()

    def wait_scatter(yb, sem):
        pltpu.make_async_copy(yb, y_hbm.at[pl.ds(0, tile_rows), :], sem).wait()

    def step(s, xb, xb_next, yb, yb_prev):
        @pl.when(i > 0)
        def _():
            wait_scatter(yb, ssem.at[s])

        wait_gather(xb, gsem.at[s])

        @pl.when(n_used > 0)
        def _():
            gather_rows(jnp.minimum(i + 1, n_used - 1), xb_next, gsem.at[1 - s])
            prev = jnp.maximum(i - 1, 0)
            scatter_rows(prev, jnp.where(i > 0, nval_ref[prev], 0), yb_prev, ssem.at[1 - s],
                         T_ALL + (1 - s) * TM_MOE)

        x, w_lo, w_hi = load_tile(xb)

        def ffn(wg_ref, wu_ref, wd_ref, w):
            act = _silu(_dot(x, wg_ref[0, 0].astype(BF16))) * _dot(x, wu_ref[0, 0].astype(BF16)) * w
            return _dot(act.astype(BF16), wd_ref[0, 0].astype(BF16))

        _store_token_tiles(yb, ffn(wgl_ref, wul_ref, wdl_ref, w_lo) + ffn(wgh_ref, wuh_ref, wdh_ref, w_hi))

        @pl.when(i == n_used - 1)
        def _():
            scatter_rows(i, nval_ref[i], yb, ssem.at[s], T_ALL + s * TM_MOE)
            wait_scatter(yb, ssem.at[s])
            wait_scatter(yb_prev, ssem.at[1 - s])
            wait_gather(xb_next, gsem.at[1 - s])

    @pl.when(i == 0)
    def _():
        ybuf1[...] = jnp.zeros(ybuf1.shape, F32)
        fill = pltpu.make_async_copy(ybuf1, y_hbm.at[pl.ds(T_ALL * N_FCHUNK, tile_rows), :], ssem.at[0])
        fill.start()
        fill.wait()
        gather_rows(0, xbuf0, gsem.at[0])

    @pl.when((i < n_used) & (i % 2 == 0))
    def _():
        step(0, xbuf0, xbuf1, ybuf0, ybuf1)

    @pl.when((i < n_used) & (i % 2 == 1))
    def _():
        step(1, xbuf1, xbuf0, ybuf1, ybuf0)


def _moe(u2t, tok_sorted, tile_start, n_valid, tile_lo, tile_hi, n_used, w_gate, w_up, w_down, layer):
    lo = lambda shape: pl.BlockSpec(shape, lambda i, tok, ts, nv, tlo, thi, nused: (layer, tlo[i], 0, 0))
    hi = lambda shape: pl.BlockSpec(shape, lambda i, tok, ts, nv, tlo, thi, nused: (layer, thi[i], 0, 0))
    up_shape = (1, 1, D_MODEL, D_FF_EXPERT)
    down_shape = (1, 1, D_FF_EXPERT, D_MODEL)
    in_buf = pltpu.VMEM((TM_MOE * N_IN_ROWS, LANES), F32)
    out_buf = pltpu.VMEM((TM_MOE * N_FCHUNK, LANES), F32)
    grid_spec = pltpu.PrefetchScalarGridSpec(
        num_scalar_prefetch=6,
        grid=(N_MOE_TILES,),
        in_specs=[pl.BlockSpec(memory_space=pl.ANY),
                  lo(up_shape), lo(up_shape), lo(down_shape),
                  hi(up_shape), hi(up_shape), hi(down_shape)],
        out_specs=pl.BlockSpec(memory_space=pl.ANY),
        scratch_shapes=[in_buf, in_buf, out_buf, out_buf,
                        pltpu.SemaphoreType.DMA((2,)),
                        pltpu.SemaphoreType.DMA((2,))],
    )
    return pl.pallas_call(
        _moe_kernel,
        grid_spec=grid_spec,
        out_shape=jax.ShapeDtypeStruct((N_Y_ROWS * N_FCHUNK, LANES), F32),
        compiler_params=pltpu.CompilerParams(dimension_semantics=("arbitrary",), vmem_limit_bytes=BIG_VMEM_LIMIT,
                                             has_side_effects=True),
        name="moe",
    )(tok_sorted, tile_start, n_valid, tile_lo, tile_hi, n_used, u2t, w_gate, w_up, w_down, w_gate, w_up, w_down)


def _route_tables(cls):
    _, tok_sorted = lax.sort((cls, jnp.arange(T_ALL, dtype=jnp.int32)), num_keys=1)
    cids = jnp.arange(N_CLASSES, dtype=jnp.int32)
    counts = jnp.sum((cls[:, None] == cids[None, :]).astype(jnp.int32), axis=0)
    offs = jnp.cumsum(counts) - counts
    padded = (counts + TM_MOE - 1) // TM_MOE * TM_MOE
    ends = jnp.cumsum(padded)
    offs_p = ends - padded
    n_used = ends[-1] // TM_MOE
    tile_start = jnp.arange(N_MOE_TILES, dtype=jnp.int32) * TM_MOE
    tile_cls = jnp.sum((ends[None, :] <= jnp.minimum(tile_start, ends[-1] - TM_MOE)[:, None]).astype(jnp.int32),
                       axis=1)
    onehot = (tile_cls[:, None] == cids[None, :]).astype(jnp.int32)
    pick = lambda table: jnp.sum(onehot * table[None, :], axis=1)
    k = tile_start - pick(offs_p)
    n_valid = jnp.where(tile_start < ends[-1], jnp.clip(pick(counts) - k, 0, TM_MOE), 0)
    tile_first = jnp.clip(pick(offs) + k, 0, T_ALL - 1)
    tile_lo = pick(cids // len(_PAIRS) * EXPERTS_PER_GROUP + jnp.array([p[0] for p in _PAIRS] * N_EGROUPS, jnp.int32))
    tile_hi = pick(cids // len(_PAIRS) * EXPERTS_PER_GROUP + jnp.array([p[1] for p in _PAIRS] * N_EGROUPS, jnp.int32))
    tok_sorted = jnp.concatenate([tok_sorted, jnp.zeros((TM_MOE,), jnp.int32)])
    i32 = lambda v: v.astype(jnp.int32)
    return tok_sorted, i32(tile_first), i32(n_valid), i32(tile_lo), i32(tile_hi), i32(n_used).reshape(1)


def _final_kernel(x1_ref, y_ref, mod_ref, pg_ref, pb_ref, *o_refs, tm):
    i = pl.program_id(0)
    r = _mod_row(i * tm)
    gate2 = mod_ref[pl.ds(r, 1), 5 * D_MODEL:6 * D_MODEL]
    out = _ln(ALPHA * x1_ref[...] + gate2 * _load_token_tiles(y_ref, tm)) * pg_ref[...] + pb_ref[...]
    if len(o_refs) == 1:
        o_refs[0][...] = out
    else:
        @pl.when(i < N_CTX_TOK // tm)
        def _():
            o_refs[0][...] = out

        @pl.when(i >= N_CTX_TOK // tm)
        def _():
            o_refs[1][...] = out


def _final(x1, y, mod, pg, pb, split):
    tm = TM_PROJ
    n_ctx_blk = N_CTX_TOK // tm
    const = lambda shape: pl.BlockSpec(shape, lambda i: (0,) * len(shape))
    rows = pl.BlockSpec((tm, D_MODEL), lambda i: (i, 0))
    if split:
        out_specs = [pl.BlockSpec((tm, D_MODEL), lambda i: (jnp.minimum(i, n_ctx_blk - 1), 0)),
                     pl.BlockSpec((tm, D_MODEL), lambda i: (jnp.maximum(i - n_ctx_blk, 0), 0))]
        out_shape = [jax.ShapeDtypeStruct((N_CTX_TOK, D_MODEL), F32), jax.ShapeDtypeStruct((N_LAT_TOK, D_MODEL), F32)]
    else:
        out_specs = [rows]
        out_shape = [jax.ShapeDtypeStruct((T_ALL, D_MODEL), F32)]
    return pl.pallas_call(
        functools.partial(_final_kernel, tm=tm),
        grid=(T_ALL // tm,),
        in_specs=[rows, pl.BlockSpec((tm * N_FCHUNK, LANES), lambda i: (i, 0)),
                  const((N_MOD_ROWS, 6 * D_MODEL)), const((1, D_MODEL)), const((1, D_MODEL))],
        out_specs=out_specs,
        out_shape=out_shape,
        compiler_params=_cparams(1),
        name="final",
    )(x1, y, mod, pg, pb)


def _grid_pos(n_tok):
    rows = n_tok // GRID_W
    r, col = jnp.meshgrid(jnp.arange(rows, dtype=F32), jnp.arange(GRID_W, dtype=F32), indexing='ij')
    quarter = D_MODEL // 4
    omega = 1.0 / (10000.0 ** (jnp.arange(quarter, dtype=F32) / quarter))

    def emb(p):
        ang = p.reshape(-1)[:, None] * omega[None, :]
        return jnp.concatenate([jnp.sin(ang), jnp.cos(ang)], axis=-1)

    return jnp.concatenate([emb(r), emb(col)], axis=-1)


def _pad_lanes(v, start):
    v = v.reshape(1, -1).astype(F32)
    return jnp.pad(v, ((0, 0), (start, LANES - start - v.shape[1])))


def kernel(x_prompt, x_sample, state_mlstm_C, state_mlstm_n, state_mlstm_m, state_ssd, c, c_ctx, w_in, w_o, mlstm_b_i, mlstm_b_f, mlstm_norm_g, ssd_conv_w, ssd_conv_b, ssd_dt_bias, ssd_A_log, ssd_D, ssd_norm_g, conv_dw_w, conv_dw_b, conv_ln_g, conv_ln_b, w_ada, b_ada, post1_g, post1_b, post2_g, post2_b, w_router, b_router, w_e_gate, w_e_up, w_e_down):
    cvec = jnp.concatenate([c_ctx[None, :], c, jnp.zeros((N_MOD_ROWS - 1 - DEC_BATCH, D_MODEL), F32)], axis=0)
    mod_all = _ada(cvec, w_ada, b_ada)
    x = _embed(x_prompt.reshape(N_CTX_TOK, D_MODEL), x_sample.reshape(N_LAT_TOK, D_MODEL), _grid_pos(DEC_SEQ))
    w_rt = w_router.T
    b_r = b_router.reshape(N_EXPERTS, 1)

    a_end = 4 * W_A + N_DIR * 2 * H_A
    b_end = a_end + W_B + W_XBC + N_DIR * H_B
    init = (state_mlstm_C,
            state_mlstm_n.reshape(DEC_BATCH, DEPTH, N_DIR * H_A, DH_A),
            jnp.broadcast_to(state_mlstm_m.reshape(DEC_BATCH, DEPTH, N_DIR * H_A, 1),
                             (DEC_BATCH, DEPTH, N_DIR * H_A, LANES)))
    st_c = st_n = st_m = st_h = None
    for l in range(DEPTH):
        w = w_in[l]
        xbc0 = a_end + W_B
        w_main = jnp.concatenate([w[:, W_A:2 * W_A], w[:, b_end:], w[:, xbc0 + W_B:xbc0 + W_B + G_B * N_B]],
                                 axis=1).astype(BF16)
        w_small = jnp.concatenate([w[:, 4 * W_A:a_end], w[:, b_end - N_DIR * H_B:b_end],
                                   jnp.zeros((D_MODEL, LANES - _DT_COL0 - N_DIR * H_B), F32)], axis=1)
        w_t = jnp.concatenate([w[:, 0:W_A], w[:, 2 * W_A:4 * W_A], w[:, a_end:xbc0 + W_B],
                               w[:, xbc0 + W_B + G_B * N_B:xbc0 + W_XBC], w_small],
                              axis=1).T.astype(BF16)
        w_small = w_small.astype(BF16)
        gate_bias = (_pad_lanes(jnp.stack([mlstm_b_i[l], mlstm_b_f[l]], axis=1), 0)
                     + _pad_lanes(ssd_dt_bias[l], _DT_COL0))
        alog_row = _pad_lanes(ssd_A_log[l], _DT_COL0)
        mod = mod_all[l]

        proj, gates, proj_t = _inproj(x, mod, w_main, w_small, w_t)

        m_norm = jnp.broadcast_to(mlstm_norm_g[l].reshape(W_A, 1), (W_A, LANES))
        mlstm_args = (proj, gates, proj_t, gate_bias, gate_bias.reshape(LANES, 1), m_norm)
        ha_c, st_c, st_n, st_m = _mlstm(*mlstm_args, SEQ, BATCH, 0, l, prev=(st_c, st_n, st_m))
        (ha_l,) = _mlstm(*mlstm_args, DEC_SEQ, DEC_BATCH, N_CTX_TOK, l, init=init)

        nb = G_B * N_B
        rep = lambda v: jnp.broadcast_to(v[..., None], v.shape + (LANES,))
        cw, cb = ssd_conv_w[l], ssd_conv_b[l]
        conv = (rep(cw[:, 0:W_B]), rep(cb[0:W_B]), rep(cw[:, W_B + nb:W_XBC]), rep(cb[W_B + nb:W_XBC]),
                jnp.pad(cw[:, W_B:W_B + nb], ((0, 8 - SSM_CONV), (0, 0))), cb[W_B:W_B + nb].reshape(1, nb))
        ssd_args = (proj, gates, proj_t, gate_bias, gate_bias.reshape(LANES, 1), alog_row, alog_row.reshape(LANES, 1),
                    conv, rep(jnp.repeat(ssd_D[l], P_B)), rep(ssd_norm_g[l]))
        hb_c, st_h = _ssd(*ssd_args, SEQ, BATCH, 0, l, prev=st_h)
        (hb_l,) = _ssd(*ssd_args, DEC_SEQ, DEC_BATCH, N_CTX_TOK, l, init=state_ssd)

        hc = _conf(proj, jnp.pad(conv_dw_w[l], ((0, 32 - CONV_W), (0, 0))), conv_dw_b[l].reshape(1, W_C),
                   conv_ln_g[l].reshape(1, W_C), conv_ln_b[l].reshape(1, W_C))

        x1, u2t, route = _outproj(ha_c, ha_l, hb_c, hb_l, hc, x, mod, w_o[l].astype(BF16), post1_g[l].reshape(1, D_MODEL),
                                  post1_b[l].reshape(1, D_MODEL), w_rt, b_r)
        y = _moe(u2t, *_route_tables(route[0]), w_e_gate, w_e_up, w_e_down, l)
        outs = _final(x1, y, mod, post2_g[l].reshape(1, D_MODEL), post2_b[l].reshape(1, D_MODEL), l == DEPTH - 1)
        x = outs[0]

    y_prompt = outs[0].reshape(BATCH, SEQ, D_MODEL)
    y_sample = outs[1].reshape(DEC_BATCH, DEC_SEQ, D_MODEL)
    return (y_prompt, y_sample, st_c, st_n.reshape(BATCH, DEPTH, N_DIR, H_A, DH_A),
            st_m[:, :, :, 0].reshape(BATCH, DEPTH, N_DIR, H_A), st_h)
```

```python
import functools

import jax
import jax.numpy as jnp
from jax import lax
from jax.experimental import pallas as pl
from jax.experimental.pallas import tpu as pltpu

D_MODEL = 1024
BATCH = 32
SEQ = 256
DEPTH = 2
DEC_BATCH = 2
DEC_SEQ = 1024
GRID_W = 64
N_DIR = 2
CHUNK = 128
H_A = 4
DH_A = 128
W_A = H_A * DH_A
H_B = 8
P_B = 64
W_B = H_B * P_B
G_B = 2
N_B = 128
W_XBC = W_B + 2 * G_B * N_B
SSM_CONV = 3
W_C = 512
CONV_W = 31
D_MIX = W_A + W_B + W_C
N_EXPERTS = 16
N_EGROUPS = 4
EXPERTS_PER_GROUP = N_EXPERTS // N_EGROUPS
D_FF_EXPERT = 512
ALPHA = (2 * DEPTH) ** 0.25
EPS = 1e-5
F32 = jnp.float32
BF16 = jnp.bfloat16

N_CTX_TOK = BATCH * SEQ
N_LAT_TOK = DEC_BATCH * DEC_SEQ
T_ALL = N_CTX_TOK + N_LAT_TOK
N_MOD_ROWS = 8
D_MAIN = W_A + 2 * W_C + G_B * N_B
D_T = 3 * W_A + 2 * W_B + G_B * N_B + 128
LANES = 128
HALO = 16
CONV_ROWS = 256
TM_PROJ = 512
TM_MOE = 256
VMEM_LIMIT = 48 * 1024 * 1024
BIG_VMEM_LIMIT = 56 * 1024 * 1024

_NT = (((1,), (1,)), ((), ()))
_TN = (((0,), (0,)), ((), ()))


def _ln(x):
    mu = jnp.mean(x, axis=-1, keepdims=True)
    xc = x - mu
    var = jnp.mean(xc * xc, axis=-1, keepdims=True)
    return xc * lax.rsqrt(var + EPS)


def _sigmoid(x):
    return 1.0 / (1.0 + jnp.exp(-x))


def _silu(x):
    return x * _sigmoid(x)


def _softplus(x):
    return jnp.maximum(x, 0.0) + jnp.log1p(jnp.exp(-jnp.abs(x)))


def _dot(a, b):
    return jnp.dot(a, b, preferred_element_type=F32)


def _dotg(a, b, dims):
    return lax.dot_general(a, b, dims, preferred_element_type=F32)


def _tri_cumsum(tri, x):
    hi = x.astype(BF16)
    r1 = x - hi.astype(F32)
    mid = r1.astype(BF16)
    lo = (r1 - mid.astype(F32)).astype(BF16)
    return _dot(tri, hi) + _dot(tri, mid) + _dot(tri, lo)


def _tri_mask(d):
    row = lax.broadcasted_iota(jnp.int32, (CHUNK, CHUNK), 0)
    col = lax.broadcasted_iota(jnp.int32, (CHUNK, CHUNK), 1)
    return (row >= col) if d == 0 else (row <= col)


def _mod_row(row_start):
    return jnp.where(row_start < N_CTX_TOK, 0, 1 + (row_start - N_CTX_TOK) // DEC_SEQ)


def _cparams(n_axes, vmem_limit=VMEM_LIMIT):
    return pltpu.CompilerParams(dimension_semantics=("arbitrary",) * n_axes, vmem_limit_bytes=vmem_limit)


def _ada_kernel(c_ref, w_ref, b_ref, o_ref):
    o_ref[0] = _dot(_silu(c_ref[...]), w_ref[0]) + b_ref[0]


def _ada(cvec, w_ada, b_ada):
    tn = 1536
    return pl.pallas_call(
        _ada_kernel,
        grid=(DEPTH, 6 * D_MODEL // tn),
        in_specs=[
            pl.BlockSpec((N_MOD_ROWS, D_MODEL), lambda l, j: (0, 0)),
            pl.BlockSpec((1, D_MODEL, tn), lambda l, j: (l, 0, j)),
            pl.BlockSpec((1, 1, tn), lambda l, j: (l, 0, j)),
        ],
        out_specs=pl.BlockSpec((1, N_MOD_ROWS, tn), lambda l, j: (l, 0, j)),
        out_shape=jax.ShapeDtypeStruct((DEPTH, N_MOD_ROWS, 6 * D_MODEL), F32),
        compiler_params=_cparams(2),
        name="ada",
    )(cvec, w_ada, b_ada.reshape(DEPTH, 1, 6 * D_MODEL))


def _x_specs(x_src, tm):
    if len(x_src) == 1:
        return [pl.BlockSpec((tm, D_MODEL), lambda i: (i, 0))]
    n_ctx_blk = N_CTX_TOK // tm
    per_seq = DEC_SEQ // tm
    return [pl.BlockSpec((tm, D_MODEL), lambda i: (jnp.minimum(i, n_ctx_blk - 1), 0)),
            pl.BlockSpec((tm, D_MODEL), lambda i: (jnp.maximum(i - n_ctx_blk, 0), 0)),
            pl.BlockSpec((tm, D_MODEL), lambda i: (jnp.where(i < n_ctx_blk, 0, (i - n_ctx_blk) % per_seq), 0))]


def _read_x(x_refs, i, tm):
    if len(x_refs) == 1:
        return x_refs[0][...]
    xp_ref, xs_ref, pos_ref = x_refs
    return jnp.where(i < N_CTX_TOK // tm, xp_ref[...], xs_ref[...] + pos_ref[...])


def _inproj_kernel(*refs, tm, n_x):
    x_refs = refs[:n_x]
    mod_ref, wm_ref, wg_ref, wt_ref, om_ref, og_ref, ot_ref = refs[n_x:]
    i = pl.program_id(0)
    r = _mod_row(i * tm)
    shift = mod_ref[pl.ds(r, 1), 0:D_MODEL]
    scale = mod_ref[pl.ds(r, 1), D_MODEL:2 * D_MODEL]
    u = (_ln(_read_x(x_refs, i, tm)) * (1.0 + scale) + shift).astype(BF16)
    om_ref[...] = _dot(u, wm_ref[...])
    og_ref[...] = _dot(u, wg_ref[...])
    ot_ref[...] = _dotg(wt_ref[...], u, _NT)


def _inproj(x_src, mod, w_main, w_small, w_t):
    tm = TM_PROJ
    const = lambda shape: pl.BlockSpec(shape, lambda i: (0, 0))
    return pl.pallas_call(
        functools.partial(_inproj_kernel, tm=tm, n_x=len(x_src)),
        grid=(T_ALL // tm,),
        in_specs=_x_specs(x_src, tm) + [
            const((N_MOD_ROWS, 6 * D_MODEL)), const((D_MODEL, D_MAIN)), const((D_MODEL, LANES)),
            const((D_T, D_MODEL)),
        ],
        out_specs=[
            pl.BlockSpec((tm, D_MAIN), lambda i: (i, 0)),
            pl.BlockSpec((tm, LANES), lambda i: (i, 0)),
            pl.BlockSpec((D_T, tm), lambda i: (0, i)),
        ],
        out_shape=[
            jax.ShapeDtypeStruct((T_ALL, D_MAIN), F32),
            jax.ShapeDtypeStruct((T_ALL, LANES), F32),
            jax.ShapeDtypeStruct((D_T, T_ALL), F32),
        ],
        compiler_params=_cparams(1, BIG_VMEM_LIMIT),
        name="inproj",
    )(*x_src, mod, w_main, w_small, w_t)


def _scan_max(x, lane, d):
    s = 1
    while s < CHUNK:
        if d == 0:
            x = jnp.where(lane >= s, jnp.maximum(x, pltpu.roll(x, s, axis=1)), x)
        else:
            x = jnp.where(lane < CHUNK - s, jnp.maximum(x, pltpu.roll(x, CHUNK - s, axis=1)), x)
        s *= 2
    return x


def _split_dot(x, tri):
    hi = x.astype(BF16)
    r1 = x - hi.astype(F32)
    mid = r1.astype(BF16)
    lo = (r1 - mid.astype(F32)).astype(BF16)
    return _dot(hi, tri) + _dot(mid, tri) + _dot(lo, tri)


def _mlstm_kernel(*refs, seq, has_init, emit_state, layer):
    it = iter(refs)
    k_ref, qt_ref, vt_ref, ot_ref, g_ref, gt_ref, gb_ref, gbt_ref, ng_ref = (next(it) for _ in range(9))
    if has_init:
        c0_ref, n0_ref, m0_ref = (next(it) for _ in range(3))
    if emit_state and layer > 0:
        prev_refs = [next(it) for _ in range(3)]
    out_ref = next(it)
    if emit_state:
        co_ref, no_ref, mo_ref = (next(it) for _ in range(3))
    tg_scr, tgt_scr, b_scr, beta_scr, pm_scr, cumc_scr, hd_scr, c_scr, n_scr, m_scr = (next(it) for _ in range(10))
    n_chunks = seq // CHUNK
    n_gate = N_DIR * 2 * H_A
    if emit_state and layer > 0:
        for prev_ref, st_ref in zip(prev_refs, (co_ref, no_ref, mo_ref)):
            st_ref[0, 0:layer] = prev_ref[0]

    y = g_ref[...] + gb_ref[...]
    lane = lax.broadcasted_iota(jnp.int32, y.shape, 1)
    tg_scr[...] = jnp.where((lane < n_gate) & ((lane & H_A) != 0), -_softplus(-y), y)
    yt = gt_ref[0:n_gate, :] + gbt_ref[0:n_gate, :]
    rowt = lax.broadcasted_iota(jnp.int32, yt.shape, 0)
    tgt_scr[...] = jnp.where((rowt & H_A) != 0, -_softplus(-yt), yt)

    lane8 = lax.broadcasted_iota(jnp.int32, (8, CHUNK), 1)
    row8 = lax.broadcasted_iota(jnp.int32, (8, LANES), 0)
    masks_t = [_tri_mask(1 - d) for d in range(N_DIR)]

    for d in range(N_DIR):
        tri_t = masks_t[d].astype(BF16)
        tri_c = _tri_mask(d).astype(BF16)
        rows = slice(d * 8, (d + 1) * 8)
        for c in range(n_chunks):
            cs = slice(c * CHUNK, (c + 1) * CHUNK)
            gr8 = tgt_scr[rows, cs]
            b8 = pltpu.roll(_split_dot(gr8, tri_t), H_A, axis=0)
            beta8 = gr8 - b8
            b_scr[rows, cs] = b8
            beta_scr[rows, cs] = beta8
            pm_scr[rows, cs] = _scan_max(beta8, lane8, d)
            cumc_scr[d, cs, :] = _tri_cumsum(tri_c, tg_scr[cs, :])

    pos = (CHUNK - 1, 0)

    for d in range(N_DIR):
        rows = slice(d * 8, (d + 1) * 8)
        if has_init:
            for h in range(H_A):
                c_scr[d, h] = c0_ref[0, 0, d, h].T
            n0 = n0_ref[0, 0]
            m0 = m0_ref[0, 0]
            if d == 1:
                n0 = pltpu.roll(n0, H_A, axis=0)
                m0 = pltpu.roll(m0, H_A, axis=0)
            n_scr[rows, :] = jnp.where(row8 < H_A, n0, 0.0)
            m_scr[rows, :] = jnp.where(row8 < H_A, m0, 0.0)
        else:
            for h in range(H_A):
                c_scr[d, h] = jnp.zeros((DH_A, DH_A), F32)
            n_scr[rows, :] = jnp.zeros((8, DH_A), F32)
            m_scr[rows, :] = jnp.zeros((8, LANES), F32)

    def chunk_prep(d, c):
        rows = slice(d * 8, (d + 1) * 8)
        cs = slice(c * CHUNK, (c + 1) * CHUNK)
        b8 = b_scr[rows, cs]
        beta8 = beta_scr[rows, cs]
        m8 = m_scr[rows, :]
        n8 = n_scr[rows, :]
        mu8 = jnp.maximum(m8, pm_scr[rows, cs])
        mu_last = mu8[:, pos[d]:pos[d] + 1]
        w_tok8 = jnp.exp(beta8 - mu_last)
        return dict(cs=cs, rows=rows, n8=n8, mu8=mu8, w_int8=jnp.exp(m8 - mu8), emt8=jnp.exp(-(b8 + mu8)),
                    w_tok8=w_tok8, w_prev8=jnp.exp(m8 - mu_last), m_new8=b8[:, pos[d]:pos[d] + 1] + mu_last,
                    n16b=jnp.concatenate([n8, jnp.zeros_like(n8)], axis=0).astype(BF16),
                    w_tok16=jnp.concatenate([w_tok8, jnp.zeros_like(w_tok8)], axis=0),
                    gc=tg_scr[cs, :], cum_c=cumc_scr[d, cs, :])

    def head_step(d, h, p):
        cs = p["cs"]
        hs = slice(h * DH_A, (h + 1) * DH_A)
        col_i = d * 2 * H_A + h
        col_f = col_i + H_A
        beta_c = p["gc"][:, col_i:col_i + 1] - p["cum_c"][:, col_f:col_f + 1]
        kb = k_ref[cs, hs].astype(BF16)
        qtb = (qt_ref[hs, cs] * (DH_A ** -0.5)).astype(BF16)
        vt = vt_ref[hs, cs]
        ct = c_scr[d, h]
        w_int_r = p["w_int8"][h:h + 1, :]

        res = _dot(jnp.concatenate([kb, ct.astype(BF16), p["n16b"]], axis=0), qtb)
        st = res[0:CHUNK] * jnp.exp(jnp.where(masks_t[d], beta_c - p["mu8"][h:h + 1, :], -jnp.inf))
        num = w_int_r * res[CHUNK:CHUNK + DH_A] + _dot(vt.astype(BF16), st.astype(BF16))
        den_r = (w_int_r * res[CHUNK + DH_A + h:CHUNK + DH_A + h + 1]
                 + jnp.sum(st, axis=0, keepdims=True))
        hd_scr[d, hs, cs] = num * (1.0 / jnp.maximum(jnp.abs(den_r), p["emt8"][h:h + 1, :]))

        upd = _dot(jnp.concatenate([vt * p["w_tok8"][h:h + 1, :], p["w_tok16"]], axis=0).astype(BF16), kb)
        c_scr[d, h] = p["w_prev8"][h:h + 1, :] * ct + upd[0:DH_A]
        return upd[DH_A:DH_A + 8]

    for ci in range(n_chunks):
        preps = [chunk_prep(0, ci), chunk_prep(1, n_chunks - 1 - ci)]
        nk_acc = [jnp.zeros((8, DH_A), F32) for _ in range(N_DIR)]
        for h in range(H_A):
            for d in range(N_DIR):
                nk_acc[d] = jnp.where(row8 == h, head_step(d, h, preps[d]), nk_acc[d])
        for d in range(N_DIR):
            p = preps[d]
            n_scr[p["rows"], :] = p["w_prev8"] * p["n8"] + nk_acc[d]
            m_scr[p["rows"], :] = jnp.where(row8 < H_A, jnp.broadcast_to(p["m_new8"], (8, LANES)), 0.0)

    if emit_state:
        for d in range(N_DIR):
            for h in range(H_A):
                co_ref[0, layer, d, h] = c_scr[d, h].T
            no_ref[0, layer, d * H_A:(d + 1) * H_A, :] = n_scr[d * 8:d * 8 + H_A, :]
            mo_ref[0, layer, d * H_A:(d + 1) * H_A, :] = m_scr[d * 8:d * 8 + H_A, :]

    for c in range(n_chunks):
        cs = slice(c * CHUNK, (c + 1) * CHUNK)
        for h in range(H_A):
            hs = slice(h * DH_A, (h + 1) * DH_A)
            hsum = hd_scr[0, hs, cs] + hd_scr[1, hs, cs]
            mean = jnp.sum(hsum, axis=0, keepdims=True) * (1.0 / DH_A)
            xc = hsum - mean
            var = jnp.sum(xc * xc, axis=0, keepdims=True) * (1.0 / DH_A)
            hn = xc * lax.rsqrt(var + EPS) * ng_ref[hs, :]
            out_ref[hs, cs] = (hn * _sigmoid(ot_ref[hs, cs])).astype(out_ref.dtype)


def _mlstm(proj, gates, proj_t, gate_bias, gate_bias_t, norm_g_rep, seq, n_seq, row_off, layer, init=None, prev=None):
    blk0 = row_off // seq
    has_init = init is not None
    emit_state = not has_init
    feat = lambda r: pl.BlockSpec((W_A, seq), lambda s: (r, s + blk0))
    const = lambda shape: pl.BlockSpec(shape, lambda s: (0,) * len(shape))
    st_shapes = [(N_DIR, H_A, DH_A, DH_A), (N_DIR * H_A, DH_A), (N_DIR * H_A, LANES)]

    def st_specs(n_layers, first):
        return [pl.BlockSpec((1, n_layers) + shp, lambda s, nd=len(shp): (s, first) + (0,) * nd) for shp in st_shapes]

    in_specs = [pl.BlockSpec((seq, W_A), lambda s: (s + blk0, 0)), feat(0), feat(1), feat(2),
                pl.BlockSpec((seq, LANES), lambda s: (s + blk0, 0)),
                pl.BlockSpec((LANES, seq), lambda s: (D_T // LANES - 1, s + blk0)),
                const((1, LANES)), const((LANES, 1)), const((W_A, LANES))]
    args = [proj, proj_t, proj_t, proj_t, gates, proj_t, gate_bias, gate_bias_t, norm_g_rep]
    if has_init:
        in_specs += st_specs(1, layer)
        args += list(init)
    if emit_state and layer > 0:
        in_specs += st_specs(layer, 0)
        args += list(prev)
    out_specs = [pl.BlockSpec((W_A, seq), lambda s: (0, s))]
    out_shape = [jax.ShapeDtypeStruct((W_A, n_seq * seq), BF16)]
    if emit_state:
        out_specs += st_specs(layer + 1, 0)
        out_shape += [jax.ShapeDtypeStruct((n_seq, layer + 1) + shp, F32) for shp in st_shapes]
    return pl.pallas_call(
        functools.partial(_mlstm_kernel, seq=seq, has_init=has_init, emit_state=emit_state, layer=layer),
        grid=(n_seq,),
        in_specs=in_specs,
        out_specs=out_specs,
        out_shape=out_shape,
        scratch_shapes=[pltpu.VMEM((seq, LANES), F32),
                        pltpu.VMEM((N_DIR * 8, seq), F32),
                        pltpu.VMEM((N_DIR * 8, seq), F32),
                        pltpu.VMEM((N_DIR * 8, seq), F32),
                        pltpu.VMEM((N_DIR * 8, seq), F32),
                        pltpu.VMEM((N_DIR, seq, LANES), F32),
                        pltpu.VMEM((N_DIR, W_A, seq), F32),
                        pltpu.VMEM((N_DIR, H_A, DH_A, DH_A), F32),
                        pltpu.VMEM((N_DIR * 8, DH_A), F32),
                        pltpu.VMEM((N_DIR * 8, LANES), F32)],
        compiler_params=_cparams(1),
        name="mlstm_lat" if has_init else "mlstm_ctx",
    )(*args)


_DT_COL0 = N_DIR * 2 * H_A
_HG = H_B // G_B
_WG = _HG * P_B


def _conv3_lanes(ref, w_ref, b_ref, out_scr, seq):
    lane = lax.broadcasted_iota(jnp.int32, (CHUNK, seq), 1)
    for r in range(ref.shape[0] // CHUNK):
        rs = slice(r * CHUNK, (r + 1) * CHUNK)
        cur = ref[rs, :]
        prev = jnp.where(lane == 0, 0.0, pltpu.roll(cur, 1, axis=1))
        nxt = jnp.where(lane == seq - 1, 0.0, pltpu.roll(cur, seq - 1, axis=1))
        for c in range(seq // CHUNK):
            cs = slice(c * CHUNK, (c + 1) * CHUNK)
            acc = (w_ref[0, rs, :] * prev[:, cs] + w_ref[1, rs, :] * cur[:, cs] + w_ref[2, rs, :] * nxt[:, cs]
                   + b_ref[rs, :])
            out_scr[rs, cs] = _silu(acc).astype(out_scr.dtype)


def _ssd_kernel(*refs, seq, has_init, emit_state, layer):
    it = iter(refs)
    (zt_ref, xt_ref, ct_ref, b_ref, g_ref, gt_ref, gb_ref, gbt_ref, alog_ref, alogt_ref,
     cwx_ref, cbx_ref, cwc_ref, cbc_ref, cwb_ref, cbb_ref, dsk_ref, ng_ref) = (next(it) for _ in range(18))
    if has_init:
        h0_ref = next(it)
    if emit_state and layer > 0:
        prev_ref = next(it)
    out_ref = next(it)
    if emit_state:
        ho_ref = next(it)
    pad_scr, xs_scr, cs_scr, bs_scr, dtr_scr, br_scr, cumc_scr, yt_scr, h_scr = (next(it) for _ in range(9))
    n_chunks = seq // CHUNK
    pad = 8
    if emit_state and layer > 0:
        ho_ref[0, 0:layer] = prev_ref[0]

    _conv3_lanes(xt_ref, cwx_ref, cbx_ref, xs_scr, seq)
    _conv3_lanes(ct_ref, cwc_ref, cbc_ref, cs_scr, seq)
    nb = G_B * N_B
    pad_scr[0:pad, :] = jnp.zeros((pad, nb), F32)
    pad_scr[pad + seq:2 * pad + seq, :] = jnp.zeros((pad, nb), F32)
    pad_scr[pad:pad + seq, :] = b_ref[...]
    for c in range(n_chunks):
        acc = cbb_ref[...]
        for k in range(SSM_CONV):
            r = c * CHUNK + pad - SSM_CONV // 2 + k
            acc = acc + cwb_ref[k:k + 1, :] * pad_scr[r:r + CHUNK, :]
        bs_scr[c * CHUNK:(c + 1) * CHUNK, :] = _silu(acc).astype(bs_scr.dtype)

    lac = _softplus(g_ref[...] + gb_ref[...]) * (-jnp.exp(alog_ref[...]))
    r0 = _DT_COL0
    dtr = _softplus(gt_ref[r0:r0 + N_DIR * H_B, :] + gbt_ref[r0:r0 + N_DIR * H_B, :])
    dtr_scr[...] = dtr
    lar = dtr * (-jnp.exp(alogt_ref[r0:r0 + N_DIR * H_B, :]))
    masks_t = [_tri_mask(1 - d) for d in range(N_DIR)]
    for d in range(N_DIR):
        tri_t = masks_t[d].astype(BF16)
        tri_c = _tri_mask(d).astype(BF16)
        rows = slice(d * H_B, (d + 1) * H_B)
        for c in range(n_chunks):
            cs = slice(c * CHUNK, (c + 1) * CHUNK)
            br_scr[rows, cs] = _split_dot(lar[rows, cs], tri_t)
            cumc_scr[d, cs, :] = _tri_cumsum(tri_c, lac[cs, :])

    row_blk = lax.broadcasted_iota(jnp.int32, (_WG, 1), 0) // P_B

    def run_direction(d):
        for g in range(G_B):
            for hh in range(_HG):
                if has_init:
                    h_scr[d, g, hh * P_B:(hh + 1) * P_B, :] = h0_ref[0, 0, d, g * _HG + hh]
                else:
                    h_scr[d, g, hh * P_B:(hh + 1) * P_B, :] = jnp.zeros((P_B, N_B), F32)

        mask_t = masks_t[d]
        pos = CHUNK - 1 if d == 0 else 0

        for ci in range(n_chunks):
            c = ci if d == 0 else n_chunks - 1 - ci
            cs = slice(c * CHUNK, (c + 1) * CHUNK)
            cum_c = cumc_scr[d, cs, :]
            for g in range(G_B):
                bg = bs_scr[cs, g * N_B:(g + 1) * N_B]
                ctg = cs_scr[g * N_B:(g + 1) * N_B, cs]
                h_st = h_scr[d, g]
                res = _dot(jnp.concatenate([bg, h_st.astype(BF16)], axis=0), ctg)
                cbt = res[0:CHUNK]
                inter = res[CHUNK:CHUNK + _WG]
                xw = []
                decay = jnp.zeros((_WG, 1), F32)
                for hh in range(_HG):
                    head = g * _HG + hh
                    r = d * H_B + head
                    b_r = br_scr[r:r + 1, cs]
                    dt_r = dtr_scr[r:r + 1, cs]
                    b_c = cum_c[:, _DT_COL0 + r:_DT_COL0 + r + 1]
                    gt = b_r[:, pos:pos + 1]
                    ps = slice(head * P_B, (head + 1) * P_B)
                    xh = xs_scr[ps, cs]
                    st = (cbt * jnp.exp(jnp.where(mask_t, b_r - b_c, -jnp.inf))).astype(BF16)
                    y = _dot((xh * dt_r).astype(BF16), st) + jnp.exp(b_r) * inter[hh * P_B:(hh + 1) * P_B]
                    xw.append(xh * (jnp.exp(gt - b_r) * dt_r))
                    decay = jnp.where(row_blk == hh, jnp.exp(gt), decay)
                    if d == 0:
                        yt_scr[ps, cs] = y
                    else:
                        yt_scr[ps, cs] = yt_scr[ps, cs] + y + dsk_ref[ps, :] * xh
                h_scr[d, g] = decay * h_st + _dot(jnp.concatenate(xw, axis=0).astype(BF16), bg)
            if d == 1:
                yz = yt_scr[:, cs] * _silu(zt_ref[:, cs])
                rms = lax.rsqrt(jnp.sum(yz * yz, axis=0, keepdims=True) * (1.0 / W_B) + EPS)
                out_ref[:, cs] = (yz * rms * ng_ref[...]).astype(out_ref.dtype)

        if emit_state:
            for g in range(G_B):
                for hh in range(_HG):
                    ho_ref[0, layer, d, g * _HG + hh] = h_scr[d, g, hh * P_B:(hh + 1) * P_B, :]

    run_direction(0)
    run_direction(1)


def _ssd(proj, gates, proj_t, gate_bias, gate_bias_t, alog_row, alog_col, conv, d_rep, norm_g_rep,
         seq, n_seq, row_off, layer, init=None, prev=None):
    blk0 = row_off // seq
    has_init = init is not None
    emit_state = not has_init
    nb = G_B * N_B
    zr = 3 * W_A // W_B
    cr = (3 * W_A + 2 * W_B) // nb
    gr = (3 * W_A + 2 * W_B + nb) // LANES
    bc = (W_A + 2 * W_C) // nb
    const = lambda shape: pl.BlockSpec(shape, lambda s: (0,) * len(shape))
    st_spec = lambda n_layers, first: pl.BlockSpec((1, n_layers, N_DIR, H_B, P_B, N_B),
                                                   lambda s: (s, first, 0, 0, 0, 0))
    in_specs = [pl.BlockSpec((W_B, seq), lambda s: (zr, s + blk0)),
                pl.BlockSpec((W_B, seq), lambda s: (zr + 1, s + blk0)),
                pl.BlockSpec((nb, seq), lambda s: (cr, s + blk0)),
                pl.BlockSpec((seq, nb), lambda s: (s + blk0, bc)),
                pl.BlockSpec((seq, LANES), lambda s: (s + blk0, 0)),
                pl.BlockSpec((LANES, seq), lambda s: (gr, s + blk0)),
                const((1, LANES)), const((LANES, 1)), const((1, LANES)), const((LANES, 1)),
                const((SSM_CONV, W_B, LANES)), const((W_B, LANES)), const((SSM_CONV, nb, LANES)), const((nb, LANES)),
                const((8, nb)), const((1, nb)), const((W_B, LANES)), const((W_B, LANES))]
    args = [proj_t, proj_t, proj_t, proj, gates, proj_t, gate_bias, gate_bias_t, alog_row, alog_col,
            *conv, d_rep, norm_g_rep]
    if has_init:
        in_specs.append(st_spec(1, layer))
        args.append(init)
    if emit_state and layer > 0:
        in_specs.append(st_spec(layer, 0))
        args.append(prev)
    out_specs = [pl.BlockSpec((W_B, seq), lambda s: (0, s))]
    out_shape = [jax.ShapeDtypeStruct((W_B, n_seq * seq), BF16)]
    if emit_state:
        out_specs.append(st_spec(layer + 1, 0))
        out_shape.append(jax.ShapeDtypeStruct((n_seq, layer + 1, N_DIR, H_B, P_B, N_B), F32))
    return pl.pallas_call(
        functools.partial(_ssd_kernel, seq=seq, has_init=has_init, emit_state=emit_state, layer=layer),
        grid=(n_seq,),
        in_specs=in_specs,
        out_specs=out_specs,
        out_shape=out_shape,
        scratch_shapes=[pltpu.VMEM((seq + 16, nb), F32),
                        pltpu.VMEM((W_B, seq), F32),
                        pltpu.VMEM((nb, seq), BF16),
                        pltpu.VMEM((seq, nb), BF16),
                        pltpu.VMEM((N_DIR * H_B, seq), F32),
                        pltpu.VMEM((N_DIR * H_B, seq), F32),
                        pltpu.VMEM((N_DIR, seq, LANES), F32),
                        pltpu.VMEM((W_B, seq), F32),
                        pltpu.VMEM((N_DIR, G_B, _WG, N_B), F32)],
        compiler_params=_cparams(1),
        name="ssd_lat" if has_init else "ssd_ctx",
    )(*args)


def _conf_kernel(a_ref, ap_ref, an_ref, g_ref, gp_ref, gn_ref, w_ref, b_ref, lg_ref, lb_ref, out_ref, pad_scr,
                 rot_scr):
    i = pl.program_id(0)
    n_ctx_blk = N_CTX_TOK // CONV_ROWS
    blk_per_lat = DEC_SEQ // CONV_ROWS
    j = (i - n_ctx_blk) % blk_per_lat
    is_ctx = i < n_ctx_blk
    keep_prev = jnp.where(is_ctx | (j == 0), 0.0, 1.0)
    keep_next = jnp.where(is_ctx | (j == blk_per_lat - 1), 0.0, 1.0)
    pad_scr[0:HALO, :] = ap_ref[...] * _sigmoid(gp_ref[...]) * keep_prev
    pad_scr[HALO:HALO + CONV_ROWS, :] = a_ref[...] * _sigmoid(g_ref[...])
    pad_scr[HALO + CONV_ROWS:2 * HALO + CONV_ROWS, :] = an_ref[...] * _sigmoid(gn_ref[...]) * keep_next
    n_rot = CONV_ROWS + 2 * HALO - 8
    for s in range(1, 8):
        rot_scr[s, :, :] = pad_scr[s:s + n_rot, :]
    rc = 64
    for c in range(CONV_ROWS // rc):
        acc = jnp.broadcast_to(b_ref[...], (rc, W_C))
        for k in range(CONV_W):
            off = HALO - CONV_W // 2 + k
            r = c * rc + off - off % 8
            src = pad_scr[r:r + rc, :] if off % 8 == 0 else rot_scr[off % 8, r:r + rc, :]
            acc = acc + w_ref[k:k + 1, :] * src
        u = _ln(acc) * lg_ref[...] + lb_ref[...]
        out_ref[c * rc:(c + 1) * rc, :] = _silu(u).astype(out_ref.dtype)


def _conf(proj, dw_w, dw_b, ln_g, ln_b):
    assert SEQ == CONV_ROWS and DEC_SEQ % CONV_ROWS == 0
    ac = W_A // W_C
    hb = CONV_ROWS // HALO
    n_halo = T_ALL // HALO
    const = lambda shape: pl.BlockSpec(shape, lambda i: (0,) * len(shape))

    def specs(c):
        return [pl.BlockSpec((CONV_ROWS, W_C), lambda i: (i, c)),
                pl.BlockSpec((HALO, W_C), lambda i: (jnp.maximum(i * hb - 1, 0), c)),
                pl.BlockSpec((HALO, W_C), lambda i: (jnp.minimum((i + 1) * hb, n_halo - 1), c))]

    return pl.pallas_call(
        _conf_kernel,
        grid=(T_ALL // CONV_ROWS,),
        in_specs=specs(ac) + specs(ac + 1) + [const((32, W_C)), const((1, W_C)), const((1, W_C)), const((1, W_C))],
        out_specs=pl.BlockSpec((CONV_ROWS, W_C), lambda i: (i, 0)),
        out_shape=jax.ShapeDtypeStruct((T_ALL, W_C), BF16),
        scratch_shapes=[pltpu.VMEM((CONV_ROWS + 2 * HALO, W_C), F32),
                        pltpu.VMEM((8, CONV_ROWS + 2 * HALO - 8, W_C), F32)],
        compiler_params=_cparams(1),
        name="conf",
    )(proj, proj, proj, proj, proj, proj, dw_w, dw_b, ln_g, ln_b)


_PAIRS = [(a, b) for a in range(EXPERTS_PER_GROUP) for b in range(a + 1, EXPERTS_PER_GROUP)]
N_CLASSES = N_EGROUPS * len(_PAIRS)
N_FCHUNK = D_MODEL // LANES
N_IN_ROWS = 2 * N_FCHUNK


def _store_token_tiles(ref, val):
    n = val.shape[0]
    for c in range(N_FCHUNK):
        ref[pl.ds(c, n, stride=N_FCHUNK), :] = val[:, c * LANES:(c + 1) * LANES]


def _load_token_tiles(ref, n):
    return jnp.concatenate([ref[pl.ds(c, n, stride=N_FCHUNK), :] for c in range(N_FCHUNK)], axis=1)
N_MOE_ROWS = T_ALL + N_CLASSES * TM_MOE
N_MOE_TILES = N_MOE_ROWS // TM_MOE


def _outproj_kernel(*refs, tm, n_x):
    x_refs = refs[:n_x]
    (hac_ref, hal_ref, hbc_ref, hbl_ref, hc_ref, mod_ref, wo_ref, pg_ref, pb_ref,
     wr_ref, br_ref, x1_ref, u2_ref, rt_ref) = refs[n_x:]
    i = pl.program_id(0)
    r = _mod_row(i * tm)
    gate1 = mod_ref[pl.ds(r, 1), 2 * D_MODEL:3 * D_MODEL]
    shift2 = mod_ref[pl.ds(r, 1), 3 * D_MODEL:4 * D_MODEL]
    scale2 = mod_ref[pl.ds(r, 1), 4 * D_MODEL:5 * D_MODEL]
    is_ctx = i < N_CTX_TOK // tm
    ha = jnp.where(is_ctx, hac_ref[...], hal_ref[...])
    hb = jnp.where(is_ctx, hbc_ref[...], hbl_ref[...])
    mix = (_dotg(ha, wo_ref[0:W_A, :], _TN) + _dotg(hb, wo_ref[W_A:W_A + W_B, :], _TN)
           + _dot(hc_ref[...], wo_ref[W_A + W_B:D_MIX, :]))
    x1 = _ln(ALPHA * _read_x(x_refs, i, tm) + gate1 * mix) * pg_ref[...] + pb_ref[...]
    x1_ref[...] = x1
    u2 = _ln(x1) * (1.0 + scale2) + shift2
    for c in range(N_FCHUNK):
        u2_ref[pl.ds(c, tm, stride=N_IN_ROWS), :] = u2[:, c * LANES:(c + 1) * LANES]

    logits = lax.dot_general(wr_ref[...], u2, _NT, precision=lax.Precision.HIGHEST,
                             preferred_element_type=F32) + br_ref[...]
    ex = jnp.exp(logits - jnp.max(logits, axis=0, keepdims=True))
    probs = ex / jnp.sum(ex, axis=0, keepdims=True)
    scores = []
    for g in range(N_EGROUPS):
        p = [probs[g * EXPERTS_PER_GROUP + e:g * EXPERTS_PER_GROUP + e + 1, :] for e in range(EXPERTS_PER_GROUP)]
        best = p[0] + p[1]
        for a in range(EXPERTS_PER_GROUP):
            for b in range(a + 1, EXPERTS_PER_GROUP):
                if (a, b) != (0, 1):
                    best = jnp.maximum(best, p[a] + p[b])
        scores.append(best)
    gmax = functools.reduce(jnp.maximum, scores)
    sel = jnp.full(gmax.shape, N_EGROUPS - 1, jnp.int32)
    for g in range(N_EGROUPS - 2, -1, -1):
        sel = jnp.where(scores[g] == gmax, g, sel)
    eidx = lax.broadcasted_iota(jnp.int32, probs.shape, 0)
    pm = jnp.where((eidx // EXPERTS_PER_GROUP) == sel, probs, -jnp.inf)
    p1 = jnp.max(pm, axis=0, keepdims=True)
    i1 = jnp.min(jnp.where(pm == p1, eidx, N_EXPERTS), axis=0, keepdims=True)
    pm2 = jnp.where(eidx == i1, -jnp.inf, pm)
    p2 = jnp.max(pm2, axis=0, keepdims=True)
    i2 = jnp.min(jnp.where(pm2 == p2, eidx, N_EXPERTS), axis=0, keepdims=True)
    den = p1 + p2
    first_lo = i1 < i2
    w_lo = jnp.where(first_lo, p1, p2) / den
    w_hi = jnp.where(first_lo, p2, p1) / den
    a = jnp.minimum(i1, i2) - sel * EXPERTS_PER_GROUP
    b = jnp.maximum(i1, i2) - sel * EXPERTS_PER_GROUP
    pair = jnp.zeros_like(a)
    for k, (pa, pb_) in enumerate(_PAIRS):
        pair = jnp.where((a == pa) & (b == pb_), k, pair)
    cls = sel * len(_PAIRS) + pair
    rt_ref[...] = jnp.broadcast_to(cls, rt_ref.shape)
    wrow = lax.broadcasted_iota(jnp.int32, (LANES, tm), 0)
    wt = jnp.where(wrow == 0, w_lo, jnp.where(wrow == 1, w_hi, 0.0))
    u2_ref[pl.ds(N_FCHUNK, tm, stride=N_IN_ROWS), :] = wt.T
    for c in range(N_FCHUNK + 1, N_IN_ROWS):
        u2_ref[pl.ds(c, tm, stride=N_IN_ROWS), :] = jnp.zeros((tm, LANES), F32)


def _outproj(x_src, ha_c, ha_l, hb_c, hb_l, hc, mod, w_o, pg, pb, w_rt, b_r):
    tm = TM_PROJ
    n_ctx_blk = N_CTX_TOK // tm
    const = lambda shape: pl.BlockSpec(shape, lambda i: (0,) * len(shape))
    rows = lambda w: pl.BlockSpec((tm, w), lambda i: (i, 0))
    ctx_feat = lambda w: pl.BlockSpec((w, tm), lambda i: (0, jnp.minimum(i, n_ctx_blk - 1)))
    lat_feat = lambda w: pl.BlockSpec((w, tm), lambda i: (0, jnp.maximum(i - n_ctx_blk, 0)))
    return pl.pallas_call(
        functools.partial(_outproj_kernel, tm=tm, n_x=len(x_src)),
        grid=(T_ALL // tm,),
        in_specs=_x_specs(x_src, tm) + [
                  ctx_feat(W_A), lat_feat(W_A), ctx_feat(W_B), lat_feat(W_B),
                  rows(W_C), const((N_MOD_ROWS, 6 * D_MODEL)),
                  const((D_MIX, D_MODEL)), const((1, D_MODEL)), const((1, D_MODEL)),
                  const((N_EXPERTS, D_MODEL)), const((N_EXPERTS, 1))],
        out_specs=[rows(D_MODEL), pl.BlockSpec((tm * N_IN_ROWS, LANES), lambda i: (i, 0)),
                   pl.BlockSpec((8, tm), lambda i: (0, i))],
        out_shape=[jax.ShapeDtypeStruct((T_ALL, D_MODEL), F32),
                   jax.ShapeDtypeStruct((T_ALL * N_IN_ROWS, LANES), F32),
                   jax.ShapeDtypeStruct((8, T_ALL), jnp.int32)],
        compiler_params=_cparams(1),
        name="outproj",
    )(*x_src, ha_c, ha_l, hb_c, hb_l, hc, mod, w_o, pg, pb, w_rt, b_r)


N_Y_ROWS = T_ALL + 2 * TM_MOE


def _moe_kernel(tok_ref, tstart_ref, nval_ref, tlo_ref, thi_ref, nused_ref,
                u_hbm, wgl_ref, wul_ref, wdl_ref, wgh_ref, wuh_ref, wdh_ref, y_hbm,
                xbuf0, xbuf1, ybuf0, ybuf1, gsem, ssem):
    del tlo_ref, thi_ref
    i = pl.program_id(0)
    n_used = nused_ref[0]
    tile_rows = TM_MOE * N_FCHUNK

    def token_tile(ref, tok, rows=N_FCHUNK):
        return ref.at[pl.ds(pl.multiple_of(tok * rows, rows), rows), :]

    def gather_rows(t, xb, sem):
        base = tstart_ref[t]
        for r in range(TM_MOE):
            pltpu.make_async_copy(token_tile(u_hbm, tok_ref[base + r], N_IN_ROWS), token_tile(xb, r, N_IN_ROWS),
                                  sem).start()

    def scatter_rows(t, n_valid, yb, sem, spare):
        base = tstart_ref[t]
        for r in range(TM_MOE):
            dst = jnp.where(r < n_valid, tok_ref[base + r], spare + r)
            pltpu.make_async_copy(token_tile(yb, r), token_tile(y_hbm, dst), sem).start()

    def load_tile(xb):
        x = jnp.concatenate([xb[pl.ds(c, TM_MOE, stride=N_IN_ROWS), :] for c in range(N_FCHUNK)], axis=1)
        w = xb[pl.ds(N_FCHUNK, TM_MOE, stride=N_IN_ROWS), :]
        return x.astype(BF16), w[:, 0:1], w[:, 1:2]

    def wait_gather(xb, sem):
        pltpu.make_async_copy(u_hbm.at[pl.ds(0, TM_MOE * N_IN_ROWS), :], xb, sem).wait()

    def wait_scatter(yb, sem):
        pltpu.make_async_copy(yb, y_hbm.at[pl.ds(0, tile_rows), :], sem).wait()

    def step(s, xb, xb_next, yb, yb_prev):
        @pl.when(i > 0)
        def _():
            wait_scatter(yb, ssem.at[s])

        wait_gather(xb, gsem.at[s])

        @pl.when(n_used > 0)
        def _():
            gather_rows(jnp.minimum(i + 1, n_used - 1), xb_next, gsem.at[1 - s])
            prev = jnp.maximum(i - 1, 0)
            scatter_rows(prev, jnp.where(i > 0, nval_ref[prev], 0), yb_prev, ssem.at[1 - s],
                         T_ALL + (1 - s) * TM_MOE)

        x, w_lo, w_hi = load_tile(xb)

        def ffn(wg_ref, wu_ref, wd_ref, w):
            act = _silu(_dot(x, wg_ref[0, 0].astype(BF16))) * _dot(x, wu_ref[0, 0].astype(BF16)) * w
            return _dot(act.astype(BF16), wd_ref[0, 0].astype(BF16))

        _store_token_tiles(yb, ffn(wgl_ref, wul_ref, wdl_ref, w_lo) + ffn(wgh_ref, wuh_ref, wdh_ref, w_hi))

        @pl.when(i == n_used - 1)
        def _():
            scatter_rows(i, nval_ref[i], yb, ssem.at[s], T_ALL + s * TM_MOE)
            wait_scatter(yb, ssem.at[s])
            wait_scatter(yb_prev, ssem.at[1 - s])
            wait_gather(xb_next, gsem.at[1 - s])

    @pl.when(i == 0)
    def _():
        ybuf1[...] = jnp.zeros(ybuf1.shape, F32)
        fill = pltpu.make_async_copy(ybuf1, y_hbm.at[pl.ds(T_ALL * N_FCHUNK, tile_rows), :], ssem.at[0])
        fill.start()
        fill.wait()
        gather_rows(0, xbuf0, gsem.at[0])

    @pl.when((i < n_used) & (i % 2 == 0))
    def _():
        step(0, xbuf0, xbuf1, ybuf0, ybuf1)

    @pl.when((i < n_used) & (i % 2 == 1))
    def _():
        step(1, xbuf1, xbuf0, ybuf1, ybuf0)


def _moe(u2t, tok_sorted, tile_start, n_valid, tile_lo, tile_hi, n_used, w_gate, w_up, w_down, layer):
    lo = lambda shape: pl.BlockSpec(shape, lambda i, tok, ts, nv, tlo, thi, nused: (layer, tlo[i], 0, 0))
    hi = lambda shape: pl.BlockSpec(shape, lambda i, tok, ts, nv, tlo, thi, nused: (layer, thi[i], 0, 0))
    up_shape = (1, 1, D_MODEL, D_FF_EXPERT)
    down_shape = (1, 1, D_FF_EXPERT, D_MODEL)
    in_buf = pltpu.VMEM((TM_MOE * N_IN_ROWS, LANES), F32)
    out_buf = pltpu.VMEM((TM_MOE * N_FCHUNK, LANES), F32)
    grid_spec = pltpu.PrefetchScalarGridSpec(
        num_scalar_prefetch=6,
        grid=(N_MOE_TILES,),
        in_specs=[pl.BlockSpec(memory_space=pl.ANY),
                  lo(up_shape), lo(up_shape), lo(down_shape),
                  hi(up_shape), hi(up_shape), hi(down_shape)],
        out_specs=pl.BlockSpec(memory_space=pl.ANY),
        scratch_shapes=[in_buf, in_buf, out_buf, out_buf,
                        pltpu.SemaphoreType.DMA((2,)),
                        pltpu.SemaphoreType.DMA((2,))],
    )
    return pl.pallas_call(
        _moe_kernel,
        grid_spec=grid_spec,
        out_shape=jax.ShapeDtypeStruct((N_Y_ROWS * N_FCHUNK, LANES), F32),
        compiler_params=pltpu.CompilerParams(dimension_semantics=("arbitrary",), vmem_limit_bytes=BIG_VMEM_LIMIT,
                                             has_side_effects=True),
        name="moe",
    )(tok_sorted, tile_start, n_valid, tile_lo, tile_hi, n_used, u2t, w_gate, w_up, w_down, w_gate, w_up, w_down)


def _route_tables(cls):
    _, tok_sorted = lax.sort((cls, jnp.arange(T_ALL, dtype=jnp.int32)), num_keys=1)
    cids = jnp.arange(N_CLASSES, dtype=jnp.int32)
    counts = jnp.sum((cls[:, None] == cids[None, :]).astype(jnp.int32), axis=0)
    offs = jnp.cumsum(counts) - counts
    padded = (counts + TM_MOE - 1) // TM_MOE * TM_MOE
    ends = jnp.cumsum(padded)
    offs_p = ends - padded
    n_used = ends[-1] // TM_MOE
    tile_start = jnp.arange(N_MOE_TILES, dtype=jnp.int32) * TM_MOE
    tile_cls = jnp.sum((ends[None, :] <= jnp.minimum(tile_start, ends[-1] - TM_MOE)[:, None]).astype(jnp.int32),
                       axis=1)
    onehot = (tile_cls[:, None] == cids[None, :]).astype(jnp.int32)
    pick = lambda table: jnp.sum(onehot * table[None, :], axis=1)
    k = tile_start - pick(offs_p)
    n_valid = jnp.where(tile_start < ends[-1], jnp.clip(pick(counts) - k, 0, TM_MOE), 0)
    tile_first = jnp.clip(pick(offs) + k, 0, T_ALL - 1)
    tile_lo = pick(cids // len(_PAIRS) * EXPERTS_PER_GROUP + jnp.array([p[0] for p in _PAIRS] * N_EGROUPS, jnp.int32))
    tile_hi = pick(cids // len(_PAIRS) * EXPERTS_PER_GROUP + jnp.array([p[1] for p in _PAIRS] * N_EGROUPS, jnp.int32))
    tok_sorted = jnp.concatenate([tok_sorted, jnp.zeros((TM_MOE,), jnp.int32)])
    i32 = lambda v: v.astype(jnp.int32)
    return tok_sorted, i32(tile_first), i32(n_valid), i32(tile_lo), i32(tile_hi), i32(n_used).reshape(1)


def _final_kernel(x1_ref, y_ref, mod_ref, pg_ref, pb_ref, *o_refs, tm):
    i = pl.program_id(0)
    r = _mod_row(i * tm)
    gate2 = mod_ref[pl.ds(r, 1), 5 * D_MODEL:6 * D_MODEL]
    out = _ln(ALPHA * x1_ref[...] + gate2 * _load_token_tiles(y_ref, tm)) * pg_ref[...] + pb_ref[...]
    if len(o_refs) == 1:
        o_refs[0][...] = out
    else:
        @pl.when(i < N_CTX_TOK // tm)
        def _():
            o_refs[0][...] = out

        @pl.when(i >= N_CTX_TOK // tm)
        def _():
            o_refs[1][...] = out


def _final(x1, y, mod, pg, pb, split):
    tm = TM_PROJ
    n_ctx_blk = N_CTX_TOK // tm
    const = lambda shape: pl.BlockSpec(shape, lambda i: (0,) * len(shape))
    rows = pl.BlockSpec((tm, D_MODEL), lambda i: (i, 0))
    if split:
        out_specs = [pl.BlockSpec((tm, D_MODEL), lambda i: (jnp.minimum(i, n_ctx_blk - 1), 0)),
                     pl.BlockSpec((tm, D_MODEL), lambda i: (jnp.maximum(i - n_ctx_blk, 0), 0))]
        out_shape = [jax.ShapeDtypeStruct((N_CTX_TOK, D_MODEL), F32), jax.ShapeDtypeStruct((N_LAT_TOK, D_MODEL), F32)]
    else:
        out_specs = [rows]
        out_shape = [jax.ShapeDtypeStruct((T_ALL, D_MODEL), F32)]
    return pl.pallas_call(
        functools.partial(_final_kernel, tm=tm),
        grid=(T_ALL // tm,),
        in_specs=[rows, pl.BlockSpec((tm * N_FCHUNK, LANES), lambda i: (i, 0)),
                  const((N_MOD_ROWS, 6 * D_MODEL)), const((1, D_MODEL)), const((1, D_MODEL))],
        out_specs=out_specs,
        out_shape=out_shape,
        compiler_params=_cparams(1),
        name="final",
    )(x1, y, mod, pg, pb)


def _grid_pos(n_tok):
    rows = n_tok // GRID_W
    r, col = jnp.meshgrid(jnp.arange(rows, dtype=F32), jnp.arange(GRID_W, dtype=F32), indexing='ij')
    quarter = D_MODEL // 4
    omega = 1.0 / (10000.0 ** (jnp.arange(quarter, dtype=F32) / quarter))

    def emb(p):
        ang = p.reshape(-1)[:, None] * omega[None, :]
        return jnp.concatenate([jnp.sin(ang), jnp.cos(ang)], axis=-1)

    return jnp.concatenate([emb(r), emb(col)], axis=-1)


def _pad_lanes(v, start):
    v = v.reshape(1, -1).astype(F32)
    return jnp.pad(v, ((0, 0), (start, LANES - start - v.shape[1])))


def kernel(x_prompt, x_sample, state_mlstm_C, state_mlstm_n, state_mlstm_m, state_ssd, c, c_ctx, w_in, w_o, mlstm_b_i, mlstm_b_f, mlstm_norm_g, ssd_conv_w, ssd_conv_b, ssd_dt_bias, ssd_A_log, ssd_D, ssd_norm_g, conv_dw_w, conv_dw_b, conv_ln_g, conv_ln_b, w_ada, b_ada, post1_g, post1_b, post2_g, post2_b, w_router, b_router, w_e_gate, w_e_up, w_e_down):
    cvec = jnp.concatenate([c_ctx[None, :], c, jnp.zeros((N_MOD_ROWS - 1 - DEC_BATCH, D_MODEL), F32)], axis=0)
    mod_all = _ada(cvec, w_ada, b_ada)
    x_src = (x_prompt.reshape(N_CTX_TOK, D_MODEL), x_sample.reshape(N_LAT_TOK, D_MODEL), _grid_pos(DEC_SEQ))
    w_rt = w_router.T
    b_r = b_router.reshape(N_EXPERTS, 1)

    a_end = 4 * W_A + N_DIR * 2 * H_A
    b_end = a_end + W_B + W_XBC + N_DIR * H_B
    init = (state_mlstm_C,
            state_mlstm_n.reshape(DEC_BATCH, DEPTH, N_DIR * H_A, DH_A),
            jnp.broadcast_to(state_mlstm_m.reshape(DEC_BATCH, DEPTH, N_DIR * H_A, 1),
                             (DEC_BATCH, DEPTH, N_DIR * H_A, LANES)))
    st_c = st_n = st_m = st_h = None
    for l in range(DEPTH):
        w = w_in[l]
        xbc0 = a_end + W_B
        w_main = jnp.concatenate([w[:, W_A:2 * W_A], w[:, b_end:], w[:, xbc0 + W_B:xbc0 + W_B + G_B * N_B]],
                                 axis=1).astype(BF16)
        w_small = jnp.concatenate([w[:, 4 * W_A:a_end], w[:, b_end - N_DIR * H_B:b_end],
                                   jnp.zeros((D_MODEL, LANES - _DT_COL0 - N_DIR * H_B), F32)], axis=1)
        w_t = jnp.concatenate([w[:, 0:W_A], w[:, 2 * W_A:4 * W_A], w[:, a_end:xbc0 + W_B],
                               w[:, xbc0 + W_B + G_B * N_B:xbc0 + W_XBC], w_small],
                              axis=1).T.astype(BF16)
        w_small = w_small.astype(BF16)
        gate_bias = (_pad_lanes(jnp.stack([mlstm_b_i[l], mlstm_b_f[l]], axis=1), 0)
                     + _pad_lanes(ssd_dt_bias[l], _DT_COL0))
        alog_row = _pad_lanes(ssd_A_log[l], _DT_COL0)
        mod = mod_all[l]

        proj, gates, proj_t = _inproj(x_src, mod, w_main, w_small, w_t)

        m_norm = jnp.broadcast_to(mlstm_norm_g[l].reshape(W_A, 1), (W_A, LANES))
        mlstm_args = (proj, gates, proj_t, gate_bias, gate_bias.reshape(LANES, 1), m_norm)
        ha_c, st_c, st_n, st_m = _mlstm(*mlstm_args, SEQ, BATCH, 0, l, prev=(st_c, st_n, st_m))
        (ha_l,) = _mlstm(*mlstm_args, DEC_SEQ, DEC_BATCH, N_CTX_TOK, l, init=init)

        nb = G_B * N_B
        rep = lambda v: jnp.broadcast_to(v[..., None], v.shape + (LANES,))
        cw, cb = ssd_conv_w[l], ssd_conv_b[l]
        conv = (rep(cw[:, 0:W_B]), rep(cb[0:W_B]), rep(cw[:, W_B + nb:W_XBC]), rep(cb[W_B + nb:W_XBC]),
                jnp.pad(cw[:, W_B:W_B + nb], ((0, 8 - SSM_CONV), (0, 0))), cb[W_B:W_B + nb].reshape(1, nb))
        ssd_args = (proj, gates, proj_t, gate_bias, gate_bias.reshape(LANES, 1), alog_row, alog_row.reshape(LANES, 1),
                    conv, rep(jnp.repeat(ssd_D[l], P_B)), rep(ssd_norm_g[l]))
        hb_c, st_h = _ssd(*ssd_args, SEQ, BATCH, 0, l, prev=st_h)
        (hb_l,) = _ssd(*ssd_args, DEC_SEQ, DEC_BATCH, N_CTX_TOK, l, init=state_ssd)

        hc = _conf(proj, jnp.pad(conv_dw_w[l], ((0, 32 - CONV_W), (0, 0))), conv_dw_b[l].reshape(1, W_C),
                   conv_ln_g[l].reshape(1, W_C), conv_ln_b[l].reshape(1, W_C))

        x1, u2t, route = _outproj(x_src, ha_c, ha_l, hb_c, hb_l, hc, mod, w_o[l].astype(BF16), post1_g[l].reshape(1, D_MODEL),
                                  post1_b[l].reshape(1, D_MODEL), w_rt, b_r)
        y = _moe(u2t, *_route_tables(route[0]), w_e_gate, w_e_up, w_e_down, l)
        outs = _final(x1, y, mod, post2_g[l].reshape(1, D_MODEL), post2_b[l].reshape(1, D_MODEL), l == DEPTH - 1)
        x_src = (outs[0],)

    y_prompt = outs[0].reshape(BATCH, SEQ, D_MODEL)
    y_sample = outs[1].reshape(DEC_BATCH, DEC_SEQ, D_MODEL)
    return (y_prompt, y_sample, st_c, st_n.reshape(BATCH, DEPTH, N_DIR, H_A, DH_A),
            st_m[:, :, :, 0].reshape(BATCH, DEPTH, N_DIR, H_A), st_h)
```

```python
import functools

import jax
import jax.numpy as jnp
from jax import lax
from jax.experimental import pallas as pl
from jax.experimental.pallas import tpu as pltpu

D_MODEL = 1024
BATCH = 32
SEQ = 256
DEPTH = 2
DEC_BATCH = 2
DEC_SEQ = 1024
GRID_W = 64
N_DIR = 2
CHUNK = 128
H_A = 4
DH_A = 128
W_A = H_A * DH_A
H_B = 8
P_B = 64
W_B = H_B * P_B
G_B = 2
N_B = 128
W_XBC = W_B + 2 * G_B * N_B
SSM_CONV = 3
W_C = 512
CONV_W = 31
D_MIX = W_A + W_B + W_C
N_EXPERTS = 16
N_EGROUPS = 4
EXPERTS_PER_GROUP = N_EXPERTS // N_EGROUPS
D_FF_EXPERT = 512
ALPHA = (2 * DEPTH) ** 0.25
EPS = 1e-5
F32 = jnp.float32
BF16 = jnp.bfloat16

N_CTX_TOK = BATCH * SEQ
N_LAT_TOK = DEC_BATCH * DEC_SEQ
T_ALL = N_CTX_TOK + N_LAT_TOK
N_MOD_ROWS = 8
D_MAIN = W_A + 2 * W_C + G_B * N_B
D_T = 3 * W_A + 2 * W_B + G_B * N_B + 128
LANES = 128
HALO = 16
CONV_ROWS = 256
TM_PROJ = 512
TM_MOE = 256
VMEM_LIMIT = 48 * 1024 * 1024
BIG_VMEM_LIMIT = 56 * 1024 * 1024

_NT = (((1,), (1,)), ((), ()))
_TN = (((0,), (0,)), ((), ()))


def _ln(x):
    mu = jnp.mean(x, axis=-1, keepdims=True)
    xc = x - mu
    var = jnp.mean(xc * xc, axis=-1, keepdims=True)
    return xc * lax.rsqrt(var + EPS)


def _sigmoid(x):
    return 1.0 / (1.0 + jnp.exp(-x))


def _silu(x):
    return x * _sigmoid(x)


def _softplus(x):
    return jnp.maximum(x, 0.0) + jnp.log1p(jnp.exp(-jnp.abs(x)))


def _dot(a, b):
    return jnp.dot(a, b, preferred_element_type=F32)


def _dotg(a, b, dims):
    return lax.dot_general(a, b, dims, preferred_element_type=F32)


def _tri_cumsum(tri, x):
    hi = x.astype(BF16)
    r1 = x - hi.astype(F32)
    mid = r1.astype(BF16)
    lo = (r1 - mid.astype(F32)).astype(BF16)
    return _dot(tri, hi) + _dot(tri, mid) + _dot(tri, lo)


def _tri_mask(d):
    row = lax.broadcasted_iota(jnp.int32, (CHUNK, CHUNK), 0)
    col = lax.broadcasted_iota(jnp.int32, (CHUNK, CHUNK), 1)
    return (row >= col) if d == 0 else (row <= col)


def _mod_row(row_start):
    return jnp.where(row_start < N_CTX_TOK, 0, 1 + (row_start - N_CTX_TOK) // DEC_SEQ)


def _cparams(n_axes, vmem_limit=VMEM_LIMIT):
    return pltpu.CompilerParams(dimension_semantics=("arbitrary",) * n_axes, vmem_limit_bytes=vmem_limit)


def _ada_kernel(c_ref, w_ref, b_ref, o_ref):
    o_ref[0] = _dot(_silu(c_ref[...]), w_ref[0]) + b_ref[0]


def _ada(cvec, w_ada, b_ada):
    tn = 1536
    return pl.pallas_call(
        _ada_kernel,
        grid=(DEPTH, 6 * D_MODEL // tn),
        in_specs=[
            pl.BlockSpec((N_MOD_ROWS, D_MODEL), lambda l, j: (0, 0)),
            pl.BlockSpec((1, D_MODEL, tn), lambda l, j: (l, 0, j)),
            pl.BlockSpec((1, 1, tn), lambda l, j: (l, 0, j)),
        ],
        out_specs=pl.BlockSpec((1, N_MOD_ROWS, tn), lambda l, j: (l, 0, j)),
        out_shape=jax.ShapeDtypeStruct((DEPTH, N_MOD_ROWS, 6 * D_MODEL), F32),
        compiler_params=_cparams(2),
        name="ada",
    )(cvec, w_ada, b_ada.reshape(DEPTH, 1, 6 * D_MODEL))


def _x_specs(x_src, tm):
    if len(x_src) == 1:
        return [pl.BlockSpec((tm, D_MODEL), lambda i: (i, 0))]
    n_ctx_blk = N_CTX_TOK // tm
    per_seq = DEC_SEQ // tm
    return [pl.BlockSpec((tm, D_MODEL), lambda i: (jnp.minimum(i, n_ctx_blk - 1), 0)),
            pl.BlockSpec((tm, D_MODEL), lambda i: (jnp.maximum(i - n_ctx_blk, 0), 0)),
            pl.BlockSpec((tm, D_MODEL), lambda i: (jnp.where(i < n_ctx_blk, 0, (i - n_ctx_blk) % per_seq), 0))]


def _read_x(x_refs, i, tm):
    if len(x_refs) == 1:
        return x_refs[0][...]
    xp_ref, xs_ref, pos_ref = x_refs
    return jnp.where(i < N_CTX_TOK // tm, xp_ref[...], xs_ref[...] + pos_ref[...])


def _inproj_kernel(*refs, tm, n_x):
    x_refs = refs[:n_x]
    mod_ref, wm_ref, wg_ref, wt_ref, om_ref, og_ref, ot_ref = refs[n_x:]
    i = pl.program_id(0)
    r = _mod_row(i * tm)
    shift = mod_ref[pl.ds(r, 1), 0:D_MODEL]
    scale = mod_ref[pl.ds(r, 1), D_MODEL:2 * D_MODEL]
    u = (_ln(_read_x(x_refs, i, tm)) * (1.0 + scale) + shift).astype(BF16)
    om_ref[...] = _dot(u, wm_ref[...])
    og_ref[...] = _dot(u, wg_ref[...])
    ot_ref[...] = _dotg(wt_ref[...], u, _NT)


def _inproj(x_src, mod, w_main, w_small, w_t):
    tm = TM_PROJ
    const = lambda shape: pl.BlockSpec(shape, lambda i: (0, 0))
    return pl.pallas_call(
        functools.partial(_inproj_kernel, tm=tm, n_x=len(x_src)),
        grid=(T_ALL // tm,),
        in_specs=_x_specs(x_src, tm) + [
            const((N_MOD_ROWS, 6 * D_MODEL)), const((D_MODEL, D_MAIN)), const((D_MODEL, LANES)),
            const((D_T, D_MODEL)),
        ],
        out_specs=[
            pl.BlockSpec((tm, D_MAIN), lambda i: (i, 0)),
            pl.BlockSpec((tm, LANES), lambda i: (i, 0)),
            pl.BlockSpec((D_T, tm), lambda i: (0, i)),
        ],
        out_shape=[
            jax.ShapeDtypeStruct((T_ALL, D_MAIN), F32),
            jax.ShapeDtypeStruct((T_ALL, LANES), F32),
            jax.ShapeDtypeStruct((D_T, T_ALL), F32),
        ],
        compiler_params=_cparams(1, BIG_VMEM_LIMIT),
        name="inproj",
    )(*x_src, mod, w_main, w_small, w_t)


def _scan_max(x, lane, d):
    s = 1
    while s < CHUNK:
        if d == 0:
            x = jnp.where(lane >= s, jnp.maximum(x, pltpu.roll(x, s, axis=1)), x)
        else:
            x = jnp.where(lane < CHUNK - s, jnp.maximum(x, pltpu.roll(x, CHUNK - s, axis=1)), x)
        s *= 2
    return x


def _split_dot(x, tri):
    hi = x.astype(BF16)
    r1 = x - hi.astype(F32)
    mid = r1.astype(BF16)
    lo = (r1 - mid.astype(F32)).astype(BF16)
    return _dot(hi, tri) + _dot(mid, tri) + _dot(lo, tri)


def _mlstm_kernel(*refs, seq, has_init, emit_state, layer):
    it = iter(refs)
    k_ref, qt_ref, vt_ref, ot_ref, g_ref, gt_ref, gb_ref, gbt_ref, ng_ref = (next(it) for _ in range(9))
    if has_init:
        c0_ref, n0_ref, m0_ref = (next(it) for _ in range(3))
    if emit_state and layer > 0:
        prev_refs = [next(it) for _ in range(3)]
    out_ref = next(it)
    if emit_state:
        co_ref, no_ref, mo_ref = (next(it) for _ in range(3))
    tg_scr, tgt_scr, b_scr, beta_scr, pm_scr, cumc_scr, hd_scr, c_scr, n_scr, m_scr = (next(it) for _ in range(10))
    n_chunks = seq // CHUNK
    n_gate = N_DIR * 2 * H_A
    if emit_state and layer > 0:
        for prev_ref, st_ref in zip(prev_refs, (co_ref, no_ref, mo_ref)):
            st_ref[0, 0:layer] = prev_ref[0]

    y = g_ref[...] + gb_ref[...]
    lane = lax.broadcasted_iota(jnp.int32, y.shape, 1)
    tg_scr[...] = jnp.where((lane < n_gate) & ((lane & H_A) != 0), -_softplus(-y), y)
    yt = gt_ref[0:n_gate, :] + gbt_ref[0:n_gate, :]
    rowt = lax.broadcasted_iota(jnp.int32, yt.shape, 0)
    tgt_scr[...] = jnp.where((rowt & H_A) != 0, -_softplus(-yt), yt)

    lane8 = lax.broadcasted_iota(jnp.int32, (8, CHUNK), 1)
    row8 = lax.broadcasted_iota(jnp.int32, (8, LANES), 0)
    masks_t = [_tri_mask(1 - d) for d in range(N_DIR)]

    for d in range(N_DIR):
        tri_t = masks_t[d].astype(BF16)
        tri_c = _tri_mask(d).astype(BF16)
        rows = slice(d * 8, (d + 1) * 8)
        for c in range(n_chunks):
            cs = slice(c * CHUNK, (c + 1) * CHUNK)
            gr8 = tgt_scr[rows, cs]
            b8 = pltpu.roll(_split_dot(gr8, tri_t), H_A, axis=0)
            beta8 = gr8 - b8
            b_scr[rows, cs] = b8
            beta_scr[rows, cs] = beta8
            pm_scr[rows, cs] = _scan_max(beta8, lane8, d)
            cumc_scr[d, cs, :] = _tri_cumsum(tri_c, tg_scr[cs, :])

    pos = (CHUNK - 1, 0)

    for d in range(N_DIR):
        rows = slice(d * 8, (d + 1) * 8)
        if has_init:
            for h in range(H_A):
                c_scr[d, h] = c0_ref[0, 0, d, h].T
            n0 = n0_ref[0, 0]
            m0 = m0_ref[0, 0]
            if d == 1:
                n0 = pltpu.roll(n0, H_A, axis=0)
                m0 = pltpu.roll(m0, H_A, axis=0)
            n_scr[rows, :] = jnp.where(row8 < H_A, n0, 0.0)
            m_scr[rows, :] = jnp.where(row8 < H_A, m0, 0.0)
        else:
            for h in range(H_A):
                c_scr[d, h] = jnp.zeros((DH_A, DH_A), F32)
            n_scr[rows, :] = jnp.zeros((8, DH_A), F32)
            m_scr[rows, :] = jnp.zeros((8, LANES), F32)

    def chunk_prep(d, c):
        rows = slice(d * 8, (d + 1) * 8)
        cs = slice(c * CHUNK, (c + 1) * CHUNK)
        b8 = b_scr[rows, cs]
        beta8 = beta_scr[rows, cs]
        m8 = m_scr[rows, :]
        n8 = n_scr[rows, :]
        mu8 = jnp.maximum(m8, pm_scr[rows, cs])
        mu_last = mu8[:, pos[d]:pos[d] + 1]
        w_tok8 = jnp.exp(beta8 - mu_last)
        return dict(cs=cs, rows=rows, n8=n8, mu8=mu8, w_int8=jnp.exp(m8 - mu8), emt8=jnp.exp(-(b8 + mu8)),
                    w_tok8=w_tok8, w_prev8=jnp.exp(m8 - mu_last), m_new8=b8[:, pos[d]:pos[d] + 1] + mu_last,
                    n16b=jnp.concatenate([n8, jnp.zeros_like(n8)], axis=0).astype(BF16),
                    w_tok16=jnp.concatenate([w_tok8, jnp.zeros_like(w_tok8)], axis=0),
                    gc=tg_scr[cs, :], cum_c=cumc_scr[d, cs, :])

    def head_step(d, h, p):
        cs = p["cs"]
        hs = slice(h * DH_A, (h + 1) * DH_A)
        col_i = d * 2 * H_A + h
        col_f = col_i + H_A
        beta_c = p["gc"][:, col_i:col_i + 1] - p["cum_c"][:, col_f:col_f + 1]
        kb = k_ref[cs, hs].astype(BF16)
        qtb = (qt_ref[hs, cs] * (DH_A ** -0.5)).astype(BF16)
        vt = vt_ref[hs, cs]
        ct = c_scr[d, h]
        w_int_r = p["w_int8"][h:h + 1, :]

        res = _dot(jnp.concatenate([kb, ct.astype(BF16), p["n16b"]], axis=0), qtb)
        st = res[0:CHUNK] * jnp.exp(jnp.where(masks_t[d], beta_c - p["mu8"][h:h + 1, :], -jnp.inf))
        num = w_int_r * res[CHUNK:CHUNK + DH_A] + _dot(vt.astype(BF16), st.astype(BF16))
        den_r = (w_int_r * res[CHUNK + DH_A + h:CHUNK + DH_A + h + 1]
                 + jnp.sum(st, axis=0, keepdims=True))
        hd_scr[d, hs, cs] = num * (1.0 / jnp.maximum(jnp.abs(den_r), p["emt8"][h:h + 1, :]))

        upd = _dot(jnp.concatenate([vt * p["w_tok8"][h:h + 1, :], p["w_tok16"]], axis=0).astype(BF16), kb)
        c_scr[d, h] = p["w_prev8"][h:h + 1, :] * ct + upd[0:DH_A]
        return upd[DH_A:DH_A + 8]

    for ci in range(n_chunks):
        preps = [chunk_prep(0, ci), chunk_prep(1, n_chunks - 1 - ci)]
        nk_acc = [jnp.zeros((8, DH_A), F32) for _ in range(N_DIR)]
        for h in range(H_A):
            for d in range(N_DIR):
                nk_acc[d] = jnp.where(row8 == h, head_step(d, h, preps[d]), nk_acc[d])
        for d in range(N_DIR):
            p = preps[d]
            n_scr[p["rows"], :] = p["w_prev8"] * p["n8"] + nk_acc[d]
            m_scr[p["rows"], :] = jnp.where(row8 < H_A, jnp.broadcast_to(p["m_new8"], (8, LANES)), 0.0)

    if emit_state:
        for d in range(N_DIR):
            for h in range(H_A):
                co_ref[0, layer, d, h] = c_scr[d, h].T
            no_ref[0, layer, d * H_A:(d + 1) * H_A, :] = n_scr[d * 8:d * 8 + H_A, :]
            mo_ref[0, layer, d * H_A:(d + 1) * H_A, :] = m_scr[d * 8:d * 8 + H_A, :]

    for c in range(n_chunks):
        cs = slice(c * CHUNK, (c + 1) * CHUNK)
        for h in range(H_A):
            hs = slice(h * DH_A, (h + 1) * DH_A)
            hsum = hd_scr[0, hs, cs] + hd_scr[1, hs, cs]
            mean = jnp.sum(hsum, axis=0, keepdims=True) * (1.0 / DH_A)
            xc = hsum - mean
            var = jnp.sum(xc * xc, axis=0, keepdims=True) * (1.0 / DH_A)
            hn = xc * lax.rsqrt(var + EPS) * ng_ref[hs, :]
            out_ref[hs, cs] = (hn * _sigmoid(ot_ref[hs, cs])).astype(out_ref.dtype)


def _mlstm(proj, gates, proj_t, gate_bias, gate_bias_t, norm_g_rep, seq, n_seq, row_off, layer, init=None, prev=None):
    blk0 = row_off // seq
    has_init = init is not None
    emit_state = not has_init
    feat = lambda r: pl.BlockSpec((W_A, seq), lambda s: (r, s + blk0))
    const = lambda shape: pl.BlockSpec(shape, lambda s: (0,) * len(shape))
    st_shapes = [(N_DIR, H_A, DH_A, DH_A), (N_DIR * H_A, DH_A), (N_DIR * H_A, LANES)]

    def st_specs(n_layers, first):
        return [pl.BlockSpec((1, n_layers) + shp, lambda s, nd=len(shp): (s, first) + (0,) * nd) for shp in st_shapes]

    in_specs = [pl.BlockSpec((seq, W_A), lambda s: (s + blk0, 0)), feat(0), feat(1), feat(2),
                pl.BlockSpec((seq, LANES), lambda s: (s + blk0, 0)),
                pl.BlockSpec((LANES, seq), lambda s: (D_T // LANES - 1, s + blk0)),
                const((1, LANES)), const((LANES, 1)), const((W_A, LANES))]
    args = [proj, proj_t, proj_t, proj_t, gates, proj_t, gate_bias, gate_bias_t, norm_g_rep]
    if has_init:
        in_specs += st_specs(1, layer)
        args += list(init)
    if emit_state and layer > 0:
        in_specs += st_specs(layer, 0)
        args += list(prev)
    out_specs = [pl.BlockSpec((W_A, seq), lambda s: (0, s))]
    out_shape = [jax.ShapeDtypeStruct((W_A, n_seq * seq), BF16)]
    if emit_state:
        out_specs += st_specs(layer + 1, 0)
        out_shape += [jax.ShapeDtypeStruct((n_seq, layer + 1) + shp, F32) for shp in st_shapes]
    return pl.pallas_call(
        functools.partial(_mlstm_kernel, seq=seq, has_init=has_init, emit_state=emit_state, layer=layer),
        grid=(n_seq,),
        in_specs=in_specs,
        out_specs=out_specs,
        out_shape=out_shape,
        scratch_shapes=[pltpu.VMEM((seq, LANES), F32),
                        pltpu.VMEM((N_DIR * 8, seq), F32),
                        pltpu.VMEM((N_DIR * 8, seq), F32),
                        pltpu.VMEM((N_DIR * 8, seq), F32),
                        pltpu.VMEM((N_DIR * 8, seq), F32),
                        pltpu.VMEM((N_DIR, seq, LANES), F32),
                        pltpu.VMEM((N_DIR, W_A, seq), F32),
                        pltpu.VMEM((N_DIR, H_A, DH_A, DH_A), F32),
                        pltpu.VMEM((N_DIR * 8, DH_A), F32),
                        pltpu.VMEM((N_DIR * 8, LANES), F32)],
        compiler_params=_cparams(1),
        name="mlstm_lat" if has_init else "mlstm_ctx",
    )(*args)


_DT_COL0 = N_DIR * 2 * H_A
_HG = H_B // G_B
_WG = _HG * P_B


def _conv3_lanes(ref, w_ref, b_ref, out_scr, seq):
    lane = lax.broadcasted_iota(jnp.int32, (CHUNK, seq), 1)
    for r in range(ref.shape[0] // CHUNK):
        rs = slice(r * CHUNK, (r + 1) * CHUNK)
        cur = ref[rs, :]
        prev = jnp.where(lane == 0, 0.0, pltpu.roll(cur, 1, axis=1))
        nxt = jnp.where(lane == seq - 1, 0.0, pltpu.roll(cur, seq - 1, axis=1))
        for c in range(seq // CHUNK):
            cs = slice(c * CHUNK, (c + 1) * CHUNK)
            acc = (w_ref[0, rs, :] * prev[:, cs] + w_ref[1, rs, :] * cur[:, cs] + w_ref[2, rs, :] * nxt[:, cs]
                   + b_ref[rs, :])
            out_scr[rs, cs] = _silu(acc).astype(out_scr.dtype)


def _ssd_kernel(*refs, seq, has_init, emit_state, layer):
    it = iter(refs)
    (zt_ref, xt_ref, ct_ref, b_ref, g_ref, gt_ref, gb_ref, gbt_ref, alog_ref, alogt_ref,
     cwx_ref, cbx_ref, cwc_ref, cbc_ref, cwb_ref, cbb_ref, dsk_ref, ng_ref) = (next(it) for _ in range(18))
    if has_init:
        h0_ref = next(it)
    if emit_state and layer > 0:
        prev_ref = next(it)
    out_ref = next(it)
    if emit_state:
        ho_ref = next(it)
    pad_scr, xs_scr, cs_scr, bs_scr, dtr_scr, br_scr, cumc_scr, yt_scr, h_scr = (next(it) for _ in range(9))
    n_chunks = seq // CHUNK
    pad = 8
    if emit_state and layer > 0:
        ho_ref[0, 0:layer] = prev_ref[0]

    _conv3_lanes(xt_ref, cwx_ref, cbx_ref, xs_scr, seq)
    _conv3_lanes(ct_ref, cwc_ref, cbc_ref, cs_scr, seq)
    nb = G_B * N_B
    pad_scr[0:pad, :] = jnp.zeros((pad, nb), F32)
    pad_scr[pad + seq:2 * pad + seq, :] = jnp.zeros((pad, nb), F32)
    pad_scr[pad:pad + seq, :] = b_ref[...]
    for c in range(n_chunks):
        acc = cbb_ref[...]
        for k in range(SSM_CONV):
            r = c * CHUNK + pad - SSM_CONV // 2 + k
            acc = acc + cwb_ref[k:k + 1, :] * pad_scr[r:r + CHUNK, :]
        bs_scr[c * CHUNK:(c + 1) * CHUNK, :] = _silu(acc).astype(bs_scr.dtype)

    lac = _softplus(g_ref[...] + gb_ref[...]) * (-jnp.exp(alog_ref[...]))
    r0 = _DT_COL0
    dtr = _softplus(gt_ref[r0:r0 + N_DIR * H_B, :] + gbt_ref[r0:r0 + N_DIR * H_B, :])
    dtr_scr[...] = dtr
    lar = dtr * (-jnp.exp(alogt_ref[r0:r0 + N_DIR * H_B, :]))
    masks_t = [_tri_mask(1 - d) for d in range(N_DIR)]
    for d in range(N_DIR):
        tri_t = masks_t[d].astype(BF16)
        tri_c = _tri_mask(d).astype(BF16)
        rows = slice(d * H_B, (d + 1) * H_B)
        for c in range(n_chunks):
            cs = slice(c * CHUNK, (c + 1) * CHUNK)
            br_scr[rows, cs] = _split_dot(lar[rows, cs], tri_t)
            cumc_scr[d, cs, :] = _tri_cumsum(tri_c, lac[cs, :])

    row_blk = lax.broadcasted_iota(jnp.int32, (_WG, 1), 0) // P_B

    def run_direction(d):
        for g in range(G_B):
            for hh in range(_HG):
                if has_init:
                    h_scr[d, g, hh * P_B:(hh + 1) * P_B, :] = h0_ref[0, 0, d, g * _HG + hh]
                else:
                    h_scr[d, g, hh * P_B:(hh + 1) * P_B, :] = jnp.zeros((P_B, N_B), F32)

        mask_t = masks_t[d]
        pos = CHUNK - 1 if d == 0 else 0

        for ci in range(n_chunks):
            c = ci if d == 0 else n_chunks - 1 - ci
            cs = slice(c * CHUNK, (c + 1) * CHUNK)
            cum_c = cumc_scr[d, cs, :]
            for g in range(G_B):
                bg = bs_scr[cs, g * N_B:(g + 1) * N_B]
                ctg = cs_scr[g * N_B:(g + 1) * N_B, cs]
                h_st = h_scr[d, g]
                res = _dot(jnp.concatenate([bg, h_st.astype(BF16)], axis=0), ctg)
                cbt = res[0:CHUNK]
                inter = res[CHUNK:CHUNK + _WG]
                xw = []
                decay = jnp.zeros((_WG, 1), F32)
                for hh in range(_HG):
                    head = g * _HG + hh
                    r = d * H_B + head
                    b_r = br_scr[r:r + 1, cs]
                    dt_r = dtr_scr[r:r + 1, cs]
                    b_c = cum_c[:, _DT_COL0 + r:_DT_COL0 + r + 1]
                    gt = b_r[:, pos:pos + 1]
                    ps = slice(head * P_B, (head + 1) * P_B)
                    xh = xs_scr[ps, cs]
                    st = (cbt * jnp.exp(jnp.where(mask_t, b_r - b_c, -jnp.inf))).astype(BF16)
                    y = _dot((xh * dt_r).astype(BF16), st) + jnp.exp(b_r) * inter[hh * P_B:(hh + 1) * P_B]
                    xw.append(xh * (jnp.exp(gt - b_r) * dt_r))
                    decay = jnp.where(row_blk == hh, jnp.exp(gt), decay)
                    if d == 0:
                        yt_scr[ps, cs] = y
                    else:
                        yt_scr[ps, cs] = yt_scr[ps, cs] + y + dsk_ref[ps, :] * xh
                h_scr[d, g] = decay * h_st + _dot(jnp.concatenate(xw, axis=0).astype(BF16), bg)
            if d == 1:
                yz = yt_scr[:, cs] * _silu(zt_ref[:, cs])
                rms = lax.rsqrt(jnp.sum(yz * yz, axis=0, keepdims=True) * (1.0 / W_B) + EPS)
                out_ref[:, cs] = (yz * rms * ng_ref[...]).astype(out_ref.dtype)

        if emit_state:
            for g in range(G_B):
                for hh in range(_HG):
                    ho_ref[0, layer, d, g * _HG + hh] = h_scr[d, g, hh * P_B:(hh + 1) * P_B, :]

    run_direction(0)
    run_direction(1)


def _ssd(proj, gates, proj_t, gate_bias, gate_bias_t, alog_row, alog_col, conv, d_rep, norm_g_rep,
         seq, n_seq, row_off, layer, init=None, prev=None):
    blk0 = row_off // seq
    has_init = init is not None
    emit_state = not has_init
    nb = G_B * N_B
    zr = 3 * W_A // W_B
    cr = (3 * W_A + 2 * W_B) // nb
    gr = (3 * W_A + 2 * W_B + nb) // LANES
    bc = (W_A + 2 * W_C) // nb
    const = lambda shape: pl.BlockSpec(shape, lambda s: (0,) * len(shape))
    st_spec = lambda n_layers, first: pl.BlockSpec((1, n_layers, N_DIR, H_B, P_B, N_B),
                                                   lambda s: (s, first, 0, 0, 0, 0))
    in_specs = [pl.BlockSpec((W_B, seq), lambda s: (zr, s + blk0)),
                pl.BlockSpec((W_B, seq), lambda s: (zr + 1, s + blk0)),
                pl.BlockSpec((nb, seq), lambda s: (cr, s + blk0)),
                pl.BlockSpec((seq, nb), lambda s: (s + blk0, bc)),
                pl.BlockSpec((seq, LANES), lambda s: (s + blk0, 0)),
                pl.BlockSpec((LANES, seq), lambda s: (gr, s + blk0)),
                const((1, LANES)), const((LANES, 1)), const((1, LANES)), const((LANES, 1)),
                const((SSM_CONV, W_B, LANES)), const((W_B, LANES)), const((SSM_CONV, nb, LANES)), const((nb, LANES)),
                const((8, nb)), const((1, nb)), const((W_B, LANES)), const((W_B, LANES))]
    args = [proj_t, proj_t, proj_t, proj, gates, proj_t, gate_bias, gate_bias_t, alog_row, alog_col,
            *conv, d_rep, norm_g_rep]
    if has_init:
        in_specs.append(st_spec(1, layer))
        args.append(init)
    if emit_state and layer > 0:
        in_specs.append(st_spec(layer, 0))
        args.append(prev)
    out_specs = [pl.BlockSpec((W_B, seq), lambda s: (0, s))]
    out_shape = [jax.ShapeDtypeStruct((W_B, n_seq * seq), BF16)]
    if emit_state:
        out_specs.append(st_spec(layer + 1, 0))
        out_shape.append(jax.ShapeDtypeStruct((n_seq, layer + 1, N_DIR, H_B, P_B, N_B), F32))
    return pl.pallas_call(
        functools.partial(_ssd_kernel, seq=seq, has_init=has_init, emit_state=emit_state, layer=layer),
        grid=(n_seq,),
        in_specs=in_specs,
        out_specs=out_specs,
        out_shape=out_shape,
        scratch_shapes=[pltpu.VMEM((seq + 16, nb), F32),
                        pltpu.VMEM((W_B, seq), F32),
                        pltpu.VMEM((nb, seq), BF16),
                        pltpu.VMEM((seq, nb), BF16),
                        pltpu.VMEM((N_DIR * H_B, seq), F32),
                        pltpu.VMEM((N_DIR * H_B, seq), F32),
                        pltpu.VMEM((N_DIR, seq, LANES), F32),
                        pltpu.VMEM((W_B, seq), F32),
                        pltpu.VMEM((N_DIR, G_B, _WG, N_B), F32)],
        compiler_params=_cparams(1),
        name="ssd_lat" if has_init else "ssd_ctx",
    )(*args)


def _conf_kernel(a_ref, ap_ref, an_ref, g_ref, gp_ref, gn_ref, w_ref, b_ref, lg_ref, lb_ref, out_ref, pad_scr,
                 rot_scr):
    i = pl.program_id(0)
    n_ctx_blk = N_CTX_TOK // CONV_ROWS
    blk_per_lat = DEC_SEQ // CONV_ROWS
    j = (i - n_ctx_blk) % blk_per_lat
    is_ctx = i < n_ctx_blk
    keep_prev = jnp.where(is_ctx | (j == 0), 0.0, 1.0)
    keep_next = jnp.where(is_ctx | (j == blk_per_lat - 1), 0.0, 1.0)
    pad_scr[0:HALO, :] = ap_ref[...] * _sigmoid(gp_ref[...]) * keep_prev
    pad_scr[HALO:HALO + CONV_ROWS, :] = a_ref[...] * _sigmoid(g_ref[...])
    pad_scr[HALO + CONV_ROWS:2 * HALO + CONV_ROWS, :] = an_ref[...] * _sigmoid(gn_ref[...]) * keep_next
    n_rot = CONV_ROWS + 2 * HALO - 8
    for s in range(1, 8):
        rot_scr[s, :, :] = pad_scr[s:s + n_rot, :]
    rc = 64
    for c in range(CONV_ROWS // rc):
        acc = jnp.broadcast_to(b_ref[...], (rc, W_C))
        for k in range(CONV_W):
            off = HALO - CONV_W // 2 + k
            r = c * rc + off - off % 8
            src = pad_scr[r:r + rc, :] if off % 8 == 0 else rot_scr[off % 8, r:r + rc, :]
            acc = acc + w_ref[k:k + 1, :] * src
        u = _ln(acc) * lg_ref[...] + lb_ref[...]
        out_ref[c * rc:(c + 1) * rc, :] = _silu(u).astype(out_ref.dtype)


def _conf(proj, dw_w, dw_b, ln_g, ln_b):
    assert SEQ == CONV_ROWS and DEC_SEQ % CONV_ROWS == 0
    ac = W_A // W_C
    hb = CONV_ROWS // HALO
    n_halo = T_ALL // HALO
    const = lambda shape: pl.BlockSpec(shape, lambda i: (0,) * len(shape))

    def specs(c):
        return [pl.BlockSpec((CONV_ROWS, W_C), lambda i: (i, c)),
                pl.BlockSpec((HALO, W_C), lambda i: (jnp.maximum(i * hb - 1, 0), c)),
                pl.BlockSpec((HALO, W_C), lambda i: (jnp.minimum((i + 1) * hb, n_halo - 1), c))]

    return pl.pallas_call(
        _conf_kernel,
        grid=(T_ALL // CONV_ROWS,),
        in_specs=specs(ac) + specs(ac + 1) + [const((32, W_C)), const((1, W_C)), const((1, W_C)), const((1, W_C))],
        out_specs=pl.BlockSpec((CONV_ROWS, W_C), lambda i: (i, 0)),
        out_shape=jax.ShapeDtypeStruct((T_ALL, W_C), BF16),
        scratch_shapes=[pltpu.VMEM((CONV_ROWS + 2 * HALO, W_C), F32),
                        pltpu.VMEM((8, CONV_ROWS + 2 * HALO - 8, W_C), F32)],
        compiler_params=_cparams(1),
        name="conf",
    )(proj, proj, proj, proj, proj, proj, dw_w, dw_b, ln_g, ln_b)


_PAIRS = [(a, b) for a in range(EXPERTS_PER_GROUP) for b in range(a + 1, EXPERTS_PER_GROUP)]
N_CLASSES = N_EGROUPS * len(_PAIRS)
N_FCHUNK = D_MODEL // LANES
N_IN_ROWS = 2 * N_FCHUNK


def _store_token_tiles(ref, val):
    n = val.shape[0]
    for c in range(N_FCHUNK):
        ref[pl.ds(c, n, stride=N_FCHUNK), :] = val[:, c * LANES:(c + 1) * LANES]


def _load_token_tiles(ref, n):
    return jnp.concatenate([ref[pl.ds(c, n, stride=N_FCHUNK), :] for c in range(N_FCHUNK)], axis=1)
N_MOE_ROWS = T_ALL + N_CLASSES * TM_MOE
N_MOE_TILES = N_MOE_ROWS // TM_MOE


def _outproj_kernel(*refs, tm, n_x):
    x_refs = refs[:n_x]
    (hac_ref, hal_ref, hbc_ref, hbl_ref, hc_ref, mod_ref, wo_ref, pg_ref, pb_ref,
     wr_ref, br_ref, x1_ref, u2_ref, rt_ref) = refs[n_x:]
    i = pl.program_id(0)
    r = _mod_row(i * tm)
    gate1 = mod_ref[pl.ds(r, 1), 2 * D_MODEL:3 * D_MODEL]
    shift2 = mod_ref[pl.ds(r, 1), 3 * D_MODEL:4 * D_MODEL]
    scale2 = mod_ref[pl.ds(r, 1), 4 * D_MODEL:5 * D_MODEL]
    is_ctx = i < N_CTX_TOK // tm
    ha = jnp.where(is_ctx, hac_ref[...], hal_ref[...])
    hb = jnp.where(is_ctx, hbc_ref[...], hbl_ref[...])
    mix = (_dotg(ha, wo_ref[0:W_A, :], _TN) + _dotg(hb, wo_ref[W_A:W_A + W_B, :], _TN)
           + _dot(hc_ref[...], wo_ref[W_A + W_B:D_MIX, :]))
    x1 = _ln(ALPHA * _read_x(x_refs, i, tm) + gate1 * mix) * pg_ref[...] + pb_ref[...]
    x1_ref[...] = x1
    u2 = _ln(x1) * (1.0 + scale2) + shift2
    for c in range(N_FCHUNK):
        u2_ref[pl.ds(c, tm, stride=N_IN_ROWS), :] = u2[:, c * LANES:(c + 1) * LANES]

    wr = wr_ref[...]
    wr_hi = wr.astype(BF16)
    wr_lo = (wr - wr_hi.astype(F32)).astype(BF16)
    u2_hi = u2.astype(BF16)
    u2_lo = (u2 - u2_hi.astype(F32)).astype(BF16)
    both = _dotg(jnp.concatenate([wr_hi, wr_lo], axis=0), u2_hi, _NT)
    logits = both[0:N_EXPERTS] + both[N_EXPERTS:2 * N_EXPERTS] + _dotg(wr_hi, u2_lo, _NT) + br_ref[...]
    ex = jnp.exp(logits - jnp.max(logits, axis=0, keepdims=True))
    probs = ex / jnp.sum(ex, axis=0, keepdims=True)
    scores = []
    for g in range(N_EGROUPS):
        p = [probs[g * EXPERTS_PER_GROUP + e:g * EXPERTS_PER_GROUP + e + 1, :] for e in range(EXPERTS_PER_GROUP)]
        best = p[0] + p[1]
        for a in range(EXPERTS_PER_GROUP):
            for b in range(a + 1, EXPERTS_PER_GROUP):
                if (a, b) != (0, 1):
                    best = jnp.maximum(best, p[a] + p[b])
        scores.append(best)
    gmax = functools.reduce(jnp.maximum, scores)
    sel = jnp.full(gmax.shape, N_EGROUPS - 1, jnp.int32)
    for g in range(N_EGROUPS - 2, -1, -1):
        sel = jnp.where(scores[g] == gmax, g, sel)
    eidx = lax.broadcasted_iota(jnp.int32, probs.shape, 0)
    pm = jnp.where((eidx // EXPERTS_PER_GROUP) == sel, probs, -jnp.inf)
    p1 = jnp.max(pm, axis=0, keepdims=True)
    i1 = jnp.min(jnp.where(pm == p1, eidx, N_EXPERTS), axis=0, keepdims=True)
    pm2 = jnp.where(eidx == i1, -jnp.inf, pm)
    p2 = jnp.max(pm2, axis=0, keepdims=True)
    i2 = jnp.min(jnp.where(pm2 == p2, eidx, N_EXPERTS), axis=0, keepdims=True)
    den = p1 + p2
    first_lo = i1 < i2
    w_lo = jnp.where(first_lo, p1, p2) / den
    w_hi = jnp.where(first_lo, p2, p1) / den
    a = jnp.minimum(i1, i2) - sel * EXPERTS_PER_GROUP
    b = jnp.maximum(i1, i2) - sel * EXPERTS_PER_GROUP
    pair = jnp.zeros_like(a)
    for k, (pa, pb_) in enumerate(_PAIRS):
        pair = jnp.where((a == pa) & (b == pb_), k, pair)
    cls = sel * len(_PAIRS) + pair
    rt_ref[...] = jnp.broadcast_to(cls, rt_ref.shape)
    wrow = lax.broadcasted_iota(jnp.int32, (LANES, tm), 0)
    wt = jnp.where(wrow == 0, w_lo, jnp.where(wrow == 1, w_hi, 0.0))
    u2_ref[pl.ds(N_FCHUNK, tm, stride=N_IN_ROWS), :] = wt.T
    for c in range(N_FCHUNK + 1, N_IN_ROWS):
        u2_ref[pl.ds(c, tm, stride=N_IN_ROWS), :] = jnp.zeros((tm, LANES), F32)


def _outproj(x_src, ha_c, ha_l, hb_c, hb_l, hc, mod, w_o, pg, pb, w_rt, b_r):
    tm = TM_PROJ
    n_ctx_blk = N_CTX_TOK // tm
    const = lambda shape: pl.BlockSpec(shape, lambda i: (0,) * len(shape))
    rows = lambda w: pl.BlockSpec((tm, w), lambda i: (i, 0))
    ctx_feat = lambda w: pl.BlockSpec((w, tm), lambda i: (0, jnp.minimum(i, n_ctx_blk - 1)))
    lat_feat = lambda w: pl.BlockSpec((w, tm), lambda i: (0, jnp.maximum(i - n_ctx_blk, 0)))
    return pl.pallas_call(
        functools.partial(_outproj_kernel, tm=tm, n_x=len(x_src)),
        grid=(T_ALL // tm,),
        in_specs=_x_specs(x_src, tm) + [
                  ctx_feat(W_A), lat_feat(W_A), ctx_feat(W_B), lat_feat(W_B),
                  rows(W_C), const((N_MOD_ROWS, 6 * D_MODEL)),
                  const((D_MIX, D_MODEL)), const((1, D_MODEL)), const((1, D_MODEL)),
                  const((N_EXPERTS, D_MODEL)), const((N_EXPERTS, 1))],
        out_specs=[rows(D_MODEL), pl.BlockSpec((tm * N_IN_ROWS, LANES), lambda i: (i, 0)),
                   pl.BlockSpec((8, tm), lambda i: (0, i))],
        out_shape=[jax.ShapeDtypeStruct((T_ALL, D_MODEL), F32),
                   jax.ShapeDtypeStruct((T_ALL * N_IN_ROWS, LANES), F32),
                   jax.ShapeDtypeStruct((8, T_ALL), jnp.int32)],
        compiler_params=_cparams(1),
        name="outproj",
    )(*x_src, ha_c, ha_l, hb_c, hb_l, hc, mod, w_o, pg, pb, w_rt, b_r)


N_Y_ROWS = T_ALL + 2 * TM_MOE


def _moe_kernel(tok_ref, tstart_ref, nval_ref, tlo_ref, thi_ref, nused_ref,
                u_hbm, wgl_ref, wul_ref, wdl_ref, wgh_ref, wuh_ref, wdh_ref, y_hbm,
                xbuf0, xbuf1, ybuf0, ybuf1, gsem, ssem):
    del tlo_ref, thi_ref
    i = pl.program_id(0)
    n_used = nused_ref[0]
    tile_rows = TM_MOE * N_FCHUNK

    def token_tile(ref, tok, rows=N_FCHUNK):
        return ref.at[pl.ds(pl.multiple_of(tok * rows, rows), rows), :]

    def gather_rows(t, xb, sem):
        base = tstart_ref[t]
        for r in range(TM_MOE):
            pltpu.make_async_copy(token_tile(u_hbm, tok_ref[base + r], N_IN_ROWS), token_tile(xb, r, N_IN_ROWS),
                                  sem).start()

    def scatter_rows(t, n_valid, yb, sem, spare):
        base = tstart_ref[t]
        for r in range(TM_MOE):
            dst = jnp.where(r < n_valid, tok_ref[base + r], spare + r)
            pltpu.make_async_copy(token_tile(yb, r), token_tile(y_hbm, dst), sem).start()

    def load_tile(xb):
        x = jnp.concatenate([xb[pl.ds(c, TM_MOE, stride=N_IN_ROWS), :] for c in range(N_FCHUNK)], axis=1)
        w = xb[pl.ds(N_FCHUNK, TM_MOE, stride=N_IN_ROWS), :]
        return x.astype(BF16), w[:, 0:1], w[:, 1:2]

    def wait_gather(xb, sem):
        pltpu.make_async_copy(u_hbm.at[pl.ds(0, TM_MOE * N_IN_ROWS), :], xb, sem).wait()

    def wait_scatter(yb, sem):
        pltpu.make_async_copy(yb, y_hbm.at[pl.ds(0, tile_rows), :], sem).wait()

    def step(s, xb, xb_next, yb, yb_prev):
        @pl.when(i > 0)
        def _():
            wait_scatter(yb, ssem.at[s])

        wait_gather(xb, gsem.at[s])

        @pl.when(n_used > 0)
        def _():
            gather_rows(jnp.minimum(i + 1, n_used - 1), xb_next, gsem.at[1 - s])
            prev = jnp.maximum(i - 1, 0)
            scatter_rows(prev, jnp.where(i > 0, nval_ref[prev], 0), yb_prev, ssem.at[1 - s],
                         T_ALL + (1 - s) * TM_MOE)

        x, w_lo, w_hi = load_tile(xb)

        def ffn(wg_ref, wu_ref, wd_ref, w):
            act = _silu(_dot(x, wg_ref[0, 0].astype(BF16))) * _dot(x, wu_ref[0, 0].astype(BF16)) * w
            return _dot(act.astype(BF16), wd_ref[0, 0].astype(BF16))

        _store_token_tiles(yb, ffn(wgl_ref, wul_ref, wdl_ref, w_lo) + ffn(wgh_ref, wuh_ref, wdh_ref, w_hi))

        @pl.when(i == n_used - 1)
        def _():
            scatter_rows(i, nval_ref[i], yb, ssem.at[s], T_ALL + s * TM_MOE)
            wait_scatter(yb, ssem.at[s])
            wait_scatter(yb_prev, ssem.at[1 - s])
            wait_gather(xb_next, gsem.at[1 - s])

    @pl.when(i == 0)
    def _():
        ybuf1[...] = jnp.zeros(ybuf1.shape, F32)
        fill = pltpu.make_async_copy(ybuf1, y_hbm.at[pl.ds(T_ALL * N_FCHUNK, tile_rows), :], ssem.at[0])
        fill.start()
        fill.wait()
        gather_rows(0, xbuf0, gsem.at[0])

    @pl.when((i < n_used) & (i % 2 == 0))
    def _():
        step(0, xbuf0, xbuf1, ybuf0, ybuf1)

    @pl.when((i < n_used) & (i % 2 == 1))
    def _():
        step(1, xbuf1, xbuf0, ybuf1, ybuf0)


def _moe(u2t, tok_sorted, tile_start, n_valid, tile_lo, tile_hi, n_used, w_gate, w_up, w_down, layer):
    lo = lambda shape: pl.BlockSpec(shape, lambda i, tok, ts, nv, tlo, thi, nused: (layer, tlo[i], 0, 0))
    hi = lambda shape: pl.BlockSpec(shape, lambda i, tok, ts, nv, tlo, thi, nused: (layer, thi[i], 0, 0))
    up_shape = (1, 1, D_MODEL, D_FF_EXPERT)
    down_shape = (1, 1, D_FF_EXPERT, D_MODEL)
    in_buf = pltpu.VMEM((TM_MOE * N_IN_ROWS, LANES), F32)
    out_buf = pltpu.VMEM((TM_MOE * N_FCHUNK, LANES), F32)
    grid_spec = pltpu.PrefetchScalarGridSpec(
        num_scalar_prefetch=6,
        grid=(N_MOE_TILES,),
        in_specs=[pl.BlockSpec(memory_space=pl.ANY),
                  lo(up_shape), lo(up_shape), lo(down_shape),
                  hi(up_shape), hi(up_shape), hi(down_shape)],
        out_specs=pl.BlockSpec(memory_space=pl.ANY),
        scratch_shapes=[in_buf, in_buf, out_buf, out_buf,
                        pltpu.SemaphoreType.DMA((2,)),
                        pltpu.SemaphoreType.DMA((2,))],
    )
    return pl.pallas_call(
        _moe_kernel,
        grid_spec=grid_spec,
        out_shape=jax.ShapeDtypeStruct((N_Y_ROWS * N_FCHUNK, LANES), F32),
        compiler_params=pltpu.CompilerParams(dimension_semantics=("arbitrary",), vmem_limit_bytes=BIG_VMEM_LIMIT,
                                             has_side_effects=True),
        name="moe",
    )(tok_sorted, tile_start, n_valid, tile_lo, tile_hi, n_used, u2t, w_gate, w_up, w_down, w_gate, w_up, w_down)


def _route_tables(cls):
    _, tok_sorted = lax.sort((cls, jnp.arange(T_ALL, dtype=jnp.int32)), num_keys=1)
    cids = jnp.arange(N_CLASSES, dtype=jnp.int32)
    counts = jnp.sum((cls[:, None] == cids[None, :]).astype(jnp.int32), axis=0)
    offs = jnp.cumsum(counts) - counts
    padded = (counts + TM_MOE - 1) // TM_MOE * TM_MOE
    ends = jnp.cumsum(padded)
    offs_p = ends - padded
    n_used = ends[-1] // TM_MOE
    tile_start = jnp.arange(N_MOE_TILES, dtype=jnp.int32) * TM_MOE
    tile_cls = jnp.sum((ends[None, :] <= jnp.minimum(tile_start, ends[-1] - TM_MOE)[:, None]).astype(jnp.int32),
                       axis=1)
    onehot = (tile_cls[:, None] == cids[None, :]).astype(jnp.int32)
    pick = lambda table: jnp.sum(onehot * table[None, :], axis=1)
    k = tile_start - pick(offs_p)
    n_valid = jnp.where(tile_start < ends[-1], jnp.clip(pick(counts) - k, 0, TM_MOE), 0)
    tile_first = jnp.clip(pick(offs) + k, 0, T_ALL - 1)
    tile_lo = pick(cids // len(_PAIRS) * EXPERTS_PER_GROUP + jnp.array([p[0] for p in _PAIRS] * N_EGROUPS, jnp.int32))
    tile_hi = pick(cids // len(_PAIRS) * EXPERTS_PER_GROUP + jnp.array([p[1] for p in _PAIRS] * N_EGROUPS, jnp.int32))
    tok_sorted = jnp.concatenate([tok_sorted, jnp.zeros((TM_MOE,), jnp.int32)])
    i32 = lambda v: v.astype(jnp.int32)
    return tok_sorted, i32(tile_first), i32(n_valid), i32(tile_lo), i32(tile_hi), i32(n_used).reshape(1)


def _final_kernel(x1_ref, y_ref, mod_ref, pg_ref, pb_ref, *o_refs, tm):
    i = pl.program_id(0)
    r = _mod_row(i * tm)
    gate2 = mod_ref[pl.ds(r, 1), 5 * D_MODEL:6 * D_MODEL]
    out = _ln(ALPHA * x1_ref[...] + gate2 * _load_token_tiles(y_ref, tm)) * pg_ref[...] + pb_ref[...]
    if len(o_refs) == 1:
        o_refs[0][...] = out
    else:
        @pl.when(i < N_CTX_TOK // tm)
        def _():
            o_refs[0][...] = out

        @pl.when(i >= N_CTX_TOK // tm)
        def _():
            o_refs[1][...] = out


def _final(x1, y, mod, pg, pb, split):
    tm = TM_PROJ
    n_ctx_blk = N_CTX_TOK // tm
    const = lambda shape: pl.BlockSpec(shape, lambda i: (0,) * len(shape))
    rows = pl.BlockSpec((tm, D_MODEL), lambda i: (i, 0))
    if split:
        out_specs = [pl.BlockSpec((tm, D_MODEL), lambda i: (jnp.minimum(i, n_ctx_blk - 1), 0)),
                     pl.BlockSpec((tm, D_MODEL), lambda i: (jnp.maximum(i - n_ctx_blk, 0), 0))]
        out_shape = [jax.ShapeDtypeStruct((N_CTX_TOK, D_MODEL), F32), jax.ShapeDtypeStruct((N_LAT_TOK, D_MODEL), F32)]
    else:
        out_specs = [rows]
        out_shape = [jax.ShapeDtypeStruct((T_ALL, D_MODEL), F32)]
    return pl.pallas_call(
        functools.partial(_final_kernel, tm=tm),
        grid=(T_ALL // tm,),
        in_specs=[rows, pl.BlockSpec((tm * N_FCHUNK, LANES), lambda i: (i, 0)),
                  const((N_MOD_ROWS, 6 * D_MODEL)), const((1, D_MODEL)), const((1, D_MODEL))],
        out_specs=out_specs,
        out_shape=out_shape,
        compiler_params=_cparams(1),
        name="final",
    )(x1, y, mod, pg, pb)


def _grid_pos(n_tok):
    rows = n_tok // GRID_W
    r, col = jnp.meshgrid(jnp.arange(rows, dtype=F32), jnp.arange(GRID_W, dtype=F32), indexing='ij')
    quarter = D_MODEL // 4
    omega = 1.0 / (10000.0 ** (jnp.arange(quarter, dtype=F32) / quarter))

    def emb(p):
        ang = p.reshape(-1)[:, None] * omega[None, :]
        return jnp.concatenate([jnp.sin(ang), jnp.cos(ang)], axis=-1)

    return jnp.concatenate([emb(r), emb(col)], axis=-1)


def _pad_lanes(v, start):
    v = v.reshape(1, -1).astype(F32)
    return jnp.pad(v, ((0, 0), (start, LANES - start - v.shape[1])))


def kernel(x_prompt, x_sample, state_mlstm_C, state_mlstm_n, state_mlstm_m, state_ssd, c, c_ctx, w_in, w_o, mlstm_b_i, mlstm_b_f, mlstm_norm_g, ssd_conv_w, ssd_conv_b, ssd_dt_bias, ssd_A_log, ssd_D, ssd_norm_g, conv_dw_w, conv_dw_b, conv_ln_g, conv_ln_b, w_ada, b_ada, post1_g, post1_b, post2_g, post2_b, w_router, b_router, w_e_gate, w_e_up, w_e_down):
    cvec = jnp.concatenate([c_ctx[None, :], c, jnp.zeros((N_MOD_ROWS - 1 - DEC_BATCH, D_MODEL), F32)], axis=0)
    mod_all = _ada(cvec, w_ada, b_ada)
    x_src = (x_prompt.reshape(N_CTX_TOK, D_MODEL), x_sample.reshape(N_LAT_TOK, D_MODEL), _grid_pos(DEC_SEQ))
    w_rt = w_router.T
    b_r = b_router.reshape(N_EXPERTS, 1)

    a_end = 4 * W_A + N_DIR * 2 * H_A
    b_end = a_end + W_B + W_XBC + N_DIR * H_B
    init = (state_mlstm_C,
            state_mlstm_n.reshape(DEC_BATCH, DEPTH, N_DIR * H_A, DH_A),
            jnp.broadcast_to(state_mlstm_m.reshape(DEC_BATCH, DEPTH, N_DIR * H_A, 1),
                             (DEC_BATCH, DEPTH, N_DIR * H_A, LANES)))
    st_c = st_n = st_m = st_h = None
    for l in range(DEPTH):
        w = w_in[l]
        xbc0 = a_end + W_B
        w_main = jnp.concatenate([w[:, W_A:2 * W_A], w[:, b_end:], w[:, xbc0 + W_B:xbc0 + W_B + G_B * N_B]],
                                 axis=1).astype(BF16)
        w_small = jnp.concatenate([w[:, 4 * W_A:a_end], w[:, b_end - N_DIR * H_B:b_end],
                                   jnp.zeros((D_MODEL, LANES - _DT_COL0 - N_DIR * H_B), F32)], axis=1)
        w_t = jnp.concatenate([w[:, 0:W_A], w[:, 2 * W_A:4 * W_A], w[:, a_end:xbc0 + W_B],
                               w[:, xbc0 + W_B + G_B * N_B:xbc0 + W_XBC], w_small],
                              axis=1).T.astype(BF16)
        w_small = w_small.astype(BF16)
        gate_bias = (_pad_lanes(jnp.stack([mlstm_b_i[l], mlstm_b_f[l]], axis=1), 0)
                     + _pad_lanes(ssd_dt_bias[l], _DT_COL0))
        alog_row = _pad_lanes(ssd_A_log[l], _DT_COL0)
        mod = mod_all[l]

        proj, gates, proj_t = _inproj(x_src, mod, w_main, w_small, w_t)

        m_norm = jnp.broadcast_to(mlstm_norm_g[l].reshape(W_A, 1), (W_A, LANES))
        mlstm_args = (proj, gates, proj_t, gate_bias, gate_bias.reshape(LANES, 1), m_norm)
        ha_c, st_c, st_n, st_m = _mlstm(*mlstm_args, SEQ, BATCH, 0, l, prev=(st_c, st_n, st_m))
        (ha_l,) = _mlstm(*mlstm_args, DEC_SEQ, DEC_BATCH, N_CTX_TOK, l, init=init)

        nb = G_B * N_B
        rep = lambda v: jnp.broadcast_to(v[..., None], v.shape + (LANES,))
        cw, cb = ssd_conv_w[l], ssd_conv_b[l]
        conv = (rep(cw[:, 0:W_B]), rep(cb[0:W_B]), rep(cw[:, W_B + nb:W_XBC]), rep(cb[W_B + nb:W_XBC]),
                jnp.pad(cw[:, W_B:W_B + nb], ((0, 8 - SSM_CONV), (0, 0))), cb[W_B:W_B + nb].reshape(1, nb))
        ssd_args = (proj, gates, proj_t, gate_bias, gate_bias.reshape(LANES, 1), alog_row, alog_row.reshape(LANES, 1),
                    conv, rep(jnp.repeat(ssd_D[l], P_B)), rep(ssd_norm_g[l]))
        hb_c, st_h = _ssd(*ssd_args, SEQ, BATCH, 0, l, prev=st_h)
        (hb_l,) = _ssd(*ssd_args, DEC_SEQ, DEC_BATCH, N_CTX_TOK, l, init=state_ssd)

        hc = _conf(proj, jnp.pad(conv_dw_w[l], ((0, 32 - CONV_W), (0, 0))), conv_dw_b[l].reshape(1, W_C),
                   conv_ln_g[l].reshape(1, W_C), conv_ln_b[l].reshape(1, W_C))

        x1, u2t, route = _outproj(x_src, ha_c, ha_l, hb_c, hb_l, hc, mod, w_o[l].astype(BF16), post1_g[l].reshape(1, D_MODEL),
                                  post1_b[l].reshape(1, D_MODEL), w_rt, b_r)
        y = _moe(u2t, *_route_tables(route[0]), w_e_gate, w_e_up, w_e_down, l)
        outs = _final(x1, y, mod, post2_g[l].reshape(1, D_MODEL), post2_b[l].reshape(1, D_MODEL), l == DEPTH - 1)
        x_src = (outs[0],)

    y_prompt = outs[0].reshape(BATCH, SEQ, D_MODEL)
    y_sample = outs[1].reshape(DEC_BATCH, DEC_SEQ, D_MODEL)
    return (y_prompt, y_sample, st_c, st_n.reshape(BATCH, DEPTH, N_DIR, H_A, DH_A),
            st_m[:, :, :, 0].reshape(BATCH, DEPTH, N_DIR, H_A), st_h)
```

```python
import functools

import jax
import jax.numpy as jnp
from jax import lax
from jax.experimental import pallas as pl
from jax.experimental.pallas import tpu as pltpu

D_MODEL = 1024
BATCH = 32
SEQ = 256
DEPTH = 2
DEC_BATCH = 2
DEC_SEQ = 1024
GRID_W = 64
N_DIR = 2
CHUNK = 128
H_A = 4
DH_A = 128
W_A = H_A * DH_A
H_B = 8
P_B = 64
W_B = H_B * P_B
G_B = 2
N_B = 128
W_XBC = W_B + 2 * G_B * N_B
SSM_CONV = 3
W_C = 512
CONV_W = 31
D_MIX = W_A + W_B + W_C
N_EXPERTS = 16
N_EGROUPS = 4
EXPERTS_PER_GROUP = N_EXPERTS // N_EGROUPS
D_FF_EXPERT = 512
ALPHA = (2 * DEPTH) ** 0.25
EPS = 1e-5
F32 = jnp.float32
BF16 = jnp.bfloat16

N_CTX_TOK = BATCH * SEQ
N_LAT_TOK = DEC_BATCH * DEC_SEQ
T_ALL = N_CTX_TOK + N_LAT_TOK
N_MOD_ROWS = 8
D_MAIN = W_A + 2 * W_C + G_B * N_B
D_T = 3 * W_A + 2 * W_B + G_B * N_B + 128
LANES = 128
HALO = 16
CONV_ROWS = 256
TM_PROJ = 512
TM_MOE = 256
ROW_GROUP = 64
VMEM_LIMIT = 48 * 1024 * 1024
BIG_VMEM_LIMIT = 56 * 1024 * 1024

_NT = (((1,), (1,)), ((), ()))
_TN = (((0,), (0,)), ((), ()))


def _ln(x):
    mu = jnp.mean(x, axis=-1, keepdims=True)
    xc = x - mu
    var = jnp.mean(xc * xc, axis=-1, keepdims=True)
    return xc * lax.rsqrt(var + EPS)


def _sigmoid(x):
    return 1.0 / (1.0 + jnp.exp(-x))


def _silu(x):
    return x * _sigmoid(x)


def _softplus(x):
    return jnp.maximum(x, 0.0) + jnp.log1p(jnp.exp(-jnp.abs(x)))


def _dot(a, b):
    return jnp.dot(a, b, preferred_element_type=F32)


def _dotg(a, b, dims):
    return lax.dot_general(a, b, dims, preferred_element_type=F32)


def _tri_cumsum(tri, x):
    hi = x.astype(BF16)
    r1 = x - hi.astype(F32)
    mid = r1.astype(BF16)
    lo = (r1 - mid.astype(F32)).astype(BF16)
    return _dot(tri, hi) + _dot(tri, mid) + _dot(tri, lo)


def _tri_mask(d):
    row = lax.broadcasted_iota(jnp.int32, (CHUNK, CHUNK), 0)
    col = lax.broadcasted_iota(jnp.int32, (CHUNK, CHUNK), 1)
    return (row >= col) if d == 0 else (row <= col)


def _mod_row(row_start):
    return jnp.where(row_start < N_CTX_TOK, 0, 1 + (row_start - N_CTX_TOK) // DEC_SEQ)


def _cparams(n_axes, vmem_limit=VMEM_LIMIT):
    return pltpu.CompilerParams(dimension_semantics=("arbitrary",) * n_axes, vmem_limit_bytes=vmem_limit)


def _ada_kernel(c_ref, w_ref, b_ref, o_ref):
    o_ref[0] = _dot(_silu(c_ref[...]), w_ref[0]) + b_ref[0]


def _ada(cvec, w_ada, b_ada):
    tn = 1536
    return pl.pallas_call(
        _ada_kernel,
        grid=(DEPTH, 6 * D_MODEL // tn),
        in_specs=[
            pl.BlockSpec((N_MOD_ROWS, D_MODEL), lambda l, j: (0, 0)),
            pl.BlockSpec((1, D_MODEL, tn), lambda l, j: (l, 0, j)),
            pl.BlockSpec((1, 1, tn), lambda l, j: (l, 0, j)),
        ],
        out_specs=pl.BlockSpec((1, N_MOD_ROWS, tn), lambda l, j: (l, 0, j)),
        out_shape=jax.ShapeDtypeStruct((DEPTH, N_MOD_ROWS, 6 * D_MODEL), F32),
        compiler_params=_cparams(2),
        name="ada",
    )(cvec, w_ada, b_ada.reshape(DEPTH, 1, 6 * D_MODEL))


def _x_specs(x_src, tm):
    if len(x_src) == 1:
        return [pl.BlockSpec((tm, D_MODEL), lambda i: (i, 0))]
    n_ctx_blk = N_CTX_TOK // tm
    per_seq = DEC_SEQ // tm
    return [pl.BlockSpec((tm, D_MODEL), lambda i: (jnp.minimum(i, n_ctx_blk - 1), 0)),
            pl.BlockSpec((tm, D_MODEL), lambda i: (jnp.maximum(i - n_ctx_blk, 0), 0)),
            pl.BlockSpec((tm, D_MODEL), lambda i: (jnp.where(i < n_ctx_blk, 0, (i - n_ctx_blk) % per_seq), 0))]


def _read_x(x_refs, i, tm):
    if len(x_refs) == 1:
        return x_refs[0][...]
    xp_ref, xs_ref, pos_ref = x_refs
    return jnp.where(i < N_CTX_TOK // tm, xp_ref[...], xs_ref[...] + pos_ref[...])


def _inproj_kernel(*refs, tm, n_x):
    x_refs = refs[:n_x]
    mod_ref, wm_ref, wg_ref, wt_ref, om_ref, og_ref, ot_ref = refs[n_x:]
    i = pl.program_id(0)
    r = _mod_row(i * tm)
    shift = mod_ref[pl.ds(r, 1), 0:D_MODEL]
    scale = mod_ref[pl.ds(r, 1), D_MODEL:2 * D_MODEL]
    u = (_ln(_read_x(x_refs, i, tm)) * (1.0 + scale) + shift).astype(BF16)
    om_ref[...] = _dot(u, wm_ref[...])
    og_ref[...] = _dot(u, wg_ref[...])
    ot_ref[...] = _dotg(wt_ref[...], u, _NT)


def _inproj(x_src, mod, w_main, w_small, w_t):
    tm = TM_PROJ
    const = lambda shape: pl.BlockSpec(shape, lambda i: (0, 0))
    return pl.pallas_call(
        functools.partial(_inproj_kernel, tm=tm, n_x=len(x_src)),
        grid=(T_ALL // tm,),
        in_specs=_x_specs(x_src, tm) + [
            const((N_MOD_ROWS, 6 * D_MODEL)), const((D_MODEL, D_MAIN)), const((D_MODEL, LANES)),
            const((D_T, D_MODEL)),
        ],
        out_specs=[
            pl.BlockSpec((tm, D_MAIN), lambda i: (i, 0)),
            pl.BlockSpec((tm, LANES), lambda i: (i, 0)),
            pl.BlockSpec((D_T, tm), lambda i: (0, i)),
        ],
        out_shape=[
            jax.ShapeDtypeStruct((T_ALL, D_MAIN), F32),
            jax.ShapeDtypeStruct((T_ALL, LANES), F32),
            jax.ShapeDtypeStruct((D_T, T_ALL), F32),
        ],
        compiler_params=_cparams(1, BIG_VMEM_LIMIT),
        name="inproj",
    )(*x_src, mod, w_main, w_small, w_t)


def _scan_max(x, lane, d):
    s = 1
    while s < CHUNK:
        if d == 0:
            x = jnp.where(lane >= s, jnp.maximum(x, pltpu.roll(x, s, axis=1)), x)
        else:
            x = jnp.where(lane < CHUNK - s, jnp.maximum(x, pltpu.roll(x, CHUNK - s, axis=1)), x)
        s *= 2
    return x


def _split_dot(x, tri):
    hi = x.astype(BF16)
    r1 = x - hi.astype(F32)
    mid = r1.astype(BF16)
    lo = (r1 - mid.astype(F32)).astype(BF16)
    return _dot(hi, tri) + _dot(mid, tri) + _dot(lo, tri)


def _mlstm_kernel(*refs, seq, has_init, emit_state, layer):
    it = iter(refs)
    k_ref, qt_ref, vt_ref, ot_ref, g_ref, gt_ref, gb_ref, gbt_ref, ng_ref = (next(it) for _ in range(9))
    if has_init:
        c0_ref, n0_ref, m0_ref = (next(it) for _ in range(3))
    if emit_state and layer > 0:
        prev_refs = [next(it) for _ in range(3)]
    out_ref = next(it)
    if emit_state:
        co_ref, no_ref, mo_ref = (next(it) for _ in range(3))
    tg_scr, tgt_scr, b_scr, beta_scr, pm_scr, cumc_scr, hd_scr, c_scr, n_scr, m_scr = (next(it) for _ in range(10))
    n_chunks = seq // CHUNK
    n_gate = N_DIR * 2 * H_A
    if emit_state and layer > 0:
        for prev_ref, st_ref in zip(prev_refs, (co_ref, no_ref, mo_ref)):
            st_ref[0, 0:layer] = prev_ref[0]

    y = g_ref[...] + gb_ref[...]
    lane = lax.broadcasted_iota(jnp.int32, y.shape, 1)
    tg_scr[...] = jnp.where((lane < n_gate) & ((lane & H_A) != 0), -_softplus(-y), y)
    yt = gt_ref[0:n_gate, :] + gbt_ref[0:n_gate, :]
    rowt = lax.broadcasted_iota(jnp.int32, yt.shape, 0)
    tgt_scr[...] = jnp.where((rowt & H_A) != 0, -_softplus(-yt), yt)

    lane8 = lax.broadcasted_iota(jnp.int32, (8, CHUNK), 1)
    row8 = lax.broadcasted_iota(jnp.int32, (8, LANES), 0)
    masks_t = [_tri_mask(1 - d) for d in range(N_DIR)]

    for d in range(N_DIR):
        tri_t = masks_t[d].astype(BF16)
        tri_c = _tri_mask(d).astype(BF16)
        rows = slice(d * 8, (d + 1) * 8)
        for c in range(n_chunks):
            cs = slice(c * CHUNK, (c + 1) * CHUNK)
            gr8 = tgt_scr[rows, cs]
            b8 = pltpu.roll(_split_dot(gr8, tri_t), H_A, axis=0)
            beta8 = gr8 - b8
            b_scr[rows, cs] = b8
            beta_scr[rows, cs] = beta8
            pm_scr[rows, cs] = _scan_max(beta8, lane8, d)
            cumc_scr[d, cs, :] = _tri_cumsum(tri_c, tg_scr[cs, :])

    pos = (CHUNK - 1, 0)

    for d in range(N_DIR):
        rows = slice(d * 8, (d + 1) * 8)
        if has_init:
            for h in range(H_A):
                c_scr[d, h] = c0_ref[0, 0, d, h].T
            n0 = n0_ref[0, 0]
            m0 = m0_ref[0, 0]
            if d == 1:
                n0 = pltpu.roll(n0, H_A, axis=0)
                m0 = pltpu.roll(m0, H_A, axis=0)
            n_scr[rows, :] = jnp.where(row8 < H_A, n0, 0.0)
            m_scr[rows, :] = jnp.where(row8 < H_A, m0, 0.0)
        else:
            for h in range(H_A):
                c_scr[d, h] = jnp.zeros((DH_A, DH_A), F32)
            n_scr[rows, :] = jnp.zeros((8, DH_A), F32)
            m_scr[rows, :] = jnp.zeros((8, LANES), F32)

    def chunk_prep(d, c):
        rows = slice(d * 8, (d + 1) * 8)
        cs = slice(c * CHUNK, (c + 1) * CHUNK)
        b8 = b_scr[rows, cs]
        beta8 = beta_scr[rows, cs]
        m8 = m_scr[rows, :]
        n8 = n_scr[rows, :]
        mu8 = jnp.maximum(m8, pm_scr[rows, cs])
        mu_last = mu8[:, pos[d]:pos[d] + 1]
        w_tok8 = jnp.exp(beta8 - mu_last)
        return dict(cs=cs, rows=rows, n8=n8, mu8=mu8, w_int8=jnp.exp(m8 - mu8), emt8=jnp.exp(-(b8 + mu8)),
                    w_tok8=w_tok8, w_prev8=jnp.exp(m8 - mu_last), m_new8=b8[:, pos[d]:pos[d] + 1] + mu_last,
                    n16b=jnp.concatenate([n8, jnp.zeros_like(n8)], axis=0).astype(BF16),
                    w_tok16=jnp.concatenate([w_tok8, jnp.zeros_like(w_tok8)], axis=0),
                    gc=tg_scr[cs, :], cum_c=cumc_scr[d, cs, :])

    def head_step(d, h, p):
        cs = p["cs"]
        hs = slice(h * DH_A, (h + 1) * DH_A)
        col_i = d * 2 * H_A + h
        col_f = col_i + H_A
        beta_c = p["gc"][:, col_i:col_i + 1] - p["cum_c"][:, col_f:col_f + 1]
        kb = k_ref[cs, hs].astype(BF16)
        qtb = (qt_ref[hs, cs] * (DH_A ** -0.5)).astype(BF16)
        vt = vt_ref[hs, cs]
        ct = c_scr[d, h]
        w_int_r = p["w_int8"][h:h + 1, :]

        res = _dot(jnp.concatenate([kb, ct.astype(BF16), p["n16b"]], axis=0), qtb)
        st = res[0:CHUNK] * jnp.exp(jnp.where(masks_t[d], beta_c - p["mu8"][h:h + 1, :], -jnp.inf))
        num = w_int_r * res[CHUNK:CHUNK + DH_A] + _dot(vt.astype(BF16), st.astype(BF16))
        den_r = (w_int_r * res[CHUNK + DH_A + h:CHUNK + DH_A + h + 1]
                 + jnp.sum(st, axis=0, keepdims=True))
        hd_scr[d, hs, cs] = num * (1.0 / jnp.maximum(jnp.abs(den_r), p["emt8"][h:h + 1, :]))

        upd = _dot(jnp.concatenate([vt * p["w_tok8"][h:h + 1, :], p["w_tok16"]], axis=0).astype(BF16), kb)
        c_scr[d, h] = p["w_prev8"][h:h + 1, :] * ct + upd[0:DH_A]
        return upd[DH_A:DH_A + 8]

    for ci in range(n_chunks):
        preps = [chunk_prep(0, ci), chunk_prep(1, n_chunks - 1 - ci)]
        nk_acc = [jnp.zeros((8, DH_A), F32) for _ in range(N_DIR)]
        for h in range(H_A):
            for d in range(N_DIR):
                nk_acc[d] = jnp.where(row8 == h, head_step(d, h, preps[d]), nk_acc[d])
        for d in range(N_DIR):
            p = preps[d]
            n_scr[p["rows"], :] = p["w_prev8"] * p["n8"] + nk_acc[d]
            m_scr[p["rows"], :] = jnp.where(row8 < H_A, jnp.broadcast_to(p["m_new8"], (8, LANES)), 0.0)

    if emit_state:
        for d in range(N_DIR):
            for h in range(H_A):
                co_ref[0, layer, d, h] = c_scr[d, h].T
            no_ref[0, layer, d * H_A:(d + 1) * H_A, :] = n_scr[d * 8:d * 8 + H_A, :]
            mo_ref[0, layer, d * H_A:(d + 1) * H_A, :] = m_scr[d * 8:d * 8 + H_A, :]

    for c in range(n_chunks):
        cs = slice(c * CHUNK, (c + 1) * CHUNK)
        for h in range(H_A):
            hs = slice(h * DH_A, (h + 1) * DH_A)
            hsum = hd_scr[0, hs, cs] + hd_scr[1, hs, cs]
            mean = jnp.sum(hsum, axis=0, keepdims=True) * (1.0 / DH_A)
            xc = hsum - mean
            var = jnp.sum(xc * xc, axis=0, keepdims=True) * (1.0 / DH_A)
            hn = xc * lax.rsqrt(var + EPS) * ng_ref[hs, :]
            out_ref[hs, cs] = (hn * _sigmoid(ot_ref[hs, cs])).astype(out_ref.dtype)


def _mlstm(proj, gates, proj_t, gate_bias, gate_bias_t, norm_g_rep, seq, n_seq, row_off, layer, init=None, prev=None):
    blk0 = row_off // seq
    has_init = init is not None
    emit_state = not has_init
    feat = lambda r: pl.BlockSpec((W_A, seq), lambda s: (r, s + blk0))
    const = lambda shape: pl.BlockSpec(shape, lambda s: (0,) * len(shape))
    st_shapes = [(N_DIR, H_A, DH_A, DH_A), (N_DIR * H_A, DH_A), (N_DIR * H_A, LANES)]

    def st_specs(n_layers, first):
        return [pl.BlockSpec((1, n_layers) + shp, lambda s, nd=len(shp): (s, first) + (0,) * nd) for shp in st_shapes]

    in_specs = [pl.BlockSpec((seq, W_A), lambda s: (s + blk0, 0)), feat(0), feat(1), feat(2),
                pl.BlockSpec((seq, LANES), lambda s: (s + blk0, 0)),
                pl.BlockSpec((LANES, seq), lambda s: (D_T // LANES - 1, s + blk0)),
                const((1, LANES)), const((LANES, 1)), const((W_A, LANES))]
    args = [proj, proj_t, proj_t, proj_t, gates, proj_t, gate_bias, gate_bias_t, norm_g_rep]
    if has_init:
        in_specs += st_specs(1, layer)
        args += list(init)
    if emit_state and layer > 0:
        in_specs += st_specs(layer, 0)
        args += list(prev)
    out_specs = [pl.BlockSpec((W_A, seq), lambda s: (0, s))]
    out_shape = [jax.ShapeDtypeStruct((W_A, n_seq * seq), BF16)]
    if emit_state:
        out_specs += st_specs(layer + 1, 0)
        out_shape += [jax.ShapeDtypeStruct((n_seq, layer + 1) + shp, F32) for shp in st_shapes]
    return pl.pallas_call(
        functools.partial(_mlstm_kernel, seq=seq, has_init=has_init, emit_state=emit_state, layer=layer),
        grid=(n_seq,),
        in_specs=in_specs,
        out_specs=out_specs,
        out_shape=out_shape,
        scratch_shapes=[pltpu.VMEM((seq, LANES), F32),
                        pltpu.VMEM((N_DIR * 8, seq), F32),
                        pltpu.VMEM((N_DIR * 8, seq), F32),
                        pltpu.VMEM((N_DIR * 8, seq), F32),
                        pltpu.VMEM((N_DIR * 8, seq), F32),
                        pltpu.VMEM((N_DIR, seq, LANES), F32),
                        pltpu.VMEM((N_DIR, W_A, seq), F32),
                        pltpu.VMEM((N_DIR, H_A, DH_A, DH_A), F32),
                        pltpu.VMEM((N_DIR * 8, DH_A), F32),
                        pltpu.VMEM((N_DIR * 8, LANES), F32)],
        compiler_params=_cparams(1),
        name="mlstm_lat" if has_init else "mlstm_ctx",
    )(*args)


_DT_COL0 = N_DIR * 2 * H_A
_HG = H_B // G_B
_WG = _HG * P_B


def _conv3_lanes(ref, w_ref, b_ref, out_scr, seq):
    lane = lax.broadcasted_iota(jnp.int32, (CHUNK, seq), 1)
    for r in range(ref.shape[0] // CHUNK):
        rs = slice(r * CHUNK, (r + 1) * CHUNK)
        cur = ref[rs, :]
        prev = jnp.where(lane == 0, 0.0, pltpu.roll(cur, 1, axis=1))
        nxt = jnp.where(lane == seq - 1, 0.0, pltpu.roll(cur, seq - 1, axis=1))
        for c in range(seq // CHUNK):
            cs = slice(c * CHUNK, (c + 1) * CHUNK)
            acc = (w_ref[0, rs, :] * prev[:, cs] + w_ref[1, rs, :] * cur[:, cs] + w_ref[2, rs, :] * nxt[:, cs]
                   + b_ref[rs, :])
            out_scr[rs, cs] = _silu(acc).astype(out_scr.dtype)


def _ssd_kernel(*refs, seq, has_init, emit_state, layer):
    it = iter(refs)
    (zt_ref, xt_ref, ct_ref, b_ref, g_ref, gt_ref, gb_ref, gbt_ref, alog_ref, alogt_ref,
     cwx_ref, cbx_ref, cwc_ref, cbc_ref, cwb_ref, cbb_ref, dsk_ref, ng_ref) = (next(it) for _ in range(18))
    if has_init:
        h0_ref = next(it)
    if emit_state and layer > 0:
        prev_ref = next(it)
    out_ref = next(it)
    if emit_state:
        ho_ref = next(it)
    pad_scr, xs_scr, cs_scr, bs_scr, dtr_scr, br_scr, cumc_scr, yt_scr, h_scr = (next(it) for _ in range(9))
    n_chunks = seq // CHUNK
    pad = 8
    if emit_state and layer > 0:
        ho_ref[0, 0:layer] = prev_ref[0]

    _conv3_lanes(xt_ref, cwx_ref, cbx_ref, xs_scr, seq)
    _conv3_lanes(ct_ref, cwc_ref, cbc_ref, cs_scr, seq)
    nb = G_B * N_B
    pad_scr[0:pad, :] = jnp.zeros((pad, nb), F32)
    pad_scr[pad + seq:2 * pad + seq, :] = jnp.zeros((pad, nb), F32)
    pad_scr[pad:pad + seq, :] = b_ref[...]
    for c in range(n_chunks):
        acc = cbb_ref[...]
        for k in range(SSM_CONV):
            r = c * CHUNK + pad - SSM_CONV // 2 + k
            acc = acc + cwb_ref[k:k + 1, :] * pad_scr[r:r + CHUNK, :]
        bs_scr[c * CHUNK:(c + 1) * CHUNK, :] = _silu(acc).astype(bs_scr.dtype)

    lac = _softplus(g_ref[...] + gb_ref[...]) * (-jnp.exp(alog_ref[...]))
    r0 = _DT_COL0
    dtr = _softplus(gt_ref[r0:r0 + N_DIR * H_B, :] + gbt_ref[r0:r0 + N_DIR * H_B, :])
    dtr_scr[...] = dtr
    lar = dtr * (-jnp.exp(alogt_ref[r0:r0 + N_DIR * H_B, :]))
    masks_t = [_tri_mask(1 - d) for d in range(N_DIR)]
    for d in range(N_DIR):
        tri_t = masks_t[d].astype(BF16)
        tri_c = _tri_mask(d).astype(BF16)
        rows = slice(d * H_B, (d + 1) * H_B)
        for c in range(n_chunks):
            cs = slice(c * CHUNK, (c + 1) * CHUNK)
            br_scr[rows, cs] = _split_dot(lar[rows, cs], tri_t)
            cumc_scr[d, cs, :] = _tri_cumsum(tri_c, lac[cs, :])

    row_blk = lax.broadcasted_iota(jnp.int32, (_WG, 1), 0) // P_B

    def run_direction(d):
        for g in range(G_B):
            for hh in range(_HG):
                if has_init:
                    h_scr[d, g, hh * P_B:(hh + 1) * P_B, :] = h0_ref[0, 0, d, g * _HG + hh]
                else:
                    h_scr[d, g, hh * P_B:(hh + 1) * P_B, :] = jnp.zeros((P_B, N_B), F32)

        mask_t = masks_t[d]
        pos = CHUNK - 1 if d == 0 else 0

        for ci in range(n_chunks):
            c = ci if d == 0 else n_chunks - 1 - ci
            cs = slice(c * CHUNK, (c + 1) * CHUNK)
            cum_c = cumc_scr[d, cs, :]
            for g in range(G_B):
                bg = bs_scr[cs, g * N_B:(g + 1) * N_B]
                ctg = cs_scr[g * N_B:(g + 1) * N_B, cs]
                h_st = h_scr[d, g]
                res = _dot(jnp.concatenate([bg, h_st.astype(BF16)], axis=0), ctg)
                cbt = res[0:CHUNK]
                inter = res[CHUNK:CHUNK + _WG]
                xw = []
                decay = jnp.zeros((_WG, 1), F32)
                for hh in range(_HG):
                    head = g * _HG + hh
                    r = d * H_B + head
                    b_r = br_scr[r:r + 1, cs]
                    dt_r = dtr_scr[r:r + 1, cs]
                    b_c = cum_c[:, _DT_COL0 + r:_DT_COL0 + r + 1]
                    gt = b_r[:, pos:pos + 1]
                    ps = slice(head * P_B, (head + 1) * P_B)
                    xh = xs_scr[ps, cs]
                    st = (cbt * jnp.exp(jnp.where(mask_t, b_r - b_c, -jnp.inf))).astype(BF16)
                    y = _dot((xh * dt_r).astype(BF16), st) + jnp.exp(b_r) * inter[hh * P_B:(hh + 1) * P_B]
                    xw.append(xh * (jnp.exp(gt - b_r) * dt_r))
                    decay = jnp.where(row_blk == hh, jnp.exp(gt), decay)
                    if d == 0:
                        yt_scr[ps, cs] = y
                    else:
                        yt_scr[ps, cs] = yt_scr[ps, cs] + y + dsk_ref[ps, :] * xh
                h_scr[d, g] = decay * h_st + _dot(jnp.concatenate(xw, axis=0).astype(BF16), bg)
            if d == 1:
                yz = yt_scr[:, cs] * _silu(zt_ref[:, cs])
                rms = lax.rsqrt(jnp.sum(yz * yz, axis=0, keepdims=True) * (1.0 / W_B) + EPS)
                out_ref[:, cs] = (yz * rms * ng_ref[...]).astype(out_ref.dtype)

        if emit_state:
            for g in range(G_B):
                for hh in range(_HG):
                    ho_ref[0, layer, d, g * _HG + hh] = h_scr[d, g, hh * P_B:(hh + 1) * P_B, :]

    run_direction(0)
    run_direction(1)


def _ssd(proj, gates, proj_t, gate_bias, gate_bias_t, alog_row, alog_col, conv, d_rep, norm_g_rep,
         seq, n_seq, row_off, layer, init=None, prev=None):
    blk0 = row_off // seq
    has_init = init is not None
    emit_state = not has_init
    nb = G_B * N_B
    zr = 3 * W_A // W_B
    cr = (3 * W_A + 2 * W_B) // nb
    gr = (3 * W_A + 2 * W_B + nb) // LANES
    bc = (W_A + 2 * W_C) // nb
    const = lambda shape: pl.BlockSpec(shape, lambda s: (0,) * len(shape))
    st_spec = lambda n_layers, first: pl.BlockSpec((1, n_layers, N_DIR, H_B, P_B, N_B),
                                                   lambda s: (s, first, 0, 0, 0, 0))
    in_specs = [pl.BlockSpec((W_B, seq), lambda s: (zr, s + blk0)),
                pl.BlockSpec((W_B, seq), lambda s: (zr + 1, s + blk0)),
                pl.BlockSpec((nb, seq), lambda s: (cr, s + blk0)),
                pl.BlockSpec((seq, nb), lambda s: (s + blk0, bc)),
                pl.BlockSpec((seq, LANES), lambda s: (s + blk0, 0)),
                pl.BlockSpec((LANES, seq), lambda s: (gr, s + blk0)),
                const((1, LANES)), const((LANES, 1)), const((1, LANES)), const((LANES, 1)),
                const((SSM_CONV, W_B, LANES)), const((W_B, LANES)), const((SSM_CONV, nb, LANES)), const((nb, LANES)),
                const((8, nb)), const((1, nb)), const((W_B, LANES)), const((W_B, LANES))]
    args = [proj_t, proj_t, proj_t, proj, gates, proj_t, gate_bias, gate_bias_t, alog_row, alog_col,
            *conv, d_rep, norm_g_rep]
    if has_init:
        in_specs.append(st_spec(1, layer))
        args.append(init)
    if emit_state and layer > 0:
        in_specs.append(st_spec(layer, 0))
        args.append(prev)
    out_specs = [pl.BlockSpec((W_B, seq), lambda s: (0, s))]
    out_shape = [jax.ShapeDtypeStruct((W_B, n_seq * seq), BF16)]
    if emit_state:
        out_specs.append(st_spec(layer + 1, 0))
        out_shape.append(jax.ShapeDtypeStruct((n_seq, layer + 1, N_DIR, H_B, P_B, N_B), F32))
    return pl.pallas_call(
        functools.partial(_ssd_kernel, seq=seq, has_init=has_init, emit_state=emit_state, layer=layer),
        grid=(n_seq,),
        in_specs=in_specs,
        out_specs=out_specs,
        out_shape=out_shape,
        scratch_shapes=[pltpu.VMEM((seq + 16, nb), F32),
                        pltpu.VMEM((W_B, seq), F32),
                        pltpu.VMEM((nb, seq), BF16),
                        pltpu.VMEM((seq, nb), BF16),
                        pltpu.VMEM((N_DIR * H_B, seq), F32),
                        pltpu.VMEM((N_DIR * H_B, seq), F32),
                        pltpu.VMEM((N_DIR, seq, LANES), F32),
                        pltpu.VMEM((W_B, seq), F32),
                        pltpu.VMEM((N_DIR, G_B, _WG, N_B), F32)],
        compiler_params=_cparams(1),
        name="ssd_lat" if has_init else "ssd_ctx",
    )(*args)


def _conf_kernel(a_ref, ap_ref, an_ref, g_ref, gp_ref, gn_ref, w_ref, b_ref, lg_ref, lb_ref, out_ref, pad_scr,
                 rot_scr):
    i = pl.program_id(0)
    n_ctx_blk = N_CTX_TOK // CONV_ROWS
    blk_per_lat = DEC_SEQ // CONV_ROWS
    j = (i - n_ctx_blk) % blk_per_lat
    is_ctx = i < n_ctx_blk
    keep_prev = jnp.where(is_ctx | (j == 0), 0.0, 1.0)
    keep_next = jnp.where(is_ctx | (j == blk_per_lat - 1), 0.0, 1.0)
    pad_scr[0:HALO, :] = ap_ref[...] * _sigmoid(gp_ref[...]) * keep_prev
    pad_scr[HALO:HALO + CONV_ROWS, :] = a_ref[...] * _sigmoid(g_ref[...])
    pad_scr[HALO + CONV_ROWS:2 * HALO + CONV_ROWS, :] = an_ref[...] * _sigmoid(gn_ref[...]) * keep_next
    n_rot = CONV_ROWS + 2 * HALO - 8
    for s in range(1, 8):
        rot_scr[s, :, :] = pad_scr[s:s + n_rot, :]
    rc = 64
    for c in range(CONV_ROWS // rc):
        acc = jnp.broadcast_to(b_ref[...], (rc, W_C))
        for k in range(CONV_W):
            off = HALO - CONV_W // 2 + k
            r = c * rc + off - off % 8
            src = pad_scr[r:r + rc, :] if off % 8 == 0 else rot_scr[off % 8, r:r + rc, :]
            acc = acc + w_ref[k:k + 1, :] * src
        u = _ln(acc) * lg_ref[...] + lb_ref[...]
        out_ref[c * rc:(c + 1) * rc, :] = _silu(u).astype(out_ref.dtype)


def _conf(proj, dw_w, dw_b, ln_g, ln_b):
    assert SEQ == CONV_ROWS and DEC_SEQ % CONV_ROWS == 0
    ac = W_A // W_C
    hb = CONV_ROWS // HALO
    n_halo = T_ALL // HALO
    const = lambda shape: pl.BlockSpec(shape, lambda i: (0,) * len(shape))

    def specs(c):
        return [pl.BlockSpec((CONV_ROWS, W_C), lambda i: (i, c)),
                pl.BlockSpec((HALO, W_C), lambda i: (jnp.maximum(i * hb - 1, 0), c)),
                pl.BlockSpec((HALO, W_C), lambda i: (jnp.minimum((i + 1) * hb, n_halo - 1), c))]

    return pl.pallas_call(
        _conf_kernel,
        grid=(T_ALL // CONV_ROWS,),
        in_specs=specs(ac) + specs(ac + 1) + [const((32, W_C)), const((1, W_C)), const((1, W_C)), const((1, W_C))],
        out_specs=pl.BlockSpec((CONV_ROWS, W_C), lambda i: (i, 0)),
        out_shape=jax.ShapeDtypeStruct((T_ALL, W_C), BF16),
        scratch_shapes=[pltpu.VMEM((CONV_ROWS + 2 * HALO, W_C), F32),
                        pltpu.VMEM((8, CONV_ROWS + 2 * HALO - 8, W_C), F32)],
        compiler_params=_cparams(1),
        name="conf",
    )(proj, proj, proj, proj, proj, proj, dw_w, dw_b, ln_g, ln_b)


_PAIRS = [(a, b) for a in range(EXPERTS_PER_GROUP) for b in range(a + 1, EXPERTS_PER_GROUP)]
N_CLASSES = N_EGROUPS * len(_PAIRS)
N_FCHUNK = D_MODEL // LANES
N_IN_ROWS = 2 * N_FCHUNK


def _store_token_tiles(ref, val):
    n = val.shape[0]
    for c in range(N_FCHUNK):
        ref[pl.ds(c, n, stride=N_FCHUNK), :] = val[:, c * LANES:(c + 1) * LANES]


def _load_token_tiles(ref, n):
    return jnp.concatenate([ref[pl.ds(c, n, stride=N_FCHUNK), :] for c in range(N_FCHUNK)], axis=1)
N_MOE_ROWS = T_ALL + N_CLASSES * TM_MOE
N_MOE_TILES = N_MOE_ROWS // TM_MOE


def _outproj_kernel(*refs, tm, n_x):
    x_refs = refs[:n_x]
    (hac_ref, hal_ref, hbc_ref, hbl_ref, hc_ref, mod_ref, wo_ref, pg_ref, pb_ref,
     wr_ref, br_ref, x1_ref, u2_ref, rt_ref) = refs[n_x:]
    i = pl.program_id(0)
    r = _mod_row(i * tm)
    gate1 = mod_ref[pl.ds(r, 1), 2 * D_MODEL:3 * D_MODEL]
    shift2 = mod_ref[pl.ds(r, 1), 3 * D_MODEL:4 * D_MODEL]
    scale2 = mod_ref[pl.ds(r, 1), 4 * D_MODEL:5 * D_MODEL]
    is_ctx = i < N_CTX_TOK // tm
    ha = jnp.where(is_ctx, hac_ref[...], hal_ref[...])
    hb = jnp.where(is_ctx, hbc_ref[...], hbl_ref[...])
    mix = (_dotg(ha, wo_ref[0:W_A, :], _TN) + _dotg(hb, wo_ref[W_A:W_A + W_B, :], _TN)
           + _dot(hc_ref[...], wo_ref[W_A + W_B:D_MIX, :]))
    x1 = _ln(ALPHA * _read_x(x_refs, i, tm) + gate1 * mix) * pg_ref[...] + pb_ref[...]
    x1_ref[...] = x1
    u2 = _ln(x1) * (1.0 + scale2) + shift2
    for c in range(N_FCHUNK):
        u2_ref[pl.ds(c, tm, stride=N_IN_ROWS), :] = u2[:, c * LANES:(c + 1) * LANES]

    wr = wr_ref[...]
    wr_hi = wr.astype(BF16)
    wr_lo = (wr - wr_hi.astype(F32)).astype(BF16)
    u2_hi = u2.astype(BF16)
    u2_lo = (u2 - u2_hi.astype(F32)).astype(BF16)
    both = _dotg(jnp.concatenate([wr_hi, wr_lo], axis=0), u2_hi, _NT)
    logits = both[0:N_EXPERTS] + both[N_EXPERTS:2 * N_EXPERTS] + _dotg(wr_hi, u2_lo, _NT) + br_ref[...]
    ex = jnp.exp(logits - jnp.max(logits, axis=0, keepdims=True))
    probs = ex / jnp.sum(ex, axis=0, keepdims=True)
    scores = []
    for g in range(N_EGROUPS):
        p = [probs[g * EXPERTS_PER_GROUP + e:g * EXPERTS_PER_GROUP + e + 1, :] for e in range(EXPERTS_PER_GROUP)]
        best = p[0] + p[1]
        for a in range(EXPERTS_PER_GROUP):
            for b in range(a + 1, EXPERTS_PER_GROUP):
                if (a, b) != (0, 1):
                    best = jnp.maximum(best, p[a] + p[b])
        scores.append(best)
    gmax = functools.reduce(jnp.maximum, scores)
    sel = jnp.full(gmax.shape, N_EGROUPS - 1, jnp.int32)
    for g in range(N_EGROUPS - 2, -1, -1):
        sel = jnp.where(scores[g] == gmax, g, sel)
    eidx = lax.broadcasted_iota(jnp.int32, probs.shape, 0)
    pm = jnp.where((eidx // EXPERTS_PER_GROUP) == sel, probs, -jnp.inf)
    p1 = jnp.max(pm, axis=0, keepdims=True)
    i1 = jnp.min(jnp.where(pm == p1, eidx, N_EXPERTS), axis=0, keepdims=True)
    pm2 = jnp.where(eidx == i1, -jnp.inf, pm)
    p2 = jnp.max(pm2, axis=0, keepdims=True)
    i2 = jnp.min(jnp.where(pm2 == p2, eidx, N_EXPERTS), axis=0, keepdims=True)
    den = p1 + p2
    first_lo = i1 < i2
    w_lo = jnp.where(first_lo, p1, p2) / den
    w_hi = jnp.where(first_lo, p2, p1) / den
    a = jnp.minimum(i1, i2) - sel * EXPERTS_PER_GROUP
    b = jnp.maximum(i1, i2) - sel * EXPERTS_PER_GROUP
    pair = jnp.zeros_like(a)
    for k, (pa, pb_) in enumerate(_PAIRS):
        pair = jnp.where((a == pa) & (b == pb_), k, pair)
    cls = sel * len(_PAIRS) + pair
    rt_ref[...] = jnp.broadcast_to(cls, rt_ref.shape)
    wrow = lax.broadcasted_iota(jnp.int32, (LANES, tm), 0)
    wt = jnp.where(wrow == 0, w_lo, jnp.where(wrow == 1, w_hi, 0.0))
    u2_ref[pl.ds(N_FCHUNK, tm, stride=N_IN_ROWS), :] = wt.T
    for c in range(N_FCHUNK + 1, N_IN_ROWS):
        u2_ref[pl.ds(c, tm, stride=N_IN_ROWS), :] = jnp.zeros((tm, LANES), F32)


def _outproj(x_src, ha_c, ha_l, hb_c, hb_l, hc, mod, w_o, pg, pb, w_rt, b_r):
    tm = TM_PROJ
    n_ctx_blk = N_CTX_TOK // tm
    const = lambda shape: pl.BlockSpec(shape, lambda i: (0,) * len(shape))
    rows = lambda w: pl.BlockSpec((tm, w), lambda i: (i, 0))
    ctx_feat = lambda w: pl.BlockSpec((w, tm), lambda i: (0, jnp.minimum(i, n_ctx_blk - 1)))
    lat_feat = lambda w: pl.BlockSpec((w, tm), lambda i: (0, jnp.maximum(i - n_ctx_blk, 0)))
    return pl.pallas_call(
        functools.partial(_outproj_kernel, tm=tm, n_x=len(x_src)),
        grid=(T_ALL // tm,),
        in_specs=_x_specs(x_src, tm) + [
                  ctx_feat(W_A), lat_feat(W_A), ctx_feat(W_B), lat_feat(W_B),
                  rows(W_C), const((N_MOD_ROWS, 6 * D_MODEL)),
                  const((D_MIX, D_MODEL)), const((1, D_MODEL)), const((1, D_MODEL)),
                  const((N_EXPERTS, D_MODEL)), const((N_EXPERTS, 1))],
        out_specs=[rows(D_MODEL), pl.BlockSpec((tm * N_IN_ROWS, LANES), lambda i: (i, 0)),
                   pl.BlockSpec((8, tm), lambda i: (0, i))],
        out_shape=[jax.ShapeDtypeStruct((T_ALL, D_MODEL), F32),
                   jax.ShapeDtypeStruct((T_ALL * N_IN_ROWS, LANES), F32),
                   jax.ShapeDtypeStruct((8, T_ALL), jnp.int32)],
        compiler_params=_cparams(1),
        name="outproj",
    )(*x_src, ha_c, ha_l, hb_c, hb_l, hc, mod, w_o, pg, pb, w_rt, b_r)


N_Y_ROWS = T_ALL + 2 * TM_MOE


def _moe_kernel(tok_ref, tstart_ref, nval_ref, tlo_ref, thi_ref, nused_ref,
                u_hbm, wgl_ref, wul_ref, wdl_ref, wgh_ref, wuh_ref, wdh_ref, y_hbm,
                xbuf0, xbuf1, ybuf0, ybuf1, gsem, ssem):
    del tlo_ref, thi_ref
    i = pl.program_id(0)
    n_used = nused_ref[0]
    tile_rows = TM_MOE * N_FCHUNK

    def token_tile(ref, tok, rows=N_FCHUNK):
        return ref.at[pl.ds(pl.multiple_of(tok * rows, rows), rows), :]

    def started_groups(n_valid):
        return [(q, q * ROW_GROUP < n_valid) for q in range(TM_MOE // ROW_GROUP)]

    def gather_rows(t, xb, sem):
        base = tstart_ref[t]
        for q, started in started_groups(nval_ref[t]):
            @pl.when(started)
            def _():
                for r in range(q * ROW_GROUP, (q + 1) * ROW_GROUP):
                    pltpu.make_async_copy(token_tile(u_hbm, tok_ref[base + r], N_IN_ROWS),
                                          token_tile(xb, r, N_IN_ROWS), sem).start()

    def scatter_rows(t, n_valid, yb, sem, spare):
        base = tstart_ref[t]
        for q, started in started_groups(n_valid):
            @pl.when(started)
            def _():
                for r in range(q * ROW_GROUP, (q + 1) * ROW_GROUP):
                    dst = jnp.where(r < n_valid, tok_ref[base + r], spare + r)
                    pltpu.make_async_copy(token_tile(yb, r), token_tile(y_hbm, dst), sem).start()

    def load_tile(xb):
        x = jnp.concatenate([xb[pl.ds(c, TM_MOE, stride=N_IN_ROWS), :] for c in range(N_FCHUNK)], axis=1)
        w = xb[pl.ds(N_FCHUNK, TM_MOE, stride=N_IN_ROWS), :]
        return x.astype(BF16), w[:, 0:1], w[:, 1:2]

    def wait_groups(n_valid, rows_per_token, copy):
        for q, started in started_groups(n_valid):
            @pl.when(started)
            def _():
                n = ROW_GROUP * rows_per_token
                copy(pl.ds(q * n, n)).wait()

    def wait_gather(t, xb, sem):
        wait_groups(nval_ref[t], N_IN_ROWS,
                    lambda rows: pltpu.make_async_copy(u_hbm.at[rows, :], xb.at[rows, :], sem))

    def wait_scatter(n_valid, yb, sem):
        wait_groups(n_valid, N_FCHUNK,
                    lambda rows: pltpu.make_async_copy(yb.at[rows, :], y_hbm.at[rows, :], sem))

    def valid_of(t):
        return jnp.where(t >= 0, nval_ref[jnp.maximum(t, 0)], 0)

    def step(s, xb, xb_next, yb, yb_prev):
        wait_scatter(valid_of(i - 2), yb, ssem.at[s])
        wait_gather(i, xb, gsem.at[s])
        nxt = jnp.minimum(i + 1, n_used - 1)

        @pl.when(n_used > 0)
        def _():
            gather_rows(nxt, xb_next, gsem.at[1 - s])
            scatter_rows(jnp.maximum(i - 1, 0), valid_of(i - 1), yb_prev, ssem.at[1 - s], T_ALL + (1 - s) * TM_MOE)

        x, w_lo, w_hi = load_tile(xb)

        def ffn(wg_ref, wu_ref, wd_ref, w):
            act = _silu(_dot(x, wg_ref[0, 0].astype(BF16))) * _dot(x, wu_ref[0, 0].astype(BF16)) * w
            return _dot(act.astype(BF16), wd_ref[0, 0].astype(BF16))

        _store_token_tiles(yb, ffn(wgl_ref, wul_ref, wdl_ref, w_lo) + ffn(wgh_ref, wuh_ref, wdh_ref, w_hi))

        @pl.when(i == n_used - 1)
        def _():
            scatter_rows(i, nval_ref[i], yb, ssem.at[s], T_ALL + s * TM_MOE)
            wait_scatter(nval_ref[i], yb, ssem.at[s])
            wait_scatter(valid_of(i - 1), yb_prev, ssem.at[1 - s])
            wait_gather(nxt, xb_next, gsem.at[1 - s])

    @pl.when(i == 0)
    def _():
        xbuf0[...] = jnp.zeros(xbuf0.shape, F32)
        xbuf1[...] = jnp.zeros(xbuf1.shape, F32)
        ybuf1[...] = jnp.zeros(ybuf1.shape, F32)
        for spare_set in range(2):
            fill = pltpu.make_async_copy(
                ybuf1, y_hbm.at[pl.ds((T_ALL + spare_set * TM_MOE) * N_FCHUNK, tile_rows), :], ssem.at[0])
            fill.start()
            fill.wait()
        gather_rows(0, xbuf0, gsem.at[0])

    @pl.when((i < n_used) & (i % 2 == 0))
    def _():
        step(0, xbuf0, xbuf1, ybuf0, ybuf1)

    @pl.when((i < n_used) & (i % 2 == 1))
    def _():
        step(1, xbuf1, xbuf0, ybuf1, ybuf0)


def _moe(u2t, tok_sorted, tile_start, n_valid, tile_lo, tile_hi, n_used, w_gate, w_up, w_down, layer):
    lo = lambda shape: pl.BlockSpec(shape, lambda i, tok, ts, nv, tlo, thi, nused: (layer, tlo[i], 0, 0))
    hi = lambda shape: pl.BlockSpec(shape, lambda i, tok, ts, nv, tlo, thi, nused: (layer, thi[i], 0, 0))
    up_shape = (1, 1, D_MODEL, D_FF_EXPERT)
    down_shape = (1, 1, D_FF_EXPERT, D_MODEL)
    in_buf = pltpu.VMEM((TM_MOE * N_IN_ROWS, LANES), F32)
    out_buf = pltpu.VMEM((TM_MOE * N_FCHUNK, LANES), F32)
    grid_spec = pltpu.PrefetchScalarGridSpec(
        num_scalar_prefetch=6,
        grid=(N_MOE_TILES,),
        in_specs=[pl.BlockSpec(memory_space=pl.ANY),
                  lo(up_shape), lo(up_shape), lo(down_shape),
                  hi(up_shape), hi(up_shape), hi(down_shape)],
        out_specs=pl.BlockSpec(memory_space=pl.ANY),
        scratch_shapes=[in_buf, in_buf, out_buf, out_buf,
                        pltpu.SemaphoreType.DMA((2,)),
                        pltpu.SemaphoreType.DMA((2,))],
    )
    return pl.pallas_call(
        _moe_kernel,
        grid_spec=grid_spec,
        out_shape=jax.ShapeDtypeStruct((N_Y_ROWS * N_FCHUNK, LANES), F32),
        compiler_params=pltpu.CompilerParams(dimension_semantics=("arbitrary",), vmem_limit_bytes=BIG_VMEM_LIMIT,
                                             has_side_effects=True),
        name="moe",
    )(tok_sorted, tile_start, n_valid, tile_lo, tile_hi, n_used, u2t, w_gate, w_up, w_down, w_gate, w_up, w_down)


def _route_tables(cls):
    _, tok_sorted = lax.sort((cls, jnp.arange(T_ALL, dtype=jnp.int32)), num_keys=1)
    cids = jnp.arange(N_CLASSES, dtype=jnp.int32)
    counts = jnp.sum((cls[:, None] == cids[None, :]).astype(jnp.int32), axis=0)
    offs = jnp.cumsum(counts) - counts
    padded = (counts + TM_MOE - 1) // TM_MOE * TM_MOE
    ends = jnp.cumsum(padded)
    offs_p = ends - padded
    n_used = ends[-1] // TM_MOE
    tile_start = jnp.arange(N_MOE_TILES, dtype=jnp.int32) * TM_MOE
    tile_cls = jnp.sum((ends[None, :] <= jnp.minimum(tile_start, ends[-1] - TM_MOE)[:, None]).astype(jnp.int32),
                       axis=1)
    onehot = (tile_cls[:, None] == cids[None, :]).astype(jnp.int32)
    pick = lambda table: jnp.sum(onehot * table[None, :], axis=1)
    k = tile_start - pick(offs_p)
    n_valid = jnp.where(tile_start < ends[-1], jnp.clip(pick(counts) - k, 0, TM_MOE), 0)
    tile_first = jnp.clip(pick(offs) + k, 0, T_ALL - 1)
    tile_lo = pick(cids // len(_PAIRS) * EXPERTS_PER_GROUP + jnp.array([p[0] for p in _PAIRS] * N_EGROUPS, jnp.int32))
    tile_hi = pick(cids // len(_PAIRS) * EXPERTS_PER_GROUP + jnp.array([p[1] for p in _PAIRS] * N_EGROUPS, jnp.int32))
    tok_sorted = jnp.concatenate([tok_sorted, jnp.zeros((TM_MOE,), jnp.int32)])
    i32 = lambda v: v.astype(jnp.int32)
    return tok_sorted, i32(tile_first), i32(n_valid), i32(tile_lo), i32(tile_hi), i32(n_used).reshape(1)


def _final_kernel(x1_ref, y_ref, mod_ref, pg_ref, pb_ref, *o_refs, tm):
    i = pl.program_id(0)
    r = _mod_row(i * tm)
    gate2 = mod_ref[pl.ds(r, 1), 5 * D_MODEL:6 * D_MODEL]
    out = _ln(ALPHA * x1_ref[...] + gate2 * _load_token_tiles(y_ref, tm)) * pg_ref[...] + pb_ref[...]
    if len(o_refs) == 1:
        o_refs[0][...] = out
    else:
        @pl.when(i < N_CTX_TOK // tm)
        def _():
            o_refs[0][...] = out

        @pl.when(i >= N_CTX_TOK // tm)
        def _():
            o_refs[1][...] = out


def _final(x1, y, mod, pg, pb, split):
    tm = TM_PROJ
    n_ctx_blk = N_CTX_TOK // tm
    const = lambda shape: pl.BlockSpec(shape, lambda i: (0,) * len(shape))
    rows = pl.BlockSpec((tm, D_MODEL), lambda i: (i, 0))
    if split:
        out_specs = [pl.BlockSpec((tm, D_MODEL), lambda i: (jnp.minimum(i, n_ctx_blk - 1), 0)),
                     pl.BlockSpec((tm, D_MODEL), lambda i: (jnp.maximum(i - n_ctx_blk, 0), 0))]
        out_shape = [jax.ShapeDtypeStruct((N_CTX_TOK, D_MODEL), F32), jax.ShapeDtypeStruct((N_LAT_TOK, D_MODEL), F32)]
    else:
        out_specs = [rows]
        out_shape = [jax.ShapeDtypeStruct((T_ALL, D_MODEL), F32)]
    return pl.pallas_call(
        functools.partial(_final_kernel, tm=tm),
        grid=(T_ALL // tm,),
        in_specs=[rows, pl.BlockSpec((tm * N_FCHUNK, LANES), lambda i: (i, 0)),
                  const((N_MOD_ROWS, 6 * D_MODEL)), const((1, D_MODEL)), const((1, D_MODEL))],
        out_specs=out_specs,
        out_shape=out_shape,
        compiler_params=_cparams(1),
        name="final",
    )(x1, y, mod, pg, pb)


def _grid_pos(n_tok):
    rows = n_tok // GRID_W
    r, col = jnp.meshgrid(jnp.arange(rows, dtype=F32), jnp.arange(GRID_W, dtype=F32), indexing='ij')
    quarter = D_MODEL // 4
    omega = 1.0 / (10000.0 ** (jnp.arange(quarter, dtype=F32) / quarter))

    def emb(p):
        ang = p.reshape(-1)[:, None] * omega[None, :]
        return jnp.concatenate([jnp.sin(ang), jnp.cos(ang)], axis=-1)

    return jnp.concatenate([emb(r), emb(col)], axis=-1)


def _pad_lanes(v, start):
    v = v.reshape(1, -1).astype(F32)
    return jnp.pad(v, ((0, 0), (start, LANES - start - v.shape[1])))


def kernel(x_prompt, x_sample, state_mlstm_C, state_mlstm_n, state_mlstm_m, state_ssd, c, c_ctx, w_in, w_o, mlstm_b_i, mlstm_b_f, mlstm_norm_g, ssd_conv_w, ssd_conv_b, ssd_dt_bias, ssd_A_log, ssd_D, ssd_norm_g, conv_dw_w, conv_dw_b, conv_ln_g, conv_ln_b, w_ada, b_ada, post1_g, post1_b, post2_g, post2_b, w_router, b_router, w_e_gate, w_e_up, w_e_down):
    cvec = jnp.concatenate([c_ctx[None, :], c, jnp.zeros((N_MOD_ROWS - 1 - DEC_BATCH, D_MODEL), F32)], axis=0)
    mod_all = _ada(cvec, w_ada, b_ada)
    x_src = (x_prompt.reshape(N_CTX_TOK, D_MODEL), x_sample.reshape(N_LAT_TOK, D_MODEL), _grid_pos(DEC_SEQ))
    w_rt = w_router.T
    b_r = b_router.reshape(N_EXPERTS, 1)

    a_end = 4 * W_A + N_DIR * 2 * H_A
    b_end = a_end + W_B + W_XBC + N_DIR * H_B
    init = (state_mlstm_C,
            state_mlstm_n.reshape(DEC_BATCH, DEPTH, N_DIR * H_A, DH_A),
            jnp.broadcast_to(state_mlstm_m.reshape(DEC_BATCH, DEPTH, N_DIR * H_A, 1),
                             (DEC_BATCH, DEPTH, N_DIR * H_A, LANES)))
    st_c = st_n = st_m = st_h = None
    for l in range(DEPTH):
        w = w_in[l]
        xbc0 = a_end + W_B
        w_main = jnp.concatenate([w[:, W_A:2 * W_A], w[:, b_end:], w[:, xbc0 + W_B:xbc0 + W_B + G_B * N_B]],
                                 axis=1).astype(BF16)
        w_small = jnp.concatenate([w[:, 4 * W_A:a_end], w[:, b_end - N_DIR * H_B:b_end],
                                   jnp.zeros((D_MODEL, LANES - _DT_COL0 - N_DIR * H_B), F32)], axis=1)
        w_t = jnp.concatenate([w[:, 0:W_A], w[:, 2 * W_A:4 * W_A], w[:, a_end:xbc0 + W_B],
                               w[:, xbc0 + W_B + G_B * N_B:xbc0 + W_XBC], w_small],
                              axis=1).T.astype(BF16)
        w_small = w_small.astype(BF16)
        gate_bias = (_pad_lanes(jnp.stack([mlstm_b_i[l], mlstm_b_f[l]], axis=1), 0)
                     + _pad_lanes(ssd_dt_bias[l], _DT_COL0))
        alog_row = _pad_lanes(ssd_A_log[l], _DT_COL0)
        mod = mod_all[l]

        proj, gates, proj_t = _inproj(x_src, mod, w_main, w_small, w_t)

        m_norm = jnp.broadcast_to(mlstm_norm_g[l].reshape(W_A, 1), (W_A, LANES))
        mlstm_args = (proj, gates, proj_t, gate_bias, gate_bias.reshape(LANES, 1), m_norm)
        ha_c, st_c, st_n, st_m = _mlstm(*mlstm_args, SEQ, BATCH, 0, l, prev=(st_c, st_n, st_m))
        (ha_l,) = _mlstm(*mlstm_args, DEC_SEQ, DEC_BATCH, N_CTX_TOK, l, init=init)

        nb = G_B * N_B
        rep = lambda v: jnp.broadcast_to(v[..., None], v.shape + (LANES,))
        cw, cb = ssd_conv_w[l], ssd_conv_b[l]
        conv = (rep(cw[:, 0:W_B]), rep(cb[0:W_B]), rep(cw[:, W_B + nb:W_XBC]), rep(cb[W_B + nb:W_XBC]),
                jnp.pad(cw[:, W_B:W_B + nb], ((0, 8 - SSM_CONV), (0, 0))), cb[W_B:W_B + nb].reshape(1, nb))
        ssd_args = (proj, gates, proj_t, gate_bias, gate_bias.reshape(LANES, 1), alog_row, alog_row.reshape(LANES, 1),
                    conv, rep(jnp.repeat(ssd_D[l], P_B)), rep(ssd_norm_g[l]))
        hb_c, st_h = _ssd(*ssd_args, SEQ, BATCH, 0, l, prev=st_h)
        (hb_l,) = _ssd(*ssd_args, DEC_SEQ, DEC_BATCH, N_CTX_TOK, l, init=state_ssd)

        hc = _conf(proj, jnp.pad(conv_dw_w[l], ((0, 32 - CONV_W), (0, 0))), conv_dw_b[l].reshape(1, W_C),
                   conv_ln_g[l].reshape(1, W_C), conv_ln_b[l].reshape(1, W_C))

        x1, u2t, route = _outproj(x_src, ha_c, ha_l, hb_c, hb_l, hc, mod, w_o[l].astype(BF16), post1_g[l].reshape(1, D_MODEL),
                                  post1_b[l].reshape(1, D_MODEL), w_rt, b_r)
        y = _moe(u2t, *_route_tables(route[0]), w_e_gate, w_e_up, w_e_down, l)
        outs = _final(x1, y, mod, post2_g[l].reshape(1, D_MODEL), post2_b[l].reshape(1, D_MODEL), l == DEPTH - 1)
        x_src = (outs[0],)

    y_prompt = outs[0].reshape(BATCH, SEQ, D_MODEL)
    y_sample = outs[1].reshape(DEC_BATCH, DEC_SEQ, D_MODEL)
    return (y_prompt, y_sample, st_c, st_n.reshape(BATCH, DEPTH, N_DIR, H_A, DH_A),
            st_m[:, :, :, 0].reshape(BATCH, DEPTH, N_DIR, H_A), st_h)
```

```python
import functools

import jax
import jax.numpy as jnp
import numpy as np
from jax import lax
from jax.experimental import pallas as pl
from jax.experimental.pallas import tpu as pltpu

D_MODEL = 1024
BATCH = 32
SEQ = 256
DEPTH = 2
DEC_BATCH = 2
DEC_SEQ = 1024
GRID_W = 64
N_DIR = 2
CHUNK = 128
H_A = 4
DH_A = 128
W_A = H_A * DH_A
H_B = 8
P_B = 64
W_B = H_B * P_B
G_B = 2
N_B = 128
W_XBC = W_B + 2 * G_B * N_B
SSM_CONV = 3
W_C = 512
CONV_W = 31
D_MIX = W_A + W_B + W_C
N_EXPERTS = 16
N_EGROUPS = 4
EXPERTS_PER_GROUP = N_EXPERTS // N_EGROUPS
D_FF_EXPERT = 512
ALPHA = (2 * DEPTH) ** 0.25
EPS = 1e-5
F32 = jnp.float32
BF16 = jnp.bfloat16

N_CTX_TOK = BATCH * SEQ
N_LAT_TOK = DEC_BATCH * DEC_SEQ
T_ALL = N_CTX_TOK + N_LAT_TOK
N_MOD_ROWS = 8
D_MAIN = W_A + 2 * W_C + G_B * N_B
D_T = 3 * W_A + 2 * W_B + G_B * N_B + 128
LANES = 128
HALO = 16
CONV_ROWS = 256
TM_PROJ = 512
TM_MOE = 256
ROW_GROUP = 32
VMEM_LIMIT = 48 * 1024 * 1024
BIG_VMEM_LIMIT = 56 * 1024 * 1024

_NT = (((1,), (1,)), ((), ()))
_TN = (((0,), (0,)), ((), ()))


def _ln(x):
    mu = jnp.mean(x, axis=-1, keepdims=True)
    xc = x - mu
    var = jnp.mean(xc * xc, axis=-1, keepdims=True)
    return xc * lax.rsqrt(var + EPS)


def _sigmoid(x):
    return 1.0 / (1.0 + jnp.exp(-x))


def _silu(x):
    return x * _sigmoid(x)


def _softplus(x):
    return jnp.maximum(x, 0.0) + jnp.log1p(jnp.exp(-jnp.abs(x)))


def _dot(a, b):
    return jnp.dot(a, b, preferred_element_type=F32)


def _dotg(a, b, dims):
    return lax.dot_general(a, b, dims, preferred_element_type=F32)


def _tri_cumsum(tri, x):
    hi = x.astype(BF16)
    r1 = x - hi.astype(F32)
    mid = r1.astype(BF16)
    lo = (r1 - mid.astype(F32)).astype(BF16)
    return _dot(tri, hi) + _dot(tri, mid) + _dot(tri, lo)


def _tri_mask(d):
    row = lax.broadcasted_iota(jnp.int32, (CHUNK, CHUNK), 0)
    col = lax.broadcasted_iota(jnp.int32, (CHUNK, CHUNK), 1)
    return (row >= col) if d == 0 else (row <= col)


def _mod_row(row_start):
    return jnp.where(row_start < N_CTX_TOK, 0, 1 + (row_start - N_CTX_TOK) // DEC_SEQ)


def _cparams(n_axes, vmem_limit=VMEM_LIMIT):
    return pltpu.CompilerParams(dimension_semantics=("arbitrary",) * n_axes, vmem_limit_bytes=vmem_limit)


def _ada_kernel(c_ref, w_ref, b_ref, o_ref):
    o_ref[0] = _dot(_silu(c_ref[...]), w_ref[0]) + b_ref[0]


def _ada(cvec, w_ada, b_ada):
    tn = 1536
    return pl.pallas_call(
        _ada_kernel,
        grid=(DEPTH, 6 * D_MODEL // tn),
        in_specs=[
            pl.BlockSpec((N_MOD_ROWS, D_MODEL), lambda l, j: (0, 0)),
            pl.BlockSpec((1, D_MODEL, tn), lambda l, j: (l, 0, j)),
            pl.BlockSpec((1, 1, tn), lambda l, j: (l, 0, j)),
        ],
        out_specs=pl.BlockSpec((1, N_MOD_ROWS, tn), lambda l, j: (l, 0, j)),
        out_shape=jax.ShapeDtypeStruct((DEPTH, N_MOD_ROWS, 6 * D_MODEL), F32),
        compiler_params=_cparams(2),
        name="ada",
    )(cvec, w_ada, b_ada.reshape(DEPTH, 1, 6 * D_MODEL))


def _x_specs(x_src, tm):
    if len(x_src) == 1:
        return [pl.BlockSpec((tm, D_MODEL), lambda i: (i, 0))]
    n_ctx_blk = N_CTX_TOK // tm
    per_seq = DEC_SEQ // tm
    return [pl.BlockSpec((tm, D_MODEL), lambda i: (jnp.minimum(i, n_ctx_blk - 1), 0)),
            pl.BlockSpec((tm, D_MODEL), lambda i: (jnp.maximum(i - n_ctx_blk, 0), 0)),
            pl.BlockSpec((tm, D_MODEL), lambda i: (jnp.where(i < n_ctx_blk, 0, (i - n_ctx_blk) % per_seq), 0))]


def _read_x(x_refs, i, tm):
    if len(x_refs) == 1:
        return x_refs[0][...]
    xp_ref, xs_ref, pos_ref = x_refs
    return jnp.where(i < N_CTX_TOK // tm, xp_ref[...], xs_ref[...] + pos_ref[...])


def _inproj_kernel(*refs, tm, n_x):
    x_refs = refs[:n_x]
    mod_ref, wm_ref, wg_ref, wt_ref, om_ref, og_ref, ot_ref = refs[n_x:]
    i = pl.program_id(0)
    r = _mod_row(i * tm)
    shift = mod_ref[pl.ds(r, 1), 0:D_MODEL]
    scale = mod_ref[pl.ds(r, 1), D_MODEL:2 * D_MODEL]
    u = (_ln(_read_x(x_refs, i, tm)) * (1.0 + scale) + shift).astype(BF16)
    om_ref[...] = _dot(u, wm_ref[...])
    og_ref[...] = _dot(u, wg_ref[...])
    ot_ref[...] = _dotg(wt_ref[...], u, _NT)


def _inproj(x_src, mod, w_main, w_small, w_t):
    tm = TM_PROJ
    const = lambda shape: pl.BlockSpec(shape, lambda i: (0, 0))
    return pl.pallas_call(
        functools.partial(_inproj_kernel, tm=tm, n_x=len(x_src)),
        grid=(T_ALL // tm,),
        in_specs=_x_specs(x_src, tm) + [
            const((N_MOD_ROWS, 6 * D_MODEL)), const((D_MODEL, D_MAIN)), const((D_MODEL, LANES)),
            const((D_T, D_MODEL)),
        ],
        out_specs=[
            pl.BlockSpec((tm, D_MAIN), lambda i: (i, 0)),
            pl.BlockSpec((tm, LANES), lambda i: (i, 0)),
            pl.BlockSpec((D_T, tm), lambda i: (0, i)),
        ],
        out_shape=[
            jax.ShapeDtypeStruct((T_ALL, D_MAIN), F32),
            jax.ShapeDtypeStruct((T_ALL, LANES), F32),
            jax.ShapeDtypeStruct((D_T, T_ALL), F32),
        ],
        compiler_params=_cparams(1, BIG_VMEM_LIMIT),
        name="inproj",
    )(*x_src, mod, w_main, w_small, w_t)


def _scan_max(x, lane, d):
    s = 1
    while s < CHUNK:
        if d == 0:
            x = jnp.where(lane >= s, jnp.maximum(x, pltpu.roll(x, s, axis=1)), x)
        else:
            x = jnp.where(lane < CHUNK - s, jnp.maximum(x, pltpu.roll(x, CHUNK - s, axis=1)), x)
        s *= 2
    return x


def _split_dot(x, tri):
    hi = x.astype(BF16)
    r1 = x - hi.astype(F32)
    mid = r1.astype(BF16)
    lo = (r1 - mid.astype(F32)).astype(BF16)
    return _dot(hi, tri) + _dot(mid, tri) + _dot(lo, tri)


def _mlstm_kernel(*refs, seq, has_init, emit_state, layer):
    it = iter(refs)
    k_ref, qt_ref, vt_ref, ot_ref, g_ref, gt_ref, gb_ref, gbt_ref, ng_ref = (next(it) for _ in range(9))
    if has_init:
        c0_ref, n0_ref, m0_ref = (next(it) for _ in range(3))
    if emit_state and layer > 0:
        prev_refs = [next(it) for _ in range(3)]
    out_ref = next(it)
    if emit_state:
        co_ref, no_ref, mo_ref = (next(it) for _ in range(3))
    tg_scr, tgt_scr, b_scr, beta_scr, pm_scr, cumc_scr, hd_scr, c_scr, n_scr, m_scr = (next(it) for _ in range(10))
    n_chunks = seq // CHUNK
    n_gate = N_DIR * 2 * H_A
    if emit_state and layer > 0:
        for prev_ref, st_ref in zip(prev_refs, (co_ref, no_ref, mo_ref)):
            st_ref[0, 0:layer] = prev_ref[0]

    y = g_ref[...] + gb_ref[...]
    lane = lax.broadcasted_iota(jnp.int32, y.shape, 1)
    tg_scr[...] = jnp.where((lane < n_gate) & ((lane & H_A) != 0), -_softplus(-y), y)
    yt = gt_ref[0:n_gate, :] + gbt_ref[0:n_gate, :]
    rowt = lax.broadcasted_iota(jnp.int32, yt.shape, 0)
    tgt_scr[...] = jnp.where((rowt & H_A) != 0, -_softplus(-yt), yt)

    lane8 = lax.broadcasted_iota(jnp.int32, (8, CHUNK), 1)
    row8 = lax.broadcasted_iota(jnp.int32, (8, LANES), 0)
    masks_t = [_tri_mask(1 - d) for d in range(N_DIR)]

    for d in range(N_DIR):
        tri_t = masks_t[d].astype(BF16)
        tri_c = _tri_mask(d).astype(BF16)
        rows = slice(d * 8, (d + 1) * 8)
        for c in range(n_chunks):
            cs = slice(c * CHUNK, (c + 1) * CHUNK)
            gr8 = tgt_scr[rows, cs]
            b8 = pltpu.roll(_split_dot(gr8, tri_t), H_A, axis=0)
            beta8 = gr8 - b8
            b_scr[rows, cs] = b8
            beta_scr[rows, cs] = beta8
            pm_scr[rows, cs] = _scan_max(beta8, lane8, d)
            cumc_scr[d, cs, :] = _tri_cumsum(tri_c, tg_scr[cs, :])

    pos = (CHUNK - 1, 0)

    for d in range(N_DIR):
        rows = slice(d * 8, (d + 1) * 8)
        if has_init:
            for h in range(H_A):
                c_scr[d, h] = c0_ref[0, 0, d, h].T
            n0 = n0_ref[0, 0]
            m0 = m0_ref[0, 0]
            if d == 1:
                n0 = pltpu.roll(n0, H_A, axis=0)
                m0 = pltpu.roll(m0, H_A, axis=0)
            n_scr[rows, :] = jnp.where(row8 < H_A, n0, 0.0)
            m_scr[rows, :] = jnp.where(row8 < H_A, m0, 0.0)
        else:
            for h in range(H_A):
                c_scr[d, h] = jnp.zeros((DH_A, DH_A), F32)
            n_scr[rows, :] = jnp.zeros((8, DH_A), F32)
            m_scr[rows, :] = jnp.zeros((8, LANES), F32)

    def chunk_prep(d, c):
        rows = slice(d * 8, (d + 1) * 8)
        cs = slice(c * CHUNK, (c + 1) * CHUNK)
        b8 = b_scr[rows, cs]
        beta8 = beta_scr[rows, cs]
        m8 = m_scr[rows, :]
        n8 = n_scr[rows, :]
        mu8 = jnp.maximum(m8, pm_scr[rows, cs])
        mu_last = mu8[:, pos[d]:pos[d] + 1]
        w_tok8 = jnp.exp(beta8 - mu_last)
        return dict(cs=cs, rows=rows, n8=n8, mu8=mu8, w_int8=jnp.exp(m8 - mu8), emt8=jnp.exp(-(b8 + mu8)),
                    w_tok8=w_tok8, w_prev8=jnp.exp(m8 - mu_last), m_new8=b8[:, pos[d]:pos[d] + 1] + mu_last,
                    n16b=jnp.concatenate([n8, jnp.zeros_like(n8)], axis=0).astype(BF16),
                    w_tok16=jnp.concatenate([w_tok8, jnp.zeros_like(w_tok8)], axis=0),
                    gc=tg_scr[cs, :], cum_c=cumc_scr[d, cs, :])

    def head_step(d, h, p):
        cs = p["cs"]
        hs = slice(h * DH_A, (h + 1) * DH_A)
        col_i = d * 2 * H_A + h
        col_f = col_i + H_A
        beta_c = p["gc"][:, col_i:col_i + 1] - p["cum_c"][:, col_f:col_f + 1]
        kb = k_ref[cs, hs].astype(BF16)
        qtb = (qt_ref[hs, cs] * (DH_A ** -0.5)).astype(BF16)
        vt = vt_ref[hs, cs]
        ct = c_scr[d, h]
        w_int_r = p["w_int8"][h:h + 1, :]

        res = _dot(jnp.concatenate([kb, ct.astype(BF16), p["n16b"]], axis=0), qtb)
        st = res[0:CHUNK] * jnp.exp(jnp.where(masks_t[d], beta_c - p["mu8"][h:h + 1, :], -jnp.inf))
        num = w_int_r * res[CHUNK:CHUNK + DH_A] + _dot(vt.astype(BF16), st.astype(BF16))
        den_r = (w_int_r * res[CHUNK + DH_A + h:CHUNK + DH_A + h + 1]
                 + jnp.sum(st, axis=0, keepdims=True))
        hd_scr[d, hs, cs] = num * (1.0 / jnp.maximum(jnp.abs(den_r), p["emt8"][h:h + 1, :]))

        upd = _dot(jnp.concatenate([vt * p["w_tok8"][h:h + 1, :], p["w_tok16"]], axis=0).astype(BF16), kb)
        c_scr[d, h] = p["w_prev8"][h:h + 1, :] * ct + upd[0:DH_A]
        return upd[DH_A:DH_A + 8]

    for ci in range(n_chunks):
        preps = [chunk_prep(0, ci), chunk_prep(1, n_chunks - 1 - ci)]
        nk_acc = [jnp.zeros((8, DH_A), F32) for _ in range(N_DIR)]
        for h in range(H_A):
            for d in range(N_DIR):
                nk_acc[d] = jnp.where(row8 == h, head_step(d, h, preps[d]), nk_acc[d])
        for d in range(N_DIR):
            p = preps[d]
            n_scr[p["rows"], :] = p["w_prev8"] * p["n8"] + nk_acc[d]
            m_scr[p["rows"], :] = jnp.where(row8 < H_A, jnp.broadcast_to(p["m_new8"], (8, LANES)), 0.0)

    if emit_state:
        for d in range(N_DIR):
            for h in range(H_A):
                co_ref[0, layer, d, h] = c_scr[d, h].T
            no_ref[0, layer, d * H_A:(d + 1) * H_A, :] = n_scr[d * 8:d * 8 + H_A, :]
            mo_ref[0, layer, d * H_A:(d + 1) * H_A, :] = m_scr[d * 8:d * 8 + H_A, :]

    for c in range(n_chunks):
        cs = slice(c * CHUNK, (c + 1) * CHUNK)
        for h in range(H_A):
            hs = slice(h * DH_A, (h + 1) * DH_A)
            hsum = hd_scr[0, hs, cs] + hd_scr[1, hs, cs]
            mean = jnp.sum(hsum, axis=0, keepdims=True) * (1.0 / DH_A)
            xc = hsum - mean
            var = jnp.sum(xc * xc, axis=0, keepdims=True) * (1.0 / DH_A)
            hn = xc * lax.rsqrt(var + EPS) * ng_ref[hs, :]
            out_ref[hs, cs] = (hn * _sigmoid(ot_ref[hs, cs])).astype(out_ref.dtype)


def _mlstm(proj, gates, proj_t, gate_bias, gate_bias_t, norm_g_rep, seq, n_seq, row_off, layer, init=None, prev=None):
    blk0 = row_off // seq
    has_init = init is not None
    emit_state = not has_init
    feat = lambda r: pl.BlockSpec((W_A, seq), lambda s: (r, s + blk0))
    const = lambda shape: pl.BlockSpec(shape, lambda s: (0,) * len(shape))
    st_shapes = [(N_DIR, H_A, DH_A, DH_A), (N_DIR * H_A, DH_A), (N_DIR * H_A, LANES)]

    def st_specs(n_layers, first):
        return [pl.BlockSpec((1, n_layers) + shp, lambda s, nd=len(shp): (s, first) + (0,) * nd) for shp in st_shapes]

    in_specs = [pl.BlockSpec((seq, W_A), lambda s: (s + blk0, 0)), feat(0), feat(1), feat(2),
                pl.BlockSpec((seq, LANES), lambda s: (s + blk0, 0)),
                pl.BlockSpec((LANES, seq), lambda s: (D_T // LANES - 1, s + blk0)),
                const((1, LANES)), const((LANES, 1)), const((W_A, LANES))]
    args = [proj, proj_t, proj_t, proj_t, gates, proj_t, gate_bias, gate_bias_t, norm_g_rep]
    if has_init:
        in_specs += st_specs(1, layer)
        args += list(init)
    if emit_state and layer > 0:
        in_specs += st_specs(layer, 0)
        args += list(prev)
    out_specs = [pl.BlockSpec((W_A, seq), lambda s: (0, s))]
    out_shape = [jax.ShapeDtypeStruct((W_A, n_seq * seq), BF16)]
    if emit_state:
        out_specs += st_specs(layer + 1, 0)
        out_shape += [jax.ShapeDtypeStruct((n_seq, layer + 1) + shp, F32) for shp in st_shapes]
    return pl.pallas_call(
        functools.partial(_mlstm_kernel, seq=seq, has_init=has_init, emit_state=emit_state, layer=layer),
        grid=(n_seq,),
        in_specs=in_specs,
        out_specs=out_specs,
        out_shape=out_shape,
        scratch_shapes=[pltpu.VMEM((seq, LANES), F32),
                        pltpu.VMEM((N_DIR * 8, seq), F32),
                        pltpu.VMEM((N_DIR * 8, seq), F32),
                        pltpu.VMEM((N_DIR * 8, seq), F32),
                        pltpu.VMEM((N_DIR * 8, seq), F32),
                        pltpu.VMEM((N_DIR, seq, LANES), F32),
                        pltpu.VMEM((N_DIR, W_A, seq), F32),
                        pltpu.VMEM((N_DIR, H_A, DH_A, DH_A), F32),
                        pltpu.VMEM((N_DIR * 8, DH_A), F32),
                        pltpu.VMEM((N_DIR * 8, LANES), F32)],
        compiler_params=_cparams(1),
        name="mlstm_lat" if has_init else "mlstm_ctx",
    )(*args)


_DT_COL0 = N_DIR * 2 * H_A
_HG = H_B // G_B
_WG = _HG * P_B


def _conv3_lanes(ref, w_ref, b_ref, out_scr, seq):
    lane = lax.broadcasted_iota(jnp.int32, (CHUNK, seq), 1)
    for r in range(ref.shape[0] // CHUNK):
        rs = slice(r * CHUNK, (r + 1) * CHUNK)
        cur = ref[rs, :]
        prev = jnp.where(lane == 0, 0.0, pltpu.roll(cur, 1, axis=1))
        nxt = jnp.where(lane == seq - 1, 0.0, pltpu.roll(cur, seq - 1, axis=1))
        for c in range(seq // CHUNK):
            cs = slice(c * CHUNK, (c + 1) * CHUNK)
            acc = (w_ref[0, rs, :] * prev[:, cs] + w_ref[1, rs, :] * cur[:, cs] + w_ref[2, rs, :] * nxt[:, cs]
                   + b_ref[rs, :])
            out_scr[rs, cs] = _silu(acc).astype(out_scr.dtype)


def _ssd_kernel(*refs, seq, has_init, emit_state, layer):
    it = iter(refs)
    (zt_ref, xt_ref, ct_ref, b_ref, g_ref, gt_ref, gb_ref, gbt_ref, alog_ref, alogt_ref,
     cwx_ref, cbx_ref, cwc_ref, cbc_ref, cwb_ref, cbb_ref, dsk_ref, ng_ref) = (next(it) for _ in range(18))
    if has_init:
        h0_ref = next(it)
    if emit_state and layer > 0:
        prev_ref = next(it)
    out_ref = next(it)
    if emit_state:
        ho_ref = next(it)
    pad_scr, xs_scr, cs_scr, bs_scr, dtr_scr, br_scr, cumc_scr, yt_scr, h_scr = (next(it) for _ in range(9))
    n_chunks = seq // CHUNK
    pad = 8
    if emit_state and layer > 0:
        ho_ref[0, 0:layer] = prev_ref[0]

    _conv3_lanes(xt_ref, cwx_ref, cbx_ref, xs_scr, seq)
    _conv3_lanes(ct_ref, cwc_ref, cbc_ref, cs_scr, seq)
    nb = G_B * N_B
    pad_scr[0:pad, :] = jnp.zeros((pad, nb), F32)
    pad_scr[pad + seq:2 * pad + seq, :] = jnp.zeros((pad, nb), F32)
    pad_scr[pad:pad + seq, :] = b_ref[...]
    for c in range(n_chunks):
        acc = cbb_ref[...]
        for k in range(SSM_CONV):
            r = c * CHUNK + pad - SSM_CONV // 2 + k
            acc = acc + cwb_ref[k:k + 1, :] * pad_scr[r:r + CHUNK, :]
        bs_scr[c * CHUNK:(c + 1) * CHUNK, :] = _silu(acc).astype(bs_scr.dtype)

    lac = _softplus(g_ref[...] + gb_ref[...]) * (-jnp.exp(alog_ref[...]))
    r0 = _DT_COL0
    dtr = _softplus(gt_ref[r0:r0 + N_DIR * H_B, :] + gbt_ref[r0:r0 + N_DIR * H_B, :])
    dtr_scr[...] = dtr
    lar = dtr * (-jnp.exp(alogt_ref[r0:r0 + N_DIR * H_B, :]))
    masks_t = [_tri_mask(1 - d) for d in range(N_DIR)]
    for d in range(N_DIR):
        tri_t = masks_t[d].astype(BF16)
        tri_c = _tri_mask(d).astype(BF16)
        rows = slice(d * H_B, (d + 1) * H_B)
        for c in range(n_chunks):
            cs = slice(c * CHUNK, (c + 1) * CHUNK)
            br_scr[rows, cs] = _split_dot(lar[rows, cs], tri_t)
            cumc_scr[d, cs, :] = _tri_cumsum(tri_c, lac[cs, :])

    row_blk = lax.broadcasted_iota(jnp.int32, (_WG, 1), 0) // P_B

    def run_direction(d):
        for g in range(G_B):
            for hh in range(_HG):
                if has_init:
                    h_scr[d, g, hh * P_B:(hh + 1) * P_B, :] = h0_ref[0, 0, d, g * _HG + hh]
                else:
                    h_scr[d, g, hh * P_B:(hh + 1) * P_B, :] = jnp.zeros((P_B, N_B), F32)

        mask_t = masks_t[d]
        pos = CHUNK - 1 if d == 0 else 0

        for ci in range(n_chunks):
            c = ci if d == 0 else n_chunks - 1 - ci
            cs = slice(c * CHUNK, (c + 1) * CHUNK)
            cum_c = cumc_scr[d, cs, :]
            for g in range(G_B):
                bg = bs_scr[cs, g * N_B:(g + 1) * N_B]
                ctg = cs_scr[g * N_B:(g + 1) * N_B, cs]
                h_st = h_scr[d, g]
                res = _dot(jnp.concatenate([bg, h_st.astype(BF16)], axis=0), ctg)
                cbt = res[0:CHUNK]
                inter = res[CHUNK:CHUNK + _WG]
                xw = []
                decay = jnp.zeros((_WG, 1), F32)
                for hh in range(_HG):
                    head = g * _HG + hh
                    r = d * H_B + head
                    b_r = br_scr[r:r + 1, cs]
                    dt_r = dtr_scr[r:r + 1, cs]
                    b_c = cum_c[:, _DT_COL0 + r:_DT_COL0 + r + 1]
                    gt = b_r[:, pos:pos + 1]
                    ps = slice(head * P_B, (head + 1) * P_B)
                    xh = xs_scr[ps, cs]
                    st = (cbt * jnp.exp(jnp.where(mask_t, b_r - b_c, -jnp.inf))).astype(BF16)
                    y = _dot((xh * dt_r).astype(BF16), st) + jnp.exp(b_r) * inter[hh * P_B:(hh + 1) * P_B]
                    xw.append(xh * (jnp.exp(gt - b_r) * dt_r))
                    decay = jnp.where(row_blk == hh, jnp.exp(gt), decay)
                    if d == 0:
                        yt_scr[ps, cs] = y
                    else:
                        yt_scr[ps, cs] = yt_scr[ps, cs] + y + dsk_ref[ps, :] * xh
                h_scr[d, g] = decay * h_st + _dot(jnp.concatenate(xw, axis=0).astype(BF16), bg)
            if d == 1:
                yz = yt_scr[:, cs] * _silu(zt_ref[:, cs])
                rms = lax.rsqrt(jnp.sum(yz * yz, axis=0, keepdims=True) * (1.0 / W_B) + EPS)
                out_ref[:, cs] = (yz * rms * ng_ref[...]).astype(out_ref.dtype)

        if emit_state:
            for g in range(G_B):
                for hh in range(_HG):
                    ho_ref[0, layer, d, g * _HG + hh] = h_scr[d, g, hh * P_B:(hh + 1) * P_B, :]

    run_direction(0)
    run_direction(1)


def _ssd(proj, gates, proj_t, gate_bias, gate_bias_t, alog_row, alog_col, conv, d_rep, norm_g_rep,
         seq, n_seq, row_off, layer, init=None, prev=None):
    blk0 = row_off // seq
    has_init = init is not None
    emit_state = not has_init
    nb = G_B * N_B
    zr = 3 * W_A // W_B
    cr = (3 * W_A + 2 * W_B) // nb
    gr = (3 * W_A + 2 * W_B + nb) // LANES
    bc = (W_A + 2 * W_C) // nb
    const = lambda shape: pl.BlockSpec(shape, lambda s: (0,) * len(shape))
    st_spec = lambda n_layers, first: pl.BlockSpec((1, n_layers, N_DIR, H_B, P_B, N_B),
                                                   lambda s: (s, first, 0, 0, 0, 0))
    in_specs = [pl.BlockSpec((W_B, seq), lambda s: (zr, s + blk0)),
                pl.BlockSpec((W_B, seq), lambda s: (zr + 1, s + blk0)),
                pl.BlockSpec((nb, seq), lambda s: (cr, s + blk0)),
                pl.BlockSpec((seq, nb), lambda s: (s + blk0, bc)),
                pl.BlockSpec((seq, LANES), lambda s: (s + blk0, 0)),
                pl.BlockSpec((LANES, seq), lambda s: (gr, s + blk0)),
                const((1, LANES)), const((LANES, 1)), const((1, LANES)), const((LANES, 1)),
                const((SSM_CONV, W_B, LANES)), const((W_B, LANES)), const((SSM_CONV, nb, LANES)), const((nb, LANES)),
                const((8, nb)), const((1, nb)), const((W_B, LANES)), const((W_B, LANES))]
    args = [proj_t, proj_t, proj_t, proj, gates, proj_t, gate_bias, gate_bias_t, alog_row, alog_col,
            *conv, d_rep, norm_g_rep]
    if has_init:
        in_specs.append(st_spec(1, layer))
        args.append(init)
    if emit_state and layer > 0:
        in_specs.append(st_spec(layer, 0))
        args.append(prev)
    out_specs = [pl.BlockSpec((W_B, seq), lambda s: (0, s))]
    out_shape = [jax.ShapeDtypeStruct((W_B, n_seq * seq), BF16)]
    if emit_state:
        out_specs.append(st_spec(layer + 1, 0))
        out_shape.append(jax.ShapeDtypeStruct((n_seq, layer + 1, N_DIR, H_B, P_B, N_B), F32))
    return pl.pallas_call(
        functools.partial(_ssd_kernel, seq=seq, has_init=has_init, emit_state=emit_state, layer=layer),
        grid=(n_seq,),
        in_specs=in_specs,
        out_specs=out_specs,
        out_shape=out_shape,
        scratch_shapes=[pltpu.VMEM((seq + 16, nb), F32),
                        pltpu.VMEM((W_B, seq), F32),
                        pltpu.VMEM((nb, seq), BF16),
                        pltpu.VMEM((seq, nb), BF16),
                        pltpu.VMEM((N_DIR * H_B, seq), F32),
                        pltpu.VMEM((N_DIR * H_B, seq), F32),
                        pltpu.VMEM((N_DIR, seq, LANES), F32),
                        pltpu.VMEM((W_B, seq), F32),
                        pltpu.VMEM((N_DIR, G_B, _WG, N_B), F32)],
        compiler_params=_cparams(1),
        name="ssd_lat" if has_init else "ssd_ctx",
    )(*args)


def _conf_kernel(a_ref, ap_ref, an_ref, g_ref, gp_ref, gn_ref, w_ref, b_ref, lg_ref, lb_ref, out_ref, pad_scr,
                 rot_scr):
    i = pl.program_id(0)
    n_ctx_blk = N_CTX_TOK // CONV_ROWS
    blk_per_lat = DEC_SEQ // CONV_ROWS
    j = (i - n_ctx_blk) % blk_per_lat
    is_ctx = i < n_ctx_blk
    keep_prev = jnp.where(is_ctx | (j == 0), 0.0, 1.0)
    keep_next = jnp.where(is_ctx | (j == blk_per_lat - 1), 0.0, 1.0)
    pad_scr[0:HALO, :] = ap_ref[...] * _sigmoid(gp_ref[...]) * keep_prev
    pad_scr[HALO:HALO + CONV_ROWS, :] = a_ref[...] * _sigmoid(g_ref[...])
    pad_scr[HALO + CONV_ROWS:2 * HALO + CONV_ROWS, :] = an_ref[...] * _sigmoid(gn_ref[...]) * keep_next
    n_rot = CONV_ROWS + 2 * HALO - 8
    for s in range(1, 8):
        rot_scr[s, :, :] = pad_scr[s:s + n_rot, :]
    rc = 64
    for c in range(CONV_ROWS // rc):
        acc = jnp.broadcast_to(b_ref[...], (rc, W_C))
        for k in range(CONV_W):
            off = HALO - CONV_W // 2 + k
            r = c * rc + off - off % 8
            src = pad_scr[r:r + rc, :] if off % 8 == 0 else rot_scr[off % 8, r:r + rc, :]
            acc = acc + w_ref[k:k + 1, :] * src
        u = _ln(acc) * lg_ref[...] + lb_ref[...]
        out_ref[c * rc:(c + 1) * rc, :] = _silu(u).astype(out_ref.dtype)


def _conf(proj, dw_w, dw_b, ln_g, ln_b):
    assert SEQ == CONV_ROWS and DEC_SEQ % CONV_ROWS == 0
    ac = W_A // W_C
    hb = CONV_ROWS // HALO
    n_halo = T_ALL // HALO
    const = lambda shape: pl.BlockSpec(shape, lambda i: (0,) * len(shape))

    def specs(c):
        return [pl.BlockSpec((CONV_ROWS, W_C), lambda i: (i, c)),
                pl.BlockSpec((HALO, W_C), lambda i: (jnp.maximum(i * hb - 1, 0), c)),
                pl.BlockSpec((HALO, W_C), lambda i: (jnp.minimum((i + 1) * hb, n_halo - 1), c))]

    return pl.pallas_call(
        _conf_kernel,
        grid=(T_ALL // CONV_ROWS,),
        in_specs=specs(ac) + specs(ac + 1) + [const((32, W_C)), const((1, W_C)), const((1, W_C)), const((1, W_C))],
        out_specs=pl.BlockSpec((CONV_ROWS, W_C), lambda i: (i, 0)),
        out_shape=jax.ShapeDtypeStruct((T_ALL, W_C), BF16),
        scratch_shapes=[pltpu.VMEM((CONV_ROWS + 2 * HALO, W_C), F32),
                        pltpu.VMEM((8, CONV_ROWS + 2 * HALO - 8, W_C), F32)],
        compiler_params=_cparams(1),
        name="conf",
    )(proj, proj, proj, proj, proj, proj, dw_w, dw_b, ln_g, ln_b)


_PAIRS = [(a, b) for a in range(EXPERTS_PER_GROUP) for b in range(a + 1, EXPERTS_PER_GROUP)]
N_CLASSES = N_EGROUPS * len(_PAIRS)
N_FCHUNK = D_MODEL // LANES
N_IN_ROWS = 2 * N_FCHUNK


def _store_token_tiles(ref, val):
    n = val.shape[0]
    for c in range(N_FCHUNK):
        ref[pl.ds(c, n, stride=N_FCHUNK), :] = val[:, c * LANES:(c + 1) * LANES]


def _load_token_tiles(ref, n):
    return jnp.concatenate([ref[pl.ds(c, n, stride=N_FCHUNK), :] for c in range(N_FCHUNK)], axis=1)
N_MOE_ROWS = T_ALL + N_CLASSES * TM_MOE
N_MOE_TILES = N_MOE_ROWS // TM_MOE


def _outproj_kernel(*refs, tm, n_x):
    x_refs = refs[:n_x]
    (hac_ref, hal_ref, hbc_ref, hbl_ref, hc_ref, mod_ref, wo_ref, pg_ref, pb_ref,
     wr_ref, br_ref, x1_ref, u2_ref, rt_ref) = refs[n_x:]
    i = pl.program_id(0)
    r = _mod_row(i * tm)
    gate1 = mod_ref[pl.ds(r, 1), 2 * D_MODEL:3 * D_MODEL]
    shift2 = mod_ref[pl.ds(r, 1), 3 * D_MODEL:4 * D_MODEL]
    scale2 = mod_ref[pl.ds(r, 1), 4 * D_MODEL:5 * D_MODEL]
    is_ctx = i < N_CTX_TOK // tm
    ha = jnp.where(is_ctx, hac_ref[...], hal_ref[...])
    hb = jnp.where(is_ctx, hbc_ref[...], hbl_ref[...])
    mix = (_dotg(ha, wo_ref[0:W_A, :], _TN) + _dotg(hb, wo_ref[W_A:W_A + W_B, :], _TN)
           + _dot(hc_ref[...], wo_ref[W_A + W_B:D_MIX, :]))
    x1 = _ln(ALPHA * _read_x(x_refs, i, tm) + gate1 * mix) * pg_ref[...] + pb_ref[...]
    x1_ref[...] = x1
    u2 = _ln(x1) * (1.0 + scale2) + shift2
    for c in range(N_FCHUNK):
        u2_ref[pl.ds(c, tm, stride=N_IN_ROWS), :] = u2[:, c * LANES:(c + 1) * LANES]

    wr = wr_ref[...]
    wr_hi = wr.astype(BF16)
    wr_lo = (wr - wr_hi.astype(F32)).astype(BF16)
    u2_hi = u2.astype(BF16)
    u2_lo = (u2 - u2_hi.astype(F32)).astype(BF16)
    both = _dotg(jnp.concatenate([wr_hi, wr_lo], axis=0), u2_hi, _NT)
    logits = both[0:N_EXPERTS] + both[N_EXPERTS:2 * N_EXPERTS] + _dotg(wr_hi, u2_lo, _NT) + br_ref[...]
    ex = jnp.exp(logits - jnp.max(logits, axis=0, keepdims=True))
    probs = ex / jnp.sum(ex, axis=0, keepdims=True)
    scores = []
    for g in range(N_EGROUPS):
        p = [probs[g * EXPERTS_PER_GROUP + e:g * EXPERTS_PER_GROUP + e + 1, :] for e in range(EXPERTS_PER_GROUP)]
        best = p[0] + p[1]
        for a in range(EXPERTS_PER_GROUP):
            for b in range(a + 1, EXPERTS_PER_GROUP):
                if (a, b) != (0, 1):
                    best = jnp.maximum(best, p[a] + p[b])
        scores.append(best)
    gmax = functools.reduce(jnp.maximum, scores)
    sel = jnp.full(gmax.shape, N_EGROUPS - 1, jnp.int32)
    for g in range(N_EGROUPS - 2, -1, -1):
        sel = jnp.where(scores[g] == gmax, g, sel)
    eidx = lax.broadcasted_iota(jnp.int32, probs.shape, 0)
    pm = jnp.where((eidx // EXPERTS_PER_GROUP) == sel, probs, -jnp.inf)
    p1 = jnp.max(pm, axis=0, keepdims=True)
    i1 = jnp.min(jnp.where(pm == p1, eidx, N_EXPERTS), axis=0, keepdims=True)
    pm2 = jnp.where(eidx == i1, -jnp.inf, pm)
    p2 = jnp.max(pm2, axis=0, keepdims=True)
    i2 = jnp.min(jnp.where(pm2 == p2, eidx, N_EXPERTS), axis=0, keepdims=True)
    den = p1 + p2
    first_lo = i1 < i2
    w_lo = jnp.where(first_lo, p1, p2) / den
    w_hi = jnp.where(first_lo, p2, p1) / den
    a = jnp.minimum(i1, i2) - sel * EXPERTS_PER_GROUP
    b = jnp.maximum(i1, i2) - sel * EXPERTS_PER_GROUP
    pair = jnp.zeros_like(a)
    for k, (pa, pb_) in enumerate(_PAIRS):
        pair = jnp.where((a == pa) & (b == pb_), k, pair)
    cls = sel * len(_PAIRS) + pair
    rt_ref[...] = jnp.broadcast_to(cls, rt_ref.shape)
    wrow = lax.broadcasted_iota(jnp.int32, (LANES, tm), 0)
    wt = jnp.where(wrow == 0, w_lo, jnp.where(wrow == 1, w_hi, 0.0))
    u2_ref[pl.ds(N_FCHUNK, tm, stride=N_IN_ROWS), :] = wt.T
    for c in range(N_FCHUNK + 1, N_IN_ROWS):
        u2_ref[pl.ds(c, tm, stride=N_IN_ROWS), :] = jnp.zeros((tm, LANES), F32)


def _outproj(x_src, ha_c, ha_l, hb_c, hb_l, hc, mod, w_o, pg, pb, w_rt, b_r):
    tm = TM_PROJ
    n_ctx_blk = N_CTX_TOK // tm
    const = lambda shape: pl.BlockSpec(shape, lambda i: (0,) * len(shape))
    rows = lambda w: pl.BlockSpec((tm, w), lambda i: (i, 0))
    ctx_feat = lambda w: pl.BlockSpec((w, tm), lambda i: (0, jnp.minimum(i, n_ctx_blk - 1)))
    lat_feat = lambda w: pl.BlockSpec((w, tm), lambda i: (0, jnp.maximum(i - n_ctx_blk, 0)))
    return pl.pallas_call(
        functools.partial(_outproj_kernel, tm=tm, n_x=len(x_src)),
        grid=(T_ALL // tm,),
        in_specs=_x_specs(x_src, tm) + [
                  ctx_feat(W_A), lat_feat(W_A), ctx_feat(W_B), lat_feat(W_B),
                  rows(W_C), const((N_MOD_ROWS, 6 * D_MODEL)),
                  const((D_MIX, D_MODEL)), const((1, D_MODEL)), const((1, D_MODEL)),
                  const((N_EXPERTS, D_MODEL)), const((N_EXPERTS, 1))],
        out_specs=[rows(D_MODEL), pl.BlockSpec((tm * N_IN_ROWS, LANES), lambda i: (i, 0)),
                   pl.BlockSpec((8, tm), lambda i: (0, i))],
        out_shape=[jax.ShapeDtypeStruct((T_ALL, D_MODEL), F32),
                   jax.ShapeDtypeStruct((T_ALL * N_IN_ROWS, LANES), F32),
                   jax.ShapeDtypeStruct((8, T_ALL), jnp.int32)],
        compiler_params=_cparams(1),
        name="outproj",
    )(*x_src, ha_c, ha_l, hb_c, hb_l, hc, mod, w_o, pg, pb, w_rt, b_r)


N_Y_ROWS = T_ALL + 2 * TM_MOE


def _moe_kernel(tok_ref, tstart_ref, nval_ref, tlo_ref, thi_ref, nused_ref,
                u_hbm, wgl_ref, wul_ref, wdl_ref, wgh_ref, wuh_ref, wdh_ref, y_hbm,
                xbuf0, xbuf1, ybuf0, ybuf1, gsem, ssem):
    del tlo_ref, thi_ref
    i = pl.program_id(0)
    n_used = nused_ref[0]
    tile_rows = TM_MOE * N_FCHUNK

    def token_tile(ref, tok, rows=N_FCHUNK):
        return ref.at[pl.ds(pl.multiple_of(tok * rows, rows), rows), :]

    def started_groups(n_valid):
        return [(q, q * ROW_GROUP < n_valid) for q in range(TM_MOE // ROW_GROUP)]

    def gather_rows(t, xb, sem):
        base = tstart_ref[t]
        for q, started in started_groups(nval_ref[t]):
            @pl.when(started)
            def _():
                for r in range(q * ROW_GROUP, (q + 1) * ROW_GROUP):
                    pltpu.make_async_copy(token_tile(u_hbm, tok_ref[base + r], N_IN_ROWS),
                                          token_tile(xb, r, N_IN_ROWS), sem).start()

    def scatter_rows(t, n_valid, yb, sem, spare):
        base = tstart_ref[t]
        for q, started in started_groups(n_valid):
            @pl.when(started)
            def _():
                for r in range(q * ROW_GROUP, (q + 1) * ROW_GROUP):
                    dst = jnp.where(r < n_valid, tok_ref[base + r], spare + r)
                    pltpu.make_async_copy(token_tile(yb, r), token_tile(y_hbm, dst), sem).start()

    def load_tile(xb):
        x = jnp.concatenate([xb[pl.ds(c, TM_MOE, stride=N_IN_ROWS), :] for c in range(N_FCHUNK)], axis=1)
        w = xb[pl.ds(N_FCHUNK, TM_MOE, stride=N_IN_ROWS), :]
        return x.astype(BF16), w[:, 0:1], w[:, 1:2]

    def wait_groups(n_valid, rows_per_token, copy):
        for q, started in started_groups(n_valid):
            @pl.when(started)
            def _():
                n = ROW_GROUP * rows_per_token
                copy(pl.ds(q * n, n)).wait()

    def wait_gather(t, xb, sem):
        wait_groups(nval_ref[t], N_IN_ROWS,
                    lambda rows: pltpu.make_async_copy(u_hbm.at[rows, :], xb.at[rows, :], sem))

    def wait_scatter(n_valid, yb, sem):
        wait_groups(n_valid, N_FCHUNK,
                    lambda rows: pltpu.make_async_copy(yb.at[rows, :], y_hbm.at[rows, :], sem))

    def valid_of(t):
        return jnp.where(t >= 0, nval_ref[jnp.maximum(t, 0)], 0)

    def step(s, xb, xb_next, yb, yb_prev):
        wait_scatter(valid_of(i - 2), yb, ssem.at[s])
        wait_gather(i, xb, gsem.at[s])
        nxt = jnp.minimum(i + 1, n_used - 1)

        @pl.when(n_used > 0)
        def _():
            gather_rows(nxt, xb_next, gsem.at[1 - s])
            scatter_rows(jnp.maximum(i - 1, 0), valid_of(i - 1), yb_prev, ssem.at[1 - s], T_ALL + (1 - s) * TM_MOE)

        x, w_lo, w_hi = load_tile(xb)

        def ffn(wg_ref, wu_ref, wd_ref, w):
            act = _silu(_dot(x, wg_ref[0, 0].astype(BF16))) * _dot(x, wu_ref[0, 0].astype(BF16)) * w
            return _dot(act.astype(BF16), wd_ref[0, 0].astype(BF16))

        _store_token_tiles(yb, ffn(wgl_ref, wul_ref, wdl_ref, w_lo) + ffn(wgh_ref, wuh_ref, wdh_ref, w_hi))

        @pl.when(i == n_used - 1)
        def _():
            scatter_rows(i, nval_ref[i], yb, ssem.at[s], T_ALL + s * TM_MOE)
            wait_scatter(nval_ref[i], yb, ssem.at[s])
            wait_scatter(valid_of(i - 1), yb_prev, ssem.at[1 - s])
            wait_gather(nxt, xb_next, gsem.at[1 - s])

    @pl.when(i == 0)
    def _():
        xbuf0[...] = jnp.zeros(xbuf0.shape, F32)
        xbuf1[...] = jnp.zeros(xbuf1.shape, F32)
        ybuf1[...] = jnp.zeros(ybuf1.shape, F32)
        for spare_set in range(2):
            fill = pltpu.make_async_copy(
                ybuf1, y_hbm.at[pl.ds((T_ALL + spare_set * TM_MOE) * N_FCHUNK, tile_rows), :], ssem.at[0])
            fill.start()
            fill.wait()
        gather_rows(0, xbuf0, gsem.at[0])

    @pl.when((i < n_used) & (i % 2 == 0))
    def _():
        step(0, xbuf0, xbuf1, ybuf0, ybuf1)

    @pl.when((i < n_used) & (i % 2 == 1))
    def _():
        step(1, xbuf1, xbuf0, ybuf1, ybuf0)


def _moe(u2t, tok_sorted, tile_start, n_valid, tile_lo, tile_hi, n_used, w_gate, w_up, w_down, layer):
    lo = lambda shape: pl.BlockSpec(shape, lambda i, tok, ts, nv, tlo, thi, nused: (layer, tlo[i], 0, 0))
    hi = lambda shape: pl.BlockSpec(shape, lambda i, tok, ts, nv, tlo, thi, nused: (layer, thi[i], 0, 0))
    up_shape = (1, 1, D_MODEL, D_FF_EXPERT)
    down_shape = (1, 1, D_FF_EXPERT, D_MODEL)
    in_buf = pltpu.VMEM((TM_MOE * N_IN_ROWS, LANES), F32)
    out_buf = pltpu.VMEM((TM_MOE * N_FCHUNK, LANES), F32)
    grid_spec = pltpu.PrefetchScalarGridSpec(
        num_scalar_prefetch=6,
        grid=(N_MOE_TILES,),
        in_specs=[pl.BlockSpec(memory_space=pl.ANY),
                  lo(up_shape), lo(up_shape), lo(down_shape),
                  hi(up_shape), hi(up_shape), hi(down_shape)],
        out_specs=pl.BlockSpec(memory_space=pl.ANY),
        scratch_shapes=[in_buf, in_buf, out_buf, out_buf,
                        pltpu.SemaphoreType.DMA((2,)),
                        pltpu.SemaphoreType.DMA((2,))],
    )
    return pl.pallas_call(
        _moe_kernel,
        grid_spec=grid_spec,
        out_shape=jax.ShapeDtypeStruct((N_Y_ROWS * N_FCHUNK, LANES), F32),
        compiler_params=pltpu.CompilerParams(dimension_semantics=("arbitrary",), vmem_limit_bytes=BIG_VMEM_LIMIT,
                                             has_side_effects=True),
        name="moe",
    )(tok_sorted, tile_start, n_valid, tile_lo, tile_hi, n_used, u2t, w_gate, w_up, w_down, w_gate, w_up, w_down)


def _route_tables(cls):
    _, tok_sorted = lax.sort((cls, jnp.arange(T_ALL, dtype=jnp.int32)), num_keys=1)
    cids = jnp.arange(N_CLASSES, dtype=jnp.int32)
    counts = jnp.sum((cls[:, None] == cids[None, :]).astype(jnp.int32), axis=0)
    offs = jnp.cumsum(counts) - counts
    padded = (counts + TM_MOE - 1) // TM_MOE * TM_MOE
    ends = jnp.cumsum(padded)
    offs_p = ends - padded
    n_used = ends[-1] // TM_MOE
    tile_start = jnp.arange(N_MOE_TILES, dtype=jnp.int32) * TM_MOE
    tile_cls = jnp.sum((ends[None, :] <= jnp.minimum(tile_start, ends[-1] - TM_MOE)[:, None]).astype(jnp.int32),
                       axis=1)
    onehot = (tile_cls[:, None] == cids[None, :]).astype(jnp.int32)
    pick = lambda table: jnp.sum(onehot * table[None, :], axis=1)
    k = tile_start - pick(offs_p)
    n_valid = jnp.where(tile_start < ends[-1], jnp.clip(pick(counts) - k, 0, TM_MOE), 0)
    tile_first = jnp.clip(pick(offs) + k, 0, T_ALL - 1)
    tile_lo = pick(cids // len(_PAIRS) * EXPERTS_PER_GROUP + jnp.array([p[0] for p in _PAIRS] * N_EGROUPS, jnp.int32))
    tile_hi = pick(cids // len(_PAIRS) * EXPERTS_PER_GROUP + jnp.array([p[1] for p in _PAIRS] * N_EGROUPS, jnp.int32))
    tok_sorted = jnp.concatenate([tok_sorted, jnp.zeros((TM_MOE,), jnp.int32)])
    i32 = lambda v: v.astype(jnp.int32)
    return tok_sorted, i32(tile_first), i32(n_valid), i32(tile_lo), i32(tile_hi), i32(n_used).reshape(1)


def _final_kernel(x1_ref, y_ref, mod_ref, pg_ref, pb_ref, *o_refs, tm):
    i = pl.program_id(0)
    r = _mod_row(i * tm)
    gate2 = mod_ref[pl.ds(r, 1), 5 * D_MODEL:6 * D_MODEL]
    out = _ln(ALPHA * x1_ref[...] + gate2 * _load_token_tiles(y_ref, tm)) * pg_ref[...] + pb_ref[...]
    if len(o_refs) == 1:
        o_refs[0][...] = out
    else:
        @pl.when(i < N_CTX_TOK // tm)
        def _():
            o_refs[0][...] = out

        @pl.when(i >= N_CTX_TOK // tm)
        def _():
            o_refs[1][...] = out


def _final(x1, y, mod, pg, pb, split):
    tm = TM_PROJ
    n_ctx_blk = N_CTX_TOK // tm
    const = lambda shape: pl.BlockSpec(shape, lambda i: (0,) * len(shape))
    rows = pl.BlockSpec((tm, D_MODEL), lambda i: (i, 0))
    if split:
        out_specs = [pl.BlockSpec((tm, D_MODEL), lambda i: (jnp.minimum(i, n_ctx_blk - 1), 0)),
                     pl.BlockSpec((tm, D_MODEL), lambda i: (jnp.maximum(i - n_ctx_blk, 0), 0))]
        out_shape = [jax.ShapeDtypeStruct((N_CTX_TOK, D_MODEL), F32), jax.ShapeDtypeStruct((N_LAT_TOK, D_MODEL), F32)]
    else:
        out_specs = [rows]
        out_shape = [jax.ShapeDtypeStruct((T_ALL, D_MODEL), F32)]
    return pl.pallas_call(
        functools.partial(_final_kernel, tm=tm),
        grid=(T_ALL // tm,),
        in_specs=[rows, pl.BlockSpec((tm * N_FCHUNK, LANES), lambda i: (i, 0)),
                  const((N_MOD_ROWS, 6 * D_MODEL)), const((1, D_MODEL)), const((1, D_MODEL))],
        out_specs=out_specs,
        out_shape=out_shape,
        compiler_params=_cparams(1),
        name="final",
    )(x1, y, mod, pg, pb)


def _grid_pos(n_tok):
    rows = n_tok // GRID_W
    r, col = np.meshgrid(np.arange(rows, dtype=np.float32), np.arange(GRID_W, dtype=np.float32), indexing='ij')
    quarter = D_MODEL // 4
    omega = (1.0 / (np.float32(10000.0) ** (np.arange(quarter, dtype=np.float32) / np.float32(quarter)))).astype(np.float32)

    def emb(p):
        ang = p.reshape(-1)[:, None] * omega[None, :]
        return np.concatenate([np.sin(ang), np.cos(ang)], axis=-1)

    return jnp.asarray(np.concatenate([emb(r), emb(col)], axis=-1), dtype=F32)


def _pad_lanes(v, start):
    v = v.reshape(1, -1).astype(F32)
    return jnp.pad(v, ((0, 0), (start, LANES - start - v.shape[1])))


def kernel(x_prompt, x_sample, state_mlstm_C, state_mlstm_n, state_mlstm_m, state_ssd, c, c_ctx, w_in, w_o, mlstm_b_i, mlstm_b_f, mlstm_norm_g, ssd_conv_w, ssd_conv_b, ssd_dt_bias, ssd_A_log, ssd_D, ssd_norm_g, conv_dw_w, conv_dw_b, conv_ln_g, conv_ln_b, w_ada, b_ada, post1_g, post1_b, post2_g, post2_b, w_router, b_router, w_e_gate, w_e_up, w_e_down):
    cvec = jnp.concatenate([c_ctx[None, :], c, jnp.zeros((N_MOD_ROWS - 1 - DEC_BATCH, D_MODEL), F32)], axis=0)
    mod_all = _ada(cvec, w_ada, b_ada)
    x_src = (x_prompt.reshape(N_CTX_TOK, D_MODEL), x_sample.reshape(N_LAT_TOK, D_MODEL), _grid_pos(DEC_SEQ))
    w_rt = w_router.T
    b_r = b_router.reshape(N_EXPERTS, 1)

    a_end = 4 * W_A + N_DIR * 2 * H_A
    b_end = a_end + W_B + W_XBC + N_DIR * H_B
    init = (state_mlstm_C,
            state_mlstm_n.reshape(DEC_BATCH, DEPTH, N_DIR * H_A, DH_A),
            jnp.broadcast_to(state_mlstm_m.reshape(DEC_BATCH, DEPTH, N_DIR * H_A, 1),
                             (DEC_BATCH, DEPTH, N_DIR * H_A, LANES)))
    st_c = st_n = st_m = st_h = None
    for l in range(DEPTH):
        w = w_in[l]
        xbc0 = a_end + W_B
        w_main = jnp.concatenate([w[:, W_A:2 * W_A], w[:, b_end:], w[:, xbc0 + W_B:xbc0 + W_B + G_B * N_B]],
                                 axis=1).astype(BF16)
        w_small = jnp.concatenate([w[:, 4 * W_A:a_end], w[:, b_end - N_DIR * H_B:b_end],
                                   jnp.zeros((D_MODEL, LANES - _DT_COL0 - N_DIR * H_B), F32)], axis=1)
        w_t = jnp.concatenate([w[:, 0:W_A], w[:, 2 * W_A:4 * W_A], w[:, a_end:xbc0 + W_B],
                               w[:, xbc0 + W_B + G_B * N_B:xbc0 + W_XBC], w_small],
                              axis=1).T.astype(BF16)
        w_small = w_small.astype(BF16)
        gate_bias = (_pad_lanes(jnp.stack([mlstm_b_i[l], mlstm_b_f[l]], axis=1), 0)
                     + _pad_lanes(ssd_dt_bias[l], _DT_COL0))
        alog_row = _pad_lanes(ssd_A_log[l], _DT_COL0)
        mod = mod_all[l]

        proj, gates, proj_t = _inproj(x_src, mod, w_main, w_small, w_t)

        m_norm = jnp.broadcast_to(mlstm_norm_g[l].reshape(W_A, 1), (W_A, LANES))
        mlstm_args = (proj, gates, proj_t, gate_bias, gate_bias.reshape(LANES, 1), m_norm)
        ha_c, st_c, st_n, st_m = _mlstm(*mlstm_args, SEQ, BATCH, 0, l, prev=(st_c, st_n, st_m))
        (ha_l,) = _mlstm(*mlstm_args, DEC_SEQ, DEC_BATCH, N_CTX_TOK, l, init=init)

        nb = G_B * N_B
        rep = lambda v: jnp.broadcast_to(v[..., None], v.shape + (LANES,))
        cw, cb = ssd_conv_w[l], ssd_conv_b[l]
        conv = (rep(cw[:, 0:W_B]), rep(cb[0:W_B]), rep(cw[:, W_B + nb:W_XBC]), rep(cb[W_B + nb:W_XBC]),
                jnp.pad(cw[:, W_B:W_B + nb], ((0, 8 - SSM_CONV), (0, 0))), cb[W_B:W_B + nb].reshape(1, nb))
        ssd_args = (proj, gates, proj_t, gate_bias, gate_bias.reshape(LANES, 1), alog_row, alog_row.reshape(LANES, 1),
                    conv, rep(jnp.repeat(ssd_D[l], P_B)), rep(ssd_norm_g[l]))
        hb_c, st_h = _ssd(*ssd_args, SEQ, BATCH, 0, l, prev=st_h)
        (hb_l,) = _ssd(*ssd_args, DEC_SEQ, DEC_BATCH, N_CTX_TOK, l, init=state_ssd)

        hc = _conf(proj, jnp.pad(conv_dw_w[l], ((0, 32 - CONV_W), (0, 0))), conv_dw_b[l].reshape(1, W_C),
                   conv_ln_g[l].reshape(1, W_C), conv_ln_b[l].reshape(1, W_C))

        x1, u2t, route = _outproj(x_src, ha_c, ha_l, hb_c, hb_l, hc, mod, w_o[l].astype(BF16), post1_g[l].reshape(1, D_MODEL),
                                  post1_b[l].reshape(1, D_MODEL), w_rt, b_r)
        y = _moe(u2t, *_route_tables(route[0]), w_e_gate, w_e_up, w_e_down, l)
        outs = _final(x1, y, mod, post2_g[l].reshape(1, D_MODEL), post2_b[l].reshape(1, D_MODEL), l == DEPTH - 1)
        x_src = (outs[0],)

    y_prompt = outs[0].reshape(BATCH, SEQ, D_MODEL)
    y_sample = outs[1].reshape(DEC_BATCH, DEC_SEQ, D_MODEL)
    return (y_prompt, y_sample, st_c, st_n.reshape(BATCH, DEPTH, N_DIR, H_A, DH_A),
            st_m[:, :, :, 0].reshape(BATCH, DEPTH, N_DIR, H_A), st_h)
```
